```python
import jax, jax.numpy as jnp
from jax import lax
import numpy as np

D_MODEL = 1024
BATCH = 8
SEQ = 4096
DEPTH = 2

CHUNK = 64
BLOCK = 128
N_META = 16
PAD_FRONT = BLOCK - N_META
HEAD_DIM = 64
NORM_EPS = 1e-6
NEG = -1e30
SWA_HEADS = 4
SWA_KV_HEADS = 2
WINDOW = 128
WIN_CHUNKS = WINDOW // CHUNK
FOX_HEADS = 4
MLA_HEADS = 4
MLA_Q_LORA = 256
MLA_KV_LORA = 128
MLA_NOPE = 64
MLA_ROPE = 32
MLA_V = 64
ROPE_THETA = 10000.0
SB_HEADS = 4
IN_SIZES = (SWA_HEADS * HEAD_DIM, SWA_KV_HEADS * HEAD_DIM, SWA_KV_HEADS * HEAD_DIM,
            FOX_HEADS * HEAD_DIM, FOX_HEADS * HEAD_DIM, FOX_HEADS * HEAD_DIM, FOX_HEADS,
            MLA_Q_LORA, MLA_KV_LORA, MLA_ROPE,
            SB_HEADS * HEAD_DIM, SB_HEADS * HEAD_DIM, SB_HEADS * HEAD_DIM)
IN_COLS = sum(IN_SIZES)
SPLIT_POINTS = tuple(int(c) for c in np.cumsum(IN_SIZES)[:-1])
MIX_WIDTH = SWA_HEADS * HEAD_DIM + FOX_HEADS * HEAD_DIM + MLA_HEADS * MLA_V + SB_HEADS * HEAD_DIM
N_GROUPS = 4
EXPERTS_PER_GROUP = 8
N_EXPERTS = N_GROUPS * EXPERTS_PER_GROUP
TOP_K = 2
EXPERT_HIDDEN = 512
MOE_BLOCK = 256

kernel_name = "hymba_style_parallel_hybrid_hmoe"


def rms_norm(x, g):
    xf = x.astype(jnp.float32)
    y = xf * lax.rsqrt(jnp.mean(xf * xf, axis=-1, keepdims=True) + NORM_EPS)
    return (y * g.astype(jnp.float32)).astype(x.dtype)


def seq_layout(L):
    p = jnp.arange(L, dtype=jnp.int32)
    pos = p - PAD_FRONT
    cid = jnp.where(p < PAD_FRONT, -1, jnp.where(p < BLOCK, 0, (p - BLOCK) // CHUNK + 1))
    return pos, cid.astype(jnp.int32)


def key_ok(ct, cs):
    return (cs >= 0) | (ct < 0)


def alibi_slopes(n):
    return jnp.asarray([2.0 ** (-8.0 * (h + 1) / n) for h in range(n)], jnp.float32)


def rope(x, pos):
    half = x.shape[-1] // 2
    inv_freq = ROPE_THETA ** (-jnp.arange(half, dtype=jnp.float32) / half)
    ang = pos.astype(jnp.float32)[:, None] * inv_freq[None, :]
    cos = jnp.cos(ang)[None, :, None, :]
    sin = jnp.sin(ang)[None, :, None, :]
    xf = x.astype(jnp.float32)
    x1, x2 = xf[..., :half], xf[..., half:]
    return jnp.concatenate([x1 * cos - x2 * sin, x1 * sin + x2 * cos], axis=-1).astype(x.dtype)


def to_blocks(x):
    B, L = x.shape[:2]
    return jnp.moveaxis(x.reshape(B, L // BLOCK, BLOCK, *x.shape[2:]), 1, 0)


def from_blocks(y):
    y = jnp.moveaxis(y, 0, 1)
    return y.reshape(y.shape[0], -1, *y.shape[3:])


def with_prev_block(x, axis, fill):
    first = jnp.full_like(lax.slice_in_dim(x, 0, 1, axis=axis), fill)
    prev = jnp.concatenate([first, lax.slice_in_dim(x, 0, x.shape[axis] - 1, axis=axis)], axis=axis)
    return jnp.concatenate([prev, x], axis=axis + 1)


def swa_sink_attention(q, k, v, sinks, pos, cid):
    B, L = q.shape[:2]
    nb = L // BLOCK
    grp = SWA_HEADS // SWA_KV_HEADS
    qb = q.reshape(B, nb, BLOCK, SWA_KV_HEADS, grp, HEAD_DIM)
    k_band = with_prev_block(k.reshape(B, nb, BLOCK, SWA_KV_HEADS, HEAD_DIM), 1, 0)
    v_band = with_prev_block(v.reshape(B, nb, BLOCK, SWA_KV_HEADS, HEAD_DIM), 1, 0)
    k_meta = k[:, PAD_FRONT:BLOCK]
    v_meta = v[:, PAD_FRONT:BLOCK]
    pos_q = pos.reshape(nb, BLOCK)
    cid_q = cid.reshape(nb, BLOCK)
    pos_band = with_prev_block(pos_q, 0, 0)
    cid_band = with_prev_block(cid_q, 0, -2)
    scale = HEAD_DIM ** -0.5
    s_meta = jnp.einsum('bnqhgd,bmhd->bnhgqm', qb, k_meta, preferred_element_type=jnp.float32) * scale
    s_band = jnp.einsum('bnqhgd,bnkhd->bnhgqk', qb, k_band, preferred_element_type=jnp.float32) * scale
    slopes = alibi_slopes(SWA_HEADS).reshape(SWA_KV_HEADS, grp)[:, :, None, None]
    ct = cid_q[:, :, None]
    cs = cid_band[:, None, :]
    band_vis = (cs >= 1) & (cs <= ct) & (cs >= ct - WIN_CHUNKS)
    meta_vis = jnp.broadcast_to(ct >= 0, (nb, BLOCK, N_META))
    d_band = jnp.abs(pos_q[:, :, None] - pos_band[:, None, :]).astype(jnp.float32)
    d_meta = jnp.minimum(jnp.abs(pos_q[:, :, None] - pos[PAD_FRONT:BLOCK][None, None, :]), WINDOW).astype(jnp.float32)
    logits = jnp.concatenate([
        jnp.where(meta_vis[:, None, None], s_meta - slopes * d_meta[:, None, None], NEG),
        jnp.where(band_vis[:, None, None], s_band - slopes * d_band[:, None, None], NEG)], axis=-1)
    sink = sinks.astype(jnp.float32).reshape(SWA_KV_HEADS, grp)[:, :, None, None]
    m = jnp.maximum(jnp.max(logits, axis=-1, keepdims=True), sink)
    e = jnp.exp(logits - m)
    p = (e / (jnp.sum(e, axis=-1, keepdims=True) + jnp.exp(sink - m))).astype(v.dtype)
    out = (jnp.einsum('bnhgqm,bmhd->bnqhgd', p[..., :N_META], v_meta)
           + jnp.einsum('bnhgqk,bnkhd->bnqhgd', p[..., N_META:], v_band))
    return out.reshape(B, L, SWA_HEADS, HEAD_DIM)


def forgetting_attention(q, k, v, log_f, pos, cid):
    F = jnp.cumsum(log_f, axis=1)
    F_keys = jnp.transpose(F, (0, 2, 1))[:, :, None, :]
    scale = HEAD_DIM ** -0.5

    def block(args):
        qb, Fb, pb, cb = args
        s = jnp.einsum('bqhd,bkhd->bhqk', qb, k, preferred_element_type=jnp.float32) * scale
        s = s + (jnp.transpose(Fb, (0, 2, 1))[..., None] - F_keys)
        vis = (pos[None, :] <= pb[:, None]) & key_ok(cb[:, None], cid[None, :])
        p = jax.nn.softmax(jnp.where(vis, s, NEG), axis=-1)
        return jnp.einsum('bhqk,bkhd->bqhd', p.astype(v.dtype), v)

    out = lax.map(block, (to_blocks(q), to_blocks(F), pos.reshape(-1, BLOCK), cid.reshape(-1, BLOCK)))
    return from_blocks(out)


def chunk_causal_attention(q, k, v, cid, scale):
    def block(args):
        qb, cb = args
        s = jnp.einsum('bqhd,bkhd->bhqk', qb, k, preferred_element_type=jnp.float32) * scale
        vis = (cid[None, :] <= cb[:, None]) & key_ok(cb[:, None], cid[None, :])
        p = jax.nn.softmax(jnp.where(vis, s, NEG), axis=-1)
        return jnp.einsum('bhqk,bkhd->bqhd', p.astype(v.dtype), v)

    out = lax.map(block, (to_blocks(q), cid.reshape(-1, BLOCK)))
    return from_blocks(out)


def stick_breaking_attention(q, k, v, pos, cid):
    scale = HEAD_DIM ** -0.5

    def block(args):
        qb, pb, cb = args
        z = jnp.einsum('bqhd,bkhd->bhqk', qb, k, preferred_element_type=jnp.float32) * scale
        vis = (pos[None, :] < pb[:, None]) & key_ok(cb[:, None], cid[None, :])
        log_keep = jnp.where(vis, jax.nn.log_sigmoid(-z), 0.0)
        log_after = lax.cumsum(log_keep, axis=3, reverse=True) - log_keep
        a = jnp.where(vis, jnp.exp(jax.nn.log_sigmoid(z) + log_after), 0.0)
        return jnp.einsum('bhqk,bkhd->bqhd', a.astype(v.dtype), v)

    out = lax.map(block, (to_blocks(q), pos.reshape(-1, BLOCK), cid.reshape(-1, BLOCK)))
    return from_blocks(out)


def token_mixers(u, pos, cid, w_in, b_forget, sinks, q_norm, kv_norm, w_uq, w_ukv, w_out):
    B, L, _ = u.shape
    proj = u @ w_in
    (a_q, a_k, a_v, f_q, f_k, f_v, f_g, c_q, c_kv, c_kr, s_q, s_k, s_v) = jnp.split(proj, SPLIT_POINTS, axis=-1)

    def heads(t, n):
        return t.reshape(B, L, n, -1)

    y_a = swa_sink_attention(heads(a_q, SWA_HEADS), heads(a_k, SWA_KV_HEADS), heads(a_v, SWA_KV_HEADS), sinks, pos, cid)
    log_f = jax.nn.log_sigmoid(f_g.astype(jnp.float32) + b_forget.astype(jnp.float32))
    y_b = forgetting_attention(heads(f_q, FOX_HEADS), heads(f_k, FOX_HEADS), heads(f_v, FOX_HEADS), log_f, pos, cid)
    q_c = (rms_norm(c_q, q_norm) @ w_uq).reshape(B, L, MLA_HEADS, MLA_NOPE + MLA_ROPE)
    kv_c = (rms_norm(c_kv, kv_norm) @ w_ukv).reshape(B, L, MLA_HEADS, MLA_NOPE + MLA_V)
    k_rope = rope(c_kr[:, :, None, :], pos)
    q_mla = jnp.concatenate([q_c[..., :MLA_NOPE], rope(q_c[..., MLA_NOPE:], pos)], axis=-1)
    k_mla = jnp.concatenate([kv_c[..., :MLA_NOPE], jnp.broadcast_to(k_rope, (B, L, MLA_HEADS, MLA_ROPE))], axis=-1)
    y_c = chunk_causal_attention(q_mla, k_mla, kv_c[..., MLA_NOPE:], cid, (MLA_NOPE + MLA_ROPE) ** -0.5)
    y_d = stick_breaking_attention(heads(s_q, SB_HEADS), heads(s_k, SB_HEADS), heads(s_v, SB_HEADS), pos, cid)
    y = jnp.concatenate([y_a.reshape(B, L, -1), y_b.reshape(B, L, -1),
                         y_c.reshape(B, L, -1), y_d.reshape(B, L, -1)], axis=-1)
    return y @ w_out


def hier_moe(x, w_group, b_group, w_router, b_router, w_gate, w_up, w_down):
    B, L, D = x.shape
    T = B * L
    xt = x.reshape(T, D)
    g_logits = (xt @ w_group).astype(jnp.float32) + b_group.astype(jnp.float32)
    g_prob = jax.nn.softmax(g_logits, axis=-1)
    g_top = jnp.argmax(g_logits, axis=-1).astype(jnp.int32)
    g_w = jnp.take_along_axis(g_prob, g_top[:, None], axis=-1)
    e_logits = ((xt @ w_router).astype(jnp.float32) + b_router.astype(jnp.float32)).reshape(T, N_GROUPS, EXPERTS_PER_GROUP)
    e_in_group = jnp.take_along_axis(e_logits, g_top[:, None, None], axis=1)[:, 0]
    top_v, top_i = lax.top_k(e_in_group, TOP_K)
    e_w = jax.nn.softmax(top_v, axis=-1) * g_w
    expert = g_top[:, None] * EXPERTS_PER_GROUP + top_i.astype(jnp.int32)

    A = T * TOP_K
    flat_e = expert.reshape(A)
    flat_tok = jnp.repeat(jnp.arange(T, dtype=jnp.int32), TOP_K)
    flat_w = e_w.reshape(A)
    order = jnp.argsort(flat_e)
    se, stok, sw = flat_e[order], flat_tok[order], flat_w[order]
    counts = jnp.bincount(flat_e, length=N_EXPERTS)
    padded = (counts + MOE_BLOCK - 1) // MOE_BLOCK * MOE_BLOCK
    pad_end = jnp.cumsum(padded)
    pad_start = pad_end - padded
    raw_start = jnp.cumsum(counts) - counts
    dest = pad_start[se] + (jnp.arange(A, dtype=jnp.int32) - raw_start[se])
    P = A + N_EXPERTS * MOE_BLOCK
    n_blk = P // MOE_BLOCK
    rows_tok = jnp.zeros((P,), jnp.int32).at[dest].set(stok)
    rows_w = jnp.zeros((P,), jnp.float32).at[dest].set(sw)
    block_e = jnp.minimum(jnp.searchsorted(pad_end, jnp.arange(n_blk) * MOE_BLOCK, side='right'), N_EXPERTS - 1)
    x_rows = xt[rows_tok].reshape(n_blk, MOE_BLOCK, D)

    def expert_block(args):
        xb, e = args
        hb = jax.nn.silu(xb @ w_gate[e]) * (xb @ w_up[e])
        return hb @ w_down[e]

    y_rows = lax.map(expert_block, (x_rows, block_e)).reshape(P, D)
    y = jnp.zeros((T, D), jnp.float32).at[rows_tok].add(y_rows.astype(jnp.float32) * rows_w[:, None])
    return y.astype(x.dtype).reshape(B, L, D)


def setup_inputs(seed: int = 0) -> dict:
    key = jax.random.key(seed)
    ks = jax.random.split(key, 20)
    f32 = jnp.float32

    def nrm(k, shape, scale):
        return jax.random.normal(k, shape, f32) * scale

    return {
        "x": nrm(ks[0], (BATCH, SEQ, D_MODEL), 1.0),
        "meta_tokens": nrm(ks[1], (N_META, D_MODEL), 1.0),
        "attn_norm": 1.0 + nrm(ks[2], (DEPTH, D_MODEL), 0.02),
        "w_in": nrm(ks[3], (DEPTH, D_MODEL, IN_COLS), D_MODEL ** -0.5),
        "b_forget": jax.random.uniform(ks[4], (DEPTH, FOX_HEADS), f32, 1.0, 4.0),
        "sinks": nrm(ks[5], (DEPTH, SWA_HEADS), 0.5),
        "mla_q_norm": 1.0 + nrm(ks[6], (DEPTH, MLA_Q_LORA), 0.02),
        "mla_kv_norm": 1.0 + nrm(ks[7], (DEPTH, MLA_KV_LORA), 0.02),
        "mla_w_uq": nrm(ks[8], (DEPTH, MLA_Q_LORA, MLA_HEADS * (MLA_NOPE + MLA_ROPE)), MLA_Q_LORA ** -0.5),
        "mla_w_ukv": nrm(ks[9], (DEPTH, MLA_KV_LORA, MLA_HEADS * (MLA_NOPE + MLA_V)), MLA_KV_LORA ** -0.5),
        "w_out": nrm(ks[10], (DEPTH, MIX_WIDTH, D_MODEL), MIX_WIDTH ** -0.5),
        "ffn_norm": 1.0 + nrm(ks[11], (DEPTH, D_MODEL), 0.02),
        "w_group": nrm(ks[12], (DEPTH, D_MODEL, N_GROUPS), D_MODEL ** -0.5),
        "b_group": nrm(ks[13], (DEPTH, N_GROUPS), 0.01),
        "w_router": nrm(ks[14], (DEPTH, D_MODEL, N_EXPERTS), D_MODEL ** -0.5),
        "b_router": nrm(ks[15], (DEPTH, N_EXPERTS), 0.01),
        "w_gate": nrm(ks[16], (DEPTH, N_EXPERTS, D_MODEL, EXPERT_HIDDEN), D_MODEL ** -0.5),
        "w_up": nrm(ks[17], (DEPTH, N_EXPERTS, D_MODEL, EXPERT_HIDDEN), D_MODEL ** -0.5),
        "w_down": nrm(ks[18], (DEPTH, N_EXPERTS, EXPERT_HIDDEN, D_MODEL), EXPERT_HIDDEN ** -0.5),
        "final_norm": 1.0 + nrm(ks[19], (D_MODEL,), 0.02),
    }


def reference(x, meta_tokens, attn_norm, w_in, b_forget, sinks, mla_q_norm, mla_kv_norm,
              mla_w_uq, mla_w_ukv, w_out, ffn_norm, w_group, b_group, w_router, b_router,
              w_gate, w_up, w_down, final_norm):
    B, S, D = x.shape
    L = S + BLOCK
    pad = jnp.zeros((B, PAD_FRONT, D), x.dtype)
    meta = jnp.broadcast_to(meta_tokens.astype(x.dtype)[None], (B, N_META, D))
    h = jnp.concatenate([pad, meta, x], axis=1)
    pos, cid = seq_layout(L)
    for i in range(DEPTH):
        h = h + token_mixers(rms_norm(h, attn_norm[i]), pos, cid, w_in[i], b_forget[i], sinks[i],
                             mla_q_norm[i], mla_kv_norm[i], mla_w_uq[i], mla_w_ukv[i], w_out[i])
        h = h + hier_moe(rms_norm(h, ffn_norm[i]), w_group[i], b_group[i], w_router[i], b_router[i],
                         w_gate[i], w_up[i], w_down[i])
    return rms_norm(h[:, BLOCK:], final_norm)
```

```python
import functools

import jax
import jax.numpy as jnp
import numpy as np
from jax import lax
from jax.experimental import pallas as pl
from jax.experimental.pallas import tpu as pltpu

F32 = jnp.float32
BF16 = jnp.bfloat16

BLOCK = 128
N_META = 16
PAD_FRONT = BLOCK - N_META
CHUNK_SHIFT = 6
HEAD_DIM = 64
NORM_EPS = 1e-6
NEG = -1e30
BIG = 1 << 30
SWA_HEADS, SWA_KV_HEADS, WINDOW = 4, 2, 128
MLA_Q_LORA, MLA_KV_LORA, MLA_NOPE, MLA_ROPE, MLA_V = 256, 128, 64, 32, 64
ROPE_THETA = 10000.0
N_GROUPS, EXPERTS_PER_GROUP, TOP_K = 4, 8, 2
N_EXPERTS = N_GROUPS * EXPERTS_PER_GROUP
MOE_BLOCK = 256
LANES = 128
ROW_TILE = 384
Q_TILE = 512
K_TILE = 256
VMEM_LIMIT = 56 * 1024 * 1024

C_A, C_B, C_D, C_CQ, C_CKV, C_G, C_GS, C_END = 0, 512, 1280, 2048, 2304, 2432, 2560, 2688


def _rms(x, g):
    return x * lax.rsqrt(jnp.mean(x * x, axis=-1, keepdims=True) + NORM_EPS) * g


def _log_sigmoid(x):
    return jnp.minimum(x, 0.0) - jnp.log(1.0 + jnp.exp(-jnp.abs(x)))


def _dot(a, b):
    return jnp.dot(a, b, preferred_element_type=F32)


def _dot_nt(a, b):
    return lax.dot_general(a, b, (((1,), (1,)), ((), ())), preferred_element_type=F32)


def _tile4(x):
    return jnp.concatenate([x, x, x, x], axis=1)


def _inproj_kernel(has_y2, *refs):
    if has_y2:
        (h_ref, y2_ref, g_ref, w_ref, qn_ref, kvn_ref, wuqa_ref, wuqb_ref, wkvk_ref, wkvv_ref,
         tab_ref, bf_ref, hout_ref, qa_ref, qb_ref, qd_ref, cq_ref, ck_ref, cv_ref, lf_ref) = refs
    else:
        (h_ref, g_ref, w_ref, qn_ref, kvn_ref, wuqa_ref, wuqb_ref, wkvk_ref, wkvv_ref,
         tab_ref, bf_ref, qa_ref, qb_ref, qd_ref, cq_ref, ck_ref, cv_ref, lf_ref) = refs
    h = h_ref[...]
    d = h.shape[1]
    if has_y2:
        h = h + y2_ref[:, :d] + y2_ref[:, d:]
        hout_ref[...] = h
    xn = _rms(h, g_ref[...]).astype(BF16)
    acc = _dot(xn, w_ref[...])
    qa_ref[...] = acc[:, C_A:C_B].astype(BF16)
    qb_ref[...] = acc[:, C_B:C_D].astype(BF16)
    qd_ref[...] = acc[:, C_D:C_CQ].astype(BF16)
    tab = tab_ref[...]
    cos_q, sin_q = tab[:, 0:128], tab[:, 128:256]
    cos_k, sin_k = tab[:, 256:384], tab[:, 384:512]
    cqn = _rms(acc[:, C_CQ:C_CKV], qn_ref[...]).astype(BF16)
    q_lin = _dot(cqn, wuqa_ref[...])
    q_swp = _dot(cqn, wuqb_ref[...])
    cq_ref[...] = (q_lin * _tile4(cos_q) + q_swp * _tile4(sin_q)).astype(BF16)
    ckvn = _rms(acc[:, C_CKV:C_G], kvn_ref[...]).astype(BF16)
    k_nope = _dot(ckvn, wkvk_ref[...])
    grp, grp_s = acc[:, C_G:C_GS], acc[:, C_GS:C_END]
    k_rope = grp * cos_k + grp_s * sin_k
    ck_ref[...] = (k_nope + _tile4(k_rope)).astype(BF16)
    cv_ref[...] = _dot(ckvn, wkvv_ref[...]).astype(BF16)
    lf_ref[...] = _log_sigmoid(acc[:, C_G:C_END] + bf_ref[...])


def _inproj(h, y2, p, seq_len):
    t, d = h.shape
    tm = ROW_TILE
    n_seq_tiles = seq_len // tm
    has_y2 = y2 is not None
    row = lambda i: (i, 0)
    fixed = lambda i: (0, 0)
    in_specs = [pl.BlockSpec((tm, d), row)]
    args = [h]
    if has_y2:
        in_specs.append(pl.BlockSpec((tm, 2 * d), row))
        args.append(y2)
    consts = [p["attn_norm"], p["w_in"], p["q_norm"], p["kv_norm"], p["w_uq_a"], p["w_uq_b"],
              p["w_kv_k"], p["w_kv_v"]]
    in_specs += [pl.BlockSpec(c.shape, fixed) for c in consts]
    args += consts
    in_specs.append(pl.BlockSpec((tm, 512), lambda i: (i % n_seq_tiles, 0)))
    args.append(p["rope_tab"])
    in_specs.append(pl.BlockSpec((1, 256), fixed))
    args.append(p["b_forget"])
    widths = [(512, BF16), (768, BF16), (768, BF16), (512, BF16), (512, BF16), (256, BF16), (256, F32)]
    out_shape = [jax.ShapeDtypeStruct((t, w), dt) for w, dt in widths]
    out_specs = [pl.BlockSpec((tm, w), row) for w, _ in widths]
    if has_y2:
        out_shape = [jax.ShapeDtypeStruct((t, d), F32)] + out_shape
        out_specs = [pl.BlockSpec((tm, d), row)] + out_specs
    outs = pl.pallas_call(
        functools.partial(_inproj_kernel, has_y2),
        grid=(t // tm,),
        in_specs=in_specs,
        out_specs=out_specs,
        out_shape=out_shape,
        compiler_params=pltpu.CompilerParams(dimension_semantics=("arbitrary",),
                                             vmem_limit_bytes=VMEM_LIMIT),
        name="inproj_y2" if has_y2 else "inproj",
    )(*args)
    if has_y2:
        return outs[0], outs[1:]
    return h, outs


def _cumsum_kernel(x_ref, o_ref):
    n_blk = x_ref.shape[1] // BLOCK
    r = lax.broadcasted_iota(jnp.int32, (BLOCK, BLOCK), 0)
    c = lax.broadcasted_iota(jnp.int32, (BLOCK, BLOCK), 1)
    tri = jnp.where(c <= r, 1.0, 0.0).astype(BF16)

    def body(b, carry):
        r0 = pl.multiple_of(b * BLOCK, BLOCK)
        x = x_ref[0, pl.ds(r0, BLOCK), :]
        hi = x.astype(BF16)
        r1 = x - hi.astype(F32)
        mid = r1.astype(BF16)
        lo = (r1 - mid.astype(F32)).astype(BF16)
        y = _dot(tri, hi) + _dot(tri, mid) + _dot(tri, lo) + carry
        o_ref[0, pl.ds(r0, BLOCK), :] = y
        return y[BLOCK - 1:BLOCK, :]

    lax.fori_loop(0, n_blk, body, jnp.zeros((1, x_ref.shape[2]), F32))


def _forget_cumsum(lf):
    b, l, w = lf.shape
    return pl.pallas_call(
        _cumsum_kernel,
        grid=(b,),
        in_specs=[pl.BlockSpec((1, l, w), lambda i: (i, 0, 0))],
        out_specs=pl.BlockSpec((1, l, w), lambda i: (i, 0, 0)),
        out_shape=jax.ShapeDtypeStruct((b, l, w), F32),
        compiler_params=pltpu.CompilerParams(dimension_semantics=("arbitrary",),
                                             vmem_limit_bytes=VMEM_LIMIT),
        name="forget_cumsum",
    )(lf)


def _swa_kernel(sink_ref, q_ref, km_ref, kp_ref, kc_ref, vm_ref, vp_ref, vc_ref, o_ref):
    i = pl.program_id(1)
    tq = q_ref.shape[1]
    lane = lax.broadcasted_iota(jnp.int32, (1, LANES), 1)
    lo_half = lane < HEAD_DIM
    half_masks = [jnp.where(lo_half, 1.0, 0.0).astype(BF16), jnp.where(lo_half, 0.0, 1.0).astype(BF16)]
    pq = i * tq + lax.broadcasted_iota(jnp.int32, (tq, 1), 0)
    col = lax.broadcasted_iota(jnp.int32, (1, tq), 1)
    cq = pq >> CHUNK_SHIFT
    pk_m = col
    pk_p = (i - 1) * tq + col
    pk_c = i * tq + col
    vis_m = pk_m >= PAD_FRONT
    d_m = jnp.minimum(jnp.abs(pq - pk_m), WINDOW).astype(F32)

    def band(pk):
        ck = jnp.where(pk >= BLOCK, pk >> CHUNK_SHIFT, BIG)
        vis = (ck <= cq) & (ck >= cq - (WINDOW >> CHUNK_SHIFT))
        return vis, jnp.abs(pq - pk).astype(F32)

    vis_p, d_p = band(pk_p)
    vis_c, d_c = band(pk_c)
    grp = SWA_HEADS // SWA_KV_HEADS
    km, kp, kc = km_ref[0], kp_ref[0], kc_ref[0]
    vm, vp, vc = vm_ref[0], vp_ref[0], vc_ref[0]
    for g in range(grp):
        qg = q_ref[0, :, g * LANES:(g + 1) * LANES]
        out_g = None
        for hk in range(SWA_KV_HEADS):
            head = hk * grp + g
            slope = 2.0 ** (-8.0 * (head + 1) / SWA_HEADS)
            sink = sink_ref[head]
            qh = qg * half_masks[hk]
            s_m = jnp.where(vis_m, _dot_nt(qh, km) - slope * d_m, NEG)
            s_p = jnp.where(vis_p, _dot_nt(qh, kp) - slope * d_p, NEG)
            s_c = jnp.where(vis_c, _dot_nt(qh, kc) - slope * d_c, NEG)
            m = jnp.maximum(jnp.maximum(jnp.max(s_m, axis=-1, keepdims=True),
                                        jnp.max(s_p, axis=-1, keepdims=True)),
                            jnp.maximum(jnp.max(s_c, axis=-1, keepdims=True), sink))
            e_m, e_p, e_c = jnp.exp(s_m - m), jnp.exp(s_p - m), jnp.exp(s_c - m)
            den = (jnp.sum(e_m, axis=-1, keepdims=True) + jnp.sum(e_p, axis=-1, keepdims=True)
                   + jnp.sum(e_c, axis=-1, keepdims=True) + jnp.exp(sink - m))
            inv = 1.0 / den
            o = (_dot((e_m * inv).astype(BF16), vm) + _dot((e_p * inv).astype(BF16), vp)
                 + _dot((e_c * inv).astype(BF16), vc))
            out_g = o if hk == 0 else jnp.where(lo_half, out_g, o)
        o_ref[0, :, g * LANES:(g + 1) * LANES] = out_g.astype(BF16)


def _swa_attention(qa, sinks, batch, seq_len):
    x = qa.reshape(batch, seq_len, 512)
    nb = seq_len // BLOCK
    blk = lambda f: pl.BlockSpec((1, BLOCK, LANES), f)
    return pl.pallas_call(
        _swa_kernel,
        grid_spec=pltpu.PrefetchScalarGridSpec(
            num_scalar_prefetch=0,
            grid=(batch, nb),
            in_specs=[
                pl.BlockSpec(memory_space=pltpu.SMEM),
                pl.BlockSpec((1, BLOCK, 2 * LANES), lambda b, i: (b, i, 0)),
                blk(lambda b, i: (b, 0, 2)),
                blk(lambda b, i: (b, jnp.maximum(i - 1, 0), 2)),
                blk(lambda b, i: (b, i, 2)),
                blk(lambda b, i: (b, 0, 3)),
                blk(lambda b, i: (b, jnp.maximum(i - 1, 0), 3)),
                blk(lambda b, i: (b, i, 3)),
            ],
            out_specs=pl.BlockSpec((1, BLOCK, 2 * LANES), lambda b, i: (b, i, 0)),
        ),
        out_shape=jax.ShapeDtypeStruct((batch, seq_len, 2 * LANES), BF16),
        compiler_params=pltpu.CompilerParams(dimension_semantics=("arbitrary", "arbitrary"),
                                             vmem_limit_bytes=VMEM_LIMIT),
        name="swa_attention",
    )(sinks, x, x, x, x, x, x, x)


def _causal_kernel(mode, *refs):
    if mode == "fox":
        q_ref, k_ref, v_ref, fq_ref, fk_ref, o_ref = refs
    else:
        q_ref, k_ref, v_ref, o_ref = refs
    seq_len = q_ref.shape[1]
    n_qt = (seq_len - BLOCK) // Q_TILE
    lane = lax.broadcasted_iota(jnp.int32, (1, LANES), 1)
    lo_half = lane < HEAD_DIM
    half_masks = [jnp.where(lo_half, 1.0, 0.0).astype(BF16), jnp.where(lo_half, 0.0, 1.0).astype(BF16)]
    if mode == "sb":
        r = lax.broadcasted_iota(jnp.int32, (K_TILE, K_TILE), 0)
        c = lax.broadcasted_iota(jnp.int32, (K_TILE, K_TILE), 1)
        later = jnp.where(r > c, 1.0, 0.0).astype(BF16)

    def get_q(q0, tq, hh):
        if mode == "mla":
            return q_ref[0, pl.ds(q0, tq), hh * LANES:(hh + 1) * LANES]
        return q_ref[0, pl.ds(q0, tq), :] * half_masks[hh]

    def get_k(k0, tk, hh):
        if mode == "mla":
            return k_ref[0, pl.ds(k0, tk), hh * LANES:(hh + 1) * LANES]
        return k_ref[0, pl.ds(k0, tk), :]

    def visible(pq, k0, tk):
        pk = k0 + lax.broadcasted_iota(jnp.int32, (1, tk), 1)
        if mode == "fox":
            return jnp.where(pk < PAD_FRONT, BIG, pk) <= pq
        if mode == "mla":
            return jnp.where(pk < PAD_FRONT, BIG, pk >> CHUNK_SHIFT) <= (pq >> CHUNK_SHIFT)
        return jnp.where(pk < PAD_FRONT, BIG, pk) < pq

    def softmax_step(qh, pq, fq, hh, k0, tk, state):
        m, l, acc = state
        s = _dot_nt(qh, get_k(k0, tk, hh))
        if mode == "fox":
            s = s + (fq - fk_ref[0, 0, hh:hh + 1, pl.ds(k0, tk)])
        s = jnp.where(visible(pq, k0, tk), s, NEG)
        m_new = jnp.maximum(m, jnp.max(s, axis=-1, keepdims=True))
        alpha = jnp.exp(m - m_new)
        p = jnp.exp(s - m_new)
        l = alpha * l + jnp.sum(p, axis=-1, keepdims=True)
        acc = alpha * acc + _dot(p.astype(BF16), v_ref[0, pl.ds(k0, tk), :])
        return m_new, l, acc

    def stick_step(qh, pq, hh, k0, tk, state):
        carry, acc = state
        z = _dot_nt(qh, get_k(k0, tk, hh))
        vis = visible(pq, k0, tk)
        ls_pos = _log_sigmoid(z)
        log_keep = jnp.where(vis, ls_pos - z, 0.0)
        hi = log_keep.astype(BF16)
        lo = (log_keep - hi.astype(F32)).astype(BF16)
        lt = later[:tk, :tk]
        after = _dot(hi, lt) + _dot(lo, lt)
        a = jnp.where(vis, jnp.exp(ls_pos + after + carry), 0.0)
        acc = acc + _dot(a.astype(BF16), v_ref[0, pl.ds(k0, tk), :])
        carry = carry + after[:, 0:1] + log_keep[:, 0:1]
        return carry, acc

    def run_tile(q0, tq, n_chunks):
        pq = q0 + lax.broadcasted_iota(jnp.int32, (tq, 1), 0)
        out = None
        for hh in range(2):
            qh = get_q(q0, tq, hh)
            if mode == "sb":
                state = (jnp.zeros((tq, 1), F32), jnp.zeros((tq, LANES), F32))
                if n_chunks is not None:
                    def body(jj, st):
                        k0 = pl.multiple_of(BLOCK + (n_chunks - 1 - jj) * K_TILE, BLOCK)
                        return stick_step(qh, pq, hh, k0, K_TILE, st)
                    state = lax.fori_loop(0, n_chunks, body, state)
                state = stick_step(qh, pq, hh, 0, BLOCK, state)
                o = state[1]
            else:
                fq = fq_ref[0, pl.ds(q0, tq), hh:hh + 1] if mode == "fox" else None
                state = (jnp.full((tq, 1), NEG, F32), jnp.zeros((tq, 1), F32), jnp.zeros((tq, LANES), F32))
                state = softmax_step(qh, pq, fq, hh, 0, BLOCK, state)
                if n_chunks is not None:
                    def body(j, st):
                        k0 = pl.multiple_of(BLOCK + j * K_TILE, BLOCK)
                        return softmax_step(qh, pq, fq, hh, k0, K_TILE, st)
                    state = lax.fori_loop(0, n_chunks, body, state)
                o = state[2] / state[1]
            out = o if hh == 0 else jnp.where(lo_half, out, o)
        o_ref[0, pl.ds(q0, tq), :] = out.astype(BF16)

    run_tile(0, BLOCK, None)

    def q_body(i, _):
        q0 = pl.multiple_of(BLOCK + i * Q_TILE, BLOCK)
        run_tile(q0, Q_TILE, (i + 1) * (Q_TILE // K_TILE))
        return 0

    lax.fori_loop(0, n_qt, q_body, 0)


def _causal_attention(mode, qkv, fq, fk, batch, seq_len):
    pair = lambda off: pl.BlockSpec((1, seq_len, LANES), lambda b, p: (b, 0, off + p))
    if mode == "mla":
        cq, ck, cv = qkv
        args = [cq.reshape(batch, seq_len, 512), ck.reshape(batch, seq_len, 512),
                cv.reshape(batch, seq_len, 256)]
        wide = pl.BlockSpec((1, seq_len, 2 * LANES), lambda b, p: (b, 0, p))
        in_specs = [wide, wide, pair(0)]
    else:
        x = qkv.reshape(batch, seq_len, 768)
        args = [x, x, x]
        in_specs = [pair(0), pair(2), pair(4)]
    if mode == "fox":
        args += [fq, fk]
        in_specs += [pair(0), pl.BlockSpec((1, 1, 8, seq_len), lambda b, p: (b, p, 0, 0))]
    return pl.pallas_call(
        functools.partial(_causal_kernel, mode),
        grid=(batch, 2),
        in_specs=in_specs,
        out_specs=pair(0),
        out_shape=jax.ShapeDtypeStruct((batch, seq_len, 2 * LANES), BF16),
        compiler_params=pltpu.CompilerParams(dimension_semantics=("arbitrary", "arbitrary"),
                                             vmem_limit_bytes=VMEM_LIMIT),
        name=mode + "_attention",
    )(*args)


def _outproj_kernel(ya_ref, yb_ref, yc_ref, yd_ref, h_ref, wo_ref, g_ref, wrh_ref, wrl_ref, br_ref,
                    h2_ref, xn_ref, lg_ref):
    o = (_dot(ya_ref[...], wo_ref[0]) + _dot(yb_ref[...], wo_ref[1])
         + _dot(yc_ref[...], wo_ref[2]) + _dot(yd_ref[...], wo_ref[3]))
    h2 = h_ref[...] + o
    h2_ref[...] = h2
    xn = _rms(h2, g_ref[...])
    xn_ref[...] = xn
    xh = xn.astype(BF16)
    xl = (xn - xh.astype(F32)).astype(BF16)
    wrh, wrl = wrh_ref[...], wrl_ref[...]
    lg_ref[...] = _dot(xh, wrh) + _dot(xl, wrh) + _dot(xh, wrl) + br_ref[...]


def _outproj(ys, h, p):
    t, d = h.shape
    tm = ROW_TILE
    row = lambda i: (i, 0)
    fixed2 = lambda i: (0, 0)
    in_specs = [pl.BlockSpec((tm, 256), row)] * 4 + [
        pl.BlockSpec((tm, d), row),
        pl.BlockSpec((4, 256, d), lambda i: (0, 0, 0)),
        pl.BlockSpec((1, d), fixed2),
        pl.BlockSpec((d, LANES), fixed2),
        pl.BlockSpec((d, LANES), fixed2),
        pl.BlockSpec((1, LANES), fixed2),
    ]
    return pl.pallas_call(
        _outproj_kernel,
        grid=(t // tm,),
        in_specs=in_specs,
        out_specs=[pl.BlockSpec((tm, d), row), pl.BlockSpec((tm, d), row), pl.BlockSpec((tm, LANES), row)],
        out_shape=[jax.ShapeDtypeStruct((t, d), F32), jax.ShapeDtypeStruct((t, d), F32),
                   jax.ShapeDtypeStruct((t, LANES), F32)],
        compiler_params=pltpu.CompilerParams(dimension_semantics=("arbitrary",),
                                             vmem_limit_bytes=VMEM_LIMIT),
        name="outproj_router",
    )(*ys, h, p["w_out"], p["ffn_norm"], p["w_r_hi"], p["w_r_lo"], p["b_r"])


def _moe_kernel(be_ref, nv_ref, asg_ref, w_ref, x_hbm, wg_ref, wu_ref, wd_ref, y_hbm,
                xbuf, ybuf, sem_in, sem_out):
    i = pl.program_id(0)
    nv = nv_ref[i]

    @pl.when(i == 0)
    def _():
        xbuf[...] = jnp.zeros_like(xbuf)

    def in_copy(r):
        tok = asg_ref[0, 0, r] >> 1
        return pltpu.make_async_copy(x_hbm.at[pl.ds(tok, 1)], xbuf.at[pl.ds(r, 1)], sem_in)

    def out_copy(r):
        return pltpu.make_async_copy(ybuf.at[pl.ds(r, 1)], y_hbm.at[pl.ds(asg_ref[0, 0, r], 1)], sem_out)

    @pl.when(nv > 0)
    def _():
        def start_in(r, _):
            in_copy(r).start()
            return 0
        lax.fori_loop(0, nv, start_in, 0)

        def wait_in(r, _):
            in_copy(r).wait()
            return 0
        lax.fori_loop(0, nv, wait_in, 0)

        x = xbuf[...].astype(BF16)
        gate = _dot(x, wg_ref[0])
        up = _dot(x, wu_ref[0])
        hid = (gate * (1.0 / (1.0 + jnp.exp(-gate))) * up).astype(BF16)
        ybuf[...] = _dot(hid, wd_ref[0]) * w_ref[...]

        def start_out(r, _):
            out_copy(r).start()
            return 0
        lax.fori_loop(0, nv, start_out, 0)

        def wait_out(r, _):
            out_copy(r).wait()
            return 0
        lax.fori_loop(0, nv, wait_out, 0)


def _moe(xn, rows_asg, rows_w, block_e, nvalid, p):
    t, d = xn.shape
    n_blk = block_e.shape[0]
    hdim = p["w_gate"].shape[2]
    wspec = lambda shape: pl.BlockSpec((1,) + shape, lambda i, be, nv: (be[i], 0, 0))
    return pl.pallas_call(
        _moe_kernel,
        grid_spec=pltpu.PrefetchScalarGridSpec(
            num_scalar_prefetch=2,
            grid=(n_blk,),
            in_specs=[
                pl.BlockSpec((1, 1, MOE_BLOCK), lambda i, be, nv: (i, 0, 0), memory_space=pltpu.SMEM),
                pl.BlockSpec((MOE_BLOCK, 1), lambda i, be, nv: (i, 0)),
                pl.BlockSpec(memory_space=pl.ANY),
                wspec((d, hdim)), wspec((d, hdim)), wspec((hdim, d)),
            ],
            out_specs=pl.BlockSpec(memory_space=pl.ANY),
            scratch_shapes=[pltpu.VMEM((MOE_BLOCK, d), F32), pltpu.VMEM((MOE_BLOCK, d), F32),
                            pltpu.SemaphoreType.DMA, pltpu.SemaphoreType.DMA],
        ),
        out_shape=jax.ShapeDtypeStruct((t * TOP_K, d), F32),
        compiler_params=pltpu.CompilerParams(dimension_semantics=("arbitrary",),
                                             vmem_limit_bytes=VMEM_LIMIT),
        name="moe_experts",
    )(block_e, nvalid, rows_asg.reshape(n_blk, 1, MOE_BLOCK), rows_w.reshape(-1, 1), xn,
      p["w_gate"], p["w_up"], p["w_down"])


def _route(logits, t):
    g_logits = logits[:, :N_GROUPS]
    e_logits = logits[:, N_GROUPS:N_GROUPS + N_EXPERTS].reshape(t, N_GROUPS, EXPERTS_PER_GROUP)
    g_prob = jax.nn.softmax(g_logits, axis=-1)
    g_top = jnp.argmax(g_logits, axis=-1).astype(jnp.int32)
    g_w = jnp.take_along_axis(g_prob, g_top[:, None], axis=-1)
    e_in_group = jnp.take_along_axis(e_logits, g_top[:, None, None], axis=1)[:, 0]
    top_v, top_i = lax.top_k(e_in_group, TOP_K)
    e_w = jax.nn.softmax(top_v, axis=-1) * g_w
    expert = g_top[:, None] * EXPERTS_PER_GROUP + top_i.astype(jnp.int32)
    a = t * TOP_K
    flat_e = expert.reshape(a)
    flat_w = e_w.reshape(a)
    order = jnp.argsort(flat_e).astype(jnp.int32)
    se = flat_e[order]
    counts = jnp.bincount(flat_e, length=N_EXPERTS).astype(jnp.int32)
    padded = (counts + MOE_BLOCK - 1) // MOE_BLOCK * MOE_BLOCK
    pad_end = jnp.cumsum(padded)
    pad_start = pad_end - padded
    raw_start = jnp.cumsum(counts) - counts
    dest = pad_start[se] + (jnp.arange(a, dtype=jnp.int32) - raw_start[se])
    n_rows = a + N_EXPERTS * MOE_BLOCK
    n_blk = n_rows // MOE_BLOCK
    rows_asg = jnp.full((n_rows,), -1, jnp.int32).at[dest].set(order)
    rows_w = jnp.zeros((n_rows,), F32).at[dest].set(flat_w[order])
    block_e = jnp.minimum(jnp.searchsorted(pad_end, jnp.arange(n_blk, dtype=jnp.int32) * MOE_BLOCK, side="right"),
                          N_EXPERTS - 1).astype(jnp.int32)
    nvalid = jnp.sum((rows_asg >= 0).reshape(n_blk, MOE_BLOCK), axis=1).astype(jnp.int32)
    return rows_asg, rows_w, block_e, nvalid


def _final_kernel(h_ref, y2_ref, g_ref, o_ref):
    d = h_ref.shape[2]
    h = h_ref[0] + y2_ref[0, :, :d] + y2_ref[0, :, d:]
    o_ref[0] = _rms(h, g_ref[...])


def _final(h, y2, g, batch, seq_len):
    d = h.shape[1]
    nb = seq_len // BLOCK - 1
    return pl.pallas_call(
        _final_kernel,
        grid=(batch, nb),
        in_specs=[pl.BlockSpec((1, BLOCK, d), lambda b, i: (b, i + 1, 0)),
                  pl.BlockSpec((1, BLOCK, 2 * d), lambda b, i: (b, i + 1, 0)),
                  pl.BlockSpec((1, d), lambda b, i: (0, 0))],
        out_specs=pl.BlockSpec((1, BLOCK, d), lambda b, i: (b, i, 0)),
        out_shape=jax.ShapeDtypeStruct((batch, seq_len - BLOCK, d), F32),
        compiler_params=pltpu.CompilerParams(dimension_semantics=("arbitrary", "arbitrary"),
                                             vmem_limit_bytes=VMEM_LIMIT),
        name="final_norm",
    )(h.reshape(batch, seq_len, d), y2.reshape(batch, seq_len, 2 * d), g)


def _rope_table(seq_len):
    half = MLA_ROPE // 2
    pos = (jnp.arange(seq_len, dtype=jnp.int32) - PAD_FRONT).astype(F32)
    inv_freq = ROPE_THETA ** (-jnp.arange(half, dtype=F32) / half)
    ang = pos[:, None] * inv_freq[None, :]
    cos, sin = jnp.cos(ang), jnp.sin(ang)
    cos2 = jnp.concatenate([cos, cos], axis=1)
    sin2 = jnp.concatenate([-sin, sin], axis=1)
    z = lambda w: jnp.zeros((seq_len, w), F32)
    scale = (MLA_NOPE + MLA_ROPE) ** -0.5
    cos_q = jnp.concatenate([jnp.ones((seq_len, MLA_NOPE), F32), cos2, z(32)], axis=1) * scale
    sin_q = jnp.concatenate([z(MLA_NOPE), sin2, z(32)], axis=1) * scale
    cos_k = jnp.concatenate([z(MLA_NOPE), cos2, z(32)], axis=1)
    sin_k = jnp.concatenate([z(MLA_NOPE), sin2, z(32)], axis=1)
    return jnp.concatenate([cos_q, sin_q, cos_k, sin_k], axis=1)


def _swap_halves(w):
    half = w.shape[-1] // 2
    return jnp.concatenate([w[..., half:], w[..., :half]], axis=-1)


def _layer_params(i, seq_len, attn_norm, w_in, b_forget, sinks, mla_q_norm, mla_kv_norm, mla_w_uq,
                  mla_w_ukv, w_out, ffn_norm, w_group, b_group, w_router, b_router, w_gate, w_up, w_down):
    d = w_in.shape[1]
    w = w_in[i]
    sizes = (256, 128, 128, 256, 256, 256, 4, 256, 128, 32, 256, 256, 256)
    offs = np.concatenate([[0], np.cumsum(sizes)])
    (a_q, a_k, a_v, f_q, f_k, f_v, f_g, c_q, c_kv, c_kr, s_q, s_k, s_v) = [
        w[:, offs[j]:offs[j + 1]] for j in range(len(sizes))]
    qscale = HEAD_DIM ** -0.5
    grp = SWA_HEADS // SWA_KV_HEADS
    a_q = a_q.reshape(d, SWA_KV_HEADS, grp, HEAD_DIM).transpose(0, 2, 1, 3).reshape(d, 256)
    z = lambda n: jnp.zeros((d, n), F32)
    g_grp = jnp.concatenate([f_g[:, 0:2], z(62), c_kr, z(32)], axis=1)
    gs_grp = jnp.concatenate([f_g[:, 2:4], z(62), _swap_halves(c_kr), z(32)], axis=1)
    w_perm = jnp.concatenate([a_q * qscale, a_k, a_v, f_q * qscale, f_k, f_v, s_q * qscale, s_k, s_v,
                              c_q, c_kv, g_grp, gs_grp], axis=1).astype(BF16)
    wuq = mla_w_uq[i].reshape(MLA_Q_LORA, 4, MLA_NOPE + MLA_ROPE)
    zq = lambda n: jnp.zeros((MLA_Q_LORA, 4, n), F32)
    w_uq_a = jnp.concatenate([wuq, zq(32)], axis=2).reshape(MLA_Q_LORA, 512).astype(BF16)
    w_uq_b = jnp.concatenate([zq(MLA_NOPE), _swap_halves(wuq[:, :, MLA_NOPE:]), zq(32)],
                             axis=2).reshape(MLA_Q_LORA, 512).astype(BF16)
    wukv = mla_w_ukv[i].reshape(MLA_KV_LORA, 4, MLA_NOPE + MLA_V)
    w_kv_k = jnp.concatenate([wukv[:, :, :MLA_NOPE], jnp.zeros((MLA_KV_LORA, 4, 64), F32)],
                             axis=2).reshape(MLA_KV_LORA, 512).astype(BF16)
    w_kv_v = wukv[:, :, MLA_NOPE:].reshape(MLA_KV_LORA, 256).astype(BF16)
    bf = b_forget[i].astype(F32)
    b_f = jnp.zeros((1, 256), F32).at[0, 0:2].set(bf[0:2]).at[0, 128:130].set(bf[2:4])
    wo = w_out[i]
    wo_a = wo[:256].reshape(SWA_KV_HEADS, grp, HEAD_DIM, d).transpose(1, 0, 2, 3).reshape(256, d)
    wo4 = jnp.concatenate([wo_a, wo[256:]], axis=0).reshape(4, 256, d).astype(BF16)
    w_r = jnp.concatenate([w_group[i], w_router[i], jnp.zeros((d, LANES - N_GROUPS - N_EXPERTS), F32)], axis=1)
    w_r_hi = w_r.astype(BF16)
    w_r_lo = (w_r - w_r_hi.astype(F32)).astype(BF16)
    b_r = jnp.concatenate([b_group[i], b_router[i], jnp.zeros((LANES - N_GROUPS - N_EXPERTS,), F32)])[None, :]
    return dict(
        attn_norm=attn_norm[i][None, :], w_in=w_perm, q_norm=mla_q_norm[i][None, :],
        kv_norm=mla_kv_norm[i][None, :], w_uq_a=w_uq_a, w_uq_b=w_uq_b, w_kv_k=w_kv_k, w_kv_v=w_kv_v,
        rope_tab=_rope_table(seq_len), b_forget=b_f, sinks=sinks[i].astype(F32), w_out=wo4,
        ffn_norm=ffn_norm[i][None, :], w_r_hi=w_r_hi, w_r_lo=w_r_lo, b_r=b_r.astype(F32),
        w_gate=w_gate[i].astype(BF16), w_up=w_up[i].astype(BF16), w_down=w_down[i].astype(BF16))


def kernel(x, meta_tokens, attn_norm, w_in, b_forget, sinks, mla_q_norm, mla_kv_norm, mla_w_uq, mla_w_ukv,
           w_out, ffn_norm, w_group, b_group, w_router, b_router, w_gate, w_up, w_down, final_norm):
    batch, seq, d = x.shape
    seq_len = seq + BLOCK
    assert seq_len % ROW_TILE == 0 and seq % Q_TILE == 0
    t = batch * seq_len
    depth = w_in.shape[0]
    pad = jnp.zeros((batch, PAD_FRONT, d), x.dtype)
    meta = jnp.broadcast_to(meta_tokens.astype(x.dtype)[None], (batch, N_META, d))
    h = jnp.concatenate([pad, meta, x], axis=1).reshape(t, d)
    y2 = None
    for i in range(depth):
        p = _layer_params(i, seq_len, attn_norm, w_in, b_forget, sinks, mla_q_norm, mla_kv_norm, mla_w_uq,
                          mla_w_ukv, w_out, ffn_norm, w_group, b_group, w_router, b_router, w_gate, w_up, w_down)
        h, (qa, qb, qd, cq, ck, cv, lf) = _inproj(h, y2, p, seq_len)
        fcum = _forget_cumsum(lf.reshape(batch, seq_len, 256))
        fk = jnp.transpose(fcum.reshape(batch, seq_len, 2, LANES)[:, :, :, :2], (0, 2, 3, 1))
        fk = jnp.pad(fk, ((0, 0), (0, 0), (0, 6), (0, 0)))
        y_a = _swa_attention(qa, p["sinks"], batch, seq_len)
        y_b = _causal_attention("fox", qb, fcum, fk, batch, seq_len)
        y_c = _causal_attention("mla", (cq, ck, cv), None, None, batch, seq_len)
        y_d = _causal_attention("sb", qd, None, None, batch, seq_len)
        ys = [y.reshape(t, 256) for y in (y_a, y_b, y_c, y_d)]
        h, xn, logits = _outproj(ys, h, p)
        rows_asg, rows_w, block_e, nvalid = _route(logits, t)
        y2 = _moe(xn, rows_asg, rows_w, block_e, nvalid, p).reshape(t, TOP_K * d)
    return _final(h, y2, final_norm[None, :], batch, seq_len)
```

```python
import functools

import jax
import jax.numpy as jnp
import numpy as np
from jax import lax
from jax.experimental import pallas as pl
from jax.experimental.pallas import tpu as pltpu

F32 = jnp.float32
BF16 = jnp.bfloat16

BLOCK = 128
N_META = 16
PAD_FRONT = BLOCK - N_META
CHUNK_SHIFT = 6
HEAD_DIM = 64
NORM_EPS = 1e-6
NEG = -1e30
PAD_KEY_LOGIT = -(2.0 ** 100)
UNDERFLOW_LOG2 = -150.0
LOG2E = 1.4426950408889634
BIG = 1 << 30
SWA_HEADS, SWA_KV_HEADS, WINDOW = 4, 2, 128
MLA_Q_LORA, MLA_KV_LORA, MLA_NOPE, MLA_ROPE, MLA_V = 256, 128, 64, 32, 64
MLA_BIAS_LANE = MLA_NOPE + MLA_ROPE
ROPE_THETA = 10000.0
N_GROUPS, EXPERTS_PER_GROUP, TOP_K = 4, 8, 2
N_EXPERTS = N_GROUPS * EXPERTS_PER_GROUP
MOE_BLOCK = 256
LANES = 128
ROW_TILE = 384
Q_TILE = 512
K_TILE = 256
ROW_PART = 256
VMEM_LIMIT = 56 * 1024 * 1024

C_A, C_B, C_D, C_CQ, C_CKV, C_G, C_GS, C_END = 0, 512, 1280, 2048, 2304, 2432, 2560, 2688
B_F0, B_F1, B_PAD = 0, 3, 6


def _rms(x, g):
    return x * lax.rsqrt(jnp.mean(x * x, axis=-1, keepdims=True) + NORM_EPS) * g


def _log_sigmoid(x):
    return jnp.minimum(x, 0.0) - jnp.log(1.0 + jnp.exp(-jnp.abs(x)))


def _dot(a, b):
    return jnp.dot(a, b, preferred_element_type=F32)


def _dot_nt(a, b):
    return lax.dot_general(a, b, (((1,), (1,)), ((), ())), preferred_element_type=F32)


def _tile4(x):
    return jnp.concatenate([x, x, x, x], axis=1)


def _split3(x):
    hi = x.astype(BF16)
    r1 = x - hi.astype(F32)
    mid = r1.astype(BF16)
    lo = (r1 - mid.astype(F32)).astype(BF16)
    return hi, mid, lo


def _free_base(head):
    return head * LANES + (HEAD_DIM if head % 2 == 0 else 0)


def _inproj_kernel(has_y2, n_seq_tiles, *refs):
    if has_y2:
        (h_ref, y2_ref, ew_ref, *rest) = refs
    else:
        (h_ref, *rest) = refs
    (g_ref, w_ref, qn_ref, kvn_ref, wuqa_ref, wuqb_ref, wkvk_ref, wkvv_ref, tab_ref, bf_ref,
     pq_ref, pk_ref, rows_ref, *outs) = rest
    if has_y2:
        hout_ref, *outs = outs
    (qa_ref, fq_ref, fk_ref, fv_ref, cq_ref, ck_ref, cv_ref, sq_ref, sk_ref, sv_ref, carry_ref) = outs
    tile = pl.program_id(0) % n_seq_tiles
    h = h_ref[...]
    tm, d = h.shape
    if has_y2:
        ew = ew_ref[...]
        h = h + ew[:, 0:1] * y2_ref[:, :d] + ew[:, 1:2] * y2_ref[:, d:]
        hout_ref[...] = h
    xn = _rms(h, g_ref[...]).astype(BF16)
    acc = _dot(xn, w_ref[...])
    lane = lax.broadcasted_iota(jnp.int32, (1, LANES), 1)
    lo_half = lane < HEAD_DIM
    pad_col = jnp.where(tile * tm + lax.broadcasted_iota(jnp.int32, (tm, 1), 0) < PAD_FRONT,
                        PAD_KEY_LOGIT, 0.0)
    rows = rows_ref[...]
    fq_one, fk_one, pad_lane, sq_one, mla_one, mla_pad = (rows[j:j + 1] for j in range(6))

    def per_head(x_pair, bias, pair, scale=None):
        x = x_pair if scale is None else x_pair * scale
        even = jnp.where(lo_half, x, bias[:, (2 * pair) * LANES:(2 * pair + 1) * LANES])
        odd = jnp.where(lo_half, bias[:, (2 * pair + 1) * LANES:(2 * pair + 2) * LANES], x)
        return even, odd

    def store_heads(ref, x_off, bias, scale=None):
        for pair in range(2):
            x_pair = acc[:, x_off + pair * LANES:x_off + (pair + 1) * LANES]
            even, odd = per_head(x_pair, bias, pair, scale)
            ref[:, (2 * pair) * LANES:(2 * pair + 1) * LANES] = even.astype(BF16)
            ref[:, (2 * pair + 1) * LANES:(2 * pair + 2) * LANES] = odd.astype(BF16)

    qa_ref[...] = acc[:, C_A:C_B].astype(BF16)

    @pl.when(tile == 0)
    def _():
        carry_ref[...] = jnp.zeros_like(carry_ref)

    lf = _log_sigmoid(acc[:, C_G:C_END] + bf_ref[...]) * LOG2E
    r = lax.broadcasted_iota(jnp.int32, (BLOCK, BLOCK), 0)
    c = lax.broadcasted_iota(jnp.int32, (BLOCK, BLOCK), 1)
    tri = jnp.where(c <= r, 1.0, 0.0).astype(BF16)
    carry = carry_ref[...]
    blocks = []
    for b in range(tm // BLOCK):
        hi, mid, lo = _split3(lf[b * BLOCK:(b + 1) * BLOCK])
        y = _dot(tri, hi) + _dot(tri, mid) + _dot(tri, lo) + carry
        carry = y[BLOCK - 1:BLOCK, :]
        blocks.append(y)
    carry_ref[...] = carry
    f_hi, f_mid, f_lo = _split3(jnp.concatenate(blocks, axis=0))
    q_bias = _dot(f_hi, pq_ref[0]) + _dot(f_mid, pq_ref[1]) + _dot(f_lo, pq_ref[2]) + fq_one
    k_bias = (_dot(f_hi, pk_ref[0]) + _dot(f_mid, pk_ref[1]) + _dot(f_lo, pk_ref[2]) + fk_one
              + pad_col * pad_lane)
    ones = jnp.ones((tm, 4 * LANES), F32)
    store_heads(fq_ref, C_B, q_bias, LOG2E)
    store_heads(fk_ref, C_B + 256, k_bias)
    store_heads(fv_ref, C_B + 512, ones)

    store_heads(sq_ref, C_D, jnp.broadcast_to(sq_one, (tm, 4 * LANES)), LOG2E)
    store_heads(sk_ref, C_D + 256, pad_col * pad_lane)
    sv_ref[...] = acc[:, C_D + 512:C_CQ].astype(BF16)

    tab = tab_ref[...]
    cos_q, sin_q = tab[:, 0:128], tab[:, 128:256]
    cos_k, sin_k = tab[:, 256:384], tab[:, 384:512]
    cqn = _rms(acc[:, C_CQ:C_CKV], qn_ref[...]).astype(BF16)
    q_lin = _dot(cqn, wuqa_ref[...])
    q_swp = _dot(cqn, wuqb_ref[...])
    cq_ref[...] = (q_lin * _tile4(cos_q) + q_swp * _tile4(sin_q) + mla_one).astype(BF16)
    ckvn = _rms(acc[:, C_CKV:C_G], kvn_ref[...]).astype(BF16)
    k_nope = _dot(ckvn, wkvk_ref[...])
    grp, grp_s = acc[:, C_G:C_GS], acc[:, C_GS:C_END]
    k_rope = grp * cos_k + grp_s * sin_k
    ck_ref[...] = (k_nope + _tile4(k_rope) + pad_col * mla_pad).astype(BF16)
    vv = _dot(ckvn, wkvv_ref[...])
    for pair in range(2):
        even, odd = per_head(vv[:, pair * LANES:(pair + 1) * LANES], ones, pair)
        cv_ref[:, (2 * pair) * LANES:(2 * pair + 1) * LANES] = even.astype(BF16)
        cv_ref[:, (2 * pair + 1) * LANES:(2 * pair + 2) * LANES] = odd.astype(BF16)


def _bias_constants():
    src = (0, 1, LANES, LANES + 1)
    pq = np.zeros((3, 2 * LANES, 4 * LANES), np.float32)
    pk = np.zeros((3, 2 * LANES, 4 * LANES), np.float32)
    rows = np.zeros((8, 4 * LANES), np.float32)
    for head in range(4):
        base = _free_base(head)
        for part in range(3):
            pq[part, src[head], base + B_F0 + part] = 1.0
            pk[part, src[head], base + B_F1 + part] = -1.0
            rows[0, base + B_F1 + part] = 1.0
            rows[1, base + B_F0 + part] = 1.0
        rows[0, base + B_PAD] = 1.0
        rows[2, base + B_PAD] = 1.0
        rows[3, base + B_PAD] = 1.0
        rows[4, head * LANES + MLA_BIAS_LANE] = 1.0
        rows[5, head * LANES + MLA_BIAS_LANE] = 1.0
    return jnp.asarray(pq, BF16), jnp.asarray(pk, BF16), jnp.asarray(rows, F32)


def _inproj(h, y2, ew, p, seq_len):
    t, d = h.shape
    tm = ROW_TILE
    n_seq_tiles = seq_len // tm
    has_y2 = y2 is not None
    row = lambda i: (i, 0)
    fixed = lambda i: (0, 0)
    in_specs = [pl.BlockSpec((tm, d), row)]
    args = [h]
    if has_y2:
        in_specs += [pl.BlockSpec((tm, 2 * d), row), pl.BlockSpec((tm, LANES), row)]
        args += [y2, ew]
    pq, pk, rows = _bias_constants()
    consts = [p["attn_norm"], p["w_in"], p["q_norm"], p["kv_norm"], p["w_uq_a"], p["w_uq_b"],
              p["w_kv_k"], p["w_kv_v"]]
    in_specs += [pl.BlockSpec(c.shape, fixed) for c in consts]
    args += consts
    in_specs.append(pl.BlockSpec((tm, 512), lambda i: (i % n_seq_tiles, 0)))
    args.append(p["rope_tab"])
    in_specs += [pl.BlockSpec((1, 256), fixed),
                 pl.BlockSpec(pq.shape, lambda i: (0, 0, 0)),
                 pl.BlockSpec(pk.shape, lambda i: (0, 0, 0)),
                 pl.BlockSpec(rows.shape, fixed)]
    args += [p["b_forget"], pq, pk, rows]
    widths = [512] * 9 + [256]
    out_shape = [jax.ShapeDtypeStruct((t, w), BF16) for w in widths]
    out_specs = [pl.BlockSpec((tm, w), row) for w in widths]
    if has_y2:
        out_shape = [jax.ShapeDtypeStruct((t, d), F32)] + out_shape
        out_specs = [pl.BlockSpec((tm, d), row)] + out_specs
    outs = pl.pallas_call(
        functools.partial(_inproj_kernel, has_y2, n_seq_tiles),
        grid=(t // tm,),
        in_specs=in_specs,
        out_specs=out_specs,
        out_shape=out_shape,
        scratch_shapes=[pltpu.VMEM((1, 2 * LANES), F32)],
        compiler_params=pltpu.CompilerParams(dimension_semantics=("arbitrary",),
                                             vmem_limit_bytes=VMEM_LIMIT),
        name="inproj_y2" if has_y2 else "inproj",
    )(*args)
    if has_y2:
        return outs[0], outs[1:]
    return h, outs


def _swa_kernel(sink_ref, q_ref, km_ref, kp_ref, kc_ref, vm_ref, vp_ref, vc_ref, o_ref):
    i = pl.program_id(1)
    tq = q_ref.shape[1]
    lane = lax.broadcasted_iota(jnp.int32, (1, LANES), 1)
    lo_half = lane < HEAD_DIM
    half_masks = [jnp.where(lo_half, 1.0, 0.0).astype(BF16), jnp.where(lo_half, 0.0, 1.0).astype(BF16)]
    pq = i * tq + lax.broadcasted_iota(jnp.int32, (tq, 1), 0)
    col = lax.broadcasted_iota(jnp.int32, (1, tq), 1)
    cq = pq >> CHUNK_SHIFT
    pk_m = col
    pk_p = (i - 1) * tq + col
    pk_c = i * tq + col
    vis_m = pk_m >= PAD_FRONT
    d_m = jnp.minimum(jnp.abs(pq - pk_m), WINDOW).astype(F32)

    def band(pk):
        ck = jnp.where(pk >= BLOCK, pk >> CHUNK_SHIFT, BIG)
        vis = (ck <= cq) & (ck >= cq - (WINDOW >> CHUNK_SHIFT))
        return vis, jnp.abs(pq - pk).astype(F32)

    vis_p, d_p = band(pk_p)
    vis_c, d_c = band(pk_c)
    grp = SWA_HEADS // SWA_KV_HEADS
    km, kp, kc = km_ref[0], kp_ref[0], kc_ref[0]
    vm, vp, vc = vm_ref[0], vp_ref[0], vc_ref[0]
    for g in range(grp):
        qg = q_ref[0, :, g * LANES:(g + 1) * LANES]
        out_g = None
        for hk in range(SWA_KV_HEADS):
            head = hk * grp + g
            slope = 2.0 ** (-8.0 * (head + 1) / SWA_HEADS)
            sink = sink_ref[head]
            qh = qg * half_masks[hk]
            s_m = jnp.where(vis_m, _dot_nt(qh, km) - slope * d_m, NEG)
            s_p = jnp.where(vis_p, _dot_nt(qh, kp) - slope * d_p, NEG)
            s_c = jnp.where(vis_c, _dot_nt(qh, kc) - slope * d_c, NEG)
            m = jnp.maximum(jnp.maximum(jnp.max(s_m, axis=-1, keepdims=True),
                                        jnp.max(s_p, axis=-1, keepdims=True)),
                            jnp.maximum(jnp.max(s_c, axis=-1, keepdims=True), sink))
            e_m, e_p, e_c = jnp.exp(s_m - m), jnp.exp(s_p - m), jnp.exp(s_c - m)
            den = (jnp.sum(e_m, axis=-1, keepdims=True) + jnp.sum(e_p, axis=-1, keepdims=True)
                   + jnp.sum(e_c, axis=-1, keepdims=True) + jnp.exp(sink - m))
            inv = 1.0 / den
            o = (_dot((e_m * inv).astype(BF16), vm) + _dot((e_p * inv).astype(BF16), vp)
                 + _dot((e_c * inv).astype(BF16), vc))
            out_g = o if hk == 0 else jnp.where(lo_half, out_g, o)
        o_ref[0, :, g * LANES:(g + 1) * LANES] = out_g.astype(BF16)


def _swa_attention(qa, sinks, batch, seq_len):
    x = qa.reshape(batch, seq_len, 512)
    nb = seq_len // BLOCK
    blk = lambda f: pl.BlockSpec((1, BLOCK, LANES), f)
    return pl.pallas_call(
        _swa_kernel,
        grid=(batch, nb),
        in_specs=[
            pl.BlockSpec(memory_space=pltpu.SMEM),
            pl.BlockSpec((1, BLOCK, 2 * LANES), lambda b, i: (b, i, 0)),
            blk(lambda b, i: (b, 0, 2)),
            blk(lambda b, i: (b, jnp.maximum(i - 1, 0), 2)),
            blk(lambda b, i: (b, i, 2)),
            blk(lambda b, i: (b, 0, 3)),
            blk(lambda b, i: (b, jnp.maximum(i - 1, 0), 3)),
            blk(lambda b, i: (b, i, 3)),
        ],
        out_specs=pl.BlockSpec((1, BLOCK, 2 * LANES), lambda b, i: (b, i, 0)),
        out_shape=jax.ShapeDtypeStruct((batch, seq_len, 2 * LANES), BF16),
        compiler_params=pltpu.CompilerParams(dimension_semantics=("arbitrary", "arbitrary"),
                                             vmem_limit_bytes=VMEM_LIMIT),
        name="swa_attention",
    )(sinks, x, x, x, x, x, x, x)


def _causal_kernel(mode, q_ref, k_ref, v_ref, o_ref, stat_ref, acc_ref):
    seq_len = q_ref.shape[1]
    n_qt = (seq_len - BLOCK) // Q_TILE
    per_tile = Q_TILE // K_TILE
    lane = lax.broadcasted_iota(jnp.int32, (1, LANES), 1)
    lo_half = lane < HEAD_DIM
    if mode == "sb":
        r = lax.broadcasted_iota(jnp.int32, (2 * K_TILE, K_TILE), 0) & (K_TILE - 1)
        c = lax.broadcasted_iota(jnp.int32, (2 * K_TILE, K_TILE), 1)
        later2 = jnp.where(r > c, 1.0, 0.0).astype(BF16)
        r1 = lax.broadcasted_iota(jnp.int32, (2 * BLOCK, BLOCK), 0) & (BLOCK - 1)
        c1 = lax.broadcasted_iota(jnp.int32, (2 * BLOCK, BLOCK), 1)
        later1 = jnp.where(r1 > c1, 1.0, 0.0).astype(BF16)

    def causal(pq, k0, tk):
        pk = k0 + lax.broadcasted_iota(jnp.int32, (1, tk), 1)
        if mode == "fox":
            return pk <= pq
        if mode == "mla":
            return (pk >> CHUNK_SHIFT) <= (pq >> CHUNK_SHIFT)
        return pk < pq

    def head_v(k0, tk, hh):
        if mode == "sb":
            return v_ref[0, pl.ds(k0, tk), :]
        return v_ref[0, pl.ds(k0, tk), hh * LANES:(hh + 1) * LANES]

    def lane_tiles(x):
        return [x[:, j * LANES:(j + 1) * LANES] for j in range(x.shape[1] // LANES)]

    def row_parts(tq):
        step = min(tq, ROW_PART)
        return [(r0, step) for r0 in range(0, tq, step)]

    def softmax_chunk(q0, tq, k0, tk, masked, first):
        for hh in range(2):
            qh = q_ref[0, pl.ds(q0, tq), hh * LANES:(hh + 1) * LANES]
            kh = k_ref[0, pl.ds(k0, tk), hh * LANES:(hh + 1) * LANES]
            s_all = _dot_nt(qh, kh)
            vh = head_v(k0, tk, hh)
            for r0, tr in row_parts(tq):
                s = s_all[r0:r0 + tr]
                if masked:
                    pq = q0 + r0 + lax.broadcasted_iota(jnp.int32, (tr, 1), 0)
                    s = jnp.where(causal(pq, k0, tk), s, NEG)
                tiles = lane_tiles(s)
                top = tiles[0]
                for x in tiles[1:]:
                    top = jnp.maximum(top, x)
                m_new = jnp.broadcast_to(jnp.max(top, axis=-1, keepdims=True), (tr, LANES))
                if not first:
                    m_old = stat_ref[hh, r0:r0 + tr, :]
                    m_new = jnp.maximum(m_old, m_new)
                p = jnp.concatenate([jnp.exp2(x - m_new) for x in tiles], axis=1).astype(BF16)
                pv = _dot(p, vh)
                if not first:
                    pv = jnp.exp2(m_old - m_new) * acc_ref[hh, r0:r0 + tr, :] + pv
                stat_ref[hh, r0:r0 + tr, :] = m_new
                acc_ref[hh, r0:r0 + tr, :] = pv

    def stick_chunk(q0, tq, k0, tk, masked, first):
        later = later2 if tk == K_TILE else later1
        for hh in range(2):
            qh = q_ref[0, pl.ds(q0, tq), hh * LANES:(hh + 1) * LANES]
            kh = k_ref[0, pl.ds(k0, tk), hh * LANES:(hh + 1) * LANES]
            z = _dot_nt(qh, kh)
            ls_pos = jnp.minimum(z, 0.0) - jnp.log(1.0 + jnp.exp2(-jnp.abs(z))) * LOG2E
            log_keep = ls_pos - z
            if masked:
                pq = q0 + lax.broadcasted_iota(jnp.int32, (tq, 1), 0)
                vis = causal(pq, k0, tk)
                log_keep = jnp.where(vis, log_keep, 0.0)
            hi = log_keep.astype(BF16)
            lo = (log_keep - hi.astype(F32)).astype(BF16)
            after = _dot(jnp.concatenate([hi, lo], axis=1), later)
            tot = ls_pos + after
            chunk_total = jnp.broadcast_to(after[:, 0:1] + log_keep[:, 0:1], (tq, LANES))
            if not first:
                carry = stat_ref[hh, 0:tq, :]
                tot = jnp.concatenate([x + carry for x in lane_tiles(tot)], axis=1)
                chunk_total = carry + chunk_total
            a = jnp.exp2(tot)
            if masked:
                a = jnp.where(vis, a, 0.0)
            pv = _dot(a.astype(BF16), head_v(k0, tk, hh))
            if not first:
                pv = acc_ref[hh, 0:tq, :] + pv
            stat_ref[hh, 0:tq, :] = chunk_total
            acc_ref[hh, 0:tq, :] = pv

    def finish(q0, tq):
        a0, a1 = acc_ref[0, 0:tq, :], acc_ref[1, 0:tq, :]
        if mode != "sb":
            a0 = a0 / a0[:, HEAD_DIM:HEAD_DIM + 1]
            a1 = a1 / a1[:, 0:1]
        o_ref[0, pl.ds(q0, tq), :] = jnp.where(lo_half, a0, a1).astype(BF16)

    def chunk_start(j):
        return pl.multiple_of(BLOCK + j * K_TILE, BLOCK)

    if mode == "sb":
        stick_chunk(0, BLOCK, 0, BLOCK, True, True)
        finish(0, BLOCK)

        def q_body(i, _):
            q0 = pl.multiple_of(BLOCK + i * Q_TILE, BLOCK)
            n_int = i * per_tile
            for d in range(per_tile):
                stick_chunk(q0, Q_TILE, chunk_start(n_int + per_tile - 1 - d), K_TILE, True, d == 0)

            def alive():
                top = jnp.maximum(jnp.max(stat_ref[0]), jnp.max(stat_ref[1]))
                return (top > UNDERFLOW_LOG2).astype(jnp.int32)

            def body(st):
                jj, _ = st
                for d in range(per_tile):
                    stick_chunk(q0, Q_TILE, chunk_start(n_int - 1 - jj * per_tile - d), K_TILE, False, False)
                return jj + 1, alive()

            _, go = lax.while_loop(lambda st: (st[0] < i) & (st[1] > 0), body, (0, alive()))

            @pl.when(go > 0)
            def _():
                stick_chunk(q0, Q_TILE, 0, BLOCK, False, False)

            finish(q0, Q_TILE)
            return 0
    else:
        softmax_chunk(0, BLOCK, 0, BLOCK, True, True)
        finish(0, BLOCK)

        def q_body(i, _):
            q0 = pl.multiple_of(BLOCK + i * Q_TILE, BLOCK)
            n_int = i * per_tile
            softmax_chunk(q0, Q_TILE, 0, BLOCK, False, True)

            def body(j, _):
                for d in range(per_tile):
                    softmax_chunk(q0, Q_TILE, chunk_start(j * per_tile + d), K_TILE, False, False)
                return 0

            lax.fori_loop(0, i, body, 0)
            for d in range(per_tile):
                softmax_chunk(q0, Q_TILE, chunk_start(n_int + d), K_TILE, True, False)
            finish(q0, Q_TILE)
            return 0

    lax.fori_loop(0, n_qt, q_body, 0)


def _causal_attention(mode, q, k, v, batch, seq_len):
    wide = pl.BlockSpec((1, seq_len, 2 * LANES), lambda b, p: (b, 0, p))
    narrow = pl.BlockSpec((1, seq_len, LANES), lambda b, p: (b, 0, p))
    args = [q.reshape(batch, seq_len, 512), k.reshape(batch, seq_len, 512),
            v.reshape(batch, seq_len, v.shape[1])]
    return pl.pallas_call(
        functools.partial(_causal_kernel, mode),
        grid=(batch, 2),
        in_specs=[wide, wide, narrow if mode == "sb" else wide],
        out_specs=narrow,
        out_shape=jax.ShapeDtypeStruct((batch, seq_len, 2 * LANES), BF16),
        scratch_shapes=[pltpu.VMEM((2, Q_TILE, LANES), F32), pltpu.VMEM((2, Q_TILE, LANES), F32)],
        compiler_params=pltpu.CompilerParams(dimension_semantics=("arbitrary", "arbitrary"),
                                             vmem_limit_bytes=VMEM_LIMIT),
        name=mode + "_attention",
    )(*args)


def _outproj_kernel(ya_ref, yb_ref, yc_ref, yd_ref, h_ref, wo_ref, g_ref, wrh_ref, wrl_ref, br_ref,
                    h2_ref, xn_ref, lg_ref):
    o = (_dot(ya_ref[...], wo_ref[0]) + _dot(yb_ref[...], wo_ref[1])
         + _dot(yc_ref[...], wo_ref[2]) + _dot(yd_ref[...], wo_ref[3]))
    h2 = h_ref[...] + o
    h2_ref[...] = h2
    xn = _rms(h2, g_ref[...])
    xn_ref[...] = xn
    xh = xn.astype(BF16)
    xl = (xn - xh.astype(F32)).astype(BF16)
    wrh, wrl = wrh_ref[...], wrl_ref[...]
    lg_ref[...] = _dot(xh, wrh) + _dot(xl, wrh) + _dot(xh, wrl) + br_ref[...]


def _outproj(ys, h, p):
    t, d = h.shape
    tm = ROW_TILE
    row = lambda i: (i, 0)
    fixed2 = lambda i: (0, 0)
    in_specs = [pl.BlockSpec((tm, 256), row)] * 4 + [
        pl.BlockSpec((tm, d), row),
        pl.BlockSpec((4, 256, d), lambda i: (0, 0, 0)),
        pl.BlockSpec((1, d), fixed2),
        pl.BlockSpec((d, LANES), fixed2),
        pl.BlockSpec((d, LANES), fixed2),
        pl.BlockSpec((1, LANES), fixed2),
    ]
    return pl.pallas_call(
        _outproj_kernel,
        grid=(t // tm,),
        in_specs=in_specs,
        out_specs=[pl.BlockSpec((tm, d), row), pl.BlockSpec((tm, d), row), pl.BlockSpec((tm, LANES), row)],
        out_shape=[jax.ShapeDtypeStruct((t, d), F32), jax.ShapeDtypeStruct((t, d), F32),
                   jax.ShapeDtypeStruct((t, LANES), F32)],
        compiler_params=pltpu.CompilerParams(dimension_semantics=("arbitrary",),
                                             vmem_limit_bytes=VMEM_LIMIT),
        name="outproj_router",
    )(*ys, h, p["w_out"], p["ffn_norm"], p["w_r_hi"], p["w_r_lo"], p["b_r"])


def _moe_kernel(be_ref, nv_ref, asg_ref, x_hbm, wg_ref, wu_ref, wd_ref, y_hbm,
                xbuf, ybuf, sem_in, sem_out):
    i = pl.program_id(0)
    nv = nv_ref[i]

    @pl.when(i == 0)
    def _():
        xbuf[...] = jnp.zeros_like(xbuf)

    def in_copy(r):
        tok = asg_ref[0, 0, r] >> 1
        return pltpu.make_async_copy(x_hbm.at[pl.ds(tok, 1)], xbuf.at[pl.ds(r, 1)], sem_in)

    def out_copy(r):
        return pltpu.make_async_copy(ybuf.at[pl.ds(r, 1)], y_hbm.at[pl.ds(asg_ref[0, 0, r], 1)], sem_out)

    @pl.when(nv > 0)
    def _():
        def start_in(r, _):
            in_copy(r).start()
            return 0
        lax.fori_loop(0, nv, start_in, 0)

        def wait_in(r, _):
            in_copy(r).wait()
            return 0
        lax.fori_loop(0, nv, wait_in, 0)

        x = xbuf[...].astype(BF16)
        gate = _dot(x, wg_ref[0])
        up = _dot(x, wu_ref[0])
        hid = (gate * (1.0 / (1.0 + jnp.exp(-gate))) * up).astype(BF16)
        ybuf[...] = _dot(hid, wd_ref[0])

        def start_out(r, _):
            out_copy(r).start()
            return 0
        lax.fori_loop(0, nv, start_out, 0)

        def wait_out(r, _):
            out_copy(r).wait()
            return 0
        lax.fori_loop(0, nv, wait_out, 0)


def _moe(xn, rows_asg, block_e, nvalid, p):
    t, d = xn.shape
    n_blk = block_e.shape[0]
    hdim = p["w_gate"].shape[2]
    wspec = lambda shape: pl.BlockSpec((1,) + shape, lambda i, be, nv: (be[i], 0, 0))
    return pl.pallas_call(
        _moe_kernel,
        grid_spec=pltpu.PrefetchScalarGridSpec(
            num_scalar_prefetch=2,
            grid=(n_blk,),
            in_specs=[
                pl.BlockSpec((1, 1, MOE_BLOCK), lambda i, be, nv: (i, 0, 0), memory_space=pltpu.SMEM),
                pl.BlockSpec(memory_space=pl.ANY),
                wspec((d, hdim)), wspec((d, hdim)), wspec((hdim, d)),
            ],
            out_specs=pl.BlockSpec(memory_space=pl.ANY),
            scratch_shapes=[pltpu.VMEM((MOE_BLOCK, d), F32), pltpu.VMEM((MOE_BLOCK, d), F32),
                            pltpu.SemaphoreType.DMA, pltpu.SemaphoreType.DMA],
        ),
        out_shape=jax.ShapeDtypeStruct((t * TOP_K, d), F32),
        compiler_params=pltpu.CompilerParams(dimension_semantics=("arbitrary",),
                                             vmem_limit_bytes=VMEM_LIMIT),
        name="moe_experts",
    )(block_e, nvalid, rows_asg.reshape(n_blk, 1, MOE_BLOCK), xn,
      p["w_gate"], p["w_up"], p["w_down"])


def _route(logits, t):
    g_logits = logits[:, :N_GROUPS]
    e_logits = logits[:, N_GROUPS:N_GROUPS + N_EXPERTS].reshape(t, N_GROUPS, EXPERTS_PER_GROUP)
    g_prob = jax.nn.softmax(g_logits, axis=-1)
    g_top = jnp.argmax(g_logits, axis=-1).astype(jnp.int32)
    g_w = jnp.take_along_axis(g_prob, g_top[:, None], axis=-1)
    e_in_group = jnp.take_along_axis(e_logits, g_top[:, None, None], axis=1)[:, 0]
    top_v, top_i = lax.top_k(e_in_group, TOP_K)
    e_w = jax.nn.softmax(top_v, axis=-1) * g_w
    expert = g_top[:, None] * EXPERTS_PER_GROUP + top_i.astype(jnp.int32)
    a = t * TOP_K
    flat_e = expert.reshape(a)
    order = jnp.argsort(flat_e).astype(jnp.int32)
    se = flat_e[order]
    counts = jnp.bincount(flat_e, length=N_EXPERTS).astype(jnp.int32)
    padded = (counts + MOE_BLOCK - 1) // MOE_BLOCK * MOE_BLOCK
    pad_end = jnp.cumsum(padded)
    pad_start = pad_end - padded
    raw_start = jnp.cumsum(counts) - counts
    dest = pad_start[se] + (jnp.arange(a, dtype=jnp.int32) - raw_start[se])
    n_rows = a + N_EXPERTS * MOE_BLOCK
    n_blk = n_rows // MOE_BLOCK
    rows_asg = jnp.full((n_rows,), -1, jnp.int32).at[dest].set(order)
    block_e = jnp.minimum(jnp.searchsorted(pad_end, jnp.arange(n_blk, dtype=jnp.int32) * MOE_BLOCK, side="right"),
                          N_EXPERTS - 1).astype(jnp.int32)
    nvalid = jnp.sum((rows_asg >= 0).reshape(n_blk, MOE_BLOCK), axis=1).astype(jnp.int32)
    ew = jnp.pad(e_w, ((0, 0), (0, LANES - TOP_K)))
    return rows_asg, ew, block_e, nvalid


def _final_kernel(h_ref, y2_ref, ew_ref, g_ref, o_ref):
    d = h_ref.shape[2]
    ew = ew_ref[0]
    h = h_ref[0] + ew[:, 0:1] * y2_ref[0, :, :d] + ew[:, 1:2] * y2_ref[0, :, d:]
    o_ref[0] = _rms(h, g_ref[...])


def _final(h, y2, ew, g, batch, seq_len):
    d = h.shape[1]
    nb = seq_len // BLOCK - 1
    return pl.pallas_call(
        _final_kernel,
        grid=(batch, nb),
        in_specs=[pl.BlockSpec((1, BLOCK, d), lambda b, i: (b, i + 1, 0)),
                  pl.BlockSpec((1, BLOCK, 2 * d), lambda b, i: (b, i + 1, 0)),
                  pl.BlockSpec((1, BLOCK, LANES), lambda b, i: (b, i + 1, 0)),
                  pl.BlockSpec((1, d), lambda b, i: (0, 0))],
        out_specs=pl.BlockSpec((1, BLOCK, d), lambda b, i: (b, i, 0)),
        out_shape=jax.ShapeDtypeStruct((batch, seq_len - BLOCK, d), F32),
        compiler_params=pltpu.CompilerParams(dimension_semantics=("arbitrary", "arbitrary"),
                                             vmem_limit_bytes=VMEM_LIMIT),
        name="final_norm",
    )(h.reshape(batch, seq_len, d), y2.reshape(batch, seq_len, 2 * d), ew.reshape(batch, seq_len, LANES), g)


def _rope_table(seq_len):
    half = MLA_ROPE // 2
    pos = (jnp.arange(seq_len, dtype=jnp.int32) - PAD_FRONT).astype(F32)
    inv_freq = ROPE_THETA ** (-jnp.arange(half, dtype=F32) / half)
    ang = pos[:, None] * inv_freq[None, :]
    cos, sin = jnp.cos(ang), jnp.sin(ang)
    cos2 = jnp.concatenate([cos, cos], axis=1)
    sin2 = jnp.concatenate([-sin, sin], axis=1)
    z = lambda w: jnp.zeros((seq_len, w), F32)
    scale = (MLA_NOPE + MLA_ROPE) ** -0.5 * LOG2E
    cos_q = jnp.concatenate([jnp.ones((seq_len, MLA_NOPE), F32), cos2, z(32)], axis=1) * scale
    sin_q = jnp.concatenate([z(MLA_NOPE), sin2, z(32)], axis=1) * scale
    cos_k = jnp.concatenate([z(MLA_NOPE), cos2, z(32)], axis=1)
    sin_k = jnp.concatenate([z(MLA_NOPE), sin2, z(32)], axis=1)
    return jnp.concatenate([cos_q, sin_q, cos_k, sin_k], axis=1)


def _swap_halves(w):
    half = w.shape[-1] // 2
    return jnp.concatenate([w[..., half:], w[..., :half]], axis=-1)


def _layer_params(i, seq_len, attn_norm, w_in, b_forget, sinks, mla_q_norm, mla_kv_norm, mla_w_uq,
                  mla_w_ukv, w_out, ffn_norm, w_group, b_group, w_router, b_router, w_gate, w_up, w_down):
    d = w_in.shape[1]
    w = w_in[i]
    sizes = (256, 128, 128, 256, 256, 256, 4, 256, 128, 32, 256, 256, 256)
    offs = np.concatenate([[0], np.cumsum(sizes)])
    (a_q, a_k, a_v, f_q, f_k, f_v, f_g, c_q, c_kv, c_kr, s_q, s_k, s_v) = [
        w[:, offs[j]:offs[j + 1]] for j in range(len(sizes))]
    qscale = HEAD_DIM ** -0.5
    grp = SWA_HEADS // SWA_KV_HEADS
    a_q = a_q.reshape(d, SWA_KV_HEADS, grp, HEAD_DIM).transpose(0, 2, 1, 3).reshape(d, 256)
    z = lambda n: jnp.zeros((d, n), F32)
    g_grp = jnp.concatenate([f_g[:, 0:2], z(62), c_kr, z(32)], axis=1)
    gs_grp = jnp.concatenate([f_g[:, 2:4], z(62), _swap_halves(c_kr), z(32)], axis=1)
    w_perm = jnp.concatenate([a_q * qscale, a_k, a_v, f_q * qscale, f_k, f_v, s_q * qscale, s_k, s_v,
                              c_q, c_kv, g_grp, gs_grp], axis=1).astype(BF16)
    wuq = mla_w_uq[i].reshape(MLA_Q_LORA, 4, MLA_NOPE + MLA_ROPE)
    zq = lambda n: jnp.zeros((MLA_Q_LORA, 4, n), F32)
    w_uq_a = jnp.concatenate([wuq, zq(32)], axis=2).reshape(MLA_Q_LORA, 512).astype(BF16)
    w_uq_b = jnp.concatenate([zq(MLA_NOPE), _swap_halves(wuq[:, :, MLA_NOPE:]), zq(32)],
                             axis=2).reshape(MLA_Q_LORA, 512).astype(BF16)
    wukv = mla_w_ukv[i].reshape(MLA_KV_LORA, 4, MLA_NOPE + MLA_V)
    w_kv_k = jnp.concatenate([wukv[:, :, :MLA_NOPE], jnp.zeros((MLA_KV_LORA, 4, 64), F32)],
                             axis=2).reshape(MLA_KV_LORA, 512).astype(BF16)
    w_kv_v = wukv[:, :, MLA_NOPE:].reshape(MLA_KV_LORA, 256).astype(BF16)
    bf = b_forget[i].astype(F32)
    b_f = jnp.zeros((1, 256), F32).at[0, 0:2].set(bf[0:2]).at[0, 128:130].set(bf[2:4])
    wo = w_out[i]
    wo_a = wo[:256].reshape(SWA_KV_HEADS, grp, HEAD_DIM, d).transpose(1, 0, 2, 3).reshape(256, d)
    wo4 = jnp.concatenate([wo_a, wo[256:]], axis=0).reshape(4, 256, d).astype(BF16)
    w_r = jnp.concatenate([w_group[i], w_router[i], jnp.zeros((d, LANES - N_GROUPS - N_EXPERTS), F32)], axis=1)
    w_r_hi = w_r.astype(BF16)
    w_r_lo = (w_r - w_r_hi.astype(F32)).astype(BF16)
    b_r = jnp.concatenate([b_group[i], b_router[i], jnp.zeros((LANES - N_GROUPS - N_EXPERTS,), F32)])[None, :]
    return dict(
        attn_norm=attn_norm[i][None, :], w_in=w_perm, q_norm=mla_q_norm[i][None, :],
        kv_norm=mla_kv_norm[i][None, :], w_uq_a=w_uq_a, w_uq_b=w_uq_b, w_kv_k=w_kv_k, w_kv_v=w_kv_v,
        rope_tab=_rope_table(seq_len), b_forget=b_f, sinks=sinks[i].astype(F32), w_out=wo4,
        ffn_norm=ffn_norm[i][None, :], w_r_hi=w_r_hi, w_r_lo=w_r_lo, b_r=b_r.astype(F32),
        w_gate=w_gate[i].astype(BF16), w_up=w_up[i].astype(BF16), w_down=w_down[i].astype(BF16))


def kernel(x, meta_tokens, attn_norm, w_in, b_forget, sinks, mla_q_norm, mla_kv_norm, mla_w_uq, mla_w_ukv,
           w_out, ffn_norm, w_group, b_group, w_router, b_router, w_gate, w_up, w_down, final_norm):
    batch, seq, d = x.shape
    seq_len = seq + BLOCK
    assert seq_len % ROW_TILE == 0 and seq % Q_TILE == 0
    t = batch * seq_len
    depth = w_in.shape[0]
    pad = jnp.zeros((batch, PAD_FRONT, d), x.dtype)
    meta = jnp.broadcast_to(meta_tokens.astype(x.dtype)[None], (batch, N_META, d))
    h = jnp.concatenate([pad, meta, x], axis=1).reshape(t, d)
    y2 = ew = None
    for i in range(depth):
        p = _layer_params(i, seq_len, attn_norm, w_in, b_forget, sinks, mla_q_norm, mla_kv_norm, mla_w_uq,
                          mla_w_ukv, w_out, ffn_norm, w_group, b_group, w_router, b_router, w_gate, w_up, w_down)
        h, (qa, fq, fk, fv, cq, ck, cv, sq, sk, sv) = _inproj(h, y2, ew, p, seq_len)
        y_a = _swa_attention(qa, p["sinks"], batch, seq_len)
        y_b = _causal_attention("fox", fq, fk, fv, batch, seq_len)
        y_c = _causal_attention("mla", cq, ck, cv, batch, seq_len)
        y_d = _causal_attention("sb", sq, sk, sv, batch, seq_len)
        ys = [y.reshape(t, 256) for y in (y_a, y_b, y_c, y_d)]
        h, xn, logits = _outproj(ys, h, p)
        rows_asg, ew, block_e, nvalid = _route(logits, t)
        y2 = _moe(xn, rows_asg, block_e, nvalid, p).reshape(t, TOP_K * d)
    return _final(h, y2, ew, final_norm[None, :], batch, seq_len)
```

```python
import functools

import jax
import jax.numpy as jnp
import numpy as np
from jax import lax
from jax.experimental import pallas as pl
from jax.experimental.pallas import tpu as pltpu

F32 = jnp.float32
BF16 = jnp.bfloat16

BLOCK = 128
N_META = 16
PAD_FRONT = BLOCK - N_META
CHUNK_SHIFT = 6
HEAD_DIM = 64
NORM_EPS = 1e-6
NEG = -1e30
PAD_KEY_LOGIT = -(2.0 ** 100)
UNDERFLOW_LOG2 = -150.0
LOG2E = 1.4426950408889634
BIG = 1 << 30
SWA_HEADS, SWA_KV_HEADS, WINDOW = 4, 2, 128
MLA_Q_LORA, MLA_KV_LORA, MLA_NOPE, MLA_ROPE, MLA_V = 256, 128, 64, 32, 64
MLA_BIAS_LANE = MLA_NOPE + MLA_ROPE
ROPE_THETA = 10000.0
N_GROUPS, EXPERTS_PER_GROUP, TOP_K = 4, 8, 2
N_EXPERTS = N_GROUPS * EXPERTS_PER_GROUP
MOE_BLOCK = 256
MOE_TRASH_ROWS = 8 * MOE_BLOCK
LANES = 128
ROW_SUB = 8
ROW_TILE = 384
Q_TILE = 512
K_TILE = 256
ROW_PART = 256
VMEM_LIMIT = 56 * 1024 * 1024

C_A, C_B, C_D, C_CQ, C_CKV, C_G, C_GS, C_END = 0, 512, 1280, 2048, 2304, 2432, 2560, 2688
B_F0, B_F1, B_PAD = 0, 3, 6


def _rms(x, g):
    return x * lax.rsqrt(jnp.mean(x * x, axis=-1, keepdims=True) + NORM_EPS) * g


def _log_sigmoid(x):
    return jnp.minimum(x, 0.0) - jnp.log(1.0 + jnp.exp(-jnp.abs(x)))


def _dot(a, b):
    return jnp.dot(a, b, preferred_element_type=F32)


def _dot_nt(a, b):
    return lax.dot_general(a, b, (((1,), (1,)), ((), ())), preferred_element_type=F32)


def _rows_from_tiles(ref, lead, n):
    return jnp.concatenate([ref[(*lead, pl.ds(j, n, stride=ROW_SUB), slice(None))] for j in range(ROW_SUB)], axis=1)


def _rows_to_tiles(ref, lead, x):
    n = x.shape[0]
    for j in range(ROW_SUB):
        ref[(*lead, pl.ds(j, n, stride=ROW_SUB), slice(None))] = x[:, j * LANES:(j + 1) * LANES]


def _tile4(x):
    return jnp.concatenate([x, x, x, x], axis=1)


def _split3(x):
    hi = x.astype(BF16)
    r1 = x - hi.astype(F32)
    mid = r1.astype(BF16)
    lo = (r1 - mid.astype(F32)).astype(BF16)
    return hi, mid, lo


def _free_base(head):
    return head * LANES + (HEAD_DIM if head % 2 == 0 else 0)


def _inproj_kernel(has_y2, n_seq_tiles, *refs):
    if has_y2:
        (h_ref, y0_ref, y1_ref, ew_ref, *rest) = refs
    else:
        (h_ref, *rest) = refs
    (g_ref, w_ref, qn_ref, kvn_ref, wuqa_ref, wuqb_ref, wkvk_ref, wkvv_ref, tab_ref, bf_ref,
     pq_ref, pk_ref, rows_ref, *outs) = rest
    if has_y2:
        hout_ref, *outs = outs
    (qa_ref, fq_ref, fk_ref, fv_ref, cq_ref, ck_ref, cv_ref, sq_ref, sk_ref, sv_ref, carry_ref) = outs
    tile = pl.program_id(0) % n_seq_tiles
    h = h_ref[...]
    tm, d = h.shape
    if has_y2:
        ew = ew_ref[...]
        h = h + ew[:, 0:1] * _rows_from_tiles(y0_ref, (), tm) + ew[:, 1:2] * _rows_from_tiles(y1_ref, (), tm)
        hout_ref[...] = h
    xn = _rms(h, g_ref[...]).astype(BF16)
    acc = _dot(xn, w_ref[...])
    lane = lax.broadcasted_iota(jnp.int32, (1, LANES), 1)
    lo_half = lane < HEAD_DIM
    pad_col = jnp.where(tile * tm + lax.broadcasted_iota(jnp.int32, (tm, 1), 0) < PAD_FRONT,
                        PAD_KEY_LOGIT, 0.0)
    rows = rows_ref[...]
    fq_one, fk_one, pad_lane, sq_one, mla_one, mla_pad = (rows[j:j + 1] for j in range(6))

    def per_head(x_pair, bias, pair, scale=None):
        x = x_pair if scale is None else x_pair * scale
        even = jnp.where(lo_half, x, bias[:, (2 * pair) * LANES:(2 * pair + 1) * LANES])
        odd = jnp.where(lo_half, bias[:, (2 * pair + 1) * LANES:(2 * pair + 2) * LANES], x)
        return even, odd

    def store_heads(ref, x_off, bias, scale=None):
        for pair in range(2):
            x_pair = acc[:, x_off + pair * LANES:x_off + (pair + 1) * LANES]
            even, odd = per_head(x_pair, bias, pair, scale)
            ref[:, (2 * pair) * LANES:(2 * pair + 1) * LANES] = even.astype(BF16)
            ref[:, (2 * pair + 1) * LANES:(2 * pair + 2) * LANES] = odd.astype(BF16)

    qa_ref[...] = acc[:, C_A:C_B].astype(BF16)

    @pl.when(tile == 0)
    def _():
        carry_ref[...] = jnp.zeros_like(carry_ref)

    lf = _log_sigmoid(acc[:, C_G:C_END] + bf_ref[...]) * LOG2E
    r = lax.broadcasted_iota(jnp.int32, (BLOCK, BLOCK), 0)
    c = lax.broadcasted_iota(jnp.int32, (BLOCK, BLOCK), 1)
    tri = jnp.where(c <= r, 1.0, 0.0).astype(BF16)
    carry = carry_ref[...]
    blocks = []
    for b in range(tm // BLOCK):
        hi, mid, lo = _split3(lf[b * BLOCK:(b + 1) * BLOCK])
        y = _dot(tri, hi) + _dot(tri, mid) + _dot(tri, lo) + carry
        carry = y[BLOCK - 1:BLOCK, :]
        blocks.append(y)
    carry_ref[...] = carry
    f_hi, f_mid, f_lo = _split3(jnp.concatenate(blocks, axis=0))
    q_bias = _dot(f_hi, pq_ref[0]) + _dot(f_mid, pq_ref[1]) + _dot(f_lo, pq_ref[2]) + fq_one
    k_bias = (_dot(f_hi, pk_ref[0]) + _dot(f_mid, pk_ref[1]) + _dot(f_lo, pk_ref[2]) + fk_one
              + pad_col * pad_lane)
    ones = jnp.ones((tm, 4 * LANES), F32)
    store_heads(fq_ref, C_B, q_bias, LOG2E)
    store_heads(fk_ref, C_B + 256, k_bias)
    store_heads(fv_ref, C_B + 512, ones)

    store_heads(sq_ref, C_D, jnp.broadcast_to(sq_one, (tm, 4 * LANES)), LOG2E)
    store_heads(sk_ref, C_D + 256, pad_col * pad_lane)
    sv_ref[...] = acc[:, C_D + 512:C_CQ].astype(BF16)

    tab = tab_ref[...]
    cos_q, sin_q = tab[:, 0:128], tab[:, 128:256]
    cos_k, sin_k = tab[:, 256:384], tab[:, 384:512]
    cqn = _rms(acc[:, C_CQ:C_CKV], qn_ref[...]).astype(BF16)
    q_lin = _dot(cqn, wuqa_ref[...])
    q_swp = _dot(cqn, wuqb_ref[...])
    cq_ref[...] = (q_lin * _tile4(cos_q) + q_swp * _tile4(sin_q) + mla_one).astype(BF16)
    ckvn = _rms(acc[:, C_CKV:C_G], kvn_ref[...]).astype(BF16)
    k_nope = _dot(ckvn, wkvk_ref[...])
    grp, grp_s = acc[:, C_G:C_GS], acc[:, C_GS:C_END]
    k_rope = grp * cos_k + grp_s * sin_k
    ck_ref[...] = (k_nope + _tile4(k_rope) + pad_col * mla_pad).astype(BF16)
    vv = _dot(ckvn, wkvv_ref[...])
    for pair in range(2):
        even, odd = per_head(vv[:, pair * LANES:(pair + 1) * LANES], ones, pair)
        cv_ref[:, (2 * pair) * LANES:(2 * pair + 1) * LANES] = even.astype(BF16)
        cv_ref[:, (2 * pair + 1) * LANES:(2 * pair + 2) * LANES] = odd.astype(BF16)


def _bias_constants():
    src = (0, 1, LANES, LANES + 1)
    pq = np.zeros((3, 2 * LANES, 4 * LANES), np.float32)
    pk = np.zeros((3, 2 * LANES, 4 * LANES), np.float32)
    rows = np.zeros((8, 4 * LANES), np.float32)
    for head in range(4):
        base = _free_base(head)
        for part in range(3):
            pq[part, src[head], base + B_F0 + part] = 1.0
            pk[part, src[head], base + B_F1 + part] = -1.0
            rows[0, base + B_F1 + part] = 1.0
            rows[1, base + B_F0 + part] = 1.0
        rows[0, base + B_PAD] = 1.0
        rows[2, base + B_PAD] = 1.0
        rows[3, base + B_PAD] = 1.0
        rows[4, head * LANES + MLA_BIAS_LANE] = 1.0
        rows[5, head * LANES + MLA_BIAS_LANE] = 1.0
    return jnp.asarray(pq, BF16), jnp.asarray(pk, BF16), jnp.asarray(rows, F32)


def _inproj(h, y2, ew, p, seq_len):
    t, d = h.shape
    tm = ROW_TILE
    n_seq_tiles = seq_len // tm
    has_y2 = y2 is not None
    row = lambda i: (i, 0)
    fixed = lambda i: (0, 0)
    in_specs = [pl.BlockSpec((tm, d), row)]
    args = [h]
    if has_y2:
        in_specs += [pl.BlockSpec((tm * ROW_SUB, LANES), row),
                     pl.BlockSpec((tm * ROW_SUB, LANES), lambda i: (i + t // tm, 0)),
                     pl.BlockSpec((tm, LANES), row)]
        args += [y2, y2, ew]
    pq, pk, rows = _bias_constants()
    consts = [p["attn_norm"], p["w_in"], p["q_norm"], p["kv_norm"], p["w_uq_a"], p["w_uq_b"],
              p["w_kv_k"], p["w_kv_v"]]
    in_specs += [pl.BlockSpec(c.shape, fixed) for c in consts]
    args += consts
    in_specs.append(pl.BlockSpec((tm, 512), lambda i: (i % n_seq_tiles, 0)))
    args.append(p["rope_tab"])
    in_specs += [pl.BlockSpec((1, 256), fixed),
                 pl.BlockSpec(pq.shape, lambda i: (0, 0, 0)),
                 pl.BlockSpec(pk.shape, lambda i: (0, 0, 0)),
                 pl.BlockSpec(rows.shape, fixed)]
    args += [p["b_forget"], pq, pk, rows]
    widths = [512] * 9 + [256]
    out_shape = [jax.ShapeDtypeStruct((t, w), BF16) for w in widths]
    out_specs = [pl.BlockSpec((tm, w), row) for w in widths]
    if has_y2:
        out_shape = [jax.ShapeDtypeStruct((t, d), F32)] + out_shape
        out_specs = [pl.BlockSpec((tm, d), row)] + out_specs
    outs = pl.pallas_call(
        functools.partial(_inproj_kernel, has_y2, n_seq_tiles),
        grid=(t // tm,),
        in_specs=in_specs,
        out_specs=out_specs,
        out_shape=out_shape,
        scratch_shapes=[pltpu.VMEM((1, 2 * LANES), F32)],
        compiler_params=pltpu.CompilerParams(dimension_semantics=("arbitrary",),
                                             vmem_limit_bytes=VMEM_LIMIT),
        name="inproj_y2" if has_y2 else "inproj",
    )(*args)
    if has_y2:
        return outs[0], outs[1:]
    return h, outs


def _swa_kernel(sink_ref, q_ref, km_ref, kp_ref, kc_ref, vm_ref, vp_ref, vc_ref, o_ref):
    i = pl.program_id(1)
    tq = q_ref.shape[1]
    lane = lax.broadcasted_iota(jnp.int32, (1, LANES), 1)
    lo_half = lane < HEAD_DIM
    half_masks = [jnp.where(lo_half, 1.0, 0.0).astype(BF16), jnp.where(lo_half, 0.0, 1.0).astype(BF16)]
    pq = i * tq + lax.broadcasted_iota(jnp.int32, (tq, 1), 0)
    col = lax.broadcasted_iota(jnp.int32, (1, tq), 1)
    cq = pq >> CHUNK_SHIFT
    pk_m = col
    pk_p = (i - 1) * tq + col
    pk_c = i * tq + col
    vis_m = pk_m >= PAD_FRONT
    d_m = jnp.minimum(jnp.abs(pq - pk_m), WINDOW).astype(F32)

    def band(pk):
        ck = jnp.where(pk >= BLOCK, pk >> CHUNK_SHIFT, BIG)
        vis = (ck <= cq) & (ck >= cq - (WINDOW >> CHUNK_SHIFT))
        return vis, jnp.abs(pq - pk).astype(F32)

    vis_p, d_p = band(pk_p)
    vis_c, d_c = band(pk_c)
    grp = SWA_HEADS // SWA_KV_HEADS
    km, kp, kc = km_ref[0], kp_ref[0], kc_ref[0]
    vm, vp, vc = vm_ref[0], vp_ref[0], vc_ref[0]
    for g in range(grp):
        qg = q_ref[0, :, g * LANES:(g + 1) * LANES]
        out_g = None
        for hk in range(SWA_KV_HEADS):
            head = hk * grp + g
            slope = 2.0 ** (-8.0 * (head + 1) / SWA_HEADS)
            sink = sink_ref[head]
            qh = qg * half_masks[hk]
            s_m = jnp.where(vis_m, _dot_nt(qh, km) - slope * d_m, NEG)
            s_p = jnp.where(vis_p, _dot_nt(qh, kp) - slope * d_p, NEG)
            s_c = jnp.where(vis_c, _dot_nt(qh, kc) - slope * d_c, NEG)
            m = jnp.maximum(jnp.maximum(jnp.max(s_m, axis=-1, keepdims=True),
                                        jnp.max(s_p, axis=-1, keepdims=True)),
                            jnp.maximum(jnp.max(s_c, axis=-1, keepdims=True), sink))
            e_m, e_p, e_c = jnp.exp(s_m - m), jnp.exp(s_p - m), jnp.exp(s_c - m)
            den = (jnp.sum(e_m, axis=-1, keepdims=True) + jnp.sum(e_p, axis=-1, keepdims=True)
                   + jnp.sum(e_c, axis=-1, keepdims=True) + jnp.exp(sink - m))
            inv = 1.0 / den
            o = (_dot((e_m * inv).astype(BF16), vm) + _dot((e_p * inv).astype(BF16), vp)
                 + _dot((e_c * inv).astype(BF16), vc))
            out_g = o if hk == 0 else jnp.where(lo_half, out_g, o)
        o_ref[0, :, g * LANES:(g + 1) * LANES] = out_g.astype(BF16)


def _swa_attention(qa, sinks, batch, seq_len):
    x = qa.reshape(batch, seq_len, 512)
    nb = seq_len // BLOCK
    blk = lambda f: pl.BlockSpec((1, BLOCK, LANES), f)
    return pl.pallas_call(
        _swa_kernel,
        grid=(batch, nb),
        in_specs=[
            pl.BlockSpec(memory_space=pltpu.SMEM),
            pl.BlockSpec((1, BLOCK, 2 * LANES), lambda b, i: (b, i, 0)),
            blk(lambda b, i: (b, 0, 2)),
            blk(lambda b, i: (b, jnp.maximum(i - 1, 0), 2)),
            blk(lambda b, i: (b, i, 2)),
            blk(lambda b, i: (b, 0, 3)),
            blk(lambda b, i: (b, jnp.maximum(i - 1, 0), 3)),
            blk(lambda b, i: (b, i, 3)),
        ],
        out_specs=pl.BlockSpec((1, BLOCK, 2 * LANES), lambda b, i: (b, i, 0)),
        out_shape=jax.ShapeDtypeStruct((batch, seq_len, 2 * LANES), BF16),
        compiler_params=pltpu.CompilerParams(dimension_semantics=("arbitrary", "arbitrary"),
                                             vmem_limit_bytes=VMEM_LIMIT),
        name="swa_attention",
    )(sinks, x, x, x, x, x, x, x)


def _causal_kernel(mode, q_ref, k_ref, v_ref, o_ref, stat_ref, acc_ref):
    seq_len = q_ref.shape[1]
    n_qt = (seq_len - BLOCK) // Q_TILE
    per_tile = Q_TILE // K_TILE
    lane = lax.broadcasted_iota(jnp.int32, (1, LANES), 1)
    lo_half = lane < HEAD_DIM
    if mode == "sb":
        r = lax.broadcasted_iota(jnp.int32, (2 * K_TILE, K_TILE), 0) & (K_TILE - 1)
        c = lax.broadcasted_iota(jnp.int32, (2 * K_TILE, K_TILE), 1)
        later2 = jnp.where(r > c, 1.0, 0.0).astype(BF16)
        r1 = lax.broadcasted_iota(jnp.int32, (2 * BLOCK, BLOCK), 0) & (BLOCK - 1)
        c1 = lax.broadcasted_iota(jnp.int32, (2 * BLOCK, BLOCK), 1)
        later1 = jnp.where(r1 > c1, 1.0, 0.0).astype(BF16)

    def causal(pq, k0, tk):
        pk = k0 + lax.broadcasted_iota(jnp.int32, (1, tk), 1)
        if mode == "fox":
            return pk <= pq
        if mode == "mla":
            return (pk >> CHUNK_SHIFT) <= (pq >> CHUNK_SHIFT)
        return pk < pq

    def head_v(k0, tk, hh):
        if mode == "sb":
            return v_ref[0, pl.ds(k0, tk), :]
        return v_ref[0, pl.ds(k0, tk), hh * LANES:(hh + 1) * LANES]

    def lane_tiles(x):
        return [x[:, j * LANES:(j + 1) * LANES] for j in range(x.shape[1] // LANES)]

    def row_parts(tq):
        step = min(tq, ROW_PART)
        return [(r0, step) for r0 in range(0, tq, step)]

    def softmax_chunk(q0, tq, k0, tk, masked, first):
        for hh in range(2):
            qh = q_ref[0, pl.ds(q0, tq), hh * LANES:(hh + 1) * LANES]
            kh = k_ref[0, pl.ds(k0, tk), hh * LANES:(hh + 1) * LANES]
            s_all = _dot_nt(qh, kh)
            vh = head_v(k0, tk, hh)
            for r0, tr in row_parts(tq):
                s = s_all[r0:r0 + tr]
                if masked:
                    pq = q0 + r0 + lax.broadcasted_iota(jnp.int32, (tr, 1), 0)
                    s = jnp.where(causal(pq, k0, tk), s, NEG)
                tiles = lane_tiles(s)
                top = tiles[0]
                for x in tiles[1:]:
                    top = jnp.maximum(top, x)
                m_new = jnp.broadcast_to(jnp.max(top, axis=-1, keepdims=True), (tr, LANES))
                if not first:
                    m_old = stat_ref[hh, r0:r0 + tr, :]
                    m_new = jnp.maximum(m_old, m_new)
                p = jnp.concatenate([jnp.exp2(x - m_new) for x in tiles], axis=1).astype(BF16)
                pv = _dot(p, vh)
                if not first:
                    pv = jnp.exp2(m_old - m_new) * acc_ref[hh, r0:r0 + tr, :] + pv
                stat_ref[hh, r0:r0 + tr, :] = m_new
                acc_ref[hh, r0:r0 + tr, :] = pv

    def stick_chunk(q0, tq, k0, tk, masked, first):
        later = later2 if tk == K_TILE else later1
        for hh in range(2):
            qh = q_ref[0, pl.ds(q0, tq), hh * LANES:(hh + 1) * LANES]
            kh = k_ref[0, pl.ds(k0, tk), hh * LANES:(hh + 1) * LANES]
            z = _dot_nt(qh, kh)
            ls_pos = jnp.minimum(z, 0.0) - jnp.log(1.0 + jnp.exp2(-jnp.abs(z))) * LOG2E
            log_keep = ls_pos - z
            if masked:
                pq = q0 + lax.broadcasted_iota(jnp.int32, (tq, 1), 0)
                vis = causal(pq, k0, tk)
                log_keep = jnp.where(vis, log_keep, 0.0)
            hi = log_keep.astype(BF16)
            lo = (log_keep - hi.astype(F32)).astype(BF16)
            after = _dot(jnp.concatenate([hi, lo], axis=1), later)
            tot = ls_pos + after
            chunk_total = jnp.broadcast_to(after[:, 0:1] + log_keep[:, 0:1], (tq, LANES))
            if not first:
                carry = stat_ref[hh, 0:tq, :]
                tot = jnp.concatenate([x + carry for x in lane_tiles(tot)], axis=1)
                chunk_total = carry + chunk_total
            a = jnp.exp2(tot)
            if masked:
                a = jnp.where(vis, a, 0.0)
            pv = _dot(a.astype(BF16), head_v(k0, tk, hh))
            if not first:
                pv = acc_ref[hh, 0:tq, :] + pv
            stat_ref[hh, 0:tq, :] = chunk_total
            acc_ref[hh, 0:tq, :] = pv

    def finish(q0, tq):
        a0, a1 = acc_ref[0, 0:tq, :], acc_ref[1, 0:tq, :]
        if mode != "sb":
            a0 = a0 / a0[:, HEAD_DIM:HEAD_DIM + 1]
            a1 = a1 / a1[:, 0:1]
        o_ref[0, pl.ds(q0, tq), :] = jnp.where(lo_half, a0, a1).astype(BF16)

    def chunk_start(j):
        return pl.multiple_of(BLOCK + j * K_TILE, BLOCK)

    if mode == "sb":
        stick_chunk(0, BLOCK, 0, BLOCK, True, True)
        finish(0, BLOCK)

        def q_body(i, _):
            q0 = pl.multiple_of(BLOCK + i * Q_TILE, BLOCK)
            n_int = i * per_tile
            for d in range(per_tile):
                stick_chunk(q0, Q_TILE, chunk_start(n_int + per_tile - 1 - d), K_TILE, True, d == 0)

            def alive():
                top = jnp.maximum(jnp.max(stat_ref[0]), jnp.max(stat_ref[1]))
                return (top > UNDERFLOW_LOG2).astype(jnp.int32)

            def body(st):
                jj, _ = st
                for d in range(per_tile):
                    stick_chunk(q0, Q_TILE, chunk_start(n_int - 1 - jj * per_tile - d), K_TILE, False, False)
                return jj + 1, alive()

            _, go = lax.while_loop(lambda st: (st[0] < i) & (st[1] > 0), body, (0, alive()))

            @pl.when(go > 0)
            def _():
                stick_chunk(q0, Q_TILE, 0, BLOCK, False, False)

            finish(q0, Q_TILE)
            return 0
    else:
        softmax_chunk(0, BLOCK, 0, BLOCK, True, True)
        finish(0, BLOCK)

        def q_body(i, _):
            q0 = pl.multiple_of(BLOCK + i * Q_TILE, BLOCK)
            n_int = i * per_tile
            softmax_chunk(q0, Q_TILE, 0, BLOCK, False, True)

            def body(j, _):
                for d in range(per_tile):
                    softmax_chunk(q0, Q_TILE, chunk_start(j * per_tile + d), K_TILE, False, False)
                return 0

            lax.fori_loop(0, i, body, 0)
            for d in range(per_tile):
                softmax_chunk(q0, Q_TILE, chunk_start(n_int + d), K_TILE, True, False)
            finish(q0, Q_TILE)
            return 0

    lax.fori_loop(0, n_qt, q_body, 0)


def _causal_attention(mode, q, k, v, batch, seq_len):
    wide = pl.BlockSpec((1, seq_len, 2 * LANES), lambda b, p: (b, 0, p))
    narrow = pl.BlockSpec((1, seq_len, LANES), lambda b, p: (b, 0, p))
    args = [q.reshape(batch, seq_len, 512), k.reshape(batch, seq_len, 512),
            v.reshape(batch, seq_len, v.shape[1])]
    return pl.pallas_call(
        functools.partial(_causal_kernel, mode),
        grid=(batch, 2),
        in_specs=[wide, wide, narrow if mode == "sb" else wide],
        out_specs=narrow,
        out_shape=jax.ShapeDtypeStruct((batch, seq_len, 2 * LANES), BF16),
        scratch_shapes=[pltpu.VMEM((2, Q_TILE, LANES), F32), pltpu.VMEM((2, Q_TILE, LANES), F32)],
        compiler_params=pltpu.CompilerParams(dimension_semantics=("arbitrary", "arbitrary"),
                                             vmem_limit_bytes=VMEM_LIMIT),
        name=mode + "_attention",
    )(*args)


def _outproj_kernel(ya_ref, yb_ref, yc_ref, yd_ref, h_ref, wo_ref, g_ref, wrh_ref, wrl_ref, br_ref,
                    h2_ref, xn_ref, lg_ref):
    o = (_dot(ya_ref[...], wo_ref[0]) + _dot(yb_ref[...], wo_ref[1])
         + _dot(yc_ref[...], wo_ref[2]) + _dot(yd_ref[...], wo_ref[3]))
    h2 = h_ref[...] + o
    h2_ref[...] = h2
    xn = _rms(h2, g_ref[...])
    _rows_to_tiles(xn_ref, (), xn)
    xh = xn.astype(BF16)
    xl = (xn - xh.astype(F32)).astype(BF16)
    wrh, wrl = wrh_ref[...], wrl_ref[...]
    lg_ref[...] = _dot(xh, wrh) + _dot(xl, wrh) + _dot(xh, wrl) + br_ref[...]


def _outproj(ys, h, p):
    t, d = h.shape
    tm = ROW_TILE
    row = lambda i: (i, 0)
    fixed2 = lambda i: (0, 0)
    in_specs = [pl.BlockSpec((tm, 256), row)] * 4 + [
        pl.BlockSpec((tm, d), row),
        pl.BlockSpec((4, 256, d), lambda i: (0, 0, 0)),
        pl.BlockSpec((1, d), fixed2),
        pl.BlockSpec((d, LANES), fixed2),
        pl.BlockSpec((d, LANES), fixed2),
        pl.BlockSpec((1, LANES), fixed2),
    ]
    return pl.pallas_call(
        _outproj_kernel,
        grid=(t // tm,),
        in_specs=in_specs,
        out_specs=[pl.BlockSpec((tm, d), row), pl.BlockSpec((tm * ROW_SUB, LANES), row),
                   pl.BlockSpec((tm, LANES), row)],
        out_shape=[jax.ShapeDtypeStruct((t, d), F32), jax.ShapeDtypeStruct((t * ROW_SUB, LANES), F32),
                   jax.ShapeDtypeStruct((t, LANES), F32)],
        compiler_params=pltpu.CompilerParams(dimension_semantics=("arbitrary",),
                                             vmem_limit_bytes=VMEM_LIMIT),
        name="outproj_router",
    )(*ys, h, p["w_out"], p["ffn_norm"], p["w_r_hi"], p["w_r_lo"], p["b_r"])


def _moe_kernel(be_ref, tok_ref, dst_ref, x_hbm, wg_ref, wu_ref, wd_ref, y_hbm,
                xbuf, ybuf, sem_in, sem_out):
    s = pl.program_id(0)
    blk = MOE_BLOCK * ROW_SUB
    trash0 = y_hbm.shape[0] - blk

    def block_in(buf_slot):
        return pltpu.make_async_copy(x_hbm.at[pl.ds(0, blk)], xbuf.at[buf_slot], sem_in.at[buf_slot])

    def block_out(buf_slot):
        return pltpu.make_async_copy(ybuf.at[buf_slot], y_hbm.at[pl.ds(trash0, blk)], sem_out.at[buf_slot])

    @pl.when(s == 0)
    def _():
        ybuf[...] = jnp.zeros_like(ybuf)
        block_in(0).start()
        block_out(0).start()

    def step(slot):
        other = 1 - slot
        block_in(slot).wait()
        x = _rows_from_tiles(xbuf, (slot,), MOE_BLOCK).astype(BF16)
        gate = _dot(x, wg_ref[0])
        up = _dot(x, wu_ref[0])
        for r in range(MOE_BLOCK):
            src = pl.multiple_of(tok_ref[0, 0, r], ROW_SUB)
            pltpu.make_async_copy(x_hbm.at[pl.ds(src, ROW_SUB)], xbuf.at[other, pl.ds(r * ROW_SUB, ROW_SUB)],
                                  sem_in.at[other]).start(priority=r % 2)
        for r in range(MOE_BLOCK):
            dst = pl.multiple_of(dst_ref[0, 0, r], ROW_SUB)
            pltpu.make_async_copy(ybuf.at[other, pl.ds(r * ROW_SUB, ROW_SUB)], y_hbm.at[pl.ds(dst, ROW_SUB)],
                                  sem_out.at[other]).start(priority=r % 2)
        hid = (gate * (1.0 / (1.0 + jnp.exp(-gate))) * up).astype(BF16)
        y = _dot(hid, wd_ref[0])
        block_out(slot).wait()
        _rows_to_tiles(ybuf, (slot,), y)

        @pl.when(s == pl.num_programs(0) - 1)
        def _():
            block_in(other).wait()
            block_out(other).wait()

    for parity in range(2):
        pl.when(s % 2 == parity)(functools.partial(step, parity))


def _moe(xn, tok_ext, dst_ext, be_ext, p):
    t, d = xn.shape[0] // ROW_SUB, xn.shape[1] * ROW_SUB
    n_steps = be_ext.shape[0]
    hdim = p["w_gate"].shape[2]
    wspec = lambda shape: pl.BlockSpec((1,) + shape, lambda s, be: (be[s], 0, 0))
    ids = pl.BlockSpec((1, 1, MOE_BLOCK), lambda s, be: (s, 0, 0), memory_space=pltpu.SMEM)
    return pl.pallas_call(
        _moe_kernel,
        grid_spec=pltpu.PrefetchScalarGridSpec(
            num_scalar_prefetch=1,
            grid=(n_steps,),
            in_specs=[ids, ids, pl.BlockSpec(memory_space=pl.ANY),
                      wspec((d, hdim)), wspec((d, hdim)), wspec((hdim, d))],
            out_specs=pl.BlockSpec(memory_space=pl.ANY),
            scratch_shapes=[pltpu.VMEM((2, MOE_BLOCK * ROW_SUB, LANES), F32),
                            pltpu.VMEM((2, MOE_BLOCK * ROW_SUB, LANES), F32),
                            pltpu.SemaphoreType.DMA((2,)), pltpu.SemaphoreType.DMA((2,))],
        ),
        out_shape=jax.ShapeDtypeStruct(((t * TOP_K + MOE_TRASH_ROWS) * ROW_SUB, LANES), F32),
        compiler_params=pltpu.CompilerParams(dimension_semantics=("arbitrary",),
                                             vmem_limit_bytes=VMEM_LIMIT),
        name="moe_experts",
    )(be_ext, tok_ext, dst_ext, xn, p["w_gate"], p["w_up"], p["w_down"])


def _route(logits, t):
    g_logits = logits[:, :N_GROUPS]
    e_logits = logits[:, N_GROUPS:N_GROUPS + N_EXPERTS].reshape(t, N_GROUPS, EXPERTS_PER_GROUP)
    g_prob = jax.nn.softmax(g_logits, axis=-1)
    g_top = jnp.argmax(g_logits, axis=-1).astype(jnp.int32)
    g_w = jnp.take_along_axis(g_prob, g_top[:, None], axis=-1)
    e_in_group = jnp.take_along_axis(e_logits, g_top[:, None, None], axis=1)[:, 0]
    top_v, top_i = lax.top_k(e_in_group, TOP_K)
    e_w = jax.nn.softmax(top_v, axis=-1) * g_w
    expert = g_top[:, None] * EXPERTS_PER_GROUP + top_i.astype(jnp.int32)
    a = t * TOP_K
    flat_e = expert.reshape(a)
    order = jnp.argsort(flat_e).astype(jnp.int32)
    se = flat_e[order]
    counts = jnp.bincount(flat_e, length=N_EXPERTS).astype(jnp.int32)
    padded = (counts + MOE_BLOCK - 1) // MOE_BLOCK * MOE_BLOCK
    pad_end = jnp.cumsum(padded)
    pad_start = pad_end - padded
    raw_start = jnp.cumsum(counts) - counts
    dest = pad_start[se] + (jnp.arange(a, dtype=jnp.int32) - raw_start[se])
    n_rows = a + N_EXPERTS * MOE_BLOCK
    n_blk = n_rows // MOE_BLOCK
    rows_asg = jnp.full((n_rows,), -1, jnp.int32).at[dest].set(order)
    block_e = jnp.minimum(jnp.searchsorted(pad_end, jnp.arange(n_blk, dtype=jnp.int32) * MOE_BLOCK, side="right"),
                          N_EXPERTS - 1).astype(jnp.int32)
    ew = jnp.pad(e_w, ((0, 0), (0, LANES - TOP_K)))
    n_steps = n_blk + 2
    asg = rows_asg.reshape(n_blk, MOE_BLOCK)
    none = jnp.full((2, MOE_BLOCK), -1, jnp.int32)
    tok_ext = jnp.maximum(jnp.concatenate([asg, none], axis=0), 0) >> 1
    asg_out = jnp.concatenate([none, asg], axis=0)
    step = jnp.arange(n_steps, dtype=jnp.int32)[:, None]
    trash = (a + (step % (MOE_TRASH_ROWS // MOE_BLOCK - 1)) * MOE_BLOCK
             + jnp.arange(MOE_BLOCK, dtype=jnp.int32)[None, :])
    dst_ext = jnp.where(asg_out >= 0, (asg_out & 1) * t + (asg_out >> 1), trash)
    be_ext = block_e[jnp.clip(step[:, 0] - 1, 0, n_blk - 1)]
    return ((tok_ext * ROW_SUB).reshape(n_steps, 1, MOE_BLOCK),
            (dst_ext * ROW_SUB).reshape(n_steps, 1, MOE_BLOCK), be_ext, ew)


def _final_kernel(h_ref, y0_ref, y1_ref, ew_ref, g_ref, o_ref):
    ew = ew_ref[...]
    n = h_ref.shape[0]
    h = h_ref[...] + ew[:, 0:1] * _rows_from_tiles(y0_ref, (), n) + ew[:, 1:2] * _rows_from_tiles(y1_ref, (), n)
    o_ref[0] = _rms(h, g_ref[...])


def _final(h, y2, ew, g, batch, seq_len):
    t, d = h.shape
    per_seq = seq_len // BLOCK
    row = lambda b, i: (b * per_seq + i + 1, 0)
    return pl.pallas_call(
        _final_kernel,
        grid=(batch, per_seq - 1),
        in_specs=[pl.BlockSpec((BLOCK, d), row),
                  pl.BlockSpec((BLOCK * ROW_SUB, LANES), row),
                  pl.BlockSpec((BLOCK * ROW_SUB, LANES), lambda b, i: (b * per_seq + i + 1 + t // BLOCK, 0)),
                  pl.BlockSpec((BLOCK, LANES), row),
                  pl.BlockSpec((1, d), lambda b, i: (0, 0))],
        out_specs=pl.BlockSpec((1, BLOCK, d), lambda b, i: (b, i, 0)),
        out_shape=jax.ShapeDtypeStruct((batch, seq_len - BLOCK, d), F32),
        compiler_params=pltpu.CompilerParams(dimension_semantics=("arbitrary", "arbitrary"),
                                             vmem_limit_bytes=VMEM_LIMIT),
        name="final_norm",
    )(h, y2, y2, ew, g)


def _rope_table(seq_len):
    half = MLA_ROPE // 2
    pos = (jnp.arange(seq_len, dtype=jnp.int32) - PAD_FRONT).astype(F32)
    inv_freq = ROPE_THETA ** (-jnp.arange(half, dtype=F32) / half)
    ang = pos[:, None] * inv_freq[None, :]
    cos, sin = jnp.cos(ang), jnp.sin(ang)
    cos2 = jnp.concatenate([cos, cos], axis=1)
    sin2 = jnp.concatenate([-sin, sin], axis=1)
    z = lambda w: jnp.zeros((seq_len, w), F32)
    scale = (MLA_NOPE + MLA_ROPE) ** -0.5 * LOG2E
    cos_q = jnp.concatenate([jnp.ones((seq_len, MLA_NOPE), F32), cos2, z(32)], axis=1) * scale
    sin_q = jnp.concatenate([z(MLA_NOPE), sin2, z(32)], axis=1) * scale
    cos_k = jnp.concatenate([z(MLA_NOPE), cos2, z(32)], axis=1)
    sin_k = jnp.concatenate([z(MLA_NOPE), sin2, z(32)], axis=1)
    return jnp.concatenate([cos_q, sin_q, cos_k, sin_k], axis=1)


def _swap_halves(w):
    half = w.shape[-1] // 2
    return jnp.concatenate([w[..., half:], w[..., :half]], axis=-1)


def _layer_params(i, seq_len, attn_norm, w_in, b_forget, sinks, mla_q_norm, mla_kv_norm, mla_w_uq,
                  mla_w_ukv, w_out, ffn_norm, w_group, b_group, w_router, b_router, w_gate, w_up, w_down):
    d = w_in.shape[1]
    w = w_in[i]
    sizes = (256, 128, 128, 256, 256, 256, 4, 256, 128, 32, 256, 256, 256)
    offs = np.concatenate([[0], np.cumsum(sizes)])
    (a_q, a_k, a_v, f_q, f_k, f_v, f_g, c_q, c_kv, c_kr, s_q, s_k, s_v) = [
        w[:, offs[j]:offs[j + 1]] for j in range(len(sizes))]
    qscale = HEAD_DIM ** -0.5
    grp = SWA_HEADS // SWA_KV_HEADS
    a_q = a_q.reshape(d, SWA_KV_HEADS, grp, HEAD_DIM).transpose(0, 2, 1, 3).reshape(d, 256)
    z = lambda n: jnp.zeros((d, n), F32)
    g_grp = jnp.concatenate([f_g[:, 0:2], z(62), c_kr, z(32)], axis=1)
    gs_grp = jnp.concatenate([f_g[:, 2:4], z(62), _swap_halves(c_kr), z(32)], axis=1)
    w_perm = jnp.concatenate([a_q * qscale, a_k, a_v, f_q * qscale, f_k, f_v, s_q * qscale, s_k, s_v,
                              c_q, c_kv, g_grp, gs_grp], axis=1).astype(BF16)
    wuq = mla_w_uq[i].reshape(MLA_Q_LORA, 4, MLA_NOPE + MLA_ROPE)
    zq = lambda n: jnp.zeros((MLA_Q_LORA, 4, n), F32)
    w_uq_a = jnp.concatenate([wuq, zq(32)], axis=2).reshape(MLA_Q_LORA, 512).astype(BF16)
    w_uq_b = jnp.concatenate([zq(MLA_NOPE), _swap_halves(wuq[:, :, MLA_NOPE:]), zq(32)],
                             axis=2).reshape(MLA_Q_LORA, 512).astype(BF16)
    wukv = mla_w_ukv[i].reshape(MLA_KV_LORA, 4, MLA_NOPE + MLA_V)
    w_kv_k = jnp.concatenate([wukv[:, :, :MLA_NOPE], jnp.zeros((MLA_KV_LORA, 4, 64), F32)],
                             axis=2).reshape(MLA_KV_LORA, 512).astype(BF16)
    w_kv_v = wukv[:, :, MLA_NOPE:].reshape(MLA_KV_LORA, 256).astype(BF16)
    bf = b_forget[i].astype(F32)
    b_f = jnp.zeros((1, 256), F32).at[0, 0:2].set(bf[0:2]).at[0, 128:130].set(bf[2:4])
    wo = w_out[i]
    wo_a = wo[:256].reshape(SWA_KV_HEADS, grp, HEAD_DIM, d).transpose(1, 0, 2, 3).reshape(256, d)
    wo4 = jnp.concatenate([wo_a, wo[256:]], axis=0).reshape(4, 256, d).astype(BF16)
    w_r = jnp.concatenate([w_group[i], w_router[i], jnp.zeros((d, LANES - N_GROUPS - N_EXPERTS), F32)], axis=1)
    w_r_hi = w_r.astype(BF16)
    w_r_lo = (w_r - w_r_hi.astype(F32)).astype(BF16)
    b_r = jnp.concatenate([b_group[i], b_router[i], jnp.zeros((LANES - N_GROUPS - N_EXPERTS,), F32)])[None, :]
    return dict(
        attn_norm=attn_norm[i][None, :], w_in=w_perm, q_norm=mla_q_norm[i][None, :],
        kv_norm=mla_kv_norm[i][None, :], w_uq_a=w_uq_a, w_uq_b=w_uq_b, w_kv_k=w_kv_k, w_kv_v=w_kv_v,
        rope_tab=_rope_table(seq_len), b_forget=b_f, sinks=sinks[i].astype(F32), w_out=wo4,
        ffn_norm=ffn_norm[i][None, :], w_r_hi=w_r_hi, w_r_lo=w_r_lo, b_r=b_r.astype(F32),
        w_gate=w_gate[i].astype(BF16), w_up=w_up[i].astype(BF16), w_down=w_down[i].astype(BF16))


def kernel(x, meta_tokens, attn_norm, w_in, b_forget, sinks, mla_q_norm, mla_kv_norm, mla_w_uq, mla_w_ukv,
           w_out, ffn_norm, w_group, b_group, w_router, b_router, w_gate, w_up, w_down, final_norm):
    batch, seq, d = x.shape
    seq_len = seq + BLOCK
    assert seq_len % ROW_TILE == 0 and seq % Q_TILE == 0
    t = batch * seq_len
    depth = w_in.shape[0]
    pad = jnp.zeros((batch, PAD_FRONT, d), x.dtype)
    meta = jnp.broadcast_to(meta_tokens.astype(x.dtype)[None], (batch, N_META, d))
    h = jnp.concatenate([pad, meta, x], axis=1).reshape(t, d)
    y2 = ew = None
    for i in range(depth):
        p = _layer_params(i, seq_len, attn_norm, w_in, b_forget, sinks, mla_q_norm, mla_kv_norm, mla_w_uq,
                          mla_w_ukv, w_out, ffn_norm, w_group, b_group, w_router, b_router, w_gate, w_up, w_down)
        h, (qa, fq, fk, fv, cq, ck, cv, sq, sk, sv) = _inproj(h, y2, ew, p, seq_len)
        y_a = _swa_attention(qa, p["sinks"], batch, seq_len)
        y_b = _causal_attention("fox", fq, fk, fv, batch, seq_len)
        y_c = _causal_attention("mla", cq, ck, cv, batch, seq_len)
        y_d = _causal_attention("sb", sq, sk, sv, batch, seq_len)
        ys = [y.reshape(t, 256) for y in (y_a, y_b, y_c, y_d)]
        h, xn, logits = _outproj(ys, h, p)
        tok_ext, dst_ext, be_ext, ew = _route(logits, t)
        y2 = _moe(xn, tok_ext, dst_ext, be_ext, p)
    return _final(h, y2, ew, final_norm[None, :], batch, seq_len)
```

```python
import functools

import jax
import jax.numpy as jnp
import numpy as np
from jax import lax
from jax.experimental import pallas as pl
from jax.experimental.pallas import tpu as pltpu

F32 = jnp.float32
BF16 = jnp.bfloat16

BLOCK = 128
N_META = 16
PAD_FRONT = BLOCK - N_META
CHUNK_SHIFT = 6
HEAD_DIM = 64
NORM_EPS = 1e-6
NEG = -1e30
PAD_KEY_LOGIT = -(2.0 ** 100)
UNDERFLOW_LOG2 = -150.0
LOG2E = 1.4426950408889634
BIG = 1 << 30
SWA_HEADS, SWA_KV_HEADS, WINDOW = 4, 2, 128
MLA_Q_LORA, MLA_KV_LORA, MLA_NOPE, MLA_ROPE, MLA_V = 256, 128, 64, 32, 64
MLA_BIAS_LANE = MLA_NOPE + MLA_ROPE
ROPE_THETA = 10000.0
N_GROUPS, EXPERTS_PER_GROUP, TOP_K = 4, 8, 2
N_EXPERTS = N_GROUPS * EXPERTS_PER_GROUP
MOE_BLOCK = 256
MOE_TRASH_ROWS = 8 * MOE_BLOCK
LANES = 128
ROW_SUB = 8
ROW_TILE = 384
Q_TILE = 512
K_TILE = 256
ROW_PART = 256
SWA_Q_BLOCKS = 3
VMEM_LIMIT = 56 * 1024 * 1024

C_A, C_B, C_D, C_CQ, C_CKV, C_G, C_GS, C_END = 0, 512, 1280, 2048, 2304, 2432, 2560, 2688
B_F0, B_F1, B_PAD = 0, 3, 6


def _rms(x, g):
    return x * lax.rsqrt(jnp.mean(x * x, axis=-1, keepdims=True) + NORM_EPS) * g


def _log_sigmoid(x):
    return jnp.minimum(x, 0.0) - jnp.log(1.0 + jnp.exp(-jnp.abs(x)))


def _dot(a, b):
    return jnp.dot(a, b, preferred_element_type=F32)


def _dot_nt(a, b):
    return lax.dot_general(a, b, (((1,), (1,)), ((), ())), preferred_element_type=F32)


def _rows_from_tiles(ref, lead, n):
    return jnp.concatenate([ref[(*lead, pl.ds(j, n, stride=ROW_SUB), slice(None))] for j in range(ROW_SUB)], axis=1)


def _rows_to_tiles(ref, lead, x):
    n = x.shape[0]
    for j in range(ROW_SUB):
        ref[(*lead, pl.ds(j, n, stride=ROW_SUB), slice(None))] = x[:, j * LANES:(j + 1) * LANES]


def _tile4(x):
    return jnp.concatenate([x, x, x, x], axis=1)


def _split3(x):
    hi = x.astype(BF16)
    r1 = x - hi.astype(F32)
    mid = r1.astype(BF16)
    lo = (r1 - mid.astype(F32)).astype(BF16)
    return hi, mid, lo


def _free_base(head):
    return head * LANES + (HEAD_DIM if head % 2 == 0 else 0)


def _inproj_kernel(has_y2, n_seq_tiles, *refs):
    if has_y2:
        (h_ref, y0_ref, y1_ref, ew_ref, *rest) = refs
    else:
        (h_ref, *rest) = refs
    (g_ref, w_ref, qn_ref, kvn_ref, wuqa_ref, wuqb_ref, wkvk_ref, wkvv_ref, tab_ref, bf_ref,
     pq_ref, pk_ref, rows_ref, *outs) = rest
    if has_y2:
        hout_ref, *outs = outs
    (qa_ref, fq_ref, fk_ref, fv_ref, cq_ref, ck_ref, cv_ref, sq_ref, sk_ref, sv_ref, carry_ref) = outs
    tile = pl.program_id(0) % n_seq_tiles
    h = h_ref[...]
    tm, d = h.shape
    if has_y2:
        ew = ew_ref[...]
        h = h + ew[:, 0:1] * _rows_from_tiles(y0_ref, (), tm) + ew[:, 1:2] * _rows_from_tiles(y1_ref, (), tm)
        hout_ref[...] = h
    xn = _rms(h, g_ref[...]).astype(BF16)
    acc = _dot(xn, w_ref[...])
    lane = lax.broadcasted_iota(jnp.int32, (1, LANES), 1)
    lo_half = lane < HEAD_DIM
    pad_col = jnp.where(tile * tm + lax.broadcasted_iota(jnp.int32, (tm, 1), 0) < PAD_FRONT,
                        PAD_KEY_LOGIT, 0.0)
    rows = rows_ref[...]
    fq_one, fk_one, pad_lane, sq_one, mla_one, mla_pad = (rows[j:j + 1] for j in range(6))

    def per_head(x_pair, bias, pair, scale=None):
        x = x_pair if scale is None else x_pair * scale
        even = jnp.where(lo_half, x, bias[:, (2 * pair) * LANES:(2 * pair + 1) * LANES])
        odd = jnp.where(lo_half, bias[:, (2 * pair + 1) * LANES:(2 * pair + 2) * LANES], x)
        return even, odd

    def store_heads(ref, x_off, bias, scale=None):
        for pair in range(2):
            x_pair = acc[:, x_off + pair * LANES:x_off + (pair + 1) * LANES]
            even, odd = per_head(x_pair, bias, pair, scale)
            ref[:, (2 * pair) * LANES:(2 * pair + 1) * LANES] = even.astype(BF16)
            ref[:, (2 * pair + 1) * LANES:(2 * pair + 2) * LANES] = odd.astype(BF16)

    qa_ref[...] = acc[:, C_A:C_B].astype(BF16)

    @pl.when(tile == 0)
    def _():
        carry_ref[...] = jnp.zeros_like(carry_ref)

    lf = _log_sigmoid(acc[:, C_G:C_END] + bf_ref[...]) * LOG2E
    r = lax.broadcasted_iota(jnp.int32, (BLOCK, BLOCK), 0)
    c = lax.broadcasted_iota(jnp.int32, (BLOCK, BLOCK), 1)
    tri = jnp.where(c <= r, 1.0, 0.0).astype(BF16)
    carry = carry_ref[...]
    blocks = []
    for b in range(tm // BLOCK):
        hi, mid, lo = _split3(lf[b * BLOCK:(b + 1) * BLOCK])
        y = _dot(tri, hi) + _dot(tri, mid) + _dot(tri, lo) + carry
        carry = y[BLOCK - 1:BLOCK, :]
        blocks.append(y)
    carry_ref[...] = carry
    f_hi, f_mid, f_lo = _split3(jnp.concatenate(blocks, axis=0))
    q_bias = _dot(f_hi, pq_ref[0]) + _dot(f_mid, pq_ref[1]) + _dot(f_lo, pq_ref[2]) + fq_one
    k_bias = (_dot(f_hi, pk_ref[0]) + _dot(f_mid, pk_ref[1]) + _dot(f_lo, pk_ref[2]) + fk_one
              + pad_col * pad_lane)
    ones = jnp.ones((tm, 4 * LANES), F32)
    store_heads(fq_ref, C_B, q_bias, LOG2E)
    store_heads(fk_ref, C_B + 256, k_bias)
    store_heads(fv_ref, C_B + 512, ones)

    store_heads(sq_ref, C_D, jnp.broadcast_to(sq_one, (tm, 4 * LANES)), LOG2E)
    store_heads(sk_ref, C_D + 256, pad_col * pad_lane)
    sv_ref[...] = acc[:, C_D + 512:C_CQ].astype(BF16)

    tab = tab_ref[...]
    cos_q, sin_q = tab[:, 0:128], tab[:, 128:256]
    cos_k, sin_k = tab[:, 256:384], tab[:, 384:512]
    cqn = _rms(acc[:, C_CQ:C_CKV], qn_ref[...]).astype(BF16)
    q_lin = _dot(cqn, wuqa_ref[...])
    q_swp = _dot(cqn, wuqb_ref[...])
    cq_ref[...] = (q_lin * _tile4(cos_q) + q_swp * _tile4(sin_q) + mla_one).astype(BF16)
    ckvn = _rms(acc[:, C_CKV:C_G], kvn_ref[...]).astype(BF16)
    k_nope = _dot(ckvn, wkvk_ref[...])
    grp, grp_s = acc[:, C_G:C_GS], acc[:, C_GS:C_END]
    k_rope = grp * cos_k + grp_s * sin_k
    ck_ref[...] = (k_nope + _tile4(k_rope) + pad_col * mla_pad).astype(BF16)
    vv = _dot(ckvn, wkvv_ref[...])
    for pair in range(2):
        even, odd = per_head(vv[:, pair * LANES:(pair + 1) * LANES], ones, pair)
        cv_ref[:, (2 * pair) * LANES:(2 * pair + 1) * LANES] = even.astype(BF16)
        cv_ref[:, (2 * pair + 1) * LANES:(2 * pair + 2) * LANES] = odd.astype(BF16)


def _bias_constants():
    src = (0, 1, LANES, LANES + 1)
    pq = np.zeros((3, 2 * LANES, 4 * LANES), np.float32)
    pk = np.zeros((3, 2 * LANES, 4 * LANES), np.float32)
    rows = np.zeros((8, 4 * LANES), np.float32)
    for head in range(4):
        base = _free_base(head)
        for part in range(3):
            pq[part, src[head], base + B_F0 + part] = 1.0
            pk[part, src[head], base + B_F1 + part] = -1.0
            rows[0, base + B_F1 + part] = 1.0
            rows[1, base + B_F0 + part] = 1.0
        rows[0, base + B_PAD] = 1.0
        rows[2, base + B_PAD] = 1.0
        rows[3, base + B_PAD] = 1.0
        rows[4, head * LANES + MLA_BIAS_LANE] = 1.0
        rows[5, head * LANES + MLA_BIAS_LANE] = 1.0
    return jnp.asarray(pq, BF16), jnp.asarray(pk, BF16), jnp.asarray(rows, F32)


def _inproj(h, y2, ew, p, seq_len):
    t, d = h.shape
    tm = ROW_TILE
    n_seq_tiles = seq_len // tm
    has_y2 = y2 is not None
    row = lambda i: (i, 0)
    fixed = lambda i: (0, 0)
    in_specs = [pl.BlockSpec((tm, d), row)]
    args = [h]
    if has_y2:
        in_specs += [pl.BlockSpec((tm * ROW_SUB, LANES), row),
                     pl.BlockSpec((tm * ROW_SUB, LANES), lambda i: (i + t // tm, 0)),
                     pl.BlockSpec((tm, LANES), row)]
        args += [y2, y2, ew]
    pq, pk, rows = _bias_constants()
    consts = [p["attn_norm"], p["w_in"], p["q_norm"], p["kv_norm"], p["w_uq_a"], p["w_uq_b"],
              p["w_kv_k"], p["w_kv_v"]]
    in_specs += [pl.BlockSpec(c.shape, fixed) for c in consts]
    args += consts
    in_specs.append(pl.BlockSpec((tm, 512), lambda i: (i % n_seq_tiles, 0)))
    args.append(p["rope_tab"])
    in_specs += [pl.BlockSpec((1, 256), fixed),
                 pl.BlockSpec(pq.shape, lambda i: (0, 0, 0)),
                 pl.BlockSpec(pk.shape, lambda i: (0, 0, 0)),
                 pl.BlockSpec(rows.shape, fixed)]
    args += [p["b_forget"], pq, pk, rows]
    widths = [512] * 9 + [256]
    out_shape = [jax.ShapeDtypeStruct((t, w), BF16) for w in widths]
    out_specs = [pl.BlockSpec((tm, w), row) for w in widths]
    if has_y2:
        out_shape = [jax.ShapeDtypeStruct((t, d), F32)] + out_shape
        out_specs = [pl.BlockSpec((tm, d), row)] + out_specs
    outs = pl.pallas_call(
        functools.partial(_inproj_kernel, has_y2, n_seq_tiles),
        grid=(t // tm,),
        in_specs=in_specs,
        out_specs=out_specs,
        out_shape=out_shape,
        scratch_shapes=[pltpu.VMEM((1, 2 * LANES), F32)],
        compiler_params=pltpu.CompilerParams(dimension_semantics=("arbitrary",),
                                             vmem_limit_bytes=VMEM_LIMIT),
        name="inproj_y2" if has_y2 else "inproj",
    )(*args)
    if has_y2:
        return outs[0], outs[1:]
    return h, outs


def _swa_kernel(sink_ref, q_ref, km_ref, kp_ref, kc_ref, vm_ref, vp_ref, vc_ref, o_ref):
    i = pl.program_id(1)
    n_sub = q_ref.shape[1] // BLOCK
    lane = lax.broadcasted_iota(jnp.int32, (1, LANES), 1)
    lo_half = lane < HEAD_DIM
    half_masks = [jnp.where(lo_half, 1.0, 0.0).astype(BF16), jnp.where(lo_half, 0.0, 1.0).astype(BF16)]
    row = lax.broadcasted_iota(jnp.int32, (BLOCK, 1), 0)
    col = lax.broadcasted_iota(jnp.int32, (1, BLOCK), 1)
    grp = SWA_HEADS // SWA_KV_HEADS
    k_all = jnp.concatenate([km_ref[0], kp_ref[0], kc_ref[0]], axis=0)
    v_all = jnp.concatenate([vm_ref[0], vp_ref[0], vc_ref[0]], axis=0)
    for j in range(n_sub):
        q0 = (i * n_sub + j) * BLOCK
        pq = q0 + row
        cq = pq >> CHUNK_SHIFT
        segs = []
        vis_m = col >= PAD_FRONT
        segs.append((vis_m, jnp.minimum(jnp.abs(pq - col), WINDOW).astype(F32)))
        for pk in (q0 - BLOCK + col, q0 + col):
            ck = jnp.where(pk >= BLOCK, pk >> CHUNK_SHIFT, BIG)
            vis = (ck <= cq) & (ck >= cq - (WINDOW >> CHUNK_SHIFT))
            segs.append((vis, jnp.abs(pq - pk).astype(F32)))
        kj = jnp.concatenate([k_all[0:BLOCK], k_all[(j + 1) * BLOCK:(j + 3) * BLOCK]], axis=0)
        vj = jnp.concatenate([v_all[0:BLOCK], v_all[(j + 1) * BLOCK:(j + 3) * BLOCK]], axis=0)
        for g in range(grp):
            qg = q_ref[0, j * BLOCK:(j + 1) * BLOCK, g * LANES:(g + 1) * LANES]
            out_g = None
            for hk in range(SWA_KV_HEADS):
                head = hk * grp + g
                slope = 2.0 ** (-8.0 * (head + 1) / SWA_HEADS)
                sink = sink_ref[head]
                s_all = _dot_nt(qg * half_masks[hk], kj)
                tiles = [jnp.where(vis, s_all[:, n * LANES:(n + 1) * LANES] - slope * dist, NEG)
                         for n, (vis, dist) in enumerate(segs)]
                top = jnp.maximum(jnp.maximum(tiles[0], tiles[1]), tiles[2])
                m = jnp.broadcast_to(jnp.maximum(jnp.max(top, axis=-1, keepdims=True), sink), (BLOCK, LANES))
                e = [jnp.exp(x - m) for x in tiles]
                den = jnp.sum(e[0] + e[1] + e[2], axis=-1, keepdims=True) + jnp.exp(sink - m[:, 0:1])
                o = _dot(jnp.concatenate(e, axis=1).astype(BF16), vj) * (1.0 / den)
                out_g = o if hk == 0 else jnp.where(lo_half, out_g, o)
            o_ref[0, j * BLOCK:(j + 1) * BLOCK, g * LANES:(g + 1) * LANES] = out_g.astype(BF16)


def _swa_attention(qa, sinks, batch, seq_len):
    x = qa.reshape(batch, seq_len, 512)
    n_sub = SWA_Q_BLOCKS
    nb = seq_len // (BLOCK * n_sub)
    blk = lambda f: pl.BlockSpec((1, BLOCK, LANES), f)
    own = lambda c: pl.BlockSpec((1, BLOCK * n_sub, LANES), lambda b, i: (b, i, c))
    return pl.pallas_call(
        _swa_kernel,
        grid=(batch, nb),
        in_specs=[
            pl.BlockSpec(memory_space=pltpu.SMEM),
            pl.BlockSpec((1, BLOCK * n_sub, 2 * LANES), lambda b, i: (b, i, 0)),
            blk(lambda b, i: (b, 0, 2)),
            blk(lambda b, i: (b, jnp.maximum(i * n_sub - 1, 0), 2)),
            own(2),
            blk(lambda b, i: (b, 0, 3)),
            blk(lambda b, i: (b, jnp.maximum(i * n_sub - 1, 0), 3)),
            own(3),
        ],
        out_specs=pl.BlockSpec((1, BLOCK * n_sub, 2 * LANES), lambda b, i: (b, i, 0)),
        out_shape=jax.ShapeDtypeStruct((batch, seq_len, 2 * LANES), BF16),
        compiler_params=pltpu.CompilerParams(dimension_semantics=("arbitrary", "arbitrary"),
                                             vmem_limit_bytes=VMEM_LIMIT),
        name="swa_attention",
    )(sinks, x, x, x, x, x, x, x)


def _causal_kernel(mode, q_ref, k_ref, v_ref, o_ref, stat_ref, acc_ref):
    seq_len = q_ref.shape[1]
    n_qt = (seq_len - BLOCK) // Q_TILE
    per_tile = Q_TILE // K_TILE
    lane = lax.broadcasted_iota(jnp.int32, (1, LANES), 1)
    lo_half = lane < HEAD_DIM
    if mode == "sb":
        r = lax.broadcasted_iota(jnp.int32, (2 * K_TILE, K_TILE), 0) & (K_TILE - 1)
        c = lax.broadcasted_iota(jnp.int32, (2 * K_TILE, K_TILE), 1)
        later2 = jnp.where(r > c, 1.0, 0.0).astype(BF16)
        r1 = lax.broadcasted_iota(jnp.int32, (2 * BLOCK, BLOCK), 0) & (BLOCK - 1)
        c1 = lax.broadcasted_iota(jnp.int32, (2 * BLOCK, BLOCK), 1)
        later1 = jnp.where(r1 > c1, 1.0, 0.0).astype(BF16)

    def causal(pq, k0, tk):
        pk = k0 + lax.broadcasted_iota(jnp.int32, (1, tk), 1)
        if mode == "fox":
            return pk <= pq
        if mode == "mla":
            return (pk >> CHUNK_SHIFT) <= (pq >> CHUNK_SHIFT)
        return pk < pq

    def head_v(k0, tk, hh):
        if mode == "sb":
            return v_ref[0, pl.ds(k0, tk), :]
        return v_ref[0, pl.ds(k0, tk), hh * LANES:(hh + 1) * LANES]

    def lane_tiles(x):
        return [x[:, j * LANES:(j + 1) * LANES] for j in range(x.shape[1] // LANES)]

    def row_parts(tq):
        step = min(tq, ROW_PART)
        return [(r0, step) for r0 in range(0, tq, step)]

    def softmax_chunk(q0, tq, k0, tk, masked, first):
        for hh in range(2):
            qh = q_ref[0, pl.ds(q0, tq), hh * LANES:(hh + 1) * LANES]
            kh = k_ref[0, pl.ds(k0, tk), hh * LANES:(hh + 1) * LANES]
            s_all = _dot_nt(qh, kh)
            vh = head_v(k0, tk, hh)
            for r0, tr in row_parts(tq):
                s = s_all[r0:r0 + tr]
                if masked:
                    pq = q0 + r0 + lax.broadcasted_iota(jnp.int32, (tr, 1), 0)
                    s = jnp.where(causal(pq, k0, tk), s, NEG)
                tiles = lane_tiles(s)
                top = tiles[0]
                for x in tiles[1:]:
                    top = jnp.maximum(top, x)
                m_new = jnp.broadcast_to(jnp.max(top, axis=-1, keepdims=True), (tr, LANES))
                if not first:
                    m_old = stat_ref[hh, r0:r0 + tr, :]
                    m_new = jnp.maximum(m_old, m_new)
                p = jnp.concatenate([jnp.exp2(x - m_new) for x in tiles], axis=1).astype(BF16)
                pv = _dot(p, vh)
                if not first:
                    pv = jnp.exp2(m_old - m_new) * acc_ref[hh, r0:r0 + tr, :] + pv
                stat_ref[hh, r0:r0 + tr, :] = m_new
                acc_ref[hh, r0:r0 + tr, :] = pv

    def stick_chunk(q0, tq, k0, tk, masked, first):
        later = later2 if tk == K_TILE else later1
        for hh in range(2):
            qh = q_ref[0, pl.ds(q0, tq), hh * LANES:(hh + 1) * LANES]
            kh = k_ref[0, pl.ds(k0, tk), hh * LANES:(hh + 1) * LANES]
            z = _dot_nt(qh, kh)
            ls_pos = jnp.minimum(z, 0.0) - jnp.log(1.0 + jnp.exp2(-jnp.abs(z))) * LOG2E
            log_keep = ls_pos - z
            if masked:
                pq = q0 + lax.broadcasted_iota(jnp.int32, (tq, 1), 0)
                vis = causal(pq, k0, tk)
                log_keep = jnp.where(vis, log_keep, 0.0)
            hi = log_keep.astype(BF16)
            lo = (log_keep - hi.astype(F32)).astype(BF16)
            after = _dot(jnp.concatenate([hi, lo], axis=1), later)
            tot = ls_pos + after
            chunk_total = jnp.broadcast_to(after[:, 0:1] + log_keep[:, 0:1], (tq, LANES))
            if not first:
                carry = stat_ref[hh, 0:tq, :]
                tot = jnp.concatenate([x + carry for x in lane_tiles(tot)], axis=1)
                chunk_total = carry + chunk_total
            a = jnp.exp2(tot)
            if masked:
                a = jnp.where(vis, a, 0.0)
            pv = _dot(a.astype(BF16), head_v(k0, tk, hh))
            if not first:
                pv = acc_ref[hh, 0:tq, :] + pv
            stat_ref[hh, 0:tq, :] = chunk_total
            acc_ref[hh, 0:tq, :] = pv

    def finish(q0, tq):
        a0, a1 = acc_ref[0, 0:tq, :], acc_ref[1, 0:tq, :]
        if mode != "sb":
            a0 = a0 / a0[:, HEAD_DIM:HEAD_DIM + 1]
            a1 = a1 / a1[:, 0:1]
        o_ref[0, pl.ds(q0, tq), :] = jnp.where(lo_half, a0, a1).astype(BF16)

    def chunk_start(j):
        return pl.multiple_of(BLOCK + j * K_TILE, BLOCK)

    if mode == "sb":
        stick_chunk(0, BLOCK, 0, BLOCK, True, True)
        finish(0, BLOCK)

        def q_body(i, _):
            q0 = pl.multiple_of(BLOCK + i * Q_TILE, BLOCK)
            n_int = i * per_tile
            for d in range(per_tile):
                stick_chunk(q0, Q_TILE, chunk_start(n_int + per_tile - 1 - d), K_TILE, True, d == 0)

            def alive():
                top = jnp.maximum(jnp.max(stat_ref[0]), jnp.max(stat_ref[1]))
                return (top > UNDERFLOW_LOG2).astype(jnp.int32)

            def body(st):
                jj, _ = st
                for d in range(per_tile):
                    stick_chunk(q0, Q_TILE, chunk_start(n_int - 1 - jj * per_tile - d), K_TILE, False, False)
                return jj + 1, alive()

            _, go = lax.while_loop(lambda st: (st[0] < i) & (st[1] > 0), body, (0, alive()))

            @pl.when(go > 0)
            def _():
                stick_chunk(q0, Q_TILE, 0, BLOCK, False, False)

            finish(q0, Q_TILE)
            return 0
    else:
        softmax_chunk(0, BLOCK, 0, BLOCK, True, True)
        finish(0, BLOCK)

        def q_body(i, _):
            q0 = pl.multiple_of(BLOCK + i * Q_TILE, BLOCK)
            n_int = i * per_tile
            softmax_chunk(q0, Q_TILE, 0, BLOCK, False, True)

            def body(j, _):
                for d in range(per_tile):
                    softmax_chunk(q0, Q_TILE, chunk_start(j * per_tile + d), K_TILE, False, False)
                return 0

            lax.fori_loop(0, i, body, 0)
            for d in range(per_tile):
                softmax_chunk(q0, Q_TILE, chunk_start(n_int + d), K_TILE, True, False)
            finish(q0, Q_TILE)
            return 0

    lax.fori_loop(0, n_qt, q_body, 0)


def _causal_attention(mode, q, k, v, batch, seq_len):
    wide = pl.BlockSpec((1, seq_len, 2 * LANES), lambda b, p: (b, 0, p))
    narrow = pl.BlockSpec((1, seq_len, LANES), lambda b, p: (b, 0, p))
    args = [q.reshape(batch, seq_len, 512), k.reshape(batch, seq_len, 512),
            v.reshape(batch, seq_len, v.shape[1])]
    return pl.pallas_call(
        functools.partial(_causal_kernel, mode),
        grid=(batch, 2),
        in_specs=[wide, wide, narrow if mode == "sb" else wide],
        out_specs=narrow,
        out_shape=jax.ShapeDtypeStruct((batch, seq_len, 2 * LANES), BF16),
        scratch_shapes=[pltpu.VMEM((2, Q_TILE, LANES), F32), pltpu.VMEM((2, Q_TILE, LANES), F32)],
        compiler_params=pltpu.CompilerParams(dimension_semantics=("arbitrary", "arbitrary"),
                                             vmem_limit_bytes=VMEM_LIMIT),
        name=mode + "_attention",
    )(*args)


def _outproj_kernel(ya_ref, yb_ref, yc_ref, yd_ref, h_ref, wo_ref, g_ref, wrh_ref, wrl_ref, br_ref, tri_ref,
                    h2_ref, xn_ref, route_ref, cnt_ref):
    o = (_dot(ya_ref[...], wo_ref[0]) + _dot(yb_ref[...], wo_ref[1])
         + _dot(yc_ref[...], wo_ref[2]) + _dot(yd_ref[...], wo_ref[3]))
    h2 = h_ref[...] + o
    h2_ref[...] = h2
    xn = _rms(h2, g_ref[...])
    _rows_to_tiles(xn_ref, (), xn)
    xh = xn.astype(BF16)
    xl = (xn - xh.astype(F32)).astype(BF16)
    wrh, wrl = wrh_ref[...], wrl_ref[...]
    lg = _dot(xh, wrh) + _dot(xl, wrh) + _dot(xh, wrl) + br_ref[...]

    tm = lg.shape[0]
    lane = lax.broadcasted_iota(jnp.int32, (tm, LANES), 1)
    ninf = -jnp.inf

    def first_max(x):
        top = jnp.max(x, axis=-1, keepdims=True)
        return top, jnp.min(jnp.where(x == top, lane, LANES), axis=-1, keepdims=True)

    gl = jnp.where(lane < N_GROUPS, lg, ninf)
    g_max, g_top = first_max(gl)
    g_w = 1.0 / jnp.sum(jnp.exp(gl - g_max), axis=-1, keepdims=True)
    e_lo = N_GROUPS + g_top * EXPERTS_PER_GROUP
    el = jnp.where((lane >= e_lo) & (lane < e_lo + EXPERTS_PER_GROUP), lg, ninf)
    v1, i1 = first_max(el)
    v2, i2 = first_max(jnp.where(lane == i1, ninf, el))
    r21 = jnp.exp(v2 - v1)
    w1 = g_w / (1.0 + r21)
    w2 = w1 * r21

    @pl.when(pl.program_id(0) == 0)
    def _():
        cnt_ref[...] = jnp.zeros_like(cnt_ref)

    m1 = jnp.where(lane == i1, 1.0, 0.0)
    m2 = jnp.where(lane == i2, 1.0, 0.0)
    both = m1 + m2
    before = _dot(tri_ref[...], both.astype(BF16)) + cnt_ref[0:1, :]
    rank1 = jnp.sum(m1 * before, axis=-1, keepdims=True)
    rank2 = jnp.sum(m2 * before, axis=-1, keepdims=True)
    cnt_ref[...] = cnt_ref[...] + jnp.sum(both, axis=0, keepdims=True)
    cols = [w1, w2, (i1 - N_GROUPS).astype(F32), (i2 - N_GROUPS).astype(F32), rank1, rank2]
    route = jnp.zeros((tm, LANES), F32)
    for j, c in enumerate(cols):
        route = jnp.where(lane == j, c, route)
    route_ref[...] = route


def _outproj(ys, h, p):
    t, d = h.shape
    tm = ROW_TILE
    row = lambda i: (i, 0)
    fixed2 = lambda i: (0, 0)
    in_specs = [pl.BlockSpec((tm, 256), row)] * 4 + [
        pl.BlockSpec((tm, d), row),
        pl.BlockSpec((4, 256, d), lambda i: (0, 0, 0)),
        pl.BlockSpec((1, d), fixed2),
        pl.BlockSpec((d, LANES), fixed2),
        pl.BlockSpec((d, LANES), fixed2),
        pl.BlockSpec((1, LANES), fixed2),
        pl.BlockSpec((tm, tm), fixed2),
    ]
    earlier = jnp.asarray(np.tril(np.ones((tm, tm), np.float32), -1), BF16)
    return pl.pallas_call(
        _outproj_kernel,
        grid=(t // tm,),
        in_specs=in_specs,
        out_specs=[pl.BlockSpec((tm, d), row), pl.BlockSpec((tm * ROW_SUB, LANES), row),
                   pl.BlockSpec((tm, LANES), row), pl.BlockSpec((8, LANES), fixed2)],
        out_shape=[jax.ShapeDtypeStruct((t, d), F32), jax.ShapeDtypeStruct((t * ROW_SUB, LANES), F32),
                   jax.ShapeDtypeStruct((t, LANES), F32), jax.ShapeDtypeStruct((8, LANES), F32)],
        compiler_params=pltpu.CompilerParams(dimension_semantics=("arbitrary",),
                                             vmem_limit_bytes=VMEM_LIMIT),
        name="outproj_router",
    )(*ys, h, p["w_out"], p["ffn_norm"], p["w_r_hi"], p["w_r_lo"], p["b_r"], earlier)


def _moe_kernel(be_ref, tok_ref, dst_ref, x_hbm, wg_ref, wu_ref, wd_ref, y_hbm,
                xbuf, ybuf, sem_in, sem_out):
    s = pl.program_id(0)
    blk = MOE_BLOCK * ROW_SUB
    trash0 = y_hbm.shape[0] - blk

    def block_in(buf_slot):
        return pltpu.make_async_copy(x_hbm.at[pl.ds(0, blk)], xbuf.at[buf_slot], sem_in.at[buf_slot])

    def block_out(buf_slot):
        return pltpu.make_async_copy(ybuf.at[buf_slot], y_hbm.at[pl.ds(trash0, blk)], sem_out.at[buf_slot])

    @pl.when(s == 0)
    def _():
        ybuf[...] = jnp.zeros_like(ybuf)
        block_in(0).start()
        block_out(0).start()

    def step(slot):
        other = 1 - slot
        for r in range(MOE_BLOCK):
            src = pl.multiple_of(tok_ref[0, 0, r], ROW_SUB)
            pltpu.make_async_copy(x_hbm.at[pl.ds(src, ROW_SUB)], xbuf.at[other, pl.ds(r * ROW_SUB, ROW_SUB)],
                                  sem_in.at[other]).start(priority=r % 2)
        for r in range(MOE_BLOCK):
            dst = pl.multiple_of(dst_ref[0, 0, r], ROW_SUB)
            pltpu.make_async_copy(ybuf.at[other, pl.ds(r * ROW_SUB, ROW_SUB)], y_hbm.at[pl.ds(dst, ROW_SUB)],
                                  sem_out.at[other]).start(priority=r % 2)
        block_in(slot).wait()
        x = _rows_from_tiles(xbuf, (slot,), MOE_BLOCK).astype(BF16)
        gate = _dot(x, wg_ref[0])
        up = _dot(x, wu_ref[0])
        hid = (gate * (1.0 / (1.0 + jnp.exp(-gate))) * up).astype(BF16)
        y = _dot(hid, wd_ref[0])
        block_out(slot).wait()
        _rows_to_tiles(ybuf, (slot,), y)

        @pl.when(s == pl.num_programs(0) - 1)
        def _():
            block_in(other).wait()
            block_out(other).wait()

    for parity in range(2):
        pl.when(s % 2 == parity)(functools.partial(step, parity))


def _moe(xn, tok_ext, dst_ext, be_ext, p):
    t, d = xn.shape[0] // ROW_SUB, xn.shape[1] * ROW_SUB
    n_steps = be_ext.shape[0]
    hdim = p["w_gate"].shape[2]
    wspec = lambda shape: pl.BlockSpec((1,) + shape, lambda s, be: (be[s], 0, 0))
    ids = pl.BlockSpec((1, 1, MOE_BLOCK), lambda s, be: (s, 0, 0), memory_space=pltpu.SMEM)
    return pl.pallas_call(
        _moe_kernel,
        grid_spec=pltpu.PrefetchScalarGridSpec(
            num_scalar_prefetch=1,
            grid=(n_steps,),
            in_specs=[ids, ids, pl.BlockSpec(memory_space=pl.ANY),
                      wspec((d, hdim)), wspec((d, hdim)), wspec((hdim, d))],
            out_specs=pl.BlockSpec(memory_space=pl.ANY),
            scratch_shapes=[pltpu.VMEM((2, MOE_BLOCK * ROW_SUB, LANES), F32),
                            pltpu.VMEM((2, MOE_BLOCK * ROW_SUB, LANES), F32),
                            pltpu.SemaphoreType.DMA((2,)), pltpu.SemaphoreType.DMA((2,))],
        ),
        out_shape=jax.ShapeDtypeStruct(((t * TOP_K + MOE_TRASH_ROWS) * ROW_SUB, LANES), F32),
        compiler_params=pltpu.CompilerParams(dimension_semantics=("arbitrary",),
                                             vmem_limit_bytes=VMEM_LIMIT),
        name="moe_experts",
    )(be_ext, tok_ext, dst_ext, xn, p["w_gate"], p["w_up"], p["w_down"])


def _route(route, counts, t):
    a = t * TOP_K
    expert = route[:, 2:4].astype(jnp.int32).reshape(a)
    pos = route[:, 4:6].astype(jnp.int32).reshape(a)
    counts = counts.astype(jnp.int32)
    padded = (counts + MOE_BLOCK - 1) // MOE_BLOCK * MOE_BLOCK
    pad_end = jnp.cumsum(padded)
    pad_start = pad_end - padded
    dest = pad_start[expert] + pos
    n_rows = a + N_EXPERTS * MOE_BLOCK
    n_blk = n_rows // MOE_BLOCK
    rows_asg = jnp.full((n_rows,), -1, jnp.int32).at[dest].set(jnp.arange(a, dtype=jnp.int32))
    block_e = jnp.minimum(jnp.searchsorted(pad_end, jnp.arange(n_blk, dtype=jnp.int32) * MOE_BLOCK, side="right"),
                          N_EXPERTS - 1).astype(jnp.int32)
    n_steps = n_blk + 2
    asg = rows_asg.reshape(n_blk, MOE_BLOCK)
    none = jnp.full((2, MOE_BLOCK), -1, jnp.int32)
    tok_ext = jnp.maximum(jnp.concatenate([asg, none], axis=0), 0) >> 1
    asg_out = jnp.concatenate([none, asg], axis=0)
    step = jnp.arange(n_steps, dtype=jnp.int32)[:, None]
    trash = (a + (step % (MOE_TRASH_ROWS // MOE_BLOCK - 1)) * MOE_BLOCK
             + jnp.arange(MOE_BLOCK, dtype=jnp.int32)[None, :])
    dst_ext = jnp.where(asg_out >= 0, (asg_out & 1) * t + (asg_out >> 1), trash)
    be_ext = block_e[jnp.clip(step[:, 0] - 1, 0, n_blk - 1)]
    return ((tok_ext * ROW_SUB).reshape(n_steps, 1, MOE_BLOCK),
            (dst_ext * ROW_SUB).reshape(n_steps, 1, MOE_BLOCK), be_ext)


def _final_kernel(h_ref, y0_ref, y1_ref, ew_ref, g_ref, o_ref):
    ew = ew_ref[...]
    n = h_ref.shape[0]
    h = h_ref[...] + ew[:, 0:1] * _rows_from_tiles(y0_ref, (), n) + ew[:, 1:2] * _rows_from_tiles(y1_ref, (), n)
    o_ref[0] = _rms(h, g_ref[...])


def _final(h, y2, ew, g, batch, seq_len):
    t, d = h.shape
    per_seq = seq_len // BLOCK
    row = lambda b, i: (b * per_seq + i + 1, 0)
    return pl.pallas_call(
        _final_kernel,
        grid=(batch, per_seq - 1),
        in_specs=[pl.BlockSpec((BLOCK, d), row),
                  pl.BlockSpec((BLOCK * ROW_SUB, LANES), row),
                  pl.BlockSpec((BLOCK * ROW_SUB, LANES), lambda b, i: (b * per_seq + i + 1 + t // BLOCK, 0)),
                  pl.BlockSpec((BLOCK, LANES), row),
                  pl.BlockSpec((1, d), lambda b, i: (0, 0))],
        out_specs=pl.BlockSpec((1, BLOCK, d), lambda b, i: (b, i, 0)),
        out_shape=jax.ShapeDtypeStruct((batch, seq_len - BLOCK, d), F32),
        compiler_params=pltpu.CompilerParams(dimension_semantics=("arbitrary", "arbitrary"),
                                             vmem_limit_bytes=VMEM_LIMIT),
        name="final_norm",
    )(h, y2, y2, ew, g)


def _rope_table(seq_len):
    half = MLA_ROPE // 2
    pos = (jnp.arange(seq_len, dtype=jnp.int32) - PAD_FRONT).astype(F32)
    inv_freq = ROPE_THETA ** (-jnp.arange(half, dtype=F32) / half)
    ang = pos[:, None] * inv_freq[None, :]
    cos, sin = jnp.cos(ang), jnp.sin(ang)
    cos2 = jnp.concatenate([cos, cos], axis=1)
    sin2 = jnp.concatenate([-sin, sin], axis=1)
    z = lambda w: jnp.zeros((seq_len, w), F32)
    scale = (MLA_NOPE + MLA_ROPE) ** -0.5 * LOG2E
    cos_q = jnp.concatenate([jnp.ones((seq_len, MLA_NOPE), F32), cos2, z(32)], axis=1) * scale
    sin_q = jnp.concatenate([z(MLA_NOPE), sin2, z(32)], axis=1) * scale
    cos_k = jnp.concatenate([z(MLA_NOPE), cos2, z(32)], axis=1)
    sin_k = jnp.concatenate([z(MLA_NOPE), sin2, z(32)], axis=1)
    return jnp.concatenate([cos_q, sin_q, cos_k, sin_k], axis=1)


def _swap_halves(w):
    half = w.shape[-1] // 2
    return jnp.concatenate([w[..., half:], w[..., :half]], axis=-1)


def _layer_params(i, seq_len, attn_norm, w_in, b_forget, sinks, mla_q_norm, mla_kv_norm, mla_w_uq,
                  mla_w_ukv, w_out, ffn_norm, w_group, b_group, w_router, b_router, w_gate, w_up, w_down):
    d = w_in.shape[1]
    w = w_in[i]
    sizes = (256, 128, 128, 256, 256, 256, 4, 256, 128, 32, 256, 256, 256)
    offs = np.concatenate([[0], np.cumsum(sizes)])
    (a_q, a_k, a_v, f_q, f_k, f_v, f_g, c_q, c_kv, c_kr, s_q, s_k, s_v) = [
        w[:, offs[j]:offs[j + 1]] for j in range(len(sizes))]
    qscale = HEAD_DIM ** -0.5
    grp = SWA_HEADS // SWA_KV_HEADS
    a_q = a_q.reshape(d, SWA_KV_HEADS, grp, HEAD_DIM).transpose(0, 2, 1, 3).reshape(d, 256)
    z = lambda n: jnp.zeros((d, n), F32)
    g_grp = jnp.concatenate([f_g[:, 0:2], z(62), c_kr, z(32)], axis=1)
    gs_grp = jnp.concatenate([f_g[:, 2:4], z(62), _swap_halves(c_kr), z(32)], axis=1)
    w_perm = jnp.concatenate([a_q * qscale, a_k, a_v, f_q * qscale, f_k, f_v, s_q * qscale, s_k, s_v,
                              c_q, c_kv, g_grp, gs_grp], axis=1).astype(BF16)
    wuq = mla_w_uq[i].reshape(MLA_Q_LORA, 4, MLA_NOPE + MLA_ROPE)
    zq = lambda n: jnp.zeros((MLA_Q_LORA, 4, n), F32)
    w_uq_a = jnp.concatenate([wuq, zq(32)], axis=2).reshape(MLA_Q_LORA, 512).astype(BF16)
    w_uq_b = jnp.concatenate([zq(MLA_NOPE), _swap_halves(wuq[:, :, MLA_NOPE:]), zq(32)],
                             axis=2).reshape(MLA_Q_LORA, 512).astype(BF16)
    wukv = mla_w_ukv[i].reshape(MLA_KV_LORA, 4, MLA_NOPE + MLA_V)
    w_kv_k = jnp.concatenate([wukv[:, :, :MLA_NOPE], jnp.zeros((MLA_KV_LORA, 4, 64), F32)],
                             axis=2).reshape(MLA_KV_LORA, 512).astype(BF16)
    w_kv_v = wukv[:, :, MLA_NOPE:].reshape(MLA_KV_LORA, 256).astype(BF16)
    bf = b_forget[i].astype(F32)
    b_f = jnp.zeros((1, 256), F32).at[0, 0:2].set(bf[0:2]).at[0, 128:130].set(bf[2:4])
    wo = w_out[i]
    wo_a = wo[:256].reshape(SWA_KV_HEADS, grp, HEAD_DIM, d).transpose(1, 0, 2, 3).reshape(256, d)
    wo4 = jnp.concatenate([wo_a, wo[256:]], axis=0).reshape(4, 256, d).astype(BF16)
    w_r = jnp.concatenate([w_group[i], w_router[i], jnp.zeros((d, LANES - N_GROUPS - N_EXPERTS), F32)], axis=1)
    w_r_hi = w_r.astype(BF16)
    w_r_lo = (w_r - w_r_hi.astype(F32)).astype(BF16)
    b_r = jnp.concatenate([b_group[i], b_router[i], jnp.zeros((LANES - N_GROUPS - N_EXPERTS,), F32)])[None, :]
    return dict(
        attn_norm=attn_norm[i][None, :], w_in=w_perm, q_norm=mla_q_norm[i][None, :],
        kv_norm=mla_kv_norm[i][None, :], w_uq_a=w_uq_a, w_uq_b=w_uq_b, w_kv_k=w_kv_k, w_kv_v=w_kv_v,
        rope_tab=_rope_table(seq_len), b_forget=b_f, sinks=sinks[i].astype(F32), w_out=wo4,
        ffn_norm=ffn_norm[i][None, :], w_r_hi=w_r_hi, w_r_lo=w_r_lo, b_r=b_r.astype(F32),
        w_gate=w_gate[i].astype(BF16), w_up=w_up[i].astype(BF16), w_down=w_down[i].astype(BF16))


def kernel(x, meta_tokens, attn_norm, w_in, b_forget, sinks, mla_q_norm, mla_kv_norm, mla_w_uq, mla_w_ukv,
           w_out, ffn_norm, w_group, b_group, w_router, b_router, w_gate, w_up, w_down, final_norm):
    batch, seq, d = x.shape
    seq_len = seq + BLOCK
    assert seq_len % ROW_TILE == 0 and seq % Q_TILE == 0 and seq_len % (BLOCK * SWA_Q_BLOCKS) == 0
    t = batch * seq_len
    depth = w_in.shape[0]
    pad = jnp.zeros((batch, PAD_FRONT, d), x.dtype)
    meta = jnp.broadcast_to(meta_tokens.astype(x.dtype)[None], (batch, N_META, d))
    h = jnp.concatenate([pad, meta, x], axis=1).reshape(t, d)
    y2 = ew = None
    for i in range(depth):
        p = _layer_params(i, seq_len, attn_norm, w_in, b_forget, sinks, mla_q_norm, mla_kv_norm, mla_w_uq,
                          mla_w_ukv, w_out, ffn_norm, w_group, b_group, w_router, b_router, w_gate, w_up, w_down)
        h, (qa, fq, fk, fv, cq, ck, cv, sq, sk, sv) = _inproj(h, y2, ew, p, seq_len)
        y_a = _swa_attention(qa, p["sinks"], batch, seq_len)
        y_b = _causal_attention("fox", fq, fk, fv, batch, seq_len)
        y_c = _causal_attention("mla", cq, ck, cv, batch, seq_len)
        y_d = _causal_attention("sb", sq, sk, sv, batch, seq_len)
        ys = [y.reshape(t, 256) for y in (y_a, y_b, y_c, y_d)]
        h, xn, ew, counts = _outproj(ys, h, p)
        tok_ext, dst_ext, be_ext = _route(ew, counts[0, N_GROUPS:N_GROUPS + N_EXPERTS], t)
        y2 = _moe(xn, tok_ext, dst_ext, be_ext, p)
    return _final(h, y2, ew, final_norm[None, :], batch, seq_len)
```

```python
import functools

import jax
import jax.numpy as jnp
import numpy as np
from jax import lax
from jax.experimental import pallas as pl
from jax.experimental.pallas import tpu as pltpu

F32 = jnp.float32
BF16 = jnp.bfloat16

BLOCK = 128
N_META = 16
PAD_FRONT = BLOCK - N_META
CHUNK_SHIFT = 6
HEAD_DIM = 64
NORM_EPS = 1e-6
NEG = -1e30
PAD_KEY_LOGIT = -(2.0 ** 100)
UNDERFLOW_LOG2 = -150.0
LOG2E = 1.4426950408889634
BIG = 1 << 30
SWA_HEADS, SWA_KV_HEADS, WINDOW = 4, 2, 128
MLA_Q_LORA, MLA_KV_LORA, MLA_NOPE, MLA_ROPE, MLA_V = 256, 128, 64, 32, 64
MLA_BIAS_LANE = MLA_NOPE + MLA_ROPE
ROPE_THETA = 10000.0
N_GROUPS, EXPERTS_PER_GROUP, TOP_K = 4, 8, 2
N_EXPERTS = N_GROUPS * EXPERTS_PER_GROUP
MOE_BLOCK = 256
MOE_TRASH_ROWS = 8 * MOE_BLOCK
LANES = 128
ROW_SUB = 8
ROW_TILE = 384
Q_TILE = 1024
K_TILE = 256
ROW_PART = 256
SWA_Q_BLOCKS = 3
VMEM_LIMIT = 56 * 1024 * 1024

C_A, C_B, C_D, C_CQ, C_CKV, C_G, C_GS, C_END = 0, 512, 1280, 2048, 2304, 2432, 2560, 2688
B_F0, B_F1, B_PAD = 0, 3, 6


def _rms(x, g):
    return x * lax.rsqrt(jnp.mean(x * x, axis=-1, keepdims=True) + NORM_EPS) * g


def _log_sigmoid(x):
    return jnp.minimum(x, 0.0) - jnp.log(1.0 + jnp.exp(-jnp.abs(x)))


def _dot(a, b):
    return jnp.dot(a, b, preferred_element_type=F32)


def _dot_nt(a, b):
    return lax.dot_general(a, b, (((1,), (1,)), ((), ())), preferred_element_type=F32)


def _rows_from_tiles(ref, lead, n):
    return jnp.concatenate([ref[(*lead, pl.ds(j, n, stride=ROW_SUB), slice(None))] for j in range(ROW_SUB)], axis=1)


def _rows_to_tiles(ref, lead, x):
    n = x.shape[0]
    for j in range(ROW_SUB):
        ref[(*lead, pl.ds(j, n, stride=ROW_SUB), slice(None))] = x[:, j * LANES:(j + 1) * LANES]


def _tile4(x):
    return jnp.concatenate([x, x, x, x], axis=1)


def _split3(x):
    hi = x.astype(BF16)
    r1 = x - hi.astype(F32)
    mid = r1.astype(BF16)
    lo = (r1 - mid.astype(F32)).astype(BF16)
    return hi, mid, lo


def _free_base(head):
    return head * LANES + (HEAD_DIM if head % 2 == 0 else 0)


def _inproj_kernel(has_y2, n_seq_tiles, *refs):
    if has_y2:
        (h_ref, y0_ref, y1_ref, ew_ref, *rest) = refs
    else:
        (h_ref, *rest) = refs
    (g_ref, w_ref, qn_ref, kvn_ref, wuqa_ref, wuqb_ref, wkvk_ref, wkvv_ref, tab_ref, bf_ref,
     pq_ref, pk_ref, rows_ref, *outs) = rest
    if has_y2:
        hout_ref, *outs = outs
    (qa_ref, fq_ref, fk_ref, fv_ref, cq_ref, ck_ref, cv_ref, sq_ref, sk_ref, sv_ref, carry_ref) = outs
    tile = pl.program_id(0) % n_seq_tiles
    h = h_ref[...]
    tm, d = h.shape
    if has_y2:
        ew = ew_ref[...]
        h = h + ew[:, 0:1] * _rows_from_tiles(y0_ref, (), tm) + ew[:, 1:2] * _rows_from_tiles(y1_ref, (), tm)
        hout_ref[...] = h
    xn = _rms(h, g_ref[...]).astype(BF16)
    acc = _dot(xn, w_ref[...])
    lane = lax.broadcasted_iota(jnp.int32, (1, LANES), 1)
    lo_half = lane < HEAD_DIM
    pad_col = jnp.where(tile * tm + lax.broadcasted_iota(jnp.int32, (tm, 1), 0) < PAD_FRONT,
                        PAD_KEY_LOGIT, 0.0)
    rows = rows_ref[...]
    fq_one, fk_one, pad_lane, sq_one, mla_one, mla_pad = (rows[j:j + 1] for j in range(6))

    def per_head(x_pair, bias, pair, scale=None):
        x = x_pair if scale is None else x_pair * scale
        even = jnp.where(lo_half, x, bias[:, (2 * pair) * LANES:(2 * pair + 1) * LANES])
        odd = jnp.where(lo_half, bias[:, (2 * pair + 1) * LANES:(2 * pair + 2) * LANES], x)
        return even, odd

    def store_heads(ref, x_off, bias, scale=None):
        for pair in range(2):
            x_pair = acc[:, x_off + pair * LANES:x_off + (pair + 1) * LANES]
            even, odd = per_head(x_pair, bias, pair, scale)
            ref[:, (2 * pair) * LANES:(2 * pair + 1) * LANES] = even.astype(BF16)
            ref[:, (2 * pair + 1) * LANES:(2 * pair + 2) * LANES] = odd.astype(BF16)

    qa_ref[...] = acc[:, C_A:C_B].astype(BF16)

    @pl.when(tile == 0)
    def _():
        carry_ref[...] = jnp.zeros_like(carry_ref)

    lf = _log_sigmoid(acc[:, C_G:C_END] + bf_ref[...]) * LOG2E
    r = lax.broadcasted_iota(jnp.int32, (BLOCK, BLOCK), 0)
    c = lax.broadcasted_iota(jnp.int32, (BLOCK, BLOCK), 1)
    tri = jnp.where(c <= r, 1.0, 0.0).astype(BF16)
    carry = carry_ref[...]
    blocks = []
    for b in range(tm // BLOCK):
        hi, mid, lo = _split3(lf[b * BLOCK:(b + 1) * BLOCK])
        y = _dot(tri, hi) + _dot(tri, mid) + _dot(tri, lo) + carry
        carry = y[BLOCK - 1:BLOCK, :]
        blocks.append(y)
    carry_ref[...] = carry
    f_hi, f_mid, f_lo = _split3(jnp.concatenate(blocks, axis=0))
    q_bias = _dot(f_hi, pq_ref[0]) + _dot(f_mid, pq_ref[1]) + _dot(f_lo, pq_ref[2]) + fq_one
    k_bias = (_dot(f_hi, pk_ref[0]) + _dot(f_mid, pk_ref[1]) + _dot(f_lo, pk_ref[2]) + fk_one
              + pad_col * pad_lane)
    ones = jnp.ones((tm, 4 * LANES), F32)
    store_heads(fq_ref, C_B, q_bias, LOG2E)
    store_heads(fk_ref, C_B + 256, k_bias)
    store_heads(fv_ref, C_B + 512, ones)

    store_heads(sq_ref, C_D, jnp.broadcast_to(sq_one, (tm, 4 * LANES)), LOG2E)
    store_heads(sk_ref, C_D + 256, pad_col * pad_lane)
    sv_ref[...] = acc[:, C_D + 512:C_CQ].astype(BF16)

    tab = tab_ref[...]
    cos_q, sin_q = tab[:, 0:128], tab[:, 128:256]
    cos_k, sin_k = tab[:, 256:384], tab[:, 384:512]
    cqn = _rms(acc[:, C_CQ:C_CKV], qn_ref[...]).astype(BF16)
    q_lin = _dot(cqn, wuqa_ref[...])
    q_swp = _dot(cqn, wuqb_ref[...])
    cq_ref[...] = (q_lin * _tile4(cos_q) + q_swp * _tile4(sin_q) + mla_one).astype(BF16)
    ckvn = _rms(acc[:, C_CKV:C_G], kvn_ref[...]).astype(BF16)
    k_nope = _dot(ckvn, wkvk_ref[...])
    grp, grp_s = acc[:, C_G:C_GS], acc[:, C_GS:C_END]
    k_rope = grp * cos_k + grp_s * sin_k
    ck_ref[...] = (k_nope + _tile4(k_rope) + pad_col * mla_pad).astype(BF16)
    vv = _dot(ckvn, wkvv_ref[...])
    for pair in range(2):
        even, odd = per_head(vv[:, pair * LANES:(pair + 1) * LANES], ones, pair)
        cv_ref[:, (2 * pair) * LANES:(2 * pair + 1) * LANES] = even.astype(BF16)
        cv_ref[:, (2 * pair + 1) * LANES:(2 * pair + 2) * LANES] = odd.astype(BF16)


def _bias_constants():
    src = (0, 1, LANES, LANES + 1)
    pq = np.zeros((3, 2 * LANES, 4 * LANES), np.float32)
    pk = np.zeros((3, 2 * LANES, 4 * LANES), np.float32)
    rows = np.zeros((8, 4 * LANES), np.float32)
    for head in range(4):
        base = _free_base(head)
        for part in range(3):
            pq[part, src[head], base + B_F0 + part] = 1.0
            pk[part, src[head], base + B_F1 + part] = -1.0
            rows[0, base + B_F1 + part] = 1.0
            rows[1, base + B_F0 + part] = 1.0
        rows[0, base + B_PAD] = 1.0
        rows[2, base + B_PAD] = 1.0
        rows[3, base + B_PAD] = 1.0
        rows[4, head * LANES + MLA_BIAS_LANE] = 1.0
        rows[5, head * LANES + MLA_BIAS_LANE] = 1.0
    return jnp.asarray(pq, BF16), jnp.asarray(pk, BF16), jnp.asarray(rows, F32)


def _inproj(h, y2, ew, p, seq_len):
    t, d = h.shape
    tm = ROW_TILE
    n_seq_tiles = seq_len // tm
    has_y2 = y2 is not None
    row = lambda i: (i, 0)
    fixed = lambda i: (0, 0)
    in_specs = [pl.BlockSpec((tm, d), row)]
    args = [h]
    if has_y2:
        in_specs += [pl.BlockSpec((tm * ROW_SUB, LANES), row),
                     pl.BlockSpec((tm * ROW_SUB, LANES), lambda i: (i + t // tm, 0)),
                     pl.BlockSpec((tm, LANES), row)]
        args += [y2, y2, ew]
    pq, pk, rows = _bias_constants()
    consts = [p["attn_norm"], p["w_in"], p["q_norm"], p["kv_norm"], p["w_uq_a"], p["w_uq_b"],
              p["w_kv_k"], p["w_kv_v"]]
    in_specs += [pl.BlockSpec(c.shape, fixed) for c in consts]
    args += consts
    in_specs.append(pl.BlockSpec((tm, 512), lambda i: (i % n_seq_tiles, 0)))
    args.append(p["rope_tab"])
    in_specs += [pl.BlockSpec((1, 256), fixed),
                 pl.BlockSpec(pq.shape, lambda i: (0, 0, 0)),
                 pl.BlockSpec(pk.shape, lambda i: (0, 0, 0)),
                 pl.BlockSpec(rows.shape, fixed)]
    args += [p["b_forget"], pq, pk, rows]
    widths = [512] * 9 + [256]
    out_shape = [jax.ShapeDtypeStruct((t, w), BF16) for w in widths]
    out_specs = [pl.BlockSpec((tm, w), row) for w in widths]
    if has_y2:
        out_shape = [jax.ShapeDtypeStruct((t, d), F32)] + out_shape
        out_specs = [pl.BlockSpec((tm, d), row)] + out_specs
    outs = pl.pallas_call(
        functools.partial(_inproj_kernel, has_y2, n_seq_tiles),
        grid=(t // tm,),
        in_specs=in_specs,
        out_specs=out_specs,
        out_shape=out_shape,
        scratch_shapes=[pltpu.VMEM((1, 2 * LANES), F32)],
        compiler_params=pltpu.CompilerParams(dimension_semantics=("arbitrary",),
                                             vmem_limit_bytes=VMEM_LIMIT),
        name="inproj_y2" if has_y2 else "inproj",
    )(*args)
    if has_y2:
        return outs[0], outs[1:]
    return h, outs


def _swa_kernel(sink_ref, q_ref, km_ref, kp_ref, kc_ref, vm_ref, vp_ref, vc_ref, o_ref):
    i = pl.program_id(1)
    n_sub = q_ref.shape[1] // BLOCK
    lane = lax.broadcasted_iota(jnp.int32, (1, LANES), 1)
    lo_half = lane < HEAD_DIM
    half_masks = [jnp.where(lo_half, 1.0, 0.0).astype(BF16), jnp.where(lo_half, 0.0, 1.0).astype(BF16)]
    row = lax.broadcasted_iota(jnp.int32, (BLOCK, 1), 0)
    col = lax.broadcasted_iota(jnp.int32, (1, BLOCK), 1)
    grp = SWA_HEADS // SWA_KV_HEADS
    k_all = jnp.concatenate([km_ref[0], kp_ref[0], kc_ref[0]], axis=0)
    v_all = jnp.concatenate([vm_ref[0], vp_ref[0], vc_ref[0]], axis=0)
    for j in range(n_sub):
        q0 = (i * n_sub + j) * BLOCK
        pq = q0 + row
        cq = pq >> CHUNK_SHIFT
        segs = []
        vis_m = col >= PAD_FRONT
        segs.append((vis_m, jnp.minimum(jnp.abs(pq - col), WINDOW).astype(F32)))
        for pk in (q0 - BLOCK + col, q0 + col):
            ck = jnp.where(pk >= BLOCK, pk >> CHUNK_SHIFT, BIG)
            vis = (ck <= cq) & (ck >= cq - (WINDOW >> CHUNK_SHIFT))
            segs.append((vis, jnp.abs(pq - pk).astype(F32)))
        kj = jnp.concatenate([k_all[0:BLOCK], k_all[(j + 1) * BLOCK:(j + 3) * BLOCK]], axis=0)
        vj = jnp.concatenate([v_all[0:BLOCK], v_all[(j + 1) * BLOCK:(j + 3) * BLOCK]], axis=0)
        for g in range(grp):
            qg = q_ref[0, j * BLOCK:(j + 1) * BLOCK, g * LANES:(g + 1) * LANES]
            out_g = None
            for hk in range(SWA_KV_HEADS):
                head = hk * grp + g
                slope = 2.0 ** (-8.0 * (head + 1) / SWA_HEADS)
                sink = sink_ref[head]
                s_all = _dot_nt(qg * half_masks[hk], kj)
                tiles = [jnp.where(vis, s_all[:, n * LANES:(n + 1) * LANES] - slope * dist, NEG)
                         for n, (vis, dist) in enumerate(segs)]
                top = jnp.maximum(jnp.maximum(tiles[0], tiles[1]), tiles[2])
                m = jnp.broadcast_to(jnp.maximum(jnp.max(top, axis=-1, keepdims=True), sink), (BLOCK, LANES))
                e = [jnp.exp(x - m) for x in tiles]
                den = jnp.sum(e[0] + e[1] + e[2], axis=-1, keepdims=True) + jnp.exp(sink - m[:, 0:1])
                o = _dot(jnp.concatenate(e, axis=1).astype(BF16), vj) * (1.0 / den)
                out_g = o if hk == 0 else jnp.where(lo_half, out_g, o)
            o_ref[0, j * BLOCK:(j + 1) * BLOCK, g * LANES:(g + 1) * LANES] = out_g.astype(BF16)


def _swa_attention(qa, sinks, batch, seq_len):
    x = qa.reshape(batch, seq_len, 512)
    n_sub = SWA_Q_BLOCKS
    nb = seq_len // (BLOCK * n_sub)
    blk = lambda f: pl.BlockSpec((1, BLOCK, LANES), f)
    own = lambda c: pl.BlockSpec((1, BLOCK * n_sub, LANES), lambda b, i: (b, i, c))
    return pl.pallas_call(
        _swa_kernel,
        grid=(batch, nb),
        in_specs=[
            pl.BlockSpec(memory_space=pltpu.SMEM),
            pl.BlockSpec((1, BLOCK * n_sub, 2 * LANES), lambda b, i: (b, i, 0)),
            blk(lambda b, i: (b, 0, 2)),
            blk(lambda b, i: (b, jnp.maximum(i * n_sub - 1, 0), 2)),
            own(2),
            blk(lambda b, i: (b, 0, 3)),
            blk(lambda b, i: (b, jnp.maximum(i * n_sub - 1, 0), 3)),
            own(3),
        ],
        out_specs=pl.BlockSpec((1, BLOCK * n_sub, 2 * LANES), lambda b, i: (b, i, 0)),
        out_shape=jax.ShapeDtypeStruct((batch, seq_len, 2 * LANES), BF16),
        compiler_params=pltpu.CompilerParams(dimension_semantics=("arbitrary", "arbitrary"),
                                             vmem_limit_bytes=VMEM_LIMIT),
        name="swa_attention",
    )(sinks, x, x, x, x, x, x, x)


def _causal_kernel(mode, q_ref, k_ref, v_ref, o_ref, stat_ref, acc_ref):
    seq_len = q_ref.shape[1]
    n_qt = (seq_len - BLOCK) // Q_TILE
    per_tile = Q_TILE // K_TILE
    lane = lax.broadcasted_iota(jnp.int32, (1, LANES), 1)
    lo_half = lane < HEAD_DIM
    if mode == "sb":
        r = lax.broadcasted_iota(jnp.int32, (2 * K_TILE, K_TILE), 0) & (K_TILE - 1)
        c = lax.broadcasted_iota(jnp.int32, (2 * K_TILE, K_TILE), 1)
        later2 = jnp.where(r > c, 1.0, 0.0).astype(BF16)
        r1 = lax.broadcasted_iota(jnp.int32, (2 * BLOCK, BLOCK), 0) & (BLOCK - 1)
        c1 = lax.broadcasted_iota(jnp.int32, (2 * BLOCK, BLOCK), 1)
        later1 = jnp.where(r1 > c1, 1.0, 0.0).astype(BF16)

    def causal(pq, k0, tk):
        pk = k0 + lax.broadcasted_iota(jnp.int32, (1, tk), 1)
        if mode == "fox":
            return pk <= pq
        if mode == "mla":
            return (pk >> CHUNK_SHIFT) <= (pq >> CHUNK_SHIFT)
        return pk < pq

    def head_v(k0, tk, hh):
        if mode == "sb":
            return v_ref[0, pl.ds(k0, tk), :]
        return v_ref[0, pl.ds(k0, tk), hh * LANES:(hh + 1) * LANES]

    def lane_tiles(x):
        return [x[:, j * LANES:(j + 1) * LANES] for j in range(x.shape[1] // LANES)]

    def row_parts(tq):
        step = min(tq, ROW_PART)
        return [(r0, step) for r0 in range(0, tq, step)]

    def softmax_chunk(q0, tq, k0, tk, masked, first):
        for hh in range(2):
            qh = q_ref[0, pl.ds(q0, tq), hh * LANES:(hh + 1) * LANES]
            kh = k_ref[0, pl.ds(k0, tk), hh * LANES:(hh + 1) * LANES]
            s_all = _dot_nt(qh, kh)
            vh = head_v(k0, tk, hh)
            for r0, tr in row_parts(tq):
                s = s_all[r0:r0 + tr]
                if masked:
                    pq = q0 + r0 + lax.broadcasted_iota(jnp.int32, (tr, 1), 0)
                    s = jnp.where(causal(pq, k0, tk), s, NEG)
                tiles = lane_tiles(s)
                top = tiles[0]
                for x in tiles[1:]:
                    top = jnp.maximum(top, x)
                m_new = jnp.broadcast_to(jnp.max(top, axis=-1, keepdims=True), (tr, LANES))
                if not first:
                    m_old = stat_ref[hh, r0:r0 + tr, :]
                    m_new = jnp.maximum(m_old, m_new)
                p = jnp.concatenate([jnp.exp2(x - m_new) for x in tiles], axis=1).astype(BF16)
                pv = _dot(p, vh)
                if not first:
                    pv = jnp.exp2(m_old - m_new) * acc_ref[hh, r0:r0 + tr, :] + pv
                stat_ref[hh, r0:r0 + tr, :] = m_new
                acc_ref[hh, r0:r0 + tr, :] = pv

    def stick_chunk(q0, tq, k0, tk, masked, first):
        later = later2 if tk == K_TILE else later1
        for hh in range(2):
            qh = q_ref[0, pl.ds(q0, tq), hh * LANES:(hh + 1) * LANES]
            kh = k_ref[0, pl.ds(k0, tk), hh * LANES:(hh + 1) * LANES]
            z = _dot_nt(qh, kh)
            ls_pos = jnp.minimum(z, 0.0) - jnp.log(1.0 + jnp.exp2(-jnp.abs(z))) * LOG2E
            log_keep = ls_pos - z
            if masked:
                pq = q0 + lax.broadcasted_iota(jnp.int32, (tq, 1), 0)
                vis = causal(pq, k0, tk)
                log_keep = jnp.where(vis, log_keep, 0.0)
            hi = log_keep.astype(BF16)
            lo = (log_keep - hi.astype(F32)).astype(BF16)
            after = _dot(jnp.concatenate([hi, lo], axis=1), later)
            tot = ls_pos + after
            chunk_total = jnp.broadcast_to(after[:, 0:1] + log_keep[:, 0:1], (tq, LANES))
            if not first:
                carry = stat_ref[hh, 0:tq, :]
                tot = jnp.concatenate([x + carry for x in lane_tiles(tot)], axis=1)
                chunk_total = carry + chunk_total
            a = jnp.exp2(tot)
            if masked:
                a = jnp.where(vis, a, 0.0)
            pv = _dot(a.astype(BF16), head_v(k0, tk, hh))
            if not first:
                pv = acc_ref[hh, 0:tq, :] + pv
            stat_ref[hh, 0:tq, :] = chunk_total
            acc_ref[hh, 0:tq, :] = pv

    def finish(q0, tq):
        a0, a1 = acc_ref[0, 0:tq, :], acc_ref[1, 0:tq, :]
        if mode != "sb":
            a0 = a0 / a0[:, HEAD_DIM:HEAD_DIM + 1]
            a1 = a1 / a1[:, 0:1]
        o_ref[0, pl.ds(q0, tq), :] = jnp.where(lo_half, a0, a1).astype(BF16)

    def chunk_start(j):
        return pl.multiple_of(BLOCK + j * K_TILE, BLOCK)

    if mode == "sb":
        stick_chunk(0, BLOCK, 0, BLOCK, True, True)
        finish(0, BLOCK)

        def q_body(i, _):
            q0 = pl.multiple_of(BLOCK + i * Q_TILE, BLOCK)
            n_int = i * per_tile
            for d in range(per_tile):
                stick_chunk(q0, Q_TILE, chunk_start(n_int + per_tile - 1 - d), K_TILE, True, d == 0)

            def alive():
                top = jnp.maximum(jnp.max(stat_ref[0]), jnp.max(stat_ref[1]))
                return (top > UNDERFLOW_LOG2).astype(jnp.int32)

            def body(st):
                jj, _ = st
                for d in range(per_tile):
                    stick_chunk(q0, Q_TILE, chunk_start(n_int - 1 - jj * per_tile - d), K_TILE, False, False)
                return jj + 1, alive()

            _, go = lax.while_loop(lambda st: (st[0] < i) & (st[1] > 0), body, (0, alive()))

            @pl.when(go > 0)
            def _():
                stick_chunk(q0, Q_TILE, 0, BLOCK, False, False)

            finish(q0, Q_TILE)
            return 0
    else:
        softmax_chunk(0, BLOCK, 0, BLOCK, True, True)
        finish(0, BLOCK)

        def q_body(i, _):
            q0 = pl.multiple_of(BLOCK + i * Q_TILE, BLOCK)
            n_int = i * per_tile
            softmax_chunk(q0, Q_TILE, 0, BLOCK, False, True)

            def body(j, _):
                for d in range(per_tile):
                    softmax_chunk(q0, Q_TILE, chunk_start(j * per_tile + d), K_TILE, False, False)
                return 0

            lax.fori_loop(0, i, body, 0)
            for d in range(per_tile):
                softmax_chunk(q0, Q_TILE, chunk_start(n_int + d), K_TILE, True, False)
            finish(q0, Q_TILE)
            return 0

    lax.fori_loop(0, n_qt, q_body, 0)


def _causal_attention(mode, q, k, v, batch, seq_len):
    wide = pl.BlockSpec((1, seq_len, 2 * LANES), lambda b, p: (b, 0, p))
    narrow = pl.BlockSpec((1, seq_len, LANES), lambda b, p: (b, 0, p))
    args = [q.reshape(batch, seq_len, 512), k.reshape(batch, seq_len, 512),
            v.reshape(batch, seq_len, v.shape[1])]
    return pl.pallas_call(
        functools.partial(_causal_kernel, mode),
        grid=(batch, 2),
        in_specs=[wide, wide, narrow if mode == "sb" else wide],
        out_specs=narrow,
        out_shape=jax.ShapeDtypeStruct((batch, seq_len, 2 * LANES), BF16),
        scratch_shapes=[pltpu.VMEM((2, Q_TILE, LANES), F32), pltpu.VMEM((2, Q_TILE, LANES), F32)],
        compiler_params=pltpu.CompilerParams(dimension_semantics=("arbitrary", "arbitrary"),
                                             vmem_limit_bytes=VMEM_LIMIT),
        name=mode + "_attention",
    )(*args)


def _outproj_kernel(ya_ref, yb_ref, yc_ref, yd_ref, h_ref, wo_ref, g_ref, wrh_ref, wrl_ref, br_ref, tri_ref,
                    h2_ref, xn_ref, route_ref, cnt_ref):
    o = (_dot(ya_ref[...], wo_ref[0]) + _dot(yb_ref[...], wo_ref[1])
         + _dot(yc_ref[...], wo_ref[2]) + _dot(yd_ref[...], wo_ref[3]))
    h2 = h_ref[...] + o
    h2_ref[...] = h2
    xn = _rms(h2, g_ref[...])
    _rows_to_tiles(xn_ref, (), xn)
    xh = xn.astype(BF16)
    xl = (xn - xh.astype(F32)).astype(BF16)
    wrh, wrl = wrh_ref[...], wrl_ref[...]
    lg = _dot(xh, wrh) + _dot(xl, wrh) + _dot(xh, wrl) + br_ref[...]

    tm = lg.shape[0]
    lane = lax.broadcasted_iota(jnp.int32, (tm, LANES), 1)
    ninf = -jnp.inf

    def first_max(x):
        top = jnp.max(x, axis=-1, keepdims=True)
        return top, jnp.min(jnp.where(x == top, lane, LANES), axis=-1, keepdims=True)

    gl = jnp.where(lane < N_GROUPS, lg, ninf)
    g_max, g_top = first_max(gl)
    g_w = 1.0 / jnp.sum(jnp.exp(gl - g_max), axis=-1, keepdims=True)
    e_lo = N_GROUPS + g_top * EXPERTS_PER_GROUP
    el = jnp.where((lane >= e_lo) & (lane < e_lo + EXPERTS_PER_GROUP), lg, ninf)
    v1, i1 = first_max(el)
    v2, i2 = first_max(jnp.where(lane == i1, ninf, el))
    r21 = jnp.exp(v2 - v1)
    w1 = g_w / (1.0 + r21)
    w2 = w1 * r21

    @pl.when(pl.program_id(0) == 0)
    def _():
        cnt_ref[...] = jnp.zeros_like(cnt_ref)

    m1 = jnp.where(lane == i1, 1.0, 0.0)
    m2 = jnp.where(lane == i2, 1.0, 0.0)
    both = m1 + m2
    before = _dot(tri_ref[...], both.astype(BF16)) + cnt_ref[0:1, :]
    rank1 = jnp.sum(m1 * before, axis=-1, keepdims=True)
    rank2 = jnp.sum(m2 * before, axis=-1, keepdims=True)
    cnt_ref[...] = cnt_ref[...] + jnp.sum(both, axis=0, keepdims=True)
    cols = [w1, w2, (i1 - N_GROUPS).astype(F32), (i2 - N_GROUPS).astype(F32), rank1, rank2]
    route = jnp.zeros((tm, LANES), F32)
    for j, c in enumerate(cols):
        route = jnp.where(lane == j, c, route)
    route_ref[...] = route


def _outproj(ys, h, p):
    t, d = h.shape
    tm = ROW_TILE
    row = lambda i: (i, 0)
    fixed2 = lambda i: (0, 0)
    in_specs = [pl.BlockSpec((tm, 256), row)] * 4 + [
        pl.BlockSpec((tm, d), row),
        pl.BlockSpec((4, 256, d), lambda i: (0, 0, 0)),
        pl.BlockSpec((1, d), fixed2),
        pl.BlockSpec((d, LANES), fixed2),
        pl.BlockSpec((d, LANES), fixed2),
        pl.BlockSpec((1, LANES), fixed2),
        pl.BlockSpec((tm, tm), fixed2),
    ]
    earlier = jnp.asarray(np.tril(np.ones((tm, tm), np.float32), -1), BF16)
    return pl.pallas_call(
        _outproj_kernel,
        grid=(t // tm,),
        in_specs=in_specs,
        out_specs=[pl.BlockSpec((tm, d), row), pl.BlockSpec((tm * ROW_SUB, LANES), row),
                   pl.BlockSpec((tm, LANES), row), pl.BlockSpec((8, LANES), fixed2)],
        out_shape=[jax.ShapeDtypeStruct((t, d), F32), jax.ShapeDtypeStruct((t * ROW_SUB, LANES), F32),
                   jax.ShapeDtypeStruct((t, LANES), F32), jax.ShapeDtypeStruct((8, LANES), F32)],
        compiler_params=pltpu.CompilerParams(dimension_semantics=("arbitrary",),
                                             vmem_limit_bytes=VMEM_LIMIT),
        name="outproj_router",
    )(*ys, h, p["w_out"], p["ffn_norm"], p["w_r_hi"], p["w_r_lo"], p["b_r"], earlier)


def _moe_kernel(be_ref, tok_ref, dst_ref, x_hbm, wg_ref, wu_ref, wd_ref, y_hbm,
                xbuf, ybuf, sem_in, sem_out):
    s = pl.program_id(0)
    blk = MOE_BLOCK * ROW_SUB
    trash0 = y_hbm.shape[0] - blk

    def block_in(buf_slot):
        return pltpu.make_async_copy(x_hbm.at[pl.ds(0, blk)], xbuf.at[buf_slot], sem_in.at[buf_slot])

    def block_out(buf_slot):
        return pltpu.make_async_copy(ybuf.at[buf_slot], y_hbm.at[pl.ds(trash0, blk)], sem_out.at[buf_slot])

    @pl.when(s == 0)
    def _():
        ybuf[...] = jnp.zeros_like(ybuf)
        block_in(0).start()
        block_out(0).start()

    def step(slot):
        other = 1 - slot
        for r in range(MOE_BLOCK):
            src = pl.multiple_of(tok_ref[0, 0, r], ROW_SUB)
            pltpu.make_async_copy(x_hbm.at[pl.ds(src, ROW_SUB)], xbuf.at[other, pl.ds(r * ROW_SUB, ROW_SUB)],
                                  sem_in.at[other]).start(priority=r % 2)
        for r in range(MOE_BLOCK):
            dst = pl.multiple_of(dst_ref[0, 0, r], ROW_SUB)
            pltpu.make_async_copy(ybuf.at[other, pl.ds(r * ROW_SUB, ROW_SUB)], y_hbm.at[pl.ds(dst, ROW_SUB)],
                                  sem_out.at[other]).start(priority=r % 2)
        block_in(slot).wait()
        x = _rows_from_tiles(xbuf, (slot,), MOE_BLOCK).astype(BF16)
        gate = _dot(x, wg_ref[0])
        up = _dot(x, wu_ref[0])
        hid = (gate * (1.0 / (1.0 + jnp.exp(-gate))) * up).astype(BF16)
        y = _dot(hid, wd_ref[0])
        block_out(slot).wait()
        _rows_to_tiles(ybuf, (slot,), y)

        @pl.when(s == pl.num_programs(0) - 1)
        def _():
            block_in(other).wait()
            block_out(other).wait()

    for parity in range(2):
        pl.when(s % 2 == parity)(functools.partial(step, parity))


def _moe(xn, tok_ext, dst_ext, be_ext, p):
    t, d = xn.shape[0] // ROW_SUB, xn.shape[1] * ROW_SUB
    n_steps = be_ext.shape[0]
    hdim = p["w_gate"].shape[2]
    wspec = lambda shape: pl.BlockSpec((1,) + shape, lambda s, be: (be[s], 0, 0))
    ids = pl.BlockSpec((1, 1, MOE_BLOCK), lambda s, be: (s, 0, 0), memory_space=pltpu.SMEM)
    return pl.pallas_call(
        _moe_kernel,
        grid_spec=pltpu.PrefetchScalarGridSpec(
            num_scalar_prefetch=1,
            grid=(n_steps,),
            in_specs=[ids, ids, pl.BlockSpec(memory_space=pl.ANY),
                      wspec((d, hdim)), wspec((d, hdim)), wspec((hdim, d))],
            out_specs=pl.BlockSpec(memory_space=pl.ANY),
            scratch_shapes=[pltpu.VMEM((2, MOE_BLOCK * ROW_SUB, LANES), F32),
                            pltpu.VMEM((2, MOE_BLOCK * ROW_SUB, LANES), F32),
                            pltpu.SemaphoreType.DMA((2,)), pltpu.SemaphoreType.DMA((2,))],
        ),
        out_shape=jax.ShapeDtypeStruct(((t * TOP_K + MOE_TRASH_ROWS) * ROW_SUB, LANES), F32),
        compiler_params=pltpu.CompilerParams(dimension_semantics=("arbitrary",),
                                             vmem_limit_bytes=VMEM_LIMIT),
        name="moe_experts",
    )(be_ext, tok_ext, dst_ext, xn, p["w_gate"], p["w_up"], p["w_down"])


def _route(route, counts, t):
    a = t * TOP_K
    expert = route[:, 2:4].astype(jnp.int32).reshape(a)
    pos = route[:, 4:6].astype(jnp.int32).reshape(a)
    counts = counts.astype(jnp.int32)
    padded = (counts + MOE_BLOCK - 1) // MOE_BLOCK * MOE_BLOCK
    pad_end = jnp.cumsum(padded)
    pad_start = pad_end - padded
    dest = pad_start[expert] + pos
    n_rows = a + N_EXPERTS * MOE_BLOCK
    n_blk = n_rows // MOE_BLOCK
    rows_asg = jnp.full((n_rows,), -1, jnp.int32).at[dest].set(jnp.arange(a, dtype=jnp.int32))
    starts = jnp.arange(n_blk, dtype=jnp.int32) * MOE_BLOCK
    block_e = jnp.minimum(jnp.sum((pad_end[None, :] <= starts[:, None]).astype(jnp.int32), axis=1), N_EXPERTS - 1)
    n_steps = n_blk + 2
    asg = rows_asg.reshape(n_blk, MOE_BLOCK)
    none = jnp.full((2, MOE_BLOCK), -1, jnp.int32)
    tok_ext = jnp.maximum(jnp.concatenate([asg, none], axis=0), 0) >> 1
    asg_out = jnp.concatenate([none, asg], axis=0)
    step = jnp.arange(n_steps, dtype=jnp.int32)[:, None]
    trash = (a + (step % (MOE_TRASH_ROWS // MOE_BLOCK - 1)) * MOE_BLOCK
             + jnp.arange(MOE_BLOCK, dtype=jnp.int32)[None, :])
    dst_ext = jnp.where(asg_out >= 0, (asg_out & 1) * t + (asg_out >> 1), trash)
    be_ext = block_e[jnp.clip(step[:, 0] - 1, 0, n_blk - 1)]
    return ((tok_ext * ROW_SUB).reshape(n_steps, 1, MOE_BLOCK),
            (dst_ext * ROW_SUB).reshape(n_steps, 1, MOE_BLOCK), be_ext)


def _final_kernel(h_ref, y0_ref, y1_ref, ew_ref, g_ref, o_ref):
    ew = ew_ref[...]
    n = h_ref.shape[0]
    h = h_ref[...] + ew[:, 0:1] * _rows_from_tiles(y0_ref, (), n) + ew[:, 1:2] * _rows_from_tiles(y1_ref, (), n)
    o_ref[0] = _rms(h, g_ref[...])


def _final(h, y2, ew, g, batch, seq_len):
    t, d = h.shape
    per_seq = seq_len // BLOCK
    row = lambda b, i: (b * per_seq + i + 1, 0)
    return pl.pallas_call(
        _final_kernel,
        grid=(batch, per_seq - 1),
        in_specs=[pl.BlockSpec((BLOCK, d), row),
                  pl.BlockSpec((BLOCK * ROW_SUB, LANES), row),
                  pl.BlockSpec((BLOCK * ROW_SUB, LANES), lambda b, i: (b * per_seq + i + 1 + t // BLOCK, 0)),
                  pl.BlockSpec((BLOCK, LANES), row),
                  pl.BlockSpec((1, d), lambda b, i: (0, 0))],
        out_specs=pl.BlockSpec((1, BLOCK, d), lambda b, i: (b, i, 0)),
        out_shape=jax.ShapeDtypeStruct((batch, seq_len - BLOCK, d), F32),
        compiler_params=pltpu.CompilerParams(dimension_semantics=("arbitrary", "arbitrary"),
                                             vmem_limit_bytes=VMEM_LIMIT),
        name="final_norm",
    )(h, y2, y2, ew, g)


def _rope_table(seq_len):
    half = MLA_ROPE // 2
    pos = (jnp.arange(seq_len, dtype=jnp.int32) - PAD_FRONT).astype(F32)
    inv_freq = ROPE_THETA ** (-jnp.arange(half, dtype=F32) / half)
    ang = pos[:, None] * inv_freq[None, :]
    cos, sin = jnp.cos(ang), jnp.sin(ang)
    cos2 = jnp.concatenate([cos, cos], axis=1)
    sin2 = jnp.concatenate([-sin, sin], axis=1)
    z = lambda w: jnp.zeros((seq_len, w), F32)
    scale = (MLA_NOPE + MLA_ROPE) ** -0.5 * LOG2E
    cos_q = jnp.concatenate([jnp.ones((seq_len, MLA_NOPE), F32), cos2, z(32)], axis=1) * scale
    sin_q = jnp.concatenate([z(MLA_NOPE), sin2, z(32)], axis=1) * scale
    cos_k = jnp.concatenate([z(MLA_NOPE), cos2, z(32)], axis=1)
    sin_k = jnp.concatenate([z(MLA_NOPE), sin2, z(32)], axis=1)
    return jnp.concatenate([cos_q, sin_q, cos_k, sin_k], axis=1)


def _swap_halves(w):
    half = w.shape[-1] // 2
    return jnp.concatenate([w[..., half:], w[..., :half]], axis=-1)


def _layer_params(i, seq_len, attn_norm, w_in, b_forget, sinks, mla_q_norm, mla_kv_norm, mla_w_uq,
                  mla_w_ukv, w_out, ffn_norm, w_group, b_group, w_router, b_router, w_gate, w_up, w_down):
    d = w_in.shape[1]
    w = w_in[i]
    sizes = (256, 128, 128, 256, 256, 256, 4, 256, 128, 32, 256, 256, 256)
    offs = np.concatenate([[0], np.cumsum(sizes)])
    (a_q, a_k, a_v, f_q, f_k, f_v, f_g, c_q, c_kv, c_kr, s_q, s_k, s_v) = [
        w[:, offs[j]:offs[j + 1]] for j in range(len(sizes))]
    qscale = HEAD_DIM ** -0.5
    grp = SWA_HEADS // SWA_KV_HEADS
    a_q = a_q.reshape(d, SWA_KV_HEADS, grp, HEAD_DIM).transpose(0, 2, 1, 3).reshape(d, 256)
    z = lambda n: jnp.zeros((d, n), F32)
    g_grp = jnp.concatenate([f_g[:, 0:2], z(62), c_kr, z(32)], axis=1)
    gs_grp = jnp.concatenate([f_g[:, 2:4], z(62), _swap_halves(c_kr), z(32)], axis=1)
    w_perm = jnp.concatenate([a_q * qscale, a_k, a_v, f_q * qscale, f_k, f_v, s_q * qscale, s_k, s_v,
                              c_q, c_kv, g_grp, gs_grp], axis=1).astype(BF16)
    wuq = mla_w_uq[i].reshape(MLA_Q_LORA, 4, MLA_NOPE + MLA_ROPE)
    zq = lambda n: jnp.zeros((MLA_Q_LORA, 4, n), F32)
    w_uq_a = jnp.concatenate([wuq, zq(32)], axis=2).reshape(MLA_Q_LORA, 512).astype(BF16)
    w_uq_b = jnp.concatenate([zq(MLA_NOPE), _swap_halves(wuq[:, :, MLA_NOPE:]), zq(32)],
                             axis=2).reshape(MLA_Q_LORA, 512).astype(BF16)
    wukv = mla_w_ukv[i].reshape(MLA_KV_LORA, 4, MLA_NOPE + MLA_V)
    w_kv_k = jnp.concatenate([wukv[:, :, :MLA_NOPE], jnp.zeros((MLA_KV_LORA, 4, 64), F32)],
                             axis=2).reshape(MLA_KV_LORA, 512).astype(BF16)
    w_kv_v = wukv[:, :, MLA_NOPE:].reshape(MLA_KV_LORA, 256).astype(BF16)
    bf = b_forget[i].astype(F32)
    b_f = jnp.zeros((1, 256), F32).at[0, 0:2].set(bf[0:2]).at[0, 128:130].set(bf[2:4])
    wo = w_out[i]
    wo_a = wo[:256].reshape(SWA_KV_HEADS, grp, HEAD_DIM, d).transpose(1, 0, 2, 3).reshape(256, d)
    wo4 = jnp.concatenate([wo_a, wo[256:]], axis=0).reshape(4, 256, d).astype(BF16)
    w_r = jnp.concatenate([w_group[i], w_router[i], jnp.zeros((d, LANES - N_GROUPS - N_EXPERTS), F32)], axis=1)
    w_r_hi = w_r.astype(BF16)
    w_r_lo = (w_r - w_r_hi.astype(F32)).astype(BF16)
    b_r = jnp.concatenate([b_group[i], b_router[i], jnp.zeros((LANES - N_GROUPS - N_EXPERTS,), F32)])[None, :]
    return dict(
        attn_norm=attn_norm[i][None, :], w_in=w_perm, q_norm=mla_q_norm[i][None, :],
        kv_norm=mla_kv_norm[i][None, :], w_uq_a=w_uq_a, w_uq_b=w_uq_b, w_kv_k=w_kv_k, w_kv_v=w_kv_v,
        rope_tab=_rope_table(seq_len), b_forget=b_f, sinks=sinks[i].astype(F32), w_out=wo4,
        ffn_norm=ffn_norm[i][None, :], w_r_hi=w_r_hi, w_r_lo=w_r_lo, b_r=b_r.astype(F32),
        w_gate=w_gate[i].astype(BF16), w_up=w_up[i].astype(BF16), w_down=w_down[i].astype(BF16))


def kernel(x, meta_tokens, attn_norm, w_in, b_forget, sinks, mla_q_norm, mla_kv_norm, mla_w_uq, mla_w_ukv,
           w_out, ffn_norm, w_group, b_group, w_router, b_router, w_gate, w_up, w_down, final_norm):
    batch, seq, d = x.shape
    seq_len = seq + BLOCK
    assert seq_len % ROW_TILE == 0 and seq % Q_TILE == 0 and seq_len % (BLOCK * SWA_Q_BLOCKS) == 0
    t = batch * seq_len
    depth = w_in.shape[0]
    pad = jnp.zeros((batch, PAD_FRONT, d), x.dtype)
    meta = jnp.broadcast_to(meta_tokens.astype(x.dtype)[None], (batch, N_META, d))
    h = jnp.concatenate([pad, meta, x], axis=1).reshape(t, d)
    y2 = ew = None
    for i in range(depth):
        p = _layer_params(i, seq_len, attn_norm, w_in, b_forget, sinks, mla_q_norm, mla_kv_norm, mla_w_uq,
                          mla_w_ukv, w_out, ffn_norm, w_group, b_group, w_router, b_router, w_gate, w_up, w_down)
        h, (qa, fq, fk, fv, cq, ck, cv, sq, sk, sv) = _inproj(h, y2, ew, p, seq_len)
        y_a = _swa_attention(qa, p["sinks"], batch, seq_len)
        y_b = _causal_attention("fox", fq, fk, fv, batch, seq_len)
        y_c = _causal_attention("mla", cq, ck, cv, batch, seq_len)
        y_d = _causal_attention("sb", sq, sk, sv, batch, seq_len)
        ys = [y.reshape(t, 256) for y in (y_a, y_b, y_c, y_d)]
        h, xn, ew, counts = _outproj(ys, h, p)
        tok_ext, dst_ext, be_ext = _route(ew, counts[0, N_GROUPS:N_GROUPS + N_EXPERTS], t)
        y2 = _moe(xn, tok_ext, dst_ext, be_ext, p)
    return _final(h, y2, ew, final_norm[None, :], batch, seq_len)
```

```python
import functools

import jax
import jax.numpy as jnp
import numpy as np
from jax import lax
from jax.experimental import pallas as pl
from jax.experimental.pallas import tpu as pltpu

F32 = jnp.float32
BF16 = jnp.bfloat16

BLOCK = 128
N_META = 16
PAD_FRONT = BLOCK - N_META
CHUNK_SHIFT = 6
HEAD_DIM = 64
NORM_EPS = 1e-6
NEG = -1e30
PAD_KEY_LOGIT = -(2.0 ** 100)
UNDERFLOW_LOG2 = -150.0
LOG2E = 1.4426950408889634
BIG = 1 << 30
SWA_HEADS, SWA_KV_HEADS, WINDOW = 4, 2, 128
MLA_Q_LORA, MLA_KV_LORA, MLA_NOPE, MLA_ROPE, MLA_V = 256, 128, 64, 32, 64
MLA_BIAS_LANE = MLA_NOPE + MLA_ROPE
ROPE_THETA = 10000.0
N_GROUPS, EXPERTS_PER_GROUP, TOP_K = 4, 8, 2
N_EXPERTS = N_GROUPS * EXPERTS_PER_GROUP
MOE_BLOCK = 256
MOE_TRASH_ROWS = 8 * MOE_BLOCK
LANES = 128
ROW_SUB = 8
ROW_TILE = 384
Q_TILES = {"fox": 1024, "mla": 1024, "sb": 512}
K_TILE = 256
ROW_PART = 256
SWA_Q_BLOCKS = 3
VMEM_LIMIT = 56 * 1024 * 1024

C_A, C_B, C_D, C_CQ, C_CKV, C_G, C_GS, C_END = 0, 512, 1280, 2048, 2304, 2432, 2560, 2688
B_F0, B_F1, B_PAD = 0, 3, 6


def _rms(x, g):
    return x * lax.rsqrt(jnp.mean(x * x, axis=-1, keepdims=True) + NORM_EPS) * g


def _log_sigmoid(x):
    return jnp.minimum(x, 0.0) - jnp.log(1.0 + jnp.exp(-jnp.abs(x)))


def _dot(a, b):
    return jnp.dot(a, b, preferred_element_type=F32)


def _dot_nt(a, b):
    return lax.dot_general(a, b, (((1,), (1,)), ((), ())), preferred_element_type=F32)


def _rows_from_tiles(ref, lead, n):
    return jnp.concatenate([ref[(*lead, pl.ds(j, n, stride=ROW_SUB), slice(None))] for j in range(ROW_SUB)], axis=1)


def _rows_to_tiles(ref, lead, x):
    n = x.shape[0]
    for j in range(ROW_SUB):
        ref[(*lead, pl.ds(j, n, stride=ROW_SUB), slice(None))] = x[:, j * LANES:(j + 1) * LANES]


def _tile4(x):
    return jnp.concatenate([x, x, x, x], axis=1)


def _split3(x):
    hi = x.astype(BF16)
    r1 = x - hi.astype(F32)
    mid = r1.astype(BF16)
    lo = (r1 - mid.astype(F32)).astype(BF16)
    return hi, mid, lo


def _free_base(head):
    return head * LANES + (HEAD_DIM if head % 2 == 0 else 0)


def _expert_rows_start(y_hbm, ids_ref, buf, sem, slot):
    for j in range(ids_ref.shape[2]):
        tok, k = divmod(j, TOP_K)
        src = pl.multiple_of(ids_ref[0, 0, j], ROW_SUB)
        pltpu.make_async_copy(y_hbm.at[pl.ds(src, ROW_SUB)], buf.at[slot, k, pl.ds(tok * ROW_SUB, ROW_SUB)],
                              sem.at[slot]).start(priority=j % 2)


def _expert_rows_wait(y_hbm, buf, sem, slot):
    for k in range(TOP_K):
        pltpu.make_async_copy(y_hbm.at[pl.ds(0, buf.shape[2])], buf.at[slot, k], sem.at[slot]).wait()


def _combine_experts(h, ew, y_hbm, ids0_ref, idsn_ref, buf, sem, step, n_steps):
    slot = step % 2

    @pl.when(step == 0)
    def _():
        _expert_rows_start(y_hbm, ids0_ref, buf, sem, 0)

    _expert_rows_start(y_hbm, idsn_ref, buf, sem, 1 - slot)
    _expert_rows_wait(y_hbm, buf, sem, slot)
    n = h.shape[0]
    out = h + ew[:, 0:1] * _rows_from_tiles(buf, (slot, 0), n) + ew[:, 1:2] * _rows_from_tiles(buf, (slot, 1), n)

    @pl.when(step == n_steps - 1)
    def _():
        _expert_rows_wait(y_hbm, buf, sem, 1 - slot)

    return out


def _inproj_kernel(has_y2, n_seq_tiles, *refs):
    if has_y2:
        (h_ref, ids0_ref, idsn_ref, y_hbm, ew_ref, *rest) = refs
    else:
        (h_ref, *rest) = refs
    (g_ref, w_ref, qn_ref, kvn_ref, wuqa_ref, wuqb_ref, wkvk_ref, wkvv_ref, tab_ref, bf_ref,
     pq_ref, pk_ref, rows_ref, *outs) = rest
    if has_y2:
        hout_ref, *outs = outs
        *outs, ybuf, ysem = outs
    (qa_ref, fq_ref, fk_ref, fv_ref, cq_ref, ck_ref, cv_ref, sq_ref, sk_ref, sv_ref, carry_ref) = outs
    tile = pl.program_id(0) % n_seq_tiles
    h = h_ref[...]
    tm, d = h.shape
    if has_y2:
        h = _combine_experts(h, ew_ref[...], y_hbm, ids0_ref, idsn_ref, ybuf, ysem,
                             pl.program_id(0), pl.num_programs(0))
        hout_ref[...] = h
    xn = _rms(h, g_ref[...]).astype(BF16)
    acc = _dot(xn, w_ref[...])
    lane = lax.broadcasted_iota(jnp.int32, (1, LANES), 1)
    lo_half = lane < HEAD_DIM
    pad_col = jnp.where(tile * tm + lax.broadcasted_iota(jnp.int32, (tm, 1), 0) < PAD_FRONT,
                        PAD_KEY_LOGIT, 0.0)
    rows = rows_ref[...]
    fq_one, fk_one, pad_lane, sq_one, mla_one, mla_pad = (rows[j:j + 1] for j in range(6))

    def per_head(x_pair, bias, pair, scale=None):
        x = x_pair if scale is None else x_pair * scale
        even = jnp.where(lo_half, x, bias[:, (2 * pair) * LANES:(2 * pair + 1) * LANES])
        odd = jnp.where(lo_half, bias[:, (2 * pair + 1) * LANES:(2 * pair + 2) * LANES], x)
        return even, odd

    def store_heads(ref, x_off, bias, scale=None):
        for pair in range(2):
            x_pair = acc[:, x_off + pair * LANES:x_off + (pair + 1) * LANES]
            even, odd = per_head(x_pair, bias, pair, scale)
            ref[:, (2 * pair) * LANES:(2 * pair + 1) * LANES] = even.astype(BF16)
            ref[:, (2 * pair + 1) * LANES:(2 * pair + 2) * LANES] = odd.astype(BF16)

    qa_ref[...] = acc[:, C_A:C_B].astype(BF16)

    @pl.when(tile == 0)
    def _():
        carry_ref[...] = jnp.zeros_like(carry_ref)

    lf = _log_sigmoid(acc[:, C_G:C_END] + bf_ref[...]) * LOG2E
    r = lax.broadcasted_iota(jnp.int32, (BLOCK, BLOCK), 0)
    c = lax.broadcasted_iota(jnp.int32, (BLOCK, BLOCK), 1)
    tri = jnp.where(c <= r, 1.0, 0.0).astype(BF16)
    carry = carry_ref[...]
    blocks = []
    for b in range(tm // BLOCK):
        hi, mid, lo = _split3(lf[b * BLOCK:(b + 1) * BLOCK])
        y = _dot(tri, hi) + _dot(tri, mid) + _dot(tri, lo) + carry
        carry = y[BLOCK - 1:BLOCK, :]
        blocks.append(y)
    carry_ref[...] = carry
    f_hi, f_mid, f_lo = _split3(jnp.concatenate(blocks, axis=0))
    q_bias = _dot(f_hi, pq_ref[0]) + _dot(f_mid, pq_ref[1]) + _dot(f_lo, pq_ref[2]) + fq_one
    k_bias = (_dot(f_hi, pk_ref[0]) + _dot(f_mid, pk_ref[1]) + _dot(f_lo, pk_ref[2]) + fk_one
              + pad_col * pad_lane)
    ones = jnp.ones((tm, 4 * LANES), F32)
    store_heads(fq_ref, C_B, q_bias, LOG2E)
    store_heads(fk_ref, C_B + 256, k_bias)
    store_heads(fv_ref, C_B + 512, ones)

    store_heads(sq_ref, C_D, jnp.broadcast_to(sq_one, (tm, 4 * LANES)), LOG2E)
    store_heads(sk_ref, C_D + 256, pad_col * pad_lane)
    sv_ref[...] = acc[:, C_D + 512:C_CQ].astype(BF16)

    tab = tab_ref[...]
    cos_q, sin_q = tab[:, 0:128], tab[:, 128:256]
    cos_k, sin_k = tab[:, 256:384], tab[:, 384:512]
    cqn = _rms(acc[:, C_CQ:C_CKV], qn_ref[...]).astype(BF16)
    q_lin = _dot(cqn, wuqa_ref[...])
    q_swp = _dot(cqn, wuqb_ref[...])
    cq_ref[...] = (q_lin * _tile4(cos_q) + q_swp * _tile4(sin_q) + mla_one).astype(BF16)
    ckvn = _rms(acc[:, C_CKV:C_G], kvn_ref[...]).astype(BF16)
    k_nope = _dot(ckvn, wkvk_ref[...])
    grp, grp_s = acc[:, C_G:C_GS], acc[:, C_GS:C_END]
    k_rope = grp * cos_k + grp_s * sin_k
    ck_ref[...] = (k_nope + _tile4(k_rope) + pad_col * mla_pad).astype(BF16)
    vv = _dot(ckvn, wkvv_ref[...])
    for pair in range(2):
        even, odd = per_head(vv[:, pair * LANES:(pair + 1) * LANES], ones, pair)
        cv_ref[:, (2 * pair) * LANES:(2 * pair + 1) * LANES] = even.astype(BF16)
        cv_ref[:, (2 * pair + 1) * LANES:(2 * pair + 2) * LANES] = odd.astype(BF16)


def _bias_constants():
    src = (0, 1, LANES, LANES + 1)
    pq = np.zeros((3, 2 * LANES, 4 * LANES), np.float32)
    pk = np.zeros((3, 2 * LANES, 4 * LANES), np.float32)
    rows = np.zeros((8, 4 * LANES), np.float32)
    for head in range(4):
        base = _free_base(head)
        for part in range(3):
            pq[part, src[head], base + B_F0 + part] = 1.0
            pk[part, src[head], base + B_F1 + part] = -1.0
            rows[0, base + B_F1 + part] = 1.0
            rows[1, base + B_F0 + part] = 1.0
        rows[0, base + B_PAD] = 1.0
        rows[2, base + B_PAD] = 1.0
        rows[3, base + B_PAD] = 1.0
        rows[4, head * LANES + MLA_BIAS_LANE] = 1.0
        rows[5, head * LANES + MLA_BIAS_LANE] = 1.0
    return jnp.asarray(pq, BF16), jnp.asarray(pk, BF16), jnp.asarray(rows, F32)


def _inproj(h, y2, y_ids, ew, p, seq_len):
    t, d = h.shape
    tm = ROW_TILE
    n_seq_tiles = seq_len // tm
    has_y2 = y2 is not None
    row = lambda i: (i, 0)
    fixed = lambda i: (0, 0)
    in_specs = [pl.BlockSpec((tm, d), row)]
    args = [h]
    if has_y2:
        ids = y_ids.reshape(t // tm, 1, TOP_K * tm)
        ids = jnp.concatenate([ids, jnp.zeros_like(ids[:1])], axis=0)
        in_specs += [pl.BlockSpec((1, 1, TOP_K * tm), lambda i: (0, 0, 0), memory_space=pltpu.SMEM),
                     pl.BlockSpec((1, 1, TOP_K * tm), lambda i: (i + 1, 0, 0), memory_space=pltpu.SMEM),
                     pl.BlockSpec(memory_space=pl.ANY),
                     pl.BlockSpec((tm, LANES), row)]
        args += [ids, ids, y2, ew]
    pq, pk, rows = _bias_constants()
    consts = [p["attn_norm"], p["w_in"], p["q_norm"], p["kv_norm"], p["w_uq_a"], p["w_uq_b"],
              p["w_kv_k"], p["w_kv_v"]]
    in_specs += [pl.BlockSpec(c.shape, fixed) for c in consts]
    args += consts
    in_specs.append(pl.BlockSpec((tm, 512), lambda i: (i % n_seq_tiles, 0)))
    args.append(p["rope_tab"])
    in_specs += [pl.BlockSpec((1, 256), fixed),
                 pl.BlockSpec(pq.shape, lambda i: (0, 0, 0)),
                 pl.BlockSpec(pk.shape, lambda i: (0, 0, 0)),
                 pl.BlockSpec(rows.shape, fixed)]
    args += [p["b_forget"], pq, pk, rows]
    widths = [512] * 9 + [256]
    out_shape = [jax.ShapeDtypeStruct((t, w), BF16) for w in widths]
    out_specs = [pl.BlockSpec((tm, w), row) for w in widths]
    if has_y2:
        out_shape = [jax.ShapeDtypeStruct((t, d), F32)] + out_shape
        out_specs = [pl.BlockSpec((tm, d), row)] + out_specs
    outs = pl.pallas_call(
        functools.partial(_inproj_kernel, has_y2, n_seq_tiles),
        grid=(t // tm,),
        in_specs=in_specs,
        out_specs=out_specs,
        out_shape=out_shape,
        scratch_shapes=[pltpu.VMEM((1, 2 * LANES), F32)] + (
            [pltpu.VMEM((2, TOP_K, tm * ROW_SUB, LANES), F32), pltpu.SemaphoreType.DMA((2,))] if has_y2 else []),
        compiler_params=pltpu.CompilerParams(dimension_semantics=("arbitrary",),
                                             vmem_limit_bytes=VMEM_LIMIT),
        name="inproj_y2" if has_y2 else "inproj",
    )(*args)
    if has_y2:
        return outs[0], outs[1:]
    return h, outs


def _swa_kernel(sink_ref, q_ref, km_ref, kp_ref, kc_ref, vm_ref, vp_ref, vc_ref, o_ref):
    i = pl.program_id(1)
    n_sub = q_ref.shape[1] // BLOCK
    lane = lax.broadcasted_iota(jnp.int32, (1, LANES), 1)
    lo_half = lane < HEAD_DIM
    half_masks = [jnp.where(lo_half, 1.0, 0.0).astype(BF16), jnp.where(lo_half, 0.0, 1.0).astype(BF16)]
    row = lax.broadcasted_iota(jnp.int32, (BLOCK, 1), 0)
    col = lax.broadcasted_iota(jnp.int32, (1, BLOCK), 1)
    grp = SWA_HEADS // SWA_KV_HEADS
    k_all = jnp.concatenate([km_ref[0], kp_ref[0], kc_ref[0]], axis=0)
    v_all = jnp.concatenate([vm_ref[0], vp_ref[0], vc_ref[0]], axis=0)
    for j in range(n_sub):
        q0 = (i * n_sub + j) * BLOCK
        pq = q0 + row
        cq = pq >> CHUNK_SHIFT
        segs = []
        vis_m = col >= PAD_FRONT
        segs.append((vis_m, jnp.minimum(jnp.abs(pq - col), WINDOW).astype(F32)))
        for pk in (q0 - BLOCK + col, q0 + col):
            ck = jnp.where(pk >= BLOCK, pk >> CHUNK_SHIFT, BIG)
            vis = (ck <= cq) & (ck >= cq - (WINDOW >> CHUNK_SHIFT))
            segs.append((vis, jnp.abs(pq - pk).astype(F32)))
        kj = jnp.concatenate([k_all[0:BLOCK], k_all[(j + 1) * BLOCK:(j + 3) * BLOCK]], axis=0)
        vj = jnp.concatenate([v_all[0:BLOCK], v_all[(j + 1) * BLOCK:(j + 3) * BLOCK]], axis=0)
        for g in range(grp):
            qg = q_ref[0, j * BLOCK:(j + 1) * BLOCK, g * LANES:(g + 1) * LANES]
            out_g = None
            for hk in range(SWA_KV_HEADS):
                head = hk * grp + g
                slope = 2.0 ** (-8.0 * (head + 1) / SWA_HEADS)
                sink = sink_ref[head]
                s_all = _dot_nt(qg * half_masks[hk], kj)
                tiles = [jnp.where(vis, s_all[:, n * LANES:(n + 1) * LANES] - slope * dist, NEG)
                         for n, (vis, dist) in enumerate(segs)]
                top = jnp.maximum(jnp.maximum(tiles[0], tiles[1]), tiles[2])
                m = jnp.broadcast_to(jnp.maximum(jnp.max(top, axis=-1, keepdims=True), sink), (BLOCK, LANES))
                e = [jnp.exp(x - m) for x in tiles]
                den = jnp.sum(e[0] + e[1] + e[2], axis=-1, keepdims=True) + jnp.exp(sink - m[:, 0:1])
                o = _dot(jnp.concatenate(e, axis=1).astype(BF16), vj) * (1.0 / den)
                out_g = o if hk == 0 else jnp.where(lo_half, out_g, o)
            o_ref[0, j * BLOCK:(j + 1) * BLOCK, g * LANES:(g + 1) * LANES] = out_g.astype(BF16)


def _swa_attention(qa, sinks, batch, seq_len):
    x = qa.reshape(batch, seq_len, 512)
    n_sub = SWA_Q_BLOCKS
    nb = seq_len // (BLOCK * n_sub)
    blk = lambda f: pl.BlockSpec((1, BLOCK, LANES), f)
    own = lambda c: pl.BlockSpec((1, BLOCK * n_sub, LANES), lambda b, i: (b, i, c))
    return pl.pallas_call(
        _swa_kernel,
        grid=(batch, nb),
        in_specs=[
            pl.BlockSpec(memory_space=pltpu.SMEM),
            pl.BlockSpec((1, BLOCK * n_sub, 2 * LANES), lambda b, i: (b, i, 0)),
            blk(lambda b, i: (b, 0, 2)),
            blk(lambda b, i: (b, jnp.maximum(i * n_sub - 1, 0), 2)),
            own(2),
            blk(lambda b, i: (b, 0, 3)),
            blk(lambda b, i: (b, jnp.maximum(i * n_sub - 1, 0), 3)),
            own(3),
        ],
        out_specs=pl.BlockSpec((1, BLOCK * n_sub, 2 * LANES), lambda b, i: (b, i, 0)),
        out_shape=jax.ShapeDtypeStruct((batch, seq_len, 2 * LANES), BF16),
        compiler_params=pltpu.CompilerParams(dimension_semantics=("arbitrary", "arbitrary"),
                                             vmem_limit_bytes=VMEM_LIMIT),
        name="swa_attention",
    )(sinks, x, x, x, x, x, x, x)


def _causal_kernel(mode, q_ref, k_ref, v_ref, o_ref, stat_ref, acc_ref):
    seq_len = q_ref.shape[1]
    Q_TILE = stat_ref.shape[1]
    n_qt = (seq_len - BLOCK) // Q_TILE
    per_tile = Q_TILE // K_TILE
    lane = lax.broadcasted_iota(jnp.int32, (1, LANES), 1)
    lo_half = lane < HEAD_DIM
    if mode == "sb":
        r = lax.broadcasted_iota(jnp.int32, (2 * K_TILE, K_TILE), 0) & (K_TILE - 1)
        c = lax.broadcasted_iota(jnp.int32, (2 * K_TILE, K_TILE), 1)
        later2 = jnp.where(r > c, 1.0, 0.0).astype(BF16)
        r1 = lax.broadcasted_iota(jnp.int32, (2 * BLOCK, BLOCK), 0) & (BLOCK - 1)
        c1 = lax.broadcasted_iota(jnp.int32, (2 * BLOCK, BLOCK), 1)
        later1 = jnp.where(r1 > c1, 1.0, 0.0).astype(BF16)

    def causal(pq, k0, tk):
        pk = k0 + lax.broadcasted_iota(jnp.int32, (1, tk), 1)
        if mode == "fox":
            return pk <= pq
        if mode == "mla":
            return (pk >> CHUNK_SHIFT) <= (pq >> CHUNK_SHIFT)
        return pk < pq

    def head_v(k0, tk, hh):
        if mode == "sb":
            return v_ref[0, pl.ds(k0, tk), :]
        return v_ref[0, pl.ds(k0, tk), hh * LANES:(hh + 1) * LANES]

    def lane_tiles(x):
        return [x[:, j * LANES:(j + 1) * LANES] for j in range(x.shape[1] // LANES)]

    def row_parts(tq):
        step = min(tq, ROW_PART)
        return [(r0, step) for r0 in range(0, tq, step)]

    def softmax_chunk(q0, tq, k0, tk, masked, first):
        for hh in range(2):
            qh = q_ref[0, pl.ds(q0, tq), hh * LANES:(hh + 1) * LANES]
            kh = k_ref[0, pl.ds(k0, tk), hh * LANES:(hh + 1) * LANES]
            s_all = _dot_nt(qh, kh)
            vh = head_v(k0, tk, hh)
            for r0, tr in row_parts(tq):
                s = s_all[r0:r0 + tr]
                if masked:
                    pq = q0 + r0 + lax.broadcasted_iota(jnp.int32, (tr, 1), 0)
                    s = jnp.where(causal(pq, k0, tk), s, NEG)
                tiles = lane_tiles(s)
                top = tiles[0]
                for x in tiles[1:]:
                    top = jnp.maximum(top, x)
                m_new = jnp.broadcast_to(jnp.max(top, axis=-1, keepdims=True), (tr, LANES))
                if not first:
                    m_old = stat_ref[hh, r0:r0 + tr, :]
                    m_new = jnp.maximum(m_old, m_new)
                p = jnp.concatenate([jnp.exp2(x - m_new) for x in tiles], axis=1).astype(BF16)
                pv = _dot(p, vh)
                if not first:
                    pv = jnp.exp2(m_old - m_new) * acc_ref[hh, r0:r0 + tr, :] + pv
                stat_ref[hh, r0:r0 + tr, :] = m_new
                acc_ref[hh, r0:r0 + tr, :] = pv

    def stick_chunk(q0, tq, k0, tk, masked, first):
        later = later2 if tk == K_TILE else later1
        for hh in range(2):
            qh = q_ref[0, pl.ds(q0, tq), hh * LANES:(hh + 1) * LANES]
            kh = k_ref[0, pl.ds(k0, tk), hh * LANES:(hh + 1) * LANES]
            z = _dot_nt(qh, kh)
            ls_pos = jnp.minimum(z, 0.0) - jnp.log(1.0 + jnp.exp2(-jnp.abs(z))) * LOG2E
            log_keep = ls_pos - z
            if masked:
                pq = q0 + lax.broadcasted_iota(jnp.int32, (tq, 1), 0)
                vis = causal(pq, k0, tk)
                log_keep = jnp.where(vis, log_keep, 0.0)
            hi = log_keep.astype(BF16)
            lo = (log_keep - hi.astype(F32)).astype(BF16)
            after = _dot(jnp.concatenate([hi, lo], axis=1), later)
            tot = ls_pos + after
            chunk_total = jnp.broadcast_to(after[:, 0:1] + log_keep[:, 0:1], (tq, LANES))
            if not first:
                carry = stat_ref[hh, 0:tq, :]
                tot = jnp.concatenate([x + carry for x in lane_tiles(tot)], axis=1)
                chunk_total = carry + chunk_total
            a = jnp.exp2(tot)
            if masked:
                a = jnp.where(vis, a, 0.0)
            pv = _dot(a.astype(BF16), head_v(k0, tk, hh))
            if not first:
                pv = acc_ref[hh, 0:tq, :] + pv
            stat_ref[hh, 0:tq, :] = chunk_total
            acc_ref[hh, 0:tq, :] = pv

    def finish(q0, tq):
        a0, a1 = acc_ref[0, 0:tq, :], acc_ref[1, 0:tq, :]
        if mode != "sb":
            a0 = a0 / a0[:, HEAD_DIM:HEAD_DIM + 1]
            a1 = a1 / a1[:, 0:1]
        o_ref[0, pl.ds(q0, tq), :] = jnp.where(lo_half, a0, a1).astype(BF16)

    def chunk_start(j):
        return pl.multiple_of(BLOCK + j * K_TILE, BLOCK)

    if mode == "sb":
        stick_chunk(0, BLOCK, 0, BLOCK, True, True)
        finish(0, BLOCK)

        def q_body(i, _):
            q0 = pl.multiple_of(BLOCK + i * Q_TILE, BLOCK)
            n_int = i * per_tile
            for d in range(per_tile):
                stick_chunk(q0, Q_TILE, chunk_start(n_int + per_tile - 1 - d), K_TILE, True, d == 0)

            def alive():
                top = jnp.maximum(jnp.max(stat_ref[0]), jnp.max(stat_ref[1]))
                return (top > UNDERFLOW_LOG2).astype(jnp.int32)

            def body(st):
                jj, _ = st
                stick_chunk(q0, Q_TILE, chunk_start(n_int - 1 - jj), K_TILE, False, False)
                return jj + 1, alive()

            _, go = lax.while_loop(lambda st: (st[0] < n_int) & (st[1] > 0), body, (0, alive()))

            @pl.when(go > 0)
            def _():
                stick_chunk(q0, Q_TILE, 0, BLOCK, False, False)

            finish(q0, Q_TILE)
            return 0
    else:
        softmax_chunk(0, BLOCK, 0, BLOCK, True, True)
        finish(0, BLOCK)

        def q_body(i, _):
            q0 = pl.multiple_of(BLOCK + i * Q_TILE, BLOCK)
            n_int = i * per_tile
            softmax_chunk(q0, Q_TILE, 0, BLOCK, False, True)

            def body(j, _):
                for d in range(per_tile):
                    softmax_chunk(q0, Q_TILE, chunk_start(j * per_tile + d), K_TILE, False, False)
                return 0

            lax.fori_loop(0, i, body, 0)
            for d in range(per_tile):
                softmax_chunk(q0, Q_TILE, chunk_start(n_int + d), K_TILE, True, False)
            finish(q0, Q_TILE)
            return 0

    lax.fori_loop(0, n_qt, q_body, 0)


def _causal_attention(mode, q, k, v, batch, seq_len):
    wide = pl.BlockSpec((1, seq_len, 2 * LANES), lambda b, p: (b, 0, p))
    narrow = pl.BlockSpec((1, seq_len, LANES), lambda b, p: (b, 0, p))
    args = [q.reshape(batch, seq_len, 512), k.reshape(batch, seq_len, 512),
            v.reshape(batch, seq_len, v.shape[1])]
    return pl.pallas_call(
        functools.partial(_causal_kernel, mode),
        grid=(batch, 2),
        in_specs=[wide, wide, narrow if mode == "sb" else wide],
        out_specs=narrow,
        out_shape=jax.ShapeDtypeStruct((batch, seq_len, 2 * LANES), BF16),
        scratch_shapes=[pltpu.VMEM((2, Q_TILES[mode], LANES), F32), pltpu.VMEM((2, Q_TILES[mode], LANES), F32)],
        compiler_params=pltpu.CompilerParams(dimension_semantics=("arbitrary", "arbitrary"),
                                             vmem_limit_bytes=VMEM_LIMIT),
        name=mode + "_attention",
    )(*args)


def _outproj_kernel(ya_ref, yb_ref, yc_ref, yd_ref, h_ref, wo_ref, g_ref, wrh_ref, wrl_ref, br_ref, tri_ref,
                    h2_ref, xn_ref, route_ref, cnt_ref):
    o = (_dot(ya_ref[...], wo_ref[0]) + _dot(yb_ref[...], wo_ref[1])
         + _dot(yc_ref[...], wo_ref[2]) + _dot(yd_ref[...], wo_ref[3]))
    h2 = h_ref[...] + o
    h2_ref[...] = h2
    xn = _rms(h2, g_ref[...])
    _rows_to_tiles(xn_ref, (), xn)
    xh = xn.astype(BF16)
    xl = (xn - xh.astype(F32)).astype(BF16)
    wrh, wrl = wrh_ref[...], wrl_ref[...]
    lg = _dot(xh, wrh) + _dot(xl, wrh) + _dot(xh, wrl) + br_ref[...]

    tm = lg.shape[0]
    lane = lax.broadcasted_iota(jnp.int32, (tm, LANES), 1)
    ninf = -jnp.inf

    def first_max(x):
        top = jnp.max(x, axis=-1, keepdims=True)
        return top, jnp.min(jnp.where(x == top, lane, LANES), axis=-1, keepdims=True)

    gl = jnp.where(lane < N_GROUPS, lg, ninf)
    g_max, g_top = first_max(gl)
    g_w = 1.0 / jnp.sum(jnp.exp(gl - g_max), axis=-1, keepdims=True)
    e_lo = N_GROUPS + g_top * EXPERTS_PER_GROUP
    el = jnp.where((lane >= e_lo) & (lane < e_lo + EXPERTS_PER_GROUP), lg, ninf)
    v1, i1 = first_max(el)
    v2, i2 = first_max(jnp.where(lane == i1, ninf, el))
    r21 = jnp.exp(v2 - v1)
    w1 = g_w / (1.0 + r21)
    w2 = w1 * r21

    @pl.when(pl.program_id(0) == 0)
    def _():
        cnt_ref[...] = jnp.zeros_like(cnt_ref)

    m1 = jnp.where(lane == i1, 1.0, 0.0)
    m2 = jnp.where(lane == i2, 1.0, 0.0)
    both = m1 + m2
    before = _dot(tri_ref[...], both.astype(BF16)) + cnt_ref[0:1, :]
    rank1 = jnp.sum(m1 * before, axis=-1, keepdims=True)
    rank2 = jnp.sum(m2 * before, axis=-1, keepdims=True)
    cnt_ref[...] = cnt_ref[...] + jnp.sum(both, axis=0, keepdims=True)
    cols = [w1, w2, (i1 - N_GROUPS).astype(F32), (i2 - N_GROUPS).astype(F32), rank1, rank2]
    route = jnp.zeros((tm, LANES), F32)
    for j, c in enumerate(cols):
        route = jnp.where(lane == j, c, route)
    route_ref[...] = route


def _outproj(ys, h, p):
    t, d = h.shape
    tm = ROW_TILE
    row = lambda i: (i, 0)
    fixed2 = lambda i: (0, 0)
    in_specs = [pl.BlockSpec((tm, 256), row)] * 4 + [
        pl.BlockSpec((tm, d), row),
        pl.BlockSpec((4, 256, d), lambda i: (0, 0, 0)),
        pl.BlockSpec((1, d), fixed2),
        pl.BlockSpec((d, LANES), fixed2),
        pl.BlockSpec((d, LANES), fixed2),
        pl.BlockSpec((1, LANES), fixed2),
        pl.BlockSpec((tm, tm), fixed2),
    ]
    earlier = jnp.asarray(np.tril(np.ones((tm, tm), np.float32), -1), BF16)
    return pl.pallas_call(
        _outproj_kernel,
        grid=(t // tm,),
        in_specs=in_specs,
        out_specs=[pl.BlockSpec((tm, d), row), pl.BlockSpec((tm * ROW_SUB, LANES), row),
                   pl.BlockSpec((tm, LANES), row), pl.BlockSpec((8, LANES), fixed2)],
        out_shape=[jax.ShapeDtypeStruct((t, d), F32), jax.ShapeDtypeStruct((t * ROW_SUB, LANES), F32),
                   jax.ShapeDtypeStruct((t, LANES), F32), jax.ShapeDtypeStruct((8, LANES), F32)],
        compiler_params=pltpu.CompilerParams(dimension_semantics=("arbitrary",),
                                             vmem_limit_bytes=VMEM_LIMIT),
        name="outproj_router",
    )(*ys, h, p["w_out"], p["ffn_norm"], p["w_r_hi"], p["w_r_lo"], p["b_r"], earlier)


def _moe_kernel(be_ref, tok_ref, x_hbm, wg_ref, wu_ref, wd_ref, y_ref, xbuf, sem_in):
    s = pl.program_id(0)
    blk = MOE_BLOCK * ROW_SUB

    def block_in(buf_slot):
        return pltpu.make_async_copy(x_hbm.at[pl.ds(0, blk)], xbuf.at[buf_slot], sem_in.at[buf_slot])

    @pl.when(s == 0)
    def _():
        block_in(0).start()

    def step(slot):
        other = 1 - slot
        for r in range(MOE_BLOCK):
            src = pl.multiple_of(tok_ref[0, 0, r], ROW_SUB)
            pltpu.make_async_copy(x_hbm.at[pl.ds(src, ROW_SUB)], xbuf.at[other, pl.ds(r * ROW_SUB, ROW_SUB)],
                                  sem_in.at[other]).start(priority=r % 2)
        block_in(slot).wait()
        x = _rows_from_tiles(xbuf, (slot,), MOE_BLOCK).astype(BF16)
        gate = _dot(x, wg_ref[0])
        up = _dot(x, wu_ref[0])
        hid = (gate * (1.0 / (1.0 + jnp.exp(-gate))) * up).astype(BF16)
        _rows_to_tiles(y_ref, (), _dot(hid, wd_ref[0]))

        @pl.when(s == pl.num_programs(0) - 1)
        def _():
            block_in(other).wait()

    for parity in range(2):
        pl.when(s % 2 == parity)(functools.partial(step, parity))


def _moe(xn, tok_ext, be_ext, p):
    d = xn.shape[1] * ROW_SUB
    n_steps = be_ext.shape[0]
    hdim = p["w_gate"].shape[2]
    wspec = lambda shape: pl.BlockSpec((1,) + shape, lambda s, be: (be[s], 0, 0))
    return pl.pallas_call(
        _moe_kernel,
        grid_spec=pltpu.PrefetchScalarGridSpec(
            num_scalar_prefetch=1,
            grid=(n_steps,),
            in_specs=[pl.BlockSpec((1, 1, MOE_BLOCK), lambda s, be: (s, 0, 0), memory_space=pltpu.SMEM),
                      pl.BlockSpec(memory_space=pl.ANY),
                      wspec((d, hdim)), wspec((d, hdim)), wspec((hdim, d))],
            out_specs=pl.BlockSpec((MOE_BLOCK * ROW_SUB, LANES), lambda s, be: (jnp.maximum(s - 1, 0), 0)),
            scratch_shapes=[pltpu.VMEM((2, MOE_BLOCK * ROW_SUB, LANES), F32), pltpu.SemaphoreType.DMA((2,))],
        ),
        out_shape=jax.ShapeDtypeStruct(((n_steps - 1) * MOE_BLOCK * ROW_SUB, LANES), F32),
        compiler_params=pltpu.CompilerParams(dimension_semantics=("arbitrary",),
                                             vmem_limit_bytes=VMEM_LIMIT),
        name="moe_experts",
    )(be_ext, tok_ext, xn, p["w_gate"], p["w_up"], p["w_down"])


def _route(route, counts, t):
    a = t * TOP_K
    expert = route[:, 2:4].astype(jnp.int32).reshape(a)
    pos = route[:, 4:6].astype(jnp.int32).reshape(a)
    counts = counts.astype(jnp.int32)
    padded = (counts + MOE_BLOCK - 1) // MOE_BLOCK * MOE_BLOCK
    pad_end = jnp.cumsum(padded)
    pad_start = pad_end - padded
    dest = pad_start[expert] + pos
    n_rows = a + N_EXPERTS * MOE_BLOCK
    n_blk = n_rows // MOE_BLOCK
    rows_tok = jnp.zeros((n_rows,), jnp.int32).at[dest].set(jnp.arange(a, dtype=jnp.int32) >> 1)
    starts = jnp.arange(n_blk, dtype=jnp.int32) * MOE_BLOCK
    block_e = jnp.minimum(jnp.sum((pad_end[None, :] <= starts[:, None]).astype(jnp.int32), axis=1), N_EXPERTS - 1)
    n_steps = n_blk + 1
    tok_ext = jnp.concatenate([rows_tok.reshape(n_blk, MOE_BLOCK), jnp.zeros((1, MOE_BLOCK), jnp.int32)], axis=0)
    be_ext = block_e[jnp.clip(jnp.arange(n_steps, dtype=jnp.int32) - 1, 0, n_blk - 1)]
    return (tok_ext * ROW_SUB).reshape(n_steps, 1, MOE_BLOCK), be_ext, dest * ROW_SUB


def _final_kernel(h_ref, ids0_ref, idsn_ref, y_hbm, ew_ref, g_ref, o_ref, ybuf, ysem):
    h = _combine_experts(h_ref[...], ew_ref[...], y_hbm, ids0_ref, idsn_ref, ybuf, ysem,
                         pl.program_id(0), pl.num_programs(0))
    o_ref[0] = _rms(h, g_ref[...])


def _final(h, y2, y_ids, ew, g, batch, seq_len):
    t, d = h.shape
    per_seq = seq_len // BLOCK
    out_blocks = per_seq - 1
    row = lambda n: ((n // out_blocks) * per_seq + n % out_blocks + 1, 0)
    ids = y_ids.reshape(batch, per_seq, TOP_K * BLOCK)[:, 1:].reshape(batch * out_blocks, 1, TOP_K * BLOCK)
    ids = jnp.concatenate([ids, jnp.zeros_like(ids[:1])], axis=0)
    return pl.pallas_call(
        _final_kernel,
        grid=(batch * out_blocks,),
        in_specs=[pl.BlockSpec((BLOCK, d), row),
                  pl.BlockSpec((1, 1, TOP_K * BLOCK), lambda n: (0, 0, 0), memory_space=pltpu.SMEM),
                  pl.BlockSpec((1, 1, TOP_K * BLOCK), lambda n: (n + 1, 0, 0), memory_space=pltpu.SMEM),
                  pl.BlockSpec(memory_space=pl.ANY),
                  pl.BlockSpec((BLOCK, LANES), row),
                  pl.BlockSpec((1, d), lambda n: (0, 0))],
        out_specs=pl.BlockSpec((1, BLOCK, d), lambda n: (n // out_blocks, n % out_blocks, 0)),
        out_shape=jax.ShapeDtypeStruct((batch, seq_len - BLOCK, d), F32),
        scratch_shapes=[pltpu.VMEM((2, TOP_K, BLOCK * ROW_SUB, LANES), F32), pltpu.SemaphoreType.DMA((2,))],
        compiler_params=pltpu.CompilerParams(dimension_semantics=("arbitrary",),
                                             vmem_limit_bytes=VMEM_LIMIT),
        name="final_norm",
    )(h, ids, ids, y2, ew, g)


def _rope_table(seq_len):
    half = MLA_ROPE // 2
    pos = (jnp.arange(seq_len, dtype=jnp.int32) - PAD_FRONT).astype(F32)
    inv_freq = ROPE_THETA ** (-jnp.arange(half, dtype=F32) / half)
    ang = pos[:, None] * inv_freq[None, :]
    cos, sin = jnp.cos(ang), jnp.sin(ang)
    cos2 = jnp.concatenate([cos, cos], axis=1)
    sin2 = jnp.concatenate([-sin, sin], axis=1)
    z = lambda w: jnp.zeros((seq_len, w), F32)
    scale = (MLA_NOPE + MLA_ROPE) ** -0.5 * LOG2E
    cos_q = jnp.concatenate([jnp.ones((seq_len, MLA_NOPE), F32), cos2, z(32)], axis=1) * scale
    sin_q = jnp.concatenate([z(MLA_NOPE), sin2, z(32)], axis=1) * scale
    cos_k = jnp.concatenate([z(MLA_NOPE), cos2, z(32)], axis=1)
    sin_k = jnp.concatenate([z(MLA_NOPE), sin2, z(32)], axis=1)
    return jnp.concatenate([cos_q, sin_q, cos_k, sin_k], axis=1)


def _swap_halves(w):
    half = w.shape[-1] // 2
    return jnp.concatenate([w[..., half:], w[..., :half]], axis=-1)


def _layer_params(i, seq_len, attn_norm, w_in, b_forget, sinks, mla_q_norm, mla_kv_norm, mla_w_uq,
                  mla_w_ukv, w_out, ffn_norm, w_group, b_group, w_router, b_router, w_gate, w_up, w_down):
    d = w_in.shape[1]
    w = w_in[i]
    sizes = (256, 128, 128, 256, 256, 256, 4, 256, 128, 32, 256, 256, 256)
    offs = np.concatenate([[0], np.cumsum(sizes)])
    (a_q, a_k, a_v, f_q, f_k, f_v, f_g, c_q, c_kv, c_kr, s_q, s_k, s_v) = [
        w[:, offs[j]:offs[j + 1]] for j in range(len(sizes))]
    qscale = HEAD_DIM ** -0.5
    grp = SWA_HEADS // SWA_KV_HEADS
    a_q = a_q.reshape(d, SWA_KV_HEADS, grp, HEAD_DIM).transpose(0, 2, 1, 3).reshape(d, 256)
    z = lambda n: jnp.zeros((d, n), F32)
    g_grp = jnp.concatenate([f_g[:, 0:2], z(62), c_kr, z(32)], axis=1)
    gs_grp = jnp.concatenate([f_g[:, 2:4], z(62), _swap_halves(c_kr), z(32)], axis=1)
    w_perm = jnp.concatenate([a_q * qscale, a_k, a_v, f_q * qscale, f_k, f_v, s_q * qscale, s_k, s_v,
                              c_q, c_kv, g_grp, gs_grp], axis=1).astype(BF16)
    wuq = mla_w_uq[i].reshape(MLA_Q_LORA, 4, MLA_NOPE + MLA_ROPE)
    zq = lambda n: jnp.zeros((MLA_Q_LORA, 4, n), F32)
    w_uq_a = jnp.concatenate([wuq, zq(32)], axis=2).reshape(MLA_Q_LORA, 512).astype(BF16)
    w_uq_b = jnp.concatenate([zq(MLA_NOPE), _swap_halves(wuq[:, :, MLA_NOPE:]), zq(32)],
                             axis=2).reshape(MLA_Q_LORA, 512).astype(BF16)
    wukv = mla_w_ukv[i].reshape(MLA_KV_LORA, 4, MLA_NOPE + MLA_V)
    w_kv_k = jnp.concatenate([wukv[:, :, :MLA_NOPE], jnp.zeros((MLA_KV_LORA, 4, 64), F32)],
                             axis=2).reshape(MLA_KV_LORA, 512).astype(BF16)
    w_kv_v = wukv[:, :, MLA_NOPE:].reshape(MLA_KV_LORA, 256).astype(BF16)
    bf = b_forget[i].astype(F32)
    b_f = jnp.zeros((1, 256), F32).at[0, 0:2].set(bf[0:2]).at[0, 128:130].set(bf[2:4])
    wo = w_out[i]
    wo_a = wo[:256].reshape(SWA_KV_HEADS, grp, HEAD_DIM, d).transpose(1, 0, 2, 3).reshape(256, d)
    wo4 = jnp.concatenate([wo_a, wo[256:]], axis=0).reshape(4, 256, d).astype(BF16)
    w_r = jnp.concatenate([w_group[i], w_router[i], jnp.zeros((d, LANES - N_GROUPS - N_EXPERTS), F32)], axis=1)
    w_r_hi = w_r.astype(BF16)
    w_r_lo = (w_r - w_r_hi.astype(F32)).astype(BF16)
    b_r = jnp.concatenate([b_group[i], b_router[i], jnp.zeros((LANES - N_GROUPS - N_EXPERTS,), F32)])[None, :]
    return dict(
        attn_norm=attn_norm[i][None, :], w_in=w_perm, q_norm=mla_q_norm[i][None, :],
        kv_norm=mla_kv_norm[i][None, :], w_uq_a=w_uq_a, w_uq_b=w_uq_b, w_kv_k=w_kv_k, w_kv_v=w_kv_v,
        rope_tab=_rope_table(seq_len), b_forget=b_f, sinks=sinks[i].astype(F32), w_out=wo4,
        ffn_norm=ffn_norm[i][None, :], w_r_hi=w_r_hi, w_r_lo=w_r_lo, b_r=b_r.astype(F32),
        w_gate=w_gate[i].astype(BF16), w_up=w_up[i].astype(BF16), w_down=w_down[i].astype(BF16))


def kernel(x, meta_tokens, attn_norm, w_in, b_forget, sinks, mla_q_norm, mla_kv_norm, mla_w_uq, mla_w_ukv,
           w_out, ffn_norm, w_group, b_group, w_router, b_router, w_gate, w_up, w_down, final_norm):
    batch, seq, d = x.shape
    seq_len = seq + BLOCK
    assert seq_len % ROW_TILE == 0 and seq_len % (BLOCK * SWA_Q_BLOCKS) == 0
    assert all(seq % tile == 0 for tile in Q_TILES.values())
    t = batch * seq_len
    depth = w_in.shape[0]
    pad = jnp.zeros((batch, PAD_FRONT, d), x.dtype)
    meta = jnp.broadcast_to(meta_tokens.astype(x.dtype)[None], (batch, N_META, d))
    h = jnp.concatenate([pad, meta, x], axis=1).reshape(t, d)
    y2 = y_ids = ew = None
    for i in range(depth):
        p = _layer_params(i, seq_len, attn_norm, w_in, b_forget, sinks, mla_q_norm, mla_kv_norm, mla_w_uq,
                          mla_w_ukv, w_out, ffn_norm, w_group, b_group, w_router, b_router, w_gate, w_up, w_down)
        h, (qa, fq, fk, fv, cq, ck, cv, sq, sk, sv) = _inproj(h, y2, y_ids, ew, p, seq_len)
        y_a = _swa_attention(qa, p["sinks"], batch, seq_len)
        y_b = _causal_attention("fox", fq, fk, fv, batch, seq_len)
        y_c = _causal_attention("mla", cq, ck, cv, batch, seq_len)
        y_d = _causal_attention("sb", sq, sk, sv, batch, seq_len)
        ys = [y.reshape(t, 256) for y in (y_a, y_b, y_c, y_d)]
        h, xn, ew, counts = _outproj(ys, h, p)
        tok_ext, be_ext, y_ids = _route(ew, counts[0, N_GROUPS:N_GROUPS + N_EXPERTS], t)
        y2 = _moe(xn, tok_ext, be_ext, p)
    return _final(h, y2, y_ids, ew, final_norm[None, :], batch, seq_len)
```

```python
import functools

import jax
import jax.numpy as jnp
import numpy as np
from jax import lax
from jax.experimental import pallas as pl
from jax.experimental.pallas import tpu as pltpu

F32 = jnp.float32
BF16 = jnp.bfloat16

BLOCK = 128
N_META = 16
PAD_FRONT = BLOCK - N_META
CHUNK_SHIFT = 6
HEAD_DIM = 64
NORM_EPS = 1e-6
NEG = -1e30
PAD_KEY_LOGIT = -(2.0 ** 100)
UNDERFLOW_LOG2 = -150.0
LOG2E = 1.4426950408889634
BIG = 1 << 30
SWA_HEADS, SWA_KV_HEADS, WINDOW = 4, 2, 128
MLA_Q_LORA, MLA_KV_LORA, MLA_NOPE, MLA_ROPE, MLA_V = 256, 128, 64, 32, 64
MLA_BIAS_LANE = MLA_NOPE + MLA_ROPE
ROPE_THETA = 10000.0
N_GROUPS, EXPERTS_PER_GROUP, TOP_K = 4, 8, 2
N_EXPERTS = N_GROUPS * EXPERTS_PER_GROUP
MOE_BLOCK = 256
LANES = 128
ROW_SUB = 8
ROW_TILE = 384
Q_TILES = {"fox": 1024, "mla": 1024, "sb": 512}
K_TILE = 256
ROW_PART = 256
SWA_Q_BLOCKS = 3
VMEM_LIMIT = 56 * 1024 * 1024

C_A, C_B, C_D, C_CQ, C_CKV, C_G, C_GS, C_END = 0, 512, 1280, 2048, 2304, 2432, 2560, 2688
B_F0, B_F1, B_PAD = 0, 3, 6


def _rms(x, g):
    return x * lax.rsqrt(jnp.mean(x * x, axis=-1, keepdims=True) + NORM_EPS) * g


def _log_sigmoid(x):
    return jnp.minimum(x, 0.0) - jnp.log(1.0 + jnp.exp(-jnp.abs(x)))


def _dot(a, b):
    return jnp.dot(a, b, preferred_element_type=F32)


def _dot_nt(a, b):
    return lax.dot_general(a, b, (((1,), (1,)), ((), ())), preferred_element_type=F32)


def _rows_from_tiles(ref, lead, n):
    return jnp.concatenate([ref[(*lead, pl.ds(j, n, stride=ROW_SUB), slice(None))] for j in range(ROW_SUB)], axis=1)


def _rows_to_tiles(ref, lead, x):
    n = x.shape[0]
    for j in range(ROW_SUB):
        ref[(*lead, pl.ds(j, n, stride=ROW_SUB), slice(None))] = x[:, j * LANES:(j + 1) * LANES]


def _tile4(x):
    return jnp.concatenate([x, x, x, x], axis=1)


def _split3(x):
    hi = x.astype(BF16)
    r1 = x - hi.astype(F32)
    mid = r1.astype(BF16)
    lo = (r1 - mid.astype(F32)).astype(BF16)
    return hi, mid, lo


def _free_base(head):
    return head * LANES + (HEAD_DIM if head % 2 == 0 else 0)


def _expert_rows_start(y_hbm, ids_ref, buf, sem, slot):
    for j in range(ids_ref.shape[2]):
        tok, k = divmod(j, TOP_K)
        src = pl.multiple_of(ids_ref[0, 0, j], ROW_SUB)
        pltpu.make_async_copy(y_hbm.at[pl.ds(src, ROW_SUB)], buf.at[slot, k, pl.ds(tok * ROW_SUB, ROW_SUB)],
                              sem.at[slot]).start(priority=j % 2)


def _expert_rows_wait(y_hbm, buf, sem, slot):
    for k in range(TOP_K):
        pltpu.make_async_copy(y_hbm.at[pl.ds(0, buf.shape[2])], buf.at[slot, k], sem.at[slot]).wait()


def _combine_experts(h, ew, y_hbm, ids0_ref, idsn_ref, buf, sem, step, n_steps):
    slot = step % 2

    @pl.when(step == 0)
    def _():
        _expert_rows_start(y_hbm, ids0_ref, buf, sem, 0)

    _expert_rows_start(y_hbm, idsn_ref, buf, sem, 1 - slot)
    _expert_rows_wait(y_hbm, buf, sem, slot)
    n = h.shape[0]
    out = h + ew[:, 0:1] * _rows_from_tiles(buf, (slot, 0), n) + ew[:, 1:2] * _rows_from_tiles(buf, (slot, 1), n)

    @pl.when(step == n_steps - 1)
    def _():
        _expert_rows_wait(y_hbm, buf, sem, 1 - slot)

    return out


def _inproj_kernel(has_y2, n_seq_tiles, *refs):
    if has_y2:
        (h_ref, ids0_ref, idsn_ref, y_hbm, ew_ref, *rest) = refs
    else:
        (h_ref, *rest) = refs
    (g_ref, w_ref, qn_ref, kvn_ref, wuqa_ref, wuqb_ref, wkvk_ref, wkvv_ref, tab_ref, bf_ref,
     pq_ref, pk_ref, rows_ref, *outs) = rest
    if has_y2:
        hout_ref, *outs = outs
        *outs, ybuf, ysem = outs
    (qa_ref, fq_ref, fk_ref, fv_ref, cq_ref, ck_ref, cv_ref, sq_ref, sk_ref, sv_ref, carry_ref) = outs
    tile = pl.program_id(0) % n_seq_tiles
    h = h_ref[...]
    tm, d = h.shape
    if has_y2:
        h = _combine_experts(h, ew_ref[...], y_hbm, ids0_ref, idsn_ref, ybuf, ysem,
                             pl.program_id(0), pl.num_programs(0))
        hout_ref[...] = h
    xn = _rms(h, g_ref[...]).astype(BF16)
    acc = _dot(xn, w_ref[...])
    lane = lax.broadcasted_iota(jnp.int32, (1, LANES), 1)
    lo_half = lane < HEAD_DIM
    pad_col = jnp.where(tile * tm + lax.broadcasted_iota(jnp.int32, (tm, 1), 0) < PAD_FRONT,
                        PAD_KEY_LOGIT, 0.0)
    rows = rows_ref[...]
    fq_one, fk_one, pad_lane, sq_one, mla_one, mla_pad = (rows[j:j + 1] for j in range(6))

    def per_head(x_pair, bias, pair, scale=None):
        x = x_pair if scale is None else x_pair * scale
        even = jnp.where(lo_half, x, bias[:, (2 * pair) * LANES:(2 * pair + 1) * LANES])
        odd = jnp.where(lo_half, bias[:, (2 * pair + 1) * LANES:(2 * pair + 2) * LANES], x)
        return even, odd

    def store_heads(ref, x_off, bias, scale=None):
        for pair in range(2):
            x_pair = acc[:, x_off + pair * LANES:x_off + (pair + 1) * LANES]
            even, odd = per_head(x_pair, bias, pair, scale)
            ref[:, (2 * pair) * LANES:(2 * pair + 1) * LANES] = even.astype(BF16)
            ref[:, (2 * pair + 1) * LANES:(2 * pair + 2) * LANES] = odd.astype(BF16)

    qa_ref[...] = acc[:, C_A:C_B].astype(BF16)

    @pl.when(tile == 0)
    def _():
        carry_ref[...] = jnp.zeros_like(carry_ref)

    lf = _log_sigmoid(acc[:, C_G:C_END] + bf_ref[...]) * LOG2E
    r = lax.broadcasted_iota(jnp.int32, (BLOCK, BLOCK), 0)
    c = lax.broadcasted_iota(jnp.int32, (BLOCK, BLOCK), 1)
    tri = jnp.where(c <= r, 1.0, 0.0).astype(BF16)
    carry = carry_ref[...]
    blocks = []
    for b in range(tm // BLOCK):
        hi, mid, lo = _split3(lf[b * BLOCK:(b + 1) * BLOCK])
        y = _dot(tri, hi) + _dot(tri, mid) + _dot(tri, lo) + carry
        carry = y[BLOCK - 1:BLOCK, :]
        blocks.append(y)
    carry_ref[...] = carry
    f_hi, f_mid, f_lo = _split3(jnp.concatenate(blocks, axis=0))
    q_bias = _dot(f_hi, pq_ref[0]) + _dot(f_mid, pq_ref[1]) + _dot(f_lo, pq_ref[2]) + fq_one
    k_bias = (_dot(f_hi, pk_ref[0]) + _dot(f_mid, pk_ref[1]) + _dot(f_lo, pk_ref[2]) + fk_one
              + pad_col * pad_lane)
    ones = jnp.ones((tm, 4 * LANES), F32)
    store_heads(fq_ref, C_B, q_bias, LOG2E)
    store_heads(fk_ref, C_B + 256, k_bias)
    store_heads(fv_ref, C_B + 512, ones)

    store_heads(sq_ref, C_D, jnp.broadcast_to(sq_one, (tm, 4 * LANES)), LOG2E)
    store_heads(sk_ref, C_D + 256, pad_col * pad_lane)
    sv_ref[...] = acc[:, C_D + 512:C_CQ].astype(BF16)

    tab = tab_ref[...]
    cos_q, sin_q = tab[:, 0:128], tab[:, 128:256]
    cos_k, sin_k = tab[:, 256:384], tab[:, 384:512]
    cqn = _rms(acc[:, C_CQ:C_CKV], qn_ref[...]).astype(BF16)
    q_lin = _dot(cqn, wuqa_ref[...])
    q_swp = _dot(cqn, wuqb_ref[...])
    cq_ref[...] = (q_lin * _tile4(cos_q) + q_swp * _tile4(sin_q) + mla_one).astype(BF16)
    ckvn = _rms(acc[:, C_CKV:C_G], kvn_ref[...]).astype(BF16)
    k_nope = _dot(ckvn, wkvk_ref[...])
    grp, grp_s = acc[:, C_G:C_GS], acc[:, C_GS:C_END]
    k_rope = grp * cos_k + grp_s * sin_k
    ck_ref[...] = (k_nope + _tile4(k_rope) + pad_col * mla_pad).astype(BF16)
    vv = _dot(ckvn, wkvv_ref[...])
    for pair in range(2):
        even, odd = per_head(vv[:, pair * LANES:(pair + 1) * LANES], ones, pair)
        cv_ref[:, (2 * pair) * LANES:(2 * pair + 1) * LANES] = even.astype(BF16)
        cv_ref[:, (2 * pair + 1) * LANES:(2 * pair + 2) * LANES] = odd.astype(BF16)


def _bias_constants():
    src = (0, 1, LANES, LANES + 1)
    pq = np.zeros((3, 2 * LANES, 4 * LANES), np.float32)
    pk = np.zeros((3, 2 * LANES, 4 * LANES), np.float32)
    rows = np.zeros((8, 4 * LANES), np.float32)
    for head in range(4):
        base = _free_base(head)
        for part in range(3):
            pq[part, src[head], base + B_F0 + part] = 1.0
            pk[part, src[head], base + B_F1 + part] = -1.0
            rows[0, base + B_F1 + part] = 1.0
            rows[1, base + B_F0 + part] = 1.0
        rows[0, base + B_PAD] = 1.0
        rows[2, base + B_PAD] = 1.0
        rows[3, base + B_PAD] = 1.0
        rows[4, head * LANES + MLA_BIAS_LANE] = 1.0
        rows[5, head * LANES + MLA_BIAS_LANE] = 1.0
    return jnp.asarray(pq, BF16), jnp.asarray(pk, BF16), jnp.asarray(rows, F32)


def _inproj(h, y2, y_ids, ew, p, seq_len):
    t, d = h.shape
    tm = ROW_TILE
    n_seq_tiles = seq_len // tm
    has_y2 = y2 is not None
    row = lambda i: (i, 0)
    fixed = lambda i: (0, 0)
    in_specs = [pl.BlockSpec((tm, d), row)]
    args = [h]
    if has_y2:
        ids = y_ids.reshape(t // tm, 1, TOP_K * tm)
        ids = jnp.concatenate([ids, jnp.zeros_like(ids[:1])], axis=0)
        in_specs += [pl.BlockSpec((1, 1, TOP_K * tm), lambda i: (0, 0, 0), memory_space=pltpu.SMEM),
                     pl.BlockSpec((1, 1, TOP_K * tm), lambda i: (i + 1, 0, 0), memory_space=pltpu.SMEM),
                     pl.BlockSpec(memory_space=pl.ANY),
                     pl.BlockSpec((tm, LANES), row)]
        args += [ids, ids, y2, ew]
    pq, pk, rows = _bias_constants()
    consts = [p["attn_norm"], p["w_in"], p["q_norm"], p["kv_norm"], p["w_uq_a"], p["w_uq_b"],
              p["w_kv_k"], p["w_kv_v"]]
    in_specs += [pl.BlockSpec(c.shape, fixed) for c in consts]
    args += consts
    in_specs.append(pl.BlockSpec((tm, 512), lambda i: (i % n_seq_tiles, 0)))
    args.append(p["rope_tab"])
    in_specs += [pl.BlockSpec((1, 256), fixed),
                 pl.BlockSpec(pq.shape, lambda i: (0, 0, 0)),
                 pl.BlockSpec(pk.shape, lambda i: (0, 0, 0)),
                 pl.BlockSpec(rows.shape, fixed)]
    args += [p["b_forget"], pq, pk, rows]
    widths = [512] * 9 + [256]
    out_shape = [jax.ShapeDtypeStruct((t, w), BF16) for w in widths]
    out_specs = [pl.BlockSpec((tm, w), row) for w in widths]
    if has_y2:
        out_shape = [jax.ShapeDtypeStruct((t, d), F32)] + out_shape
        out_specs = [pl.BlockSpec((tm, d), row)] + out_specs
    outs = pl.pallas_call(
        functools.partial(_inproj_kernel, has_y2, n_seq_tiles),
        grid=(t // tm,),
        in_specs=in_specs,
        out_specs=out_specs,
        out_shape=out_shape,
        scratch_shapes=[pltpu.VMEM((1, 2 * LANES), F32)] + (
            [pltpu.VMEM((2, TOP_K, tm * ROW_SUB, LANES), F32), pltpu.SemaphoreType.DMA((2,))] if has_y2 else []),
        compiler_params=pltpu.CompilerParams(dimension_semantics=("arbitrary",),
                                             vmem_limit_bytes=VMEM_LIMIT),
        name="inproj_y2" if has_y2 else "inproj",
    )(*args)
    if has_y2:
        return outs[0], outs[1:]
    return h, outs


def _swa_kernel(sink_ref, q_ref, km_ref, kp_ref, kc_ref, vm_ref, vp_ref, vc_ref, o_ref):
    i = pl.program_id(1)
    n_sub = q_ref.shape[1] // BLOCK
    lane = lax.broadcasted_iota(jnp.int32, (1, LANES), 1)
    lo_half = lane < HEAD_DIM
    half_masks = [jnp.where(lo_half, 1.0, 0.0).astype(BF16), jnp.where(lo_half, 0.0, 1.0).astype(BF16)]
    row = lax.broadcasted_iota(jnp.int32, (BLOCK, 1), 0)
    col = lax.broadcasted_iota(jnp.int32, (1, BLOCK), 1)
    grp = SWA_HEADS // SWA_KV_HEADS
    k_all = jnp.concatenate([km_ref[0], kp_ref[0], kc_ref[0]], axis=0)
    v_all = jnp.concatenate([vm_ref[0], vp_ref[0], vc_ref[0]], axis=0)
    for j in range(n_sub):
        q0 = (i * n_sub + j) * BLOCK
        pq = q0 + row
        cq = pq >> CHUNK_SHIFT
        segs = []
        vis_m = col >= PAD_FRONT
        segs.append((vis_m, jnp.minimum(jnp.abs(pq - col), WINDOW).astype(F32)))
        for pk in (q0 - BLOCK + col, q0 + col):
            ck = jnp.where(pk >= BLOCK, pk >> CHUNK_SHIFT, BIG)
            vis = (ck <= cq) & (ck >= cq - (WINDOW >> CHUNK_SHIFT))
            segs.append((vis, jnp.abs(pq - pk).astype(F32)))
        kj = jnp.concatenate([k_all[0:BLOCK], k_all[(j + 1) * BLOCK:(j + 3) * BLOCK]], axis=0)
        vj = jnp.concatenate([v_all[0:BLOCK], v_all[(j + 1) * BLOCK:(j + 3) * BLOCK]], axis=0)
        for g in range(grp):
            qg = q_ref[0, j * BLOCK:(j + 1) * BLOCK, g * LANES:(g + 1) * LANES]
            out_g = None
            for hk in range(SWA_KV_HEADS):
                head = hk * grp + g
                slope = 2.0 ** (-8.0 * (head + 1) / SWA_HEADS)
                sink = sink_ref[head]
                s_all = _dot_nt(qg * half_masks[hk], kj)
                tiles = [jnp.where(vis, s_all[:, n * LANES:(n + 1) * LANES] - slope * dist, NEG)
                         for n, (vis, dist) in enumerate(segs)]
                top = jnp.maximum(jnp.maximum(tiles[0], tiles[1]), tiles[2])
                m = jnp.broadcast_to(jnp.maximum(jnp.max(top, axis=-1, keepdims=True), sink), (BLOCK, LANES))
                e = [jnp.exp(x - m) for x in tiles]
                den = jnp.sum(e[0] + e[1] + e[2], axis=-1, keepdims=True) + jnp.exp(sink - m[:, 0:1])
                o = _dot(jnp.concatenate(e, axis=1).astype(BF16), vj) * (1.0 / den)
                out_g = o if hk == 0 else jnp.where(lo_half, out_g, o)
            o_ref[0, j * BLOCK:(j + 1) * BLOCK, g * LANES:(g + 1) * LANES] = out_g.astype(BF16)


def _swa_attention(qa, sinks, batch, seq_len):
    x = qa.reshape(batch, seq_len, 512)
    n_sub = SWA_Q_BLOCKS
    nb = seq_len // (BLOCK * n_sub)
    blk = lambda f: pl.BlockSpec((1, BLOCK, LANES), f)
    own = lambda c: pl.BlockSpec((1, BLOCK * n_sub, LANES), lambda b, i: (b, i, c))
    return pl.pallas_call(
        _swa_kernel,
        grid=(batch, nb),
        in_specs=[
            pl.BlockSpec(memory_space=pltpu.SMEM),
            pl.BlockSpec((1, BLOCK * n_sub, 2 * LANES), lambda b, i: (b, i, 0)),
            blk(lambda b, i: (b, 0, 2)),
            blk(lambda b, i: (b, jnp.maximum(i * n_sub - 1, 0), 2)),
            own(2),
            blk(lambda b, i: (b, 0, 3)),
            blk(lambda b, i: (b, jnp.maximum(i * n_sub - 1, 0), 3)),
            own(3),
        ],
        out_specs=pl.BlockSpec((1, BLOCK * n_sub, 2 * LANES), lambda b, i: (b, i, 0)),
        out_shape=jax.ShapeDtypeStruct((batch, seq_len, 2 * LANES), BF16),
        compiler_params=pltpu.CompilerParams(dimension_semantics=("arbitrary", "arbitrary"),
                                             vmem_limit_bytes=VMEM_LIMIT),
        name="swa_attention",
    )(sinks, x, x, x, x, x, x, x)


def _causal_kernel(mode, q_ref, k_ref, v_ref, o_ref, stat_ref, acc_ref):
    seq_len = q_ref.shape[1]
    Q_TILE = stat_ref.shape[1]
    n_qt = (seq_len - BLOCK) // Q_TILE
    per_tile = Q_TILE // K_TILE
    lane = lax.broadcasted_iota(jnp.int32, (1, LANES), 1)
    lo_half = lane < HEAD_DIM
    if mode == "sb":
        r = lax.broadcasted_iota(jnp.int32, (2 * K_TILE, K_TILE), 0) & (K_TILE - 1)
        c = lax.broadcasted_iota(jnp.int32, (2 * K_TILE, K_TILE), 1)
        later2 = jnp.where(r > c, 1.0, 0.0).astype(BF16)
        r1 = lax.broadcasted_iota(jnp.int32, (2 * BLOCK, BLOCK), 0) & (BLOCK - 1)
        c1 = lax.broadcasted_iota(jnp.int32, (2 * BLOCK, BLOCK), 1)
        later1 = jnp.where(r1 > c1, 1.0, 0.0).astype(BF16)

    def causal(pq, k0, tk):
        pk = k0 + lax.broadcasted_iota(jnp.int32, (1, tk), 1)
        if mode == "fox":
            return pk <= pq
        if mode == "mla":
            return (pk >> CHUNK_SHIFT) <= (pq >> CHUNK_SHIFT)
        return pk < pq

    def head_v(k0, tk, hh):
        if mode == "sb":
            return v_ref[0, pl.ds(k0, tk), :]
        return v_ref[0, pl.ds(k0, tk), hh * LANES:(hh + 1) * LANES]

    def lane_tiles(x):
        return [x[:, j * LANES:(j + 1) * LANES] for j in range(x.shape[1] // LANES)]

    def row_parts(tq):
        step = min(tq, ROW_PART)
        return [(r0, step) for r0 in range(0, tq, step)]

    def softmax_chunk(q0, tq, k0, tk, masked, first):
        for hh in range(2):
            qh = q_ref[0, pl.ds(q0, tq), hh * LANES:(hh + 1) * LANES]
            kh = k_ref[0, pl.ds(k0, tk), hh * LANES:(hh + 1) * LANES]
            s_all = _dot_nt(qh, kh)
            vh = head_v(k0, tk, hh)
            for r0, tr in row_parts(tq):
                s = s_all[r0:r0 + tr]
                if masked:
                    pq = q0 + r0 + lax.broadcasted_iota(jnp.int32, (tr, 1), 0)
                    s = jnp.where(causal(pq, k0, tk), s, NEG)
                tiles = lane_tiles(s)
                top = tiles[0]
                for x in tiles[1:]:
                    top = jnp.maximum(top, x)
                m_new = jnp.broadcast_to(jnp.max(top, axis=-1, keepdims=True), (tr, LANES))
                if not first:
                    m_old = stat_ref[hh, r0:r0 + tr, :]
                    m_new = jnp.maximum(m_old, m_new)
                p = jnp.concatenate([jnp.exp2(x - m_new) for x in tiles], axis=1).astype(BF16)
                pv = _dot(p, vh)
                if not first:
                    pv = jnp.exp2(m_old - m_new) * acc_ref[hh, r0:r0 + tr, :] + pv
                stat_ref[hh, r0:r0 + tr, :] = m_new
                acc_ref[hh, r0:r0 + tr, :] = pv

    def stick_chunk(q0, tq, k0, tk, masked, first):
        later = later2 if tk == K_TILE else later1
        for hh in range(2):
            qh = q_ref[0, pl.ds(q0, tq), hh * LANES:(hh + 1) * LANES]
            kh = k_ref[0, pl.ds(k0, tk), hh * LANES:(hh + 1) * LANES]
            z = _dot_nt(qh, kh)
            ls_pos = jnp.minimum(z, 0.0) - jnp.log(1.0 + jnp.exp2(-jnp.abs(z))) * LOG2E
            log_keep = ls_pos - z
            if masked:
                pq = q0 + lax.broadcasted_iota(jnp.int32, (tq, 1), 0)
                vis = causal(pq, k0, tk)
                log_keep = jnp.where(vis, log_keep, 0.0)
            hi = log_keep.astype(BF16)
            lo = (log_keep - hi.astype(F32)).astype(BF16)
            after = _dot(jnp.concatenate([hi, lo], axis=1), later)
            tot = ls_pos + after
            chunk_total = jnp.broadcast_to(after[:, 0:1] + log_keep[:, 0:1], (tq, LANES))
            if not first:
                carry = stat_ref[hh, 0:tq, :]
                tot = jnp.concatenate([x + carry for x in lane_tiles(tot)], axis=1)
                chunk_total = carry + chunk_total
            a = jnp.exp2(tot)
            if masked:
                a = jnp.where(vis, a, 0.0)
            pv = _dot(a.astype(BF16), head_v(k0, tk, hh))
            if not first:
                pv = acc_ref[hh, 0:tq, :] + pv
            stat_ref[hh, 0:tq, :] = chunk_total
            acc_ref[hh, 0:tq, :] = pv

    def finish(q0, tq):
        a0, a1 = acc_ref[0, 0:tq, :], acc_ref[1, 0:tq, :]
        if mode != "sb":
            a0 = a0 / a0[:, HEAD_DIM:HEAD_DIM + 1]
            a1 = a1 / a1[:, 0:1]
        o_ref[0, pl.ds(q0, tq), :] = jnp.where(lo_half, a0, a1).astype(BF16)

    def chunk_start(j):
        return pl.multiple_of(BLOCK + j * K_TILE, BLOCK)

    if mode == "sb":
        stick_chunk(0, BLOCK, 0, BLOCK, True, True)
        finish(0, BLOCK)

        def q_body(i, _):
            q0 = pl.multiple_of(BLOCK + i * Q_TILE, BLOCK)
            n_int = i * per_tile
            for d in range(per_tile):
                stick_chunk(q0, Q_TILE, chunk_start(n_int + per_tile - 1 - d), K_TILE, True, d == 0)

            def alive():
                top = jnp.maximum(jnp.max(stat_ref[0]), jnp.max(stat_ref[1]))
                return (top > UNDERFLOW_LOG2).astype(jnp.int32)

            def body(st):
                jj, _ = st
                stick_chunk(q0, Q_TILE, chunk_start(n_int - 1 - jj), K_TILE, False, False)
                return jj + 1, alive()

            _, go = lax.while_loop(lambda st: (st[0] < n_int) & (st[1] > 0), body, (0, alive()))

            @pl.when(go > 0)
            def _():
                stick_chunk(q0, Q_TILE, 0, BLOCK, False, False)

            finish(q0, Q_TILE)
            return 0
    else:
        softmax_chunk(0, BLOCK, 0, BLOCK, True, True)
        finish(0, BLOCK)

        def q_body(i, _):
            q0 = pl.multiple_of(BLOCK + i * Q_TILE, BLOCK)
            n_int = i * per_tile
            softmax_chunk(q0, Q_TILE, 0, BLOCK, False, True)

            def body(j, _):
                for d in range(per_tile):
                    softmax_chunk(q0, Q_TILE, chunk_start(j * per_tile + d), K_TILE, False, False)
                return 0

            lax.fori_loop(0, i, body, 0)
            for d in range(per_tile):
                softmax_chunk(q0, Q_TILE, chunk_start(n_int + d), K_TILE, True, False)
            finish(q0, Q_TILE)
            return 0

    lax.fori_loop(0, n_qt, q_body, 0)


def _causal_attention(mode, q, k, v, batch, seq_len):
    wide = pl.BlockSpec((1, seq_len, 2 * LANES), lambda b, p: (b, 0, p))
    narrow = pl.BlockSpec((1, seq_len, LANES), lambda b, p: (b, 0, p))
    args = [q.reshape(batch, seq_len, 512), k.reshape(batch, seq_len, 512),
            v.reshape(batch, seq_len, v.shape[1])]
    return pl.pallas_call(
        functools.partial(_causal_kernel, mode),
        grid=(batch, 2),
        in_specs=[wide, wide, narrow if mode == "sb" else wide],
        out_specs=narrow,
        out_shape=jax.ShapeDtypeStruct((batch, seq_len, 2 * LANES), BF16),
        scratch_shapes=[pltpu.VMEM((2, Q_TILES[mode], LANES), F32), pltpu.VMEM((2, Q_TILES[mode], LANES), F32)],
        compiler_params=pltpu.CompilerParams(dimension_semantics=("arbitrary", "arbitrary"),
                                             vmem_limit_bytes=VMEM_LIMIT),
        name=mode + "_attention",
    )(*args)


def _outproj_kernel(ya_ref, yb_ref, yc_ref, yd_ref, h_ref, wo_ref, g_ref, wrh_ref, wrl_ref, br_ref, tri_ref,
                    h2_ref, xn_ref, route_ref, cnt_ref):
    o = (_dot(ya_ref[...], wo_ref[0]) + _dot(yb_ref[...], wo_ref[1])
         + _dot(yc_ref[...], wo_ref[2]) + _dot(yd_ref[...], wo_ref[3]))
    h2 = h_ref[...] + o
    h2_ref[...] = h2
    xn = _rms(h2, g_ref[...])
    _rows_to_tiles(xn_ref, (), xn)
    xh = xn.astype(BF16)
    xl = (xn - xh.astype(F32)).astype(BF16)
    wrh, wrl = wrh_ref[...], wrl_ref[...]
    lg = _dot(xh, wrh) + _dot(xl, wrh) + _dot(xh, wrl) + br_ref[...]

    tm = lg.shape[0]
    lane = lax.broadcasted_iota(jnp.int32, (tm, LANES), 1)
    ninf = -jnp.inf

    def first_max(x):
        top = jnp.max(x, axis=-1, keepdims=True)
        return top, jnp.min(jnp.where(x == top, lane, LANES), axis=-1, keepdims=True)

    gl = jnp.where(lane < N_GROUPS, lg, ninf)
    g_max, g_top = first_max(gl)
    g_w = 1.0 / jnp.sum(jnp.exp(gl - g_max), axis=-1, keepdims=True)
    e_lo = N_GROUPS + g_top * EXPERTS_PER_GROUP
    el = jnp.where((lane >= e_lo) & (lane < e_lo + EXPERTS_PER_GROUP), lg, ninf)
    v1, i1 = first_max(el)
    v2, i2 = first_max(jnp.where(lane == i1, ninf, el))
    r21 = jnp.exp(v2 - v1)
    w1 = g_w / (1.0 + r21)
    w2 = w1 * r21

    @pl.when(pl.program_id(0) == 0)
    def _():
        cnt_ref[...] = jnp.zeros_like(cnt_ref)

    m1 = jnp.where(lane == i1, 1.0, 0.0)
    m2 = jnp.where(lane == i2, 1.0, 0.0)
    both = m1 + m2
    before = _dot(tri_ref[...], both.astype(BF16)) + cnt_ref[0:1, :]
    rank1 = jnp.sum(m1 * before, axis=-1, keepdims=True)
    rank2 = jnp.sum(m2 * before, axis=-1, keepdims=True)
    cnt_ref[...] = cnt_ref[...] + jnp.sum(both, axis=0, keepdims=True)
    cols = [w1, w2, (i1 - N_GROUPS).astype(F32), (i2 - N_GROUPS).astype(F32), rank1, rank2]
    route = jnp.zeros((tm, LANES), F32)
    for j, c in enumerate(cols):
        route = jnp.where(lane == j, c, route)
    route_ref[...] = route


def _outproj(ys, h, p):
    t, d = h.shape
    tm = ROW_TILE
    row = lambda i: (i, 0)
    fixed2 = lambda i: (0, 0)
    in_specs = [pl.BlockSpec((tm, 256), row)] * 4 + [
        pl.BlockSpec((tm, d), row),
        pl.BlockSpec((4, 256, d), lambda i: (0, 0, 0)),
        pl.BlockSpec((1, d), fixed2),
        pl.BlockSpec((d, LANES), fixed2),
        pl.BlockSpec((d, LANES), fixed2),
        pl.BlockSpec((1, LANES), fixed2),
        pl.BlockSpec((tm, tm), fixed2),
    ]
    earlier = jnp.asarray(np.tril(np.ones((tm, tm), np.float32), -1), BF16)
    return pl.pallas_call(
        _outproj_kernel,
        grid=(t // tm,),
        in_specs=in_specs,
        out_specs=[pl.BlockSpec((tm, d), row), pl.BlockSpec((tm * ROW_SUB, LANES), row),
                   pl.BlockSpec((tm, LANES), row), pl.BlockSpec((8, LANES), fixed2)],
        out_shape=[jax.ShapeDtypeStruct((t, d), F32), jax.ShapeDtypeStruct((t * ROW_SUB, LANES), F32),
                   jax.ShapeDtypeStruct((t, LANES), F32), jax.ShapeDtypeStruct((8, LANES), F32)],
        compiler_params=pltpu.CompilerParams(dimension_semantics=("arbitrary",),
                                             vmem_limit_bytes=VMEM_LIMIT),
        name="outproj_router",
    )(*ys, h, p["w_out"], p["ffn_norm"], p["w_r_hi"], p["w_r_lo"], p["b_r"], earlier)


def _dispatch_kernel(nv_ref, ids_ref, x_ref, xs_hbm, zero_buf, sem):
    blk = zero_buf.shape[0]

    @pl.when(pl.program_id(0) == 0)
    def _():
        zero_buf[...] = jnp.zeros_like(zero_buf)
        for phase in ("start", "wait"):
            for b in range(nv_ref.shape[0]):
                @pl.when(nv_ref[b] < MOE_BLOCK)
                def _():
                    copy = pltpu.make_async_copy(zero_buf, xs_hbm.at[pl.ds(b * blk, blk)], sem)
                    copy.start() if phase == "start" else copy.wait()

    for j in range(ids_ref.shape[2]):
        tok, _ = divmod(j, TOP_K)
        dst = pl.multiple_of(ids_ref[0, 0, j], ROW_SUB)
        pltpu.make_async_copy(x_ref.at[pl.ds(tok * ROW_SUB, ROW_SUB)], xs_hbm.at[pl.ds(dst, ROW_SUB)],
                              sem).start(priority=j % 2)
    for _ in range(TOP_K):
        pltpu.make_async_copy(x_ref, xs_hbm.at[pl.ds(0, x_ref.shape[0])], sem).wait()


def _dispatch(xn, y_ids, nvalid):
    tm = ROW_TILE
    t = xn.shape[0] // ROW_SUB
    n_blk = nvalid.shape[0]
    return pl.pallas_call(
        _dispatch_kernel,
        grid_spec=pltpu.PrefetchScalarGridSpec(
            num_scalar_prefetch=1,
            grid=(t // tm,),
            in_specs=[pl.BlockSpec((1, 1, TOP_K * tm), lambda i, nv: (i, 0, 0), memory_space=pltpu.SMEM),
                      pl.BlockSpec((tm * ROW_SUB, LANES), lambda i, nv: (i, 0))],
            out_specs=pl.BlockSpec(memory_space=pl.ANY),
            scratch_shapes=[pltpu.VMEM((MOE_BLOCK * ROW_SUB, LANES), F32), pltpu.SemaphoreType.DMA],
        ),
        out_shape=jax.ShapeDtypeStruct((n_blk * MOE_BLOCK * ROW_SUB, LANES), F32),
        compiler_params=pltpu.CompilerParams(dimension_semantics=("arbitrary",),
                                             vmem_limit_bytes=VMEM_LIMIT),
        name="moe_dispatch",
    )(nvalid, y_ids.reshape(t // tm, 1, TOP_K * tm), xn)


def _moe_kernel(be_ref, nv_ref, x_ref, wg_ref, wu_ref, wd_ref, y_ref, wg_bf, wu_bf, wd_bf):
    s = pl.program_id(0)
    nv = nv_ref[s]

    @pl.when((s == 0) | (be_ref[s] != be_ref[jnp.maximum(s - 1, 0)]))
    def _():
        wg_bf[...] = wg_ref[0].astype(BF16)
        wu_bf[...] = wu_ref[0].astype(BF16)
        wd_bf[...] = wd_ref[0].astype(BF16)

    @pl.when(nv > 0)
    def _():
        x = _rows_from_tiles(x_ref, (), MOE_BLOCK).astype(BF16)
        gate = _dot(x, wg_bf[...])
        up = _dot(x, wu_bf[...])
        hid = (gate * (1.0 / (1.0 + jnp.exp(-gate))) * up).astype(BF16)
        _rows_to_tiles(y_ref, (), _dot(hid, wd_bf[...]))

    @pl.when(nv == 0)
    def _():
        y_ref[...] = jnp.zeros_like(y_ref)


def _moe(xs, block_e, nvalid, p):
    d = xs.shape[1] * ROW_SUB
    n_blk = block_e.shape[0]
    hdim = p["w_gate"].shape[2]
    wspec = lambda shape: pl.BlockSpec((1,) + shape, lambda s, be, nv: (be[s], 0, 0))
    rows = pl.BlockSpec((MOE_BLOCK * ROW_SUB, LANES), lambda s, be, nv: (s, 0))
    return pl.pallas_call(
        _moe_kernel,
        grid_spec=pltpu.PrefetchScalarGridSpec(
            num_scalar_prefetch=2,
            grid=(n_blk,),
            in_specs=[rows, wspec((d, hdim)), wspec((d, hdim)), wspec((hdim, d))],
            out_specs=rows,
            scratch_shapes=[pltpu.VMEM((d, hdim), BF16), pltpu.VMEM((d, hdim), BF16), pltpu.VMEM((hdim, d), BF16)],
        ),
        out_shape=jax.ShapeDtypeStruct(xs.shape, F32),
        compiler_params=pltpu.CompilerParams(dimension_semantics=("arbitrary",),
                                             vmem_limit_bytes=VMEM_LIMIT),
        name="moe_experts",
    )(block_e, nvalid, xs, p["w_gate"], p["w_up"], p["w_down"])


def _route(route, counts, t):
    a = t * TOP_K
    expert = route[:, 2:4].astype(jnp.int32).reshape(a)
    pos = route[:, 4:6].astype(jnp.int32).reshape(a)
    counts = counts.astype(jnp.int32)
    padded = (counts + MOE_BLOCK - 1) // MOE_BLOCK * MOE_BLOCK
    pad_end = jnp.cumsum(padded)
    pad_start = pad_end - padded
    dest = pad_start[expert] + pos
    n_blk = (a + N_EXPERTS * MOE_BLOCK) // MOE_BLOCK
    starts = jnp.arange(n_blk, dtype=jnp.int32) * MOE_BLOCK
    block_e = jnp.minimum(jnp.sum((pad_end[None, :] <= starts[:, None]).astype(jnp.int32), axis=1), N_EXPERTS - 1)
    nvalid = jnp.clip((pad_start + counts)[block_e] - starts, 0, MOE_BLOCK)
    return block_e, nvalid, dest * ROW_SUB


def _final_kernel(h_ref, ids0_ref, idsn_ref, y_hbm, ew_ref, g_ref, o_ref, ybuf, ysem):
    h = _combine_experts(h_ref[...], ew_ref[...], y_hbm, ids0_ref, idsn_ref, ybuf, ysem,
                         pl.program_id(0), pl.num_programs(0))
    o_ref[0] = _rms(h, g_ref[...])


def _final(h, y2, y_ids, ew, g, batch, seq_len):
    t, d = h.shape
    per_seq = seq_len // BLOCK
    out_blocks = per_seq - 1
    row = lambda n: ((n // out_blocks) * per_seq + n % out_blocks + 1, 0)
    ids = y_ids.reshape(batch, per_seq, TOP_K * BLOCK)[:, 1:].reshape(batch * out_blocks, 1, TOP_K * BLOCK)
    ids = jnp.concatenate([ids, jnp.zeros_like(ids[:1])], axis=0)
    return pl.pallas_call(
        _final_kernel,
        grid=(batch * out_blocks,),
        in_specs=[pl.BlockSpec((BLOCK, d), row),
                  pl.BlockSpec((1, 1, TOP_K * BLOCK), lambda n: (0, 0, 0), memory_space=pltpu.SMEM),
                  pl.BlockSpec((1, 1, TOP_K * BLOCK), lambda n: (n + 1, 0, 0), memory_space=pltpu.SMEM),
                  pl.BlockSpec(memory_space=pl.ANY),
                  pl.BlockSpec((BLOCK, LANES), row),
                  pl.BlockSpec((1, d), lambda n: (0, 0))],
        out_specs=pl.BlockSpec((1, BLOCK, d), lambda n: (n // out_blocks, n % out_blocks, 0)),
        out_shape=jax.ShapeDtypeStruct((batch, seq_len - BLOCK, d), F32),
        scratch_shapes=[pltpu.VMEM((2, TOP_K, BLOCK * ROW_SUB, LANES), F32), pltpu.SemaphoreType.DMA((2,))],
        compiler_params=pltpu.CompilerParams(dimension_semantics=("arbitrary",),
                                             vmem_limit_bytes=VMEM_LIMIT),
        name="final_norm",
    )(h, ids, ids, y2, ew, g)


def _rope_table(seq_len):
    half = MLA_ROPE // 2
    pos = (jnp.arange(seq_len, dtype=jnp.int32) - PAD_FRONT).astype(F32)
    inv_freq = ROPE_THETA ** (-jnp.arange(half, dtype=F32) / half)
    ang = pos[:, None] * inv_freq[None, :]
    cos, sin = jnp.cos(ang), jnp.sin(ang)
    cos2 = jnp.concatenate([cos, cos], axis=1)
    sin2 = jnp.concatenate([-sin, sin], axis=1)
    z = lambda w: jnp.zeros((seq_len, w), F32)
    scale = (MLA_NOPE + MLA_ROPE) ** -0.5 * LOG2E
    cos_q = jnp.concatenate([jnp.ones((seq_len, MLA_NOPE), F32), cos2, z(32)], axis=1) * scale
    sin_q = jnp.concatenate([z(MLA_NOPE), sin2, z(32)], axis=1) * scale
    cos_k = jnp.concatenate([z(MLA_NOPE), cos2, z(32)], axis=1)
    sin_k = jnp.concatenate([z(MLA_NOPE), sin2, z(32)], axis=1)
    return jnp.concatenate([cos_q, sin_q, cos_k, sin_k], axis=1)


def _swap_halves(w):
    half = w.shape[-1] // 2
    return jnp.concatenate([w[..., half:], w[..., :half]], axis=-1)


def _layer_params(i, seq_len, attn_norm, w_in, b_forget, sinks, mla_q_norm, mla_kv_norm, mla_w_uq,
                  mla_w_ukv, w_out, ffn_norm, w_group, b_group, w_router, b_router, w_gate, w_up, w_down):
    d = w_in.shape[1]
    w = w_in[i]
    sizes = (256, 128, 128, 256, 256, 256, 4, 256, 128, 32, 256, 256, 256)
    offs = np.concatenate([[0], np.cumsum(sizes)])
    (a_q, a_k, a_v, f_q, f_k, f_v, f_g, c_q, c_kv, c_kr, s_q, s_k, s_v) = [
        w[:, offs[j]:offs[j + 1]] for j in range(len(sizes))]
    qscale = HEAD_DIM ** -0.5
    grp = SWA_HEADS // SWA_KV_HEADS
    a_q = a_q.reshape(d, SWA_KV_HEADS, grp, HEAD_DIM).transpose(0, 2, 1, 3).reshape(d, 256)
    z = lambda n: jnp.zeros((d, n), F32)
    g_grp = jnp.concatenate([f_g[:, 0:2], z(62), c_kr, z(32)], axis=1)
    gs_grp = jnp.concatenate([f_g[:, 2:4], z(62), _swap_halves(c_kr), z(32)], axis=1)
    w_perm = jnp.concatenate([a_q * qscale, a_k, a_v, f_q * qscale, f_k, f_v, s_q * qscale, s_k, s_v,
                              c_q, c_kv, g_grp, gs_grp], axis=1).astype(BF16)
    wuq = mla_w_uq[i].reshape(MLA_Q_LORA, 4, MLA_NOPE + MLA_ROPE)
    zq = lambda n: jnp.zeros((MLA_Q_LORA, 4, n), F32)
    w_uq_a = jnp.concatenate([wuq, zq(32)], axis=2).reshape(MLA_Q_LORA, 512).astype(BF16)
    w_uq_b = jnp.concatenate([zq(MLA_NOPE), _swap_halves(wuq[:, :, MLA_NOPE:]), zq(32)],
                             axis=2).reshape(MLA_Q_LORA, 512).astype(BF16)
    wukv = mla_w_ukv[i].reshape(MLA_KV_LORA, 4, MLA_NOPE + MLA_V)
    w_kv_k = jnp.concatenate([wukv[:, :, :MLA_NOPE], jnp.zeros((MLA_KV_LORA, 4, 64), F32)],
                             axis=2).reshape(MLA_KV_LORA, 512).astype(BF16)
    w_kv_v = wukv[:, :, MLA_NOPE:].reshape(MLA_KV_LORA, 256).astype(BF16)
    bf = b_forget[i].astype(F32)
    b_f = jnp.zeros((1, 256), F32).at[0, 0:2].set(bf[0:2]).at[0, 128:130].set(bf[2:4])
    wo = w_out[i]
    wo_a = wo[:256].reshape(SWA_KV_HEADS, grp, HEAD_DIM, d).transpose(1, 0, 2, 3).reshape(256, d)
    wo4 = jnp.concatenate([wo_a, wo[256:]], axis=0).reshape(4, 256, d).astype(BF16)
    w_r = jnp.concatenate([w_group[i], w_router[i], jnp.zeros((d, LANES - N_GROUPS - N_EXPERTS), F32)], axis=1)
    w_r_hi = w_r.astype(BF16)
    w_r_lo = (w_r - w_r_hi.astype(F32)).astype(BF16)
    b_r = jnp.concatenate([b_group[i], b_router[i], jnp.zeros((LANES - N_GROUPS - N_EXPERTS,), F32)])[None, :]
    return dict(
        attn_norm=attn_norm[i][None, :], w_in=w_perm, q_norm=mla_q_norm[i][None, :],
        kv_norm=mla_kv_norm[i][None, :], w_uq_a=w_uq_a, w_uq_b=w_uq_b, w_kv_k=w_kv_k, w_kv_v=w_kv_v,
        rope_tab=_rope_table(seq_len), b_forget=b_f, sinks=sinks[i].astype(F32), w_out=wo4,
        ffn_norm=ffn_norm[i][None, :], w_r_hi=w_r_hi, w_r_lo=w_r_lo, b_r=b_r.astype(F32),
        w_gate=w_gate[i], w_up=w_up[i], w_down=w_down[i])


def kernel(x, meta_tokens, attn_norm, w_in, b_forget, sinks, mla_q_norm, mla_kv_norm, mla_w_uq, mla_w_ukv,
           w_out, ffn_norm, w_group, b_group, w_router, b_router, w_gate, w_up, w_down, final_norm):
    batch, seq, d = x.shape
    seq_len = seq + BLOCK
    assert seq_len % ROW_TILE == 0 and seq_len % (BLOCK * SWA_Q_BLOCKS) == 0
    assert all(seq % tile == 0 for tile in Q_TILES.values())
    t = batch * seq_len
    depth = w_in.shape[0]
    pad = jnp.zeros((batch, PAD_FRONT, d), x.dtype)
    meta = jnp.broadcast_to(meta_tokens.astype(x.dtype)[None], (batch, N_META, d))
    h = jnp.concatenate([pad, meta, x], axis=1).reshape(t, d)
    y2 = y_ids = ew = None
    for i in range(depth):
        p = _layer_params(i, seq_len, attn_norm, w_in, b_forget, sinks, mla_q_norm, mla_kv_norm, mla_w_uq,
                          mla_w_ukv, w_out, ffn_norm, w_group, b_group, w_router, b_router, w_gate, w_up, w_down)
        h, (qa, fq, fk, fv, cq, ck, cv, sq, sk, sv) = _inproj(h, y2, y_ids, ew, p, seq_len)
        y_a = _swa_attention(qa, p["sinks"], batch, seq_len)
        y_b = _causal_attention("fox", fq, fk, fv, batch, seq_len)
        y_c = _causal_attention("mla", cq, ck, cv, batch, seq_len)
        y_d = _causal_attention("sb", sq, sk, sv, batch, seq_len)
        ys = [y.reshape(t, 256) for y in (y_a, y_b, y_c, y_d)]
        h, xn, ew, counts = _outproj(ys, h, p)
        block_e, nvalid, y_ids = _route(ew, counts[0, N_GROUPS:N_GROUPS + N_EXPERTS], t)
        y2 = _moe(_dispatch(xn, y_ids, nvalid), block_e, nvalid, p)
    return _final(h, y2, y_ids, ew, final_norm[None, :], batch, seq_len)
```

```python
import functools

import jax
import jax.numpy as jnp
import numpy as np
from jax import lax
from jax.experimental import pallas as pl
from jax.experimental.pallas import tpu as pltpu

F32 = jnp.float32
BF16 = jnp.bfloat16

BLOCK = 128
N_META = 16
PAD_FRONT = BLOCK - N_META
CHUNK_SHIFT = 6
HEAD_DIM = 64
NORM_EPS = 1e-6
NEG = -1e30
PAD_KEY_LOGIT = -(2.0 ** 100)
UNDERFLOW_LOG2 = -150.0
LOG2E = 1.4426950408889634
BIG = 1 << 30
SWA_HEADS, SWA_KV_HEADS, WINDOW = 4, 2, 128
MLA_Q_LORA, MLA_KV_LORA, MLA_NOPE, MLA_ROPE, MLA_V = 256, 128, 64, 32, 64
MLA_BIAS_LANE = MLA_NOPE + MLA_ROPE
ROPE_THETA = 10000.0
N_GROUPS, EXPERTS_PER_GROUP, TOP_K = 4, 8, 2
N_EXPERTS = N_GROUPS * EXPERTS_PER_GROUP
MOE_BLOCK = 256
LANES = 128
ROW_SUB = 8
ROW_TILE = 384
Q_TILES = {"fox": 1024, "mla": 1024, "sb": 512}
K_TILE = 256
ROW_PART = 256
SWA_Q_BLOCKS = 3
VMEM_LIMIT = 56 * 1024 * 1024

C_A, C_B, C_D, C_CQ, C_CKV, C_G, C_GS, C_END = 0, 512, 1280, 2048, 2304, 2432, 2560, 2688
B_F0, B_F1, B_PAD = 0, 3, 6


def _rms(x, g):
    return x * lax.rsqrt(jnp.mean(x * x, axis=-1, keepdims=True) + NORM_EPS) * g


def _log_sigmoid(x):
    return jnp.minimum(x, 0.0) - jnp.log(1.0 + jnp.exp(-jnp.abs(x)))


def _dot(a, b):
    return jnp.dot(a, b, preferred_element_type=F32)


def _dot_nt(a, b):
    return lax.dot_general(a, b, (((1,), (1,)), ((), ())), preferred_element_type=F32)


def _rows_from_tiles(ref, lead, n):
    return jnp.concatenate([ref[(*lead, pl.ds(j, n, stride=ROW_SUB), slice(None))] for j in range(ROW_SUB)], axis=1)


def _rows_to_tiles(ref, lead, x):
    n = x.shape[0]
    for j in range(ROW_SUB):
        ref[(*lead, pl.ds(j, n, stride=ROW_SUB), slice(None))] = x[:, j * LANES:(j + 1) * LANES]


def _tile4(x):
    return jnp.concatenate([x, x, x, x], axis=1)


def _split3(x):
    hi = x.astype(BF16)
    r1 = x - hi.astype(F32)
    mid = r1.astype(BF16)
    lo = (r1 - mid.astype(F32)).astype(BF16)
    return hi, mid, lo


def _free_base(head):
    return head * LANES + (HEAD_DIM if head % 2 == 0 else 0)


def _expert_rows_start(y_hbm, ids_ref, buf, sem, slot):
    for j in range(ids_ref.shape[2]):
        tok, k = divmod(j, TOP_K)
        src = pl.multiple_of(ids_ref[0, 0, j], ROW_SUB)
        pltpu.make_async_copy(y_hbm.at[pl.ds(src, ROW_SUB)], buf.at[slot, k, pl.ds(tok * ROW_SUB, ROW_SUB)],
                              sem.at[slot]).start(priority=j % 2)


def _expert_rows_wait(y_hbm, buf, sem, slot):
    for k in range(TOP_K):
        pltpu.make_async_copy(y_hbm.at[pl.ds(0, buf.shape[2])], buf.at[slot, k], sem.at[slot]).wait()


def _combine_experts(h, ew, y_hbm, ids0_ref, idsn_ref, buf, sem, step, n_steps):
    slot = step % 2

    @pl.when(step == 0)
    def _():
        _expert_rows_start(y_hbm, ids0_ref, buf, sem, 0)

    _expert_rows_start(y_hbm, idsn_ref, buf, sem, 1 - slot)
    _expert_rows_wait(y_hbm, buf, sem, slot)
    n = h.shape[0]
    out = h + ew[:, 0:1] * _rows_from_tiles(buf, (slot, 0), n) + ew[:, 1:2] * _rows_from_tiles(buf, (slot, 1), n)

    @pl.when(step == n_steps - 1)
    def _():
        _expert_rows_wait(y_hbm, buf, sem, 1 - slot)

    return out


def _inproj_kernel(has_y2, n_seq_tiles, *refs):
    if has_y2:
        (h_ref, ids0_ref, idsn_ref, y_hbm, ew_ref, *rest) = refs
    else:
        (lead_ref, *x_refs) = refs[:1 + ROW_TILE // BLOCK]
        rest = refs[1 + ROW_TILE // BLOCK:]
    (g_ref, w_ref, qn_ref, kvn_ref, wuqa_ref, wuqb_ref, wkvk_ref, wkvv_ref, tab_ref, bf_ref,
     pq_ref, pk_ref, rows_ref, hout_ref, *outs) = rest
    if has_y2:
        *outs, ybuf, ysem = outs
    (qa_ref, fq_ref, fk_ref, fv_ref, cq_ref, ck_ref, cv_ref, sq_ref, sk_ref, sv_ref, carry_ref) = outs
    tile = pl.program_id(0) % n_seq_tiles
    if has_y2:
        h = _combine_experts(h_ref[...], ew_ref[...], y_hbm, ids0_ref, idsn_ref, ybuf, ysem,
                             pl.program_id(0), pl.num_programs(0))
    else:
        first = jnp.where(tile == 0, lead_ref[...], x_refs[0][0])
        h = jnp.concatenate([first] + [r[0] for r in x_refs[1:]], axis=0)
    hout_ref[...] = h
    tm, d = h.shape
    xn = _rms(h, g_ref[...]).astype(BF16)
    acc = _dot(xn, w_ref[...])
    lane = lax.broadcasted_iota(jnp.int32, (1, LANES), 1)
    lo_half = lane < HEAD_DIM
    pad_col = jnp.where(tile * tm + lax.broadcasted_iota(jnp.int32, (tm, 1), 0) < PAD_FRONT,
                        PAD_KEY_LOGIT, 0.0)
    rows = rows_ref[...]
    fq_one, fk_one, pad_lane, sq_one, mla_one, mla_pad = (rows[j:j + 1] for j in range(6))

    def per_head(x_pair, bias, pair, scale=None):
        x = x_pair if scale is None else x_pair * scale
        even = jnp.where(lo_half, x, bias[:, (2 * pair) * LANES:(2 * pair + 1) * LANES])
        odd = jnp.where(lo_half, bias[:, (2 * pair + 1) * LANES:(2 * pair + 2) * LANES], x)
        return even, odd

    def store_heads(ref, x_off, bias, scale=None):
        for pair in range(2):
            x_pair = acc[:, x_off + pair * LANES:x_off + (pair + 1) * LANES]
            even, odd = per_head(x_pair, bias, pair, scale)
            ref[:, (2 * pair) * LANES:(2 * pair + 1) * LANES] = even.astype(BF16)
            ref[:, (2 * pair + 1) * LANES:(2 * pair + 2) * LANES] = odd.astype(BF16)

    qa_ref[...] = acc[:, C_A:C_B].astype(BF16)

    @pl.when(tile == 0)
    def _():
        carry_ref[...] = jnp.zeros_like(carry_ref)

    lf = _log_sigmoid(acc[:, C_G:C_END] + bf_ref[...]) * LOG2E
    r = lax.broadcasted_iota(jnp.int32, (BLOCK, BLOCK), 0)
    c = lax.broadcasted_iota(jnp.int32, (BLOCK, BLOCK), 1)
    tri = jnp.where(c <= r, 1.0, 0.0).astype(BF16)
    carry = carry_ref[...]
    blocks = []
    for b in range(tm // BLOCK):
        hi, mid, lo = _split3(lf[b * BLOCK:(b + 1) * BLOCK])
        y = _dot(tri, hi) + _dot(tri, mid) + _dot(tri, lo) + carry
        carry = y[BLOCK - 1:BLOCK, :]
        blocks.append(y)
    carry_ref[...] = carry
    f_hi, f_mid, f_lo = _split3(jnp.concatenate(blocks, axis=0))
    q_bias = _dot(f_hi, pq_ref[0]) + _dot(f_mid, pq_ref[1]) + _dot(f_lo, pq_ref[2]) + fq_one
    k_bias = (_dot(f_hi, pk_ref[0]) + _dot(f_mid, pk_ref[1]) + _dot(f_lo, pk_ref[2]) + fk_one
              + pad_col * pad_lane)
    ones = jnp.ones((tm, 4 * LANES), F32)
    store_heads(fq_ref, C_B, q_bias, LOG2E)
    store_heads(fk_ref, C_B + 256, k_bias)
    store_heads(fv_ref, C_B + 512, ones)

    store_heads(sq_ref, C_D, jnp.broadcast_to(sq_one, (tm, 4 * LANES)), LOG2E)
    store_heads(sk_ref, C_D + 256, pad_col * pad_lane)
    sv_ref[...] = acc[:, C_D + 512:C_CQ].astype(BF16)

    tab = tab_ref[...]
    cos_q, sin_q = tab[:, 0:128], tab[:, 128:256]
    cos_k, sin_k = tab[:, 256:384], tab[:, 384:512]
    cqn = _rms(acc[:, C_CQ:C_CKV], qn_ref[...]).astype(BF16)
    q_lin = _dot(cqn, wuqa_ref[...])
    q_swp = _dot(cqn, wuqb_ref[...])
    cq_ref[...] = (q_lin * _tile4(cos_q) + q_swp * _tile4(sin_q) + mla_one).astype(BF16)
    ckvn = _rms(acc[:, C_CKV:C_G], kvn_ref[...]).astype(BF16)
    k_nope = _dot(ckvn, wkvk_ref[...])
    grp, grp_s = acc[:, C_G:C_GS], acc[:, C_GS:C_END]
    k_rope = grp * cos_k + grp_s * sin_k
    ck_ref[...] = (k_nope + _tile4(k_rope) + pad_col * mla_pad).astype(BF16)
    vv = _dot(ckvn, wkvv_ref[...])
    for pair in range(2):
        even, odd = per_head(vv[:, pair * LANES:(pair + 1) * LANES], ones, pair)
        cv_ref[:, (2 * pair) * LANES:(2 * pair + 1) * LANES] = even.astype(BF16)
        cv_ref[:, (2 * pair + 1) * LANES:(2 * pair + 2) * LANES] = odd.astype(BF16)


def _bias_constants():
    src = (0, 1, LANES, LANES + 1)
    pq = np.zeros((3, 2 * LANES, 4 * LANES), np.float32)
    pk = np.zeros((3, 2 * LANES, 4 * LANES), np.float32)
    rows = np.zeros((8, 4 * LANES), np.float32)
    for head in range(4):
        base = _free_base(head)
        for part in range(3):
            pq[part, src[head], base + B_F0 + part] = 1.0
            pk[part, src[head], base + B_F1 + part] = -1.0
            rows[0, base + B_F1 + part] = 1.0
            rows[1, base + B_F0 + part] = 1.0
        rows[0, base + B_PAD] = 1.0
        rows[2, base + B_PAD] = 1.0
        rows[3, base + B_PAD] = 1.0
        rows[4, head * LANES + MLA_BIAS_LANE] = 1.0
        rows[5, head * LANES + MLA_BIAS_LANE] = 1.0
    return jnp.asarray(pq, BF16), jnp.asarray(pk, BF16), jnp.asarray(rows, F32)


def _inproj(h, y2, y_ids, ew, p, seq_len):
    tm = ROW_TILE
    n_seq_tiles = seq_len // tm
    has_y2 = y2 is not None
    row = lambda i: (i, 0)
    fixed = lambda i: (0, 0)
    if has_y2:
        t, d = h.shape
        in_specs = [pl.BlockSpec((tm, d), row)]
        args = [h]
    else:
        lead, x = h
        batch, seq, d = x.shape
        t = batch * seq_len
        per_tile = tm // BLOCK
        frame = lambda k: pl.BlockSpec((1, BLOCK, d), lambda i: (
            i // n_seq_tiles, jnp.maximum((i % n_seq_tiles) * per_tile - 1 + k, 0), 0))
        in_specs = [pl.BlockSpec((BLOCK, d), fixed)] + [frame(k) for k in range(per_tile)]
        args = [lead] + [x] * per_tile
    if has_y2:
        ids = y_ids.reshape(t // tm, 1, TOP_K * tm)
        ids = jnp.concatenate([ids, jnp.zeros_like(ids[:1])], axis=0)
        in_specs += [pl.BlockSpec((1, 1, TOP_K * tm), lambda i: (0, 0, 0), memory_space=pltpu.SMEM),
                     pl.BlockSpec((1, 1, TOP_K * tm), lambda i: (i + 1, 0, 0), memory_space=pltpu.SMEM),
                     pl.BlockSpec(memory_space=pl.ANY),
                     pl.BlockSpec((tm, LANES), row)]
        args += [ids, ids, y2, ew]
    pq, pk, rows = _bias_constants()
    consts = [p["attn_norm"], p["w_in"], p["q_norm"], p["kv_norm"], p["w_uq_a"], p["w_uq_b"],
              p["w_kv_k"], p["w_kv_v"]]
    in_specs += [pl.BlockSpec(c.shape, fixed) for c in consts]
    args += consts
    in_specs.append(pl.BlockSpec((tm, 512), lambda i: (i % n_seq_tiles, 0)))
    args.append(p["rope_tab"])
    in_specs += [pl.BlockSpec((1, 256), fixed),
                 pl.BlockSpec(pq.shape, lambda i: (0, 0, 0)),
                 pl.BlockSpec(pk.shape, lambda i: (0, 0, 0)),
                 pl.BlockSpec(rows.shape, fixed)]
    args += [p["b_forget"], pq, pk, rows]
    widths = [512] * 9 + [256]
    out_shape = [jax.ShapeDtypeStruct((t, d), F32)] + [jax.ShapeDtypeStruct((t, w), BF16) for w in widths]
    out_specs = [pl.BlockSpec((tm, d), row)] + [pl.BlockSpec((tm, w), row) for w in widths]
    outs = pl.pallas_call(
        functools.partial(_inproj_kernel, has_y2, n_seq_tiles),
        grid=(t // tm,),
        in_specs=in_specs,
        out_specs=out_specs,
        out_shape=out_shape,
        scratch_shapes=[pltpu.VMEM((1, 2 * LANES), F32)] + (
            [pltpu.VMEM((2, TOP_K, tm * ROW_SUB, LANES), F32), pltpu.SemaphoreType.DMA((2,))] if has_y2 else []),
        compiler_params=pltpu.CompilerParams(dimension_semantics=("arbitrary",),
                                             vmem_limit_bytes=VMEM_LIMIT),
        name="inproj_y2" if has_y2 else "inproj",
    )(*args)
    return outs[0], outs[1:]


def _swa_kernel(sink_ref, q_ref, km_ref, kp_ref, kc_ref, vm_ref, vp_ref, vc_ref, o_ref):
    i = pl.program_id(1)
    n_sub = q_ref.shape[1] // BLOCK
    lane = lax.broadcasted_iota(jnp.int32, (1, LANES), 1)
    lo_half = lane < HEAD_DIM
    half_masks = [jnp.where(lo_half, 1.0, 0.0).astype(BF16), jnp.where(lo_half, 0.0, 1.0).astype(BF16)]
    row = lax.broadcasted_iota(jnp.int32, (BLOCK, 1), 0)
    col = lax.broadcasted_iota(jnp.int32, (1, BLOCK), 1)
    grp = SWA_HEADS // SWA_KV_HEADS
    k_all = jnp.concatenate([km_ref[0], kp_ref[0], kc_ref[0]], axis=0)
    v_all = jnp.concatenate([vm_ref[0], vp_ref[0], vc_ref[0]], axis=0)
    for j in range(n_sub):
        q0 = (i * n_sub + j) * BLOCK
        pq = q0 + row
        cq = pq >> CHUNK_SHIFT
        segs = []
        vis_m = col >= PAD_FRONT
        segs.append((vis_m, jnp.minimum(jnp.abs(pq - col), WINDOW).astype(F32)))
        for pk in (q0 - BLOCK + col, q0 + col):
            ck = jnp.where(pk >= BLOCK, pk >> CHUNK_SHIFT, BIG)
            vis = (ck <= cq) & (ck >= cq - (WINDOW >> CHUNK_SHIFT))
            segs.append((vis, jnp.abs(pq - pk).astype(F32)))
        kj = jnp.concatenate([k_all[0:BLOCK], k_all[(j + 1) * BLOCK:(j + 3) * BLOCK]], axis=0)
        vj = jnp.concatenate([v_all[0:BLOCK], v_all[(j + 1) * BLOCK:(j + 3) * BLOCK]], axis=0)
        for g in range(grp):
            qg = q_ref[0, j * BLOCK:(j + 1) * BLOCK, g * LANES:(g + 1) * LANES]
            out_g = None
            for hk in range(SWA_KV_HEADS):
                head = hk * grp + g
                slope = 2.0 ** (-8.0 * (head + 1) / SWA_HEADS)
                sink = sink_ref[head]
                s_all = _dot_nt(qg * half_masks[hk], kj)
                tiles = [jnp.where(vis, s_all[:, n * LANES:(n + 1) * LANES] - slope * dist, NEG)
                         for n, (vis, dist) in enumerate(segs)]
                top = jnp.maximum(jnp.maximum(tiles[0], tiles[1]), tiles[2])
                m = jnp.broadcast_to(jnp.maximum(jnp.max(top, axis=-1, keepdims=True), sink), (BLOCK, LANES))
                e = [jnp.exp(x - m) for x in tiles]
                den = jnp.sum(e[0] + e[1] + e[2], axis=-1, keepdims=True) + jnp.exp(sink - m[:, 0:1])
                o = _dot(jnp.concatenate(e, axis=1).astype(BF16), vj) * (1.0 / den)
                out_g = o if hk == 0 else jnp.where(lo_half, out_g, o)
            o_ref[0, j * BLOCK:(j + 1) * BLOCK, g * LANES:(g + 1) * LANES] = out_g.astype(BF16)


def _swa_attention(qa, sinks, batch, seq_len):
    x = qa.reshape(batch, seq_len, 512)
    n_sub = SWA_Q_BLOCKS
    nb = seq_len // (BLOCK * n_sub)
    blk = lambda f: pl.BlockSpec((1, BLOCK, LANES), f)
    own = lambda c: pl.BlockSpec((1, BLOCK * n_sub, LANES), lambda b, i: (b, i, c))
    return pl.pallas_call(
        _swa_kernel,
        grid=(batch, nb),
        in_specs=[
            pl.BlockSpec(memory_space=pltpu.SMEM),
            pl.BlockSpec((1, BLOCK * n_sub, 2 * LANES), lambda b, i: (b, i, 0)),
            blk(lambda b, i: (b, 0, 2)),
            blk(lambda b, i: (b, jnp.maximum(i * n_sub - 1, 0), 2)),
            own(2),
            blk(lambda b, i: (b, 0, 3)),
            blk(lambda b, i: (b, jnp.maximum(i * n_sub - 1, 0), 3)),
            own(3),
        ],
        out_specs=pl.BlockSpec((1, BLOCK * n_sub, 2 * LANES), lambda b, i: (b, i, 0)),
        out_shape=jax.ShapeDtypeStruct((batch, seq_len, 2 * LANES), BF16),
        compiler_params=pltpu.CompilerParams(dimension_semantics=("arbitrary", "arbitrary"),
                                             vmem_limit_bytes=VMEM_LIMIT),
        name="swa_attention",
    )(sinks, x, x, x, x, x, x, x)


def _causal_kernel(mode, q_ref, k_ref, v_ref, o_ref, stat_ref, acc_ref):
    seq_len = q_ref.shape[1]
    Q_TILE = stat_ref.shape[1]
    n_qt = (seq_len - BLOCK) // Q_TILE
    per_tile = Q_TILE // K_TILE
    lane = lax.broadcasted_iota(jnp.int32, (1, LANES), 1)
    lo_half = lane < HEAD_DIM
    if mode == "sb":
        r = lax.broadcasted_iota(jnp.int32, (2 * K_TILE, K_TILE), 0) & (K_TILE - 1)
        c = lax.broadcasted_iota(jnp.int32, (2 * K_TILE, K_TILE), 1)
        later2 = jnp.where(r > c, 1.0, 0.0).astype(BF16)
        r1 = lax.broadcasted_iota(jnp.int32, (2 * BLOCK, BLOCK), 0) & (BLOCK - 1)
        c1 = lax.broadcasted_iota(jnp.int32, (2 * BLOCK, BLOCK), 1)
        later1 = jnp.where(r1 > c1, 1.0, 0.0).astype(BF16)

    def causal(pq, k0, tk):
        pk = k0 + lax.broadcasted_iota(jnp.int32, (1, tk), 1)
        if mode == "fox":
            return pk <= pq
        if mode == "mla":
            return (pk >> CHUNK_SHIFT) <= (pq >> CHUNK_SHIFT)
        return pk < pq

    def head_v(k0, tk, hh):
        if mode == "sb":
            return v_ref[0, pl.ds(k0, tk), :]
        return v_ref[0, pl.ds(k0, tk), hh * LANES:(hh + 1) * LANES]

    def lane_tiles(x):
        return [x[:, j * LANES:(j + 1) * LANES] for j in range(x.shape[1] // LANES)]

    def row_parts(tq):
        step = min(tq, ROW_PART)
        return [(r0, step) for r0 in range(0, tq, step)]

    def softmax_chunk(q0, tq, k0, tk, masked, first):
        for hh in range(2):
            qh = q_ref[0, pl.ds(q0, tq), hh * LANES:(hh + 1) * LANES]
            kh = k_ref[0, pl.ds(k0, tk), hh * LANES:(hh + 1) * LANES]
            s_all = _dot_nt(qh, kh)
            vh = head_v(k0, tk, hh)
            for r0, tr in row_parts(tq):
                s = s_all[r0:r0 + tr]
                if masked:
                    pq = q0 + r0 + lax.broadcasted_iota(jnp.int32, (tr, 1), 0)
                    s = jnp.where(causal(pq, k0, tk), s, NEG)
                tiles = lane_tiles(s)
                top = tiles[0]
                for x in tiles[1:]:
                    top = jnp.maximum(top, x)
                m_new = jnp.broadcast_to(jnp.max(top, axis=-1, keepdims=True), (tr, LANES))
                if not first:
                    m_old = stat_ref[hh, r0:r0 + tr, :]
                    m_new = jnp.maximum(m_old, m_new)
                p = jnp.concatenate([jnp.exp2(x - m_new) for x in tiles], axis=1).astype(BF16)
                pv = _dot(p, vh)
                if not first:
                    pv = jnp.exp2(m_old - m_new) * acc_ref[hh, r0:r0 + tr, :] + pv
                stat_ref[hh, r0:r0 + tr, :] = m_new
                acc_ref[hh, r0:r0 + tr, :] = pv

    def stick_chunk(q0, tq, k0, tk, masked, first):
        later = later2 if tk == K_TILE else later1
        for hh in range(2):
            qh = q_ref[0, pl.ds(q0, tq), hh * LANES:(hh + 1) * LANES]
            kh = k_ref[0, pl.ds(k0, tk), hh * LANES:(hh + 1) * LANES]
            z = _dot_nt(qh, kh)
            ls_pos = jnp.minimum(z, 0.0) - jnp.log(1.0 + jnp.exp2(-jnp.abs(z))) * LOG2E
            log_keep = ls_pos - z
            if masked:
                pq = q0 + lax.broadcasted_iota(jnp.int32, (tq, 1), 0)
                vis = causal(pq, k0, tk)
                log_keep = jnp.where(vis, log_keep, 0.0)
            hi = log_keep.astype(BF16)
            lo = (log_keep - hi.astype(F32)).astype(BF16)
            after = _dot(jnp.concatenate([hi, lo], axis=1), later)
            tot = ls_pos + after
            chunk_total = jnp.broadcast_to(after[:, 0:1] + log_keep[:, 0:1], (tq, LANES))
            if not first:
                carry = stat_ref[hh, 0:tq, :]
                tot = jnp.concatenate([x + carry for x in lane_tiles(tot)], axis=1)
                chunk_total = carry + chunk_total
            a = jnp.exp2(tot)
            if masked:
                a = jnp.where(vis, a, 0.0)
            pv = _dot(a.astype(BF16), head_v(k0, tk, hh))
            if not first:
                pv = acc_ref[hh, 0:tq, :] + pv
            stat_ref[hh, 0:tq, :] = chunk_total
            acc_ref[hh, 0:tq, :] = pv

    def finish(q0, tq):
        a0, a1 = acc_ref[0, 0:tq, :], acc_ref[1, 0:tq, :]
        if mode != "sb":
            a0 = a0 / a0[:, HEAD_DIM:HEAD_DIM + 1]
            a1 = a1 / a1[:, 0:1]
        o_ref[0, pl.ds(q0, tq), :] = jnp.where(lo_half, a0, a1).astype(BF16)

    def chunk_start(j):
        return pl.multiple_of(BLOCK + j * K_TILE, BLOCK)

    if mode == "sb":
        stick_chunk(0, BLOCK, 0, BLOCK, True, True)
        finish(0, BLOCK)

        def q_body(i, _):
            q0 = pl.multiple_of(BLOCK + i * Q_TILE, BLOCK)
            n_int = i * per_tile
            for d in range(per_tile):
                stick_chunk(q0, Q_TILE, chunk_start(n_int + per_tile - 1 - d), K_TILE, True, d == 0)

            def alive():
                top = jnp.maximum(jnp.max(stat_ref[0]), jnp.max(stat_ref[1]))
                return (top > UNDERFLOW_LOG2).astype(jnp.int32)

            def body(st):
                jj, _ = st
                stick_chunk(q0, Q_TILE, chunk_start(n_int - 1 - jj), K_TILE, False, False)
                return jj + 1, alive()

            _, go = lax.while_loop(lambda st: (st[0] < n_int) & (st[1] > 0), body, (0, alive()))

            @pl.when(go > 0)
            def _():
                stick_chunk(q0, Q_TILE, 0, BLOCK, False, False)

            finish(q0, Q_TILE)
            return 0
    else:
        softmax_chunk(0, BLOCK, 0, BLOCK, True, True)
        finish(0, BLOCK)

        def q_body(i, _):
            q0 = pl.multiple_of(BLOCK + i * Q_TILE, BLOCK)
            n_int = i * per_tile
            softmax_chunk(q0, Q_TILE, 0, BLOCK, False, True)

            def body(j, _):
                for d in range(per_tile):
                    softmax_chunk(q0, Q_TILE, chunk_start(j * per_tile + d), K_TILE, False, False)
                return 0

            lax.fori_loop(0, i, body, 0)
            for d in range(per_tile):
                softmax_chunk(q0, Q_TILE, chunk_start(n_int + d), K_TILE, True, False)
            finish(q0, Q_TILE)
            return 0

    lax.fori_loop(0, n_qt, q_body, 0)


def _causal_attention(mode, q, k, v, batch, seq_len):
    wide = pl.BlockSpec((1, seq_len, 2 * LANES), lambda b, p: (b, 0, p))
    narrow = pl.BlockSpec((1, seq_len, LANES), lambda b, p: (b, 0, p))
    args = [q.reshape(batch, seq_len, 512), k.reshape(batch, seq_len, 512),
            v.reshape(batch, seq_len, v.shape[1])]
    return pl.pallas_call(
        functools.partial(_causal_kernel, mode),
        grid=(batch, 2),
        in_specs=[wide, wide, narrow if mode == "sb" else wide],
        out_specs=narrow,
        out_shape=jax.ShapeDtypeStruct((batch, seq_len, 2 * LANES), BF16),
        scratch_shapes=[pltpu.VMEM((2, Q_TILES[mode], LANES), F32), pltpu.VMEM((2, Q_TILES[mode], LANES), F32)],
        compiler_params=pltpu.CompilerParams(dimension_semantics=("arbitrary", "arbitrary"),
                                             vmem_limit_bytes=VMEM_LIMIT),
        name=mode + "_attention",
    )(*args)


def _outproj_kernel(ya_ref, yb_ref, yc_ref, yd_ref, h_ref, wo_ref, g_ref, wrh_ref, wrl_ref, br_ref, tri_ref,
                    h2_ref, xn_ref, route_ref, cnt_ref):
    o = (_dot(ya_ref[...], wo_ref[0]) + _dot(yb_ref[...], wo_ref[1])
         + _dot(yc_ref[...], wo_ref[2]) + _dot(yd_ref[...], wo_ref[3]))
    h2 = h_ref[...] + o
    h2_ref[...] = h2
    xn = _rms(h2, g_ref[...])
    _rows_to_tiles(xn_ref, (), xn)
    xh = xn.astype(BF16)
    xl = (xn - xh.astype(F32)).astype(BF16)
    wrh, wrl = wrh_ref[...], wrl_ref[...]
    lg = _dot(xh, wrh) + _dot(xl, wrh) + _dot(xh, wrl) + br_ref[...]

    tm = lg.shape[0]
    lane = lax.broadcasted_iota(jnp.int32, (tm, LANES), 1)
    ninf = -jnp.inf

    def first_max(x):
        top = jnp.max(x, axis=-1, keepdims=True)
        return top, jnp.min(jnp.where(x == top, lane, LANES), axis=-1, keepdims=True)

    gl = jnp.where(lane < N_GROUPS, lg, ninf)
    g_max, g_top = first_max(gl)
    g_w = 1.0 / jnp.sum(jnp.exp(gl - g_max), axis=-1, keepdims=True)
    e_lo = N_GROUPS + g_top * EXPERTS_PER_GROUP
    el = jnp.where((lane >= e_lo) & (lane < e_lo + EXPERTS_PER_GROUP), lg, ninf)
    v1, i1 = first_max(el)
    v2, i2 = first_max(jnp.where(lane == i1, ninf, el))
    r21 = jnp.exp(v2 - v1)
    w1 = g_w / (1.0 + r21)
    w2 = w1 * r21

    @pl.when(pl.program_id(0) == 0)
    def _():
        cnt_ref[...] = jnp.zeros_like(cnt_ref)

    m1 = jnp.where(lane == i1, 1.0, 0.0)
    m2 = jnp.where(lane == i2, 1.0, 0.0)
    both = m1 + m2
    before = _dot(tri_ref[...], both.astype(BF16)) + cnt_ref[0:1, :]
    rank1 = jnp.sum(m1 * before, axis=-1, keepdims=True)
    rank2 = jnp.sum(m2 * before, axis=-1, keepdims=True)
    cnt_ref[...] = cnt_ref[...] + jnp.sum(both, axis=0, keepdims=True)
    cols = [w1, w2, (i1 - N_GROUPS).astype(F32), (i2 - N_GROUPS).astype(F32), rank1, rank2]
    route = jnp.zeros((tm, LANES), F32)
    for j, c in enumerate(cols):
        route = jnp.where(lane == j, c, route)
    route_ref[...] = route


def _outproj(ys, h, p):
    t, d = h.shape
    tm = ROW_TILE
    row = lambda i: (i, 0)
    fixed2 = lambda i: (0, 0)
    in_specs = [pl.BlockSpec((tm, 256), row)] * 4 + [
        pl.BlockSpec((tm, d), row),
        pl.BlockSpec((4, 256, d), lambda i: (0, 0, 0)),
        pl.BlockSpec((1, d), fixed2),
        pl.BlockSpec((d, LANES), fixed2),
        pl.BlockSpec((d, LANES), fixed2),
        pl.BlockSpec((1, LANES), fixed2),
        pl.BlockSpec((tm, tm), fixed2),
    ]
    earlier = jnp.asarray(np.tril(np.ones((tm, tm), np.float32), -1), BF16)
    return pl.pallas_call(
        _outproj_kernel,
        grid=(t // tm,),
        in_specs=in_specs,
        out_specs=[pl.BlockSpec((tm, d), row), pl.BlockSpec((tm * ROW_SUB, LANES), row),
                   pl.BlockSpec((tm, LANES), row), pl.BlockSpec((8, LANES), fixed2)],
        out_shape=[jax.ShapeDtypeStruct((t, d), F32), jax.ShapeDtypeStruct((t * ROW_SUB, LANES), F32),
                   jax.ShapeDtypeStruct((t, LANES), F32), jax.ShapeDtypeStruct((8, LANES), F32)],
        compiler_params=pltpu.CompilerParams(dimension_semantics=("arbitrary",),
                                             vmem_limit_bytes=VMEM_LIMIT),
        name="outproj_router",
    )(*ys, h, p["w_out"], p["ffn_norm"], p["w_r_hi"], p["w_r_lo"], p["b_r"], earlier)


def _dispatch_kernel(nv_ref, ids_ref, x_ref, xs_hbm, zero_buf, sem):
    blk = zero_buf.shape[0]

    @pl.when(pl.program_id(0) == 0)
    def _():
        zero_buf[...] = jnp.zeros_like(zero_buf)
        for phase in ("start", "wait"):
            for b in range(nv_ref.shape[0]):
                @pl.when(nv_ref[b] < MOE_BLOCK)
                def _():
                    copy = pltpu.make_async_copy(zero_buf, xs_hbm.at[pl.ds(b * blk, blk)], sem)
                    copy.start() if phase == "start" else copy.wait()

    for j in range(ids_ref.shape[2]):
        tok, _ = divmod(j, TOP_K)
        dst = pl.multiple_of(ids_ref[0, 0, j], ROW_SUB)
        pltpu.make_async_copy(x_ref.at[pl.ds(tok * ROW_SUB, ROW_SUB)], xs_hbm.at[pl.ds(dst, ROW_SUB)],
                              sem).start(priority=j % 2)
    for _ in range(TOP_K):
        pltpu.make_async_copy(x_ref, xs_hbm.at[pl.ds(0, x_ref.shape[0])], sem).wait()


def _dispatch(xn, y_ids, nvalid):
    tm = ROW_TILE
    t = xn.shape[0] // ROW_SUB
    n_blk = nvalid.shape[0]
    return pl.pallas_call(
        _dispatch_kernel,
        grid_spec=pltpu.PrefetchScalarGridSpec(
            num_scalar_prefetch=1,
            grid=(t // tm,),
            in_specs=[pl.BlockSpec((1, 1, TOP_K * tm), lambda i, nv: (i, 0, 0), memory_space=pltpu.SMEM),
                      pl.BlockSpec((tm * ROW_SUB, LANES), lambda i, nv: (i, 0))],
            out_specs=pl.BlockSpec(memory_space=pl.ANY),
            scratch_shapes=[pltpu.VMEM((MOE_BLOCK * ROW_SUB, LANES), F32), pltpu.SemaphoreType.DMA],
        ),
        out_shape=jax.ShapeDtypeStruct((n_blk * MOE_BLOCK * ROW_SUB, LANES), F32),
        compiler_params=pltpu.CompilerParams(dimension_semantics=("arbitrary",),
                                             vmem_limit_bytes=VMEM_LIMIT),
        name="moe_dispatch",
    )(nvalid, y_ids.reshape(t // tm, 1, TOP_K * tm), xn)


def _moe_kernel(be_ref, nv_ref, x_ref, wg_ref, wu_ref, wd_ref, y_ref, wg_bf, wu_bf, wd_bf):
    s = pl.program_id(0)
    nv = nv_ref[s]

    @pl.when((s == 0) | (be_ref[s] != be_ref[jnp.maximum(s - 1, 0)]))
    def _():
        wg_bf[...] = wg_ref[0].astype(BF16)
        wu_bf[...] = wu_ref[0].astype(BF16)
        wd_bf[...] = wd_ref[0].astype(BF16)

    @pl.when(nv > 0)
    def _():
        x = _rows_from_tiles(x_ref, (), MOE_BLOCK).astype(BF16)
        gate = _dot(x, wg_bf[...])
        up = _dot(x, wu_bf[...])
        hid = (gate * (1.0 / (1.0 + jnp.exp(-gate))) * up).astype(BF16)
        _rows_to_tiles(y_ref, (), _dot(hid, wd_bf[...]))

    @pl.when(nv == 0)
    def _():
        y_ref[...] = jnp.zeros_like(y_ref)


def _moe(xs, block_e, nvalid, p):
    d = xs.shape[1] * ROW_SUB
    n_blk = block_e.shape[0]
    layer = p["layer"]
    hdim = p["w_gate"].shape[3]
    wspec = lambda shape: pl.BlockSpec((None, 1) + shape, lambda s, be, nv: (layer, be[s], 0, 0))
    rows = pl.BlockSpec((MOE_BLOCK * ROW_SUB, LANES), lambda s, be, nv: (s, 0))
    return pl.pallas_call(
        _moe_kernel,
        grid_spec=pltpu.PrefetchScalarGridSpec(
            num_scalar_prefetch=2,
            grid=(n_blk,),
            in_specs=[rows, wspec((d, hdim)), wspec((d, hdim)), wspec((hdim, d))],
            out_specs=rows,
            scratch_shapes=[pltpu.VMEM((d, hdim), BF16), pltpu.VMEM((d, hdim), BF16), pltpu.VMEM((hdim, d), BF16)],
        ),
        out_shape=jax.ShapeDtypeStruct(xs.shape, F32),
        compiler_params=pltpu.CompilerParams(dimension_semantics=("arbitrary",),
                                             vmem_limit_bytes=VMEM_LIMIT),
        name="moe_experts",
    )(block_e, nvalid, xs, p["w_gate"], p["w_up"], p["w_down"])


def _route(route, counts, t):
    a = t * TOP_K
    expert = route[:, 2:4].astype(jnp.int32).reshape(a)
    pos = route[:, 4:6].astype(jnp.int32).reshape(a)
    counts = counts.astype(jnp.int32)
    padded = (counts + MOE_BLOCK - 1) // MOE_BLOCK * MOE_BLOCK
    pad_end = jnp.cumsum(padded)
    pad_start = pad_end - padded
    dest = pad_start[expert] + pos
    n_blk = (a + N_EXPERTS * MOE_BLOCK) // MOE_BLOCK
    starts = jnp.arange(n_blk, dtype=jnp.int32) * MOE_BLOCK
    block_e = jnp.minimum(jnp.sum((pad_end[None, :] <= starts[:, None]).astype(jnp.int32), axis=1), N_EXPERTS - 1)
    nvalid = jnp.clip((pad_start + counts)[block_e] - starts, 0, MOE_BLOCK)
    return block_e, nvalid, dest * ROW_SUB


def _final_kernel(h_ref, ids0_ref, idsn_ref, y_hbm, ew_ref, g_ref, o_ref, ybuf, ysem):
    h = _combine_experts(h_ref[...], ew_ref[...], y_hbm, ids0_ref, idsn_ref, ybuf, ysem,
                         pl.program_id(0), pl.num_programs(0))
    o_ref[0] = _rms(h, g_ref[...])


def _final(h, y2, y_ids, ew, g, batch, seq_len):
    t, d = h.shape
    per_seq = seq_len // BLOCK
    out_blocks = per_seq - 1
    row = lambda n: ((n // out_blocks) * per_seq + n % out_blocks + 1, 0)
    ids = y_ids.reshape(batch, per_seq, TOP_K * BLOCK)[:, 1:].reshape(batch * out_blocks, 1, TOP_K * BLOCK)
    ids = jnp.concatenate([ids, jnp.zeros_like(ids[:1])], axis=0)
    return pl.pallas_call(
        _final_kernel,
        grid=(batch * out_blocks,),
        in_specs=[pl.BlockSpec((BLOCK, d), row),
                  pl.BlockSpec((1, 1, TOP_K * BLOCK), lambda n: (0, 0, 0), memory_space=pltpu.SMEM),
                  pl.BlockSpec((1, 1, TOP_K * BLOCK), lambda n: (n + 1, 0, 0), memory_space=pltpu.SMEM),
                  pl.BlockSpec(memory_space=pl.ANY),
                  pl.BlockSpec((BLOCK, LANES), row),
                  pl.BlockSpec((1, d), lambda n: (0, 0))],
        out_specs=pl.BlockSpec((1, BLOCK, d), lambda n: (n // out_blocks, n % out_blocks, 0)),
        out_shape=jax.ShapeDtypeStruct((batch, seq_len - BLOCK, d), F32),
        scratch_shapes=[pltpu.VMEM((2, TOP_K, BLOCK * ROW_SUB, LANES), F32), pltpu.SemaphoreType.DMA((2,))],
        compiler_params=pltpu.CompilerParams(dimension_semantics=("arbitrary",),
                                             vmem_limit_bytes=VMEM_LIMIT),
        name="final_norm",
    )(h, ids, ids, y2, ew, g)


def _rope_table(seq_len):
    half = MLA_ROPE // 2
    pos = (jnp.arange(seq_len, dtype=jnp.int32) - PAD_FRONT).astype(F32)
    inv_freq = ROPE_THETA ** (-jnp.arange(half, dtype=F32) / half)
    ang = pos[:, None] * inv_freq[None, :]
    cos, sin = jnp.cos(ang), jnp.sin(ang)
    cos2 = jnp.concatenate([cos, cos], axis=1)
    sin2 = jnp.concatenate([-sin, sin], axis=1)
    z = lambda w: jnp.zeros((seq_len, w), F32)
    scale = (MLA_NOPE + MLA_ROPE) ** -0.5 * LOG2E
    cos_q = jnp.concatenate([jnp.ones((seq_len, MLA_NOPE), F32), cos2, z(32)], axis=1) * scale
    sin_q = jnp.concatenate([z(MLA_NOPE), sin2, z(32)], axis=1) * scale
    cos_k = jnp.concatenate([z(MLA_NOPE), cos2, z(32)], axis=1)
    sin_k = jnp.concatenate([z(MLA_NOPE), sin2, z(32)], axis=1)
    return jnp.concatenate([cos_q, sin_q, cos_k, sin_k], axis=1)


def _swap_halves(w):
    half = w.shape[-1] // 2
    return jnp.concatenate([w[..., half:], w[..., :half]], axis=-1)


def _layer_params(i, seq_len, attn_norm, w_in, b_forget, sinks, mla_q_norm, mla_kv_norm, mla_w_uq,
                  mla_w_ukv, w_out, ffn_norm, w_group, b_group, w_router, b_router, w_gate, w_up, w_down):
    d = w_in.shape[1]
    w = w_in[i]
    sizes = (256, 128, 128, 256, 256, 256, 4, 256, 128, 32, 256, 256, 256)
    offs = np.concatenate([[0], np.cumsum(sizes)])
    (a_q, a_k, a_v, f_q, f_k, f_v, f_g, c_q, c_kv, c_kr, s_q, s_k, s_v) = [
        w[:, offs[j]:offs[j + 1]] for j in range(len(sizes))]
    qscale = HEAD_DIM ** -0.5
    grp = SWA_HEADS // SWA_KV_HEADS
    a_q = a_q.reshape(d, SWA_KV_HEADS, grp, HEAD_DIM).transpose(0, 2, 1, 3).reshape(d, 256)
    z = lambda n: jnp.zeros((d, n), F32)
    g_grp = jnp.concatenate([f_g[:, 0:2], z(62), c_kr, z(32)], axis=1)
    gs_grp = jnp.concatenate([f_g[:, 2:4], z(62), _swap_halves(c_kr), z(32)], axis=1)
    w_perm = jnp.concatenate([a_q * qscale, a_k, a_v, f_q * qscale, f_k, f_v, s_q * qscale, s_k, s_v,
                              c_q, c_kv, g_grp, gs_grp], axis=1).astype(BF16)
    wuq = mla_w_uq[i].reshape(MLA_Q_LORA, 4, MLA_NOPE + MLA_ROPE)
    zq = lambda n: jnp.zeros((MLA_Q_LORA, 4, n), F32)
    w_uq_a = jnp.concatenate([wuq, zq(32)], axis=2).reshape(MLA_Q_LORA, 512).astype(BF16)
    w_uq_b = jnp.concatenate([zq(MLA_NOPE), _swap_halves(wuq[:, :, MLA_NOPE:]), zq(32)],
                             axis=2).reshape(MLA_Q_LORA, 512).astype(BF16)
    wukv = mla_w_ukv[i].reshape(MLA_KV_LORA, 4, MLA_NOPE + MLA_V)
    w_kv_k = jnp.concatenate([wukv[:, :, :MLA_NOPE], jnp.zeros((MLA_KV_LORA, 4, 64), F32)],
                             axis=2).reshape(MLA_KV_LORA, 512).astype(BF16)
    w_kv_v = wukv[:, :, MLA_NOPE:].reshape(MLA_KV_LORA, 256).astype(BF16)
    bf = b_forget[i].astype(F32)
    b_f = jnp.zeros((1, 256), F32).at[0, 0:2].set(bf[0:2]).at[0, 128:130].set(bf[2:4])
    wo = w_out[i]
    wo_a = wo[:256].reshape(SWA_KV_HEADS, grp, HEAD_DIM, d).transpose(1, 0, 2, 3).reshape(256, d)
    wo4 = jnp.concatenate([wo_a, wo[256:]], axis=0).reshape(4, 256, d).astype(BF16)
    w_r = jnp.concatenate([w_group[i], w_router[i], jnp.zeros((d, LANES - N_GROUPS - N_EXPERTS), F32)], axis=1)
    w_r_hi = w_r.astype(BF16)
    w_r_lo = (w_r - w_r_hi.astype(F32)).astype(BF16)
    b_r = jnp.concatenate([b_group[i], b_router[i], jnp.zeros((LANES - N_GROUPS - N_EXPERTS,), F32)])[None, :]
    return dict(
        attn_norm=attn_norm[i][None, :], w_in=w_perm, q_norm=mla_q_norm[i][None, :],
        kv_norm=mla_kv_norm[i][None, :], w_uq_a=w_uq_a, w_uq_b=w_uq_b, w_kv_k=w_kv_k, w_kv_v=w_kv_v,
        rope_tab=_rope_table(seq_len), b_forget=b_f, sinks=sinks[i].astype(F32), w_out=wo4,
        ffn_norm=ffn_norm[i][None, :], w_r_hi=w_r_hi, w_r_lo=w_r_lo, b_r=b_r.astype(F32),
        w_gate=w_gate, w_up=w_up, w_down=w_down, layer=i)


def kernel(x, meta_tokens, attn_norm, w_in, b_forget, sinks, mla_q_norm, mla_kv_norm, mla_w_uq, mla_w_ukv,
           w_out, ffn_norm, w_group, b_group, w_router, b_router, w_gate, w_up, w_down, final_norm):
    batch, seq, d = x.shape
    seq_len = seq + BLOCK
    assert seq_len % ROW_TILE == 0 and seq_len % (BLOCK * SWA_Q_BLOCKS) == 0
    assert all(seq % tile == 0 for tile in Q_TILES.values())
    t = batch * seq_len
    depth = w_in.shape[0]
    lead = jnp.concatenate([jnp.zeros((PAD_FRONT, d), x.dtype), meta_tokens.astype(x.dtype)], axis=0)
    h = (lead, x)
    y2 = y_ids = ew = None
    for i in range(depth):
        p = _layer_params(i, seq_len, attn_norm, w_in, b_forget, sinks, mla_q_norm, mla_kv_norm, mla_w_uq,
                          mla_w_ukv, w_out, ffn_norm, w_group, b_group, w_router, b_router, w_gate, w_up, w_down)
        h, (qa, fq, fk, fv, cq, ck, cv, sq, sk, sv) = _inproj(h, y2, y_ids, ew, p, seq_len)
        y_a = _swa_attention(qa, p["sinks"], batch, seq_len)
        y_b = _causal_attention("fox", fq, fk, fv, batch, seq_len)
        y_c = _causal_attention("mla", cq, ck, cv, batch, seq_len)
        y_d = _causal_attention("sb", sq, sk, sv, batch, seq_len)
        ys = [y.reshape(t, 256) for y in (y_a, y_b, y_c, y_d)]
        h, xn, ew, counts = _outproj(ys, h, p)
        block_e, nvalid, y_ids = _route(ew, counts[0, N_GROUPS:N_GROUPS + N_EXPERTS], t)
        y2 = _moe(_dispatch(xn, y_ids, nvalid), block_e, nvalid, p)
    return _final(h, y2, y_ids, ew, final_norm[None, :], batch, seq_len)
```

```python
import functools

import jax
import jax.numpy as jnp
import numpy as np
from jax import lax
from jax.experimental import pallas as pl
from jax.experimental.pallas import tpu as pltpu

F32 = jnp.float32
BF16 = jnp.bfloat16

BLOCK = 128
N_META = 16
PAD_FRONT = BLOCK - N_META
CHUNK_SHIFT = 6
HEAD_DIM = 64
NORM_EPS = 1e-6
NEG = -1e30
PAD_KEY_LOGIT = -(2.0 ** 100)
UNDERFLOW_LOG2 = -150.0
LOG2E = 1.4426950408889634
BIG = 1 << 30
SWA_HEADS, SWA_KV_HEADS, WINDOW = 4, 2, 128
MLA_Q_LORA, MLA_KV_LORA, MLA_NOPE, MLA_ROPE, MLA_V = 256, 128, 64, 32, 64
MLA_BIAS_LANE = MLA_NOPE + MLA_ROPE
ROPE_THETA = 10000.0
N_GROUPS, EXPERTS_PER_GROUP, TOP_K = 4, 8, 2
N_EXPERTS = N_GROUPS * EXPERTS_PER_GROUP
MOE_BLOCK = 256
LANES = 128
ROW_SUB = 8
ROW_TILE = 384
Q_TILES = {"fox": 1024, "mla": 1024, "sb": 512}
K_TILE = 256
ROW_PART = 128
SWA_Q_BLOCKS = 3
VMEM_LIMIT = 56 * 1024 * 1024

C_A, C_B, C_D, C_CQ, C_CKV, C_G, C_GS, C_END = 0, 512, 1280, 2048, 2304, 2432, 2560, 2688
B_F0, B_F1, B_PAD = 0, 3, 6


def _rms(x, g):
    return x * lax.rsqrt(jnp.mean(x * x, axis=-1, keepdims=True) + NORM_EPS) * g


def _log_sigmoid(x):
    return jnp.minimum(x, 0.0) - jnp.log(1.0 + jnp.exp(-jnp.abs(x)))


def _dot(a, b):
    return jnp.dot(a, b, preferred_element_type=F32)


def _dot_nt(a, b):
    return lax.dot_general(a, b, (((1,), (1,)), ((), ())), preferred_element_type=F32)


def _rows_from_tiles(ref, lead, n):
    return jnp.concatenate([ref[(*lead, pl.ds(j, n, stride=ROW_SUB), slice(None))] for j in range(ROW_SUB)], axis=1)


def _rows_to_tiles(ref, lead, x):
    n = x.shape[0]
    for j in range(ROW_SUB):
        ref[(*lead, pl.ds(j, n, stride=ROW_SUB), slice(None))] = x[:, j * LANES:(j + 1) * LANES]


def _tile4(x):
    return jnp.concatenate([x, x, x, x], axis=1)


def _split3(x):
    hi = x.astype(BF16)
    r1 = x - hi.astype(F32)
    mid = r1.astype(BF16)
    lo = (r1 - mid.astype(F32)).astype(BF16)
    return hi, mid, lo


def _free_base(head):
    return head * LANES + (HEAD_DIM if head % 2 == 0 else 0)


def _expert_rows_start(y_hbm, ids_ref, buf, sem, slot):
    for j in range(ids_ref.shape[2]):
        tok, k = divmod(j, TOP_K)
        src = pl.multiple_of(ids_ref[0, 0, j], ROW_SUB)
        pltpu.make_async_copy(y_hbm.at[pl.ds(src, ROW_SUB)], buf.at[slot, k, pl.ds(tok * ROW_SUB, ROW_SUB)],
                              sem.at[slot]).start(priority=j % 2)


def _expert_rows_wait(y_hbm, buf, sem, slot):
    for k in range(TOP_K):
        pltpu.make_async_copy(y_hbm.at[pl.ds(0, buf.shape[2])], buf.at[slot, k], sem.at[slot]).wait()


def _combine_experts(h, ew, y_hbm, ids0_ref, idsn_ref, buf, sem, step, n_steps):
    slot = step % 2

    @pl.when(step == 0)
    def _():
        _expert_rows_start(y_hbm, ids0_ref, buf, sem, 0)

    _expert_rows_start(y_hbm, idsn_ref, buf, sem, 1 - slot)
    _expert_rows_wait(y_hbm, buf, sem, slot)
    n = h.shape[0]
    out = h + ew[:, 0:1] * _rows_from_tiles(buf, (slot, 0), n) + ew[:, 1:2] * _rows_from_tiles(buf, (slot, 1), n)

    @pl.when(step == n_steps - 1)
    def _():
        _expert_rows_wait(y_hbm, buf, sem, 1 - slot)

    return out


def _inproj_kernel(has_y2, n_seq_tiles, *refs):
    if has_y2:
        (h_ref, ids0_ref, idsn_ref, y_hbm, ew_ref, *rest) = refs
    else:
        (lead_ref, *x_refs) = refs[:1 + ROW_TILE // BLOCK]
        rest = refs[1 + ROW_TILE // BLOCK:]
    (g_ref, w_ref, qn_ref, kvn_ref, wuqa_ref, wuqb_ref, wkvk_ref, wkvv_ref, tab_ref, bf_ref,
     pq_ref, pk_ref, rows_ref, hout_ref, *outs) = rest
    if has_y2:
        *outs, ybuf, ysem = outs
    (qa_ref, fq_ref, fk_ref, fv_ref, cq_ref, ck_ref, cv_ref, sq_ref, sk_ref, sv_ref, carry_ref) = outs
    tile = pl.program_id(0) % n_seq_tiles
    if has_y2:
        h = _combine_experts(h_ref[...], ew_ref[...], y_hbm, ids0_ref, idsn_ref, ybuf, ysem,
                             pl.program_id(0), pl.num_programs(0))
    else:
        first = jnp.where(tile == 0, lead_ref[...], x_refs[0][0])
        h = jnp.concatenate([first] + [r[0] for r in x_refs[1:]], axis=0)
    hout_ref[...] = h
    tm, d = h.shape
    xn = _rms(h, g_ref[...]).astype(BF16)
    acc = _dot(xn, w_ref[...])
    lane = lax.broadcasted_iota(jnp.int32, (1, LANES), 1)
    lo_half = lane < HEAD_DIM
    pad_col = jnp.where(tile * tm + lax.broadcasted_iota(jnp.int32, (tm, 1), 0) < PAD_FRONT,
                        PAD_KEY_LOGIT, 0.0)
    rows = rows_ref[...]
    fq_one, fk_one, pad_lane, sq_one, mla_one, mla_pad = (rows[j:j + 1] for j in range(6))

    def per_head(x_pair, bias, pair, scale=None):
        x = x_pair if scale is None else x_pair * scale
        even = jnp.where(lo_half, x, bias[:, (2 * pair) * LANES:(2 * pair + 1) * LANES])
        odd = jnp.where(lo_half, bias[:, (2 * pair + 1) * LANES:(2 * pair + 2) * LANES], x)
        return even, odd

    def store_heads(ref, x_off, bias, scale=None):
        for pair in range(2):
            x_pair = acc[:, x_off + pair * LANES:x_off + (pair + 1) * LANES]
            even, odd = per_head(x_pair, bias, pair, scale)
            ref[:, (2 * pair) * LANES:(2 * pair + 1) * LANES] = even.astype(BF16)
            ref[:, (2 * pair + 1) * LANES:(2 * pair + 2) * LANES] = odd.astype(BF16)

    qa_ref[...] = acc[:, C_A:C_B].astype(BF16)

    @pl.when(tile == 0)
    def _():
        carry_ref[...] = jnp.zeros_like(carry_ref)

    lf = _log_sigmoid(acc[:, C_G:C_END] + bf_ref[...]) * LOG2E
    r = lax.broadcasted_iota(jnp.int32, (BLOCK, BLOCK), 0)
    c = lax.broadcasted_iota(jnp.int32, (BLOCK, BLOCK), 1)
    tri = jnp.where(c <= r, 1.0, 0.0).astype(BF16)
    carry = carry_ref[...]
    blocks = []
    for b in range(tm // BLOCK):
        hi, mid, lo = _split3(lf[b * BLOCK:(b + 1) * BLOCK])
        y = _dot(tri, hi) + _dot(tri, mid) + _dot(tri, lo) + carry
        carry = y[BLOCK - 1:BLOCK, :]
        blocks.append(y)
    carry_ref[...] = carry
    f_hi, f_mid, f_lo = _split3(jnp.concatenate(blocks, axis=0))
    q_bias = _dot(f_hi, pq_ref[0]) + _dot(f_mid, pq_ref[1]) + _dot(f_lo, pq_ref[2]) + fq_one
    k_bias = (_dot(f_hi, pk_ref[0]) + _dot(f_mid, pk_ref[1]) + _dot(f_lo, pk_ref[2]) + fk_one
              + pad_col * pad_lane)
    ones = jnp.ones((tm, 4 * LANES), F32)
    store_heads(fq_ref, C_B, q_bias, LOG2E)
    store_heads(fk_ref, C_B + 256, k_bias)
    store_heads(fv_ref, C_B + 512, ones)

    store_heads(sq_ref, C_D, jnp.broadcast_to(sq_one, (tm, 4 * LANES)), LOG2E)
    store_heads(sk_ref, C_D + 256, pad_col * pad_lane)
    sv_ref[...] = acc[:, C_D + 512:C_CQ].astype(BF16)

    tab = tab_ref[...]
    cos_q, sin_q = tab[:, 0:128], tab[:, 128:256]
    cos_k, sin_k = tab[:, 256:384], tab[:, 384:512]
    cqn = _rms(acc[:, C_CQ:C_CKV], qn_ref[...]).astype(BF16)
    q_lin = _dot(cqn, wuqa_ref[...])
    q_swp = _dot(cqn, wuqb_ref[...])
    cq_ref[...] = (q_lin * _tile4(cos_q) + q_swp * _tile4(sin_q) + mla_one).astype(BF16)
    ckvn = _rms(acc[:, C_CKV:C_G], kvn_ref[...]).astype(BF16)
    k_nope = _dot(ckvn, wkvk_ref[...])
    grp, grp_s = acc[:, C_G:C_GS], acc[:, C_GS:C_END]
    k_rope = grp * cos_k + grp_s * sin_k
    ck_ref[...] = (k_nope + _tile4(k_rope) + pad_col * mla_pad).astype(BF16)
    vv = _dot(ckvn, wkvv_ref[...])
    for pair in range(2):
        even, odd = per_head(vv[:, pair * LANES:(pair + 1) * LANES], ones, pair)
        cv_ref[:, (2 * pair) * LANES:(2 * pair + 1) * LANES] = even.astype(BF16)
        cv_ref[:, (2 * pair + 1) * LANES:(2 * pair + 2) * LANES] = odd.astype(BF16)


def _bias_constants():
    src = (0, 1, LANES, LANES + 1)
    pq = np.zeros((3, 2 * LANES, 4 * LANES), np.float32)
    pk = np.zeros((3, 2 * LANES, 4 * LANES), np.float32)
    rows = np.zeros((8, 4 * LANES), np.float32)
    for head in range(4):
        base = _free_base(head)
        for part in range(3):
            pq[part, src[head], base + B_F0 + part] = 1.0
            pk[part, src[head], base + B_F1 + part] = -1.0
            rows[0, base + B_F1 + part] = 1.0
            rows[1, base + B_F0 + part] = 1.0
        rows[0, base + B_PAD] = 1.0
        rows[2, base + B_PAD] = 1.0
        rows[3, base + B_PAD] = 1.0
        rows[4, head * LANES + MLA_BIAS_LANE] = 1.0
        rows[5, head * LANES + MLA_BIAS_LANE] = 1.0
    return jnp.asarray(pq, BF16), jnp.asarray(pk, BF16), jnp.asarray(rows, F32)


def _inproj(h, y2, y_ids, ew, p, seq_len):
    tm = ROW_TILE
    n_seq_tiles = seq_len // tm
    has_y2 = y2 is not None
    row = lambda i: (i, 0)
    fixed = lambda i: (0, 0)
    if has_y2:
        t, d = h.shape
        in_specs = [pl.BlockSpec((tm, d), row)]
        args = [h]
    else:
        lead, x = h
        batch, seq, d = x.shape
        t = batch * seq_len
        per_tile = tm // BLOCK
        frame = lambda k: pl.BlockSpec((1, BLOCK, d), lambda i: (
            i // n_seq_tiles, jnp.maximum((i % n_seq_tiles) * per_tile - 1 + k, 0), 0))
        in_specs = [pl.BlockSpec((BLOCK, d), fixed)] + [frame(k) for k in range(per_tile)]
        args = [lead] + [x] * per_tile
    if has_y2:
        ids = y_ids.reshape(t // tm, 1, TOP_K * tm)
        ids = jnp.concatenate([ids, jnp.zeros_like(ids[:1])], axis=0)
        in_specs += [pl.BlockSpec((1, 1, TOP_K * tm), lambda i: (0, 0, 0), memory_space=pltpu.SMEM),
                     pl.BlockSpec((1, 1, TOP_K * tm), lambda i: (i + 1, 0, 0), memory_space=pltpu.SMEM),
                     pl.BlockSpec(memory_space=pl.ANY),
                     pl.BlockSpec((tm, LANES), row)]
        args += [ids, ids, y2, ew]
    pq, pk, rows = _bias_constants()
    consts = [p["attn_norm"], p["w_in"], p["q_norm"], p["kv_norm"], p["w_uq_a"], p["w_uq_b"],
              p["w_kv_k"], p["w_kv_v"]]
    in_specs += [pl.BlockSpec(c.shape, fixed) for c in consts]
    args += consts
    in_specs.append(pl.BlockSpec((tm, 512), lambda i: (i % n_seq_tiles, 0)))
    args.append(p["rope_tab"])
    in_specs += [pl.BlockSpec((1, 256), fixed),
                 pl.BlockSpec(pq.shape, lambda i: (0, 0, 0)),
                 pl.BlockSpec(pk.shape, lambda i: (0, 0, 0)),
                 pl.BlockSpec(rows.shape, fixed)]
    args += [p["b_forget"], pq, pk, rows]
    widths = [512] * 9 + [256]
    out_shape = [jax.ShapeDtypeStruct((t, d), F32)] + [jax.ShapeDtypeStruct((t, w), BF16) for w in widths]
    out_specs = [pl.BlockSpec((tm, d), row)] + [pl.BlockSpec((tm, w), row) for w in widths]
    outs = pl.pallas_call(
        functools.partial(_inproj_kernel, has_y2, n_seq_tiles),
        grid=(t // tm,),
        in_specs=in_specs,
        out_specs=out_specs,
        out_shape=out_shape,
        scratch_shapes=[pltpu.VMEM((1, 2 * LANES), F32)] + (
            [pltpu.VMEM((2, TOP_K, tm * ROW_SUB, LANES), F32), pltpu.SemaphoreType.DMA((2,))] if has_y2 else []),
        compiler_params=pltpu.CompilerParams(dimension_semantics=("arbitrary",),
                                             vmem_limit_bytes=VMEM_LIMIT),
        name="inproj_y2" if has_y2 else "inproj",
    )(*args)
    return outs[0], outs[1:]


def _swa_kernel(sink_ref, q_ref, km_ref, kp_ref, kc_ref, vm_ref, vp_ref, vc_ref, o_ref):
    i = pl.program_id(1)
    n_sub = q_ref.shape[1] // BLOCK
    lane = lax.broadcasted_iota(jnp.int32, (1, LANES), 1)
    lo_half = lane < HEAD_DIM
    half_masks = [jnp.where(lo_half, 1.0, 0.0).astype(BF16), jnp.where(lo_half, 0.0, 1.0).astype(BF16)]
    row = lax.broadcasted_iota(jnp.int32, (BLOCK, 1), 0)
    col = lax.broadcasted_iota(jnp.int32, (1, BLOCK), 1)
    grp = SWA_HEADS // SWA_KV_HEADS
    k_all = jnp.concatenate([km_ref[0], kp_ref[0], kc_ref[0]], axis=0)
    v_all = jnp.concatenate([vm_ref[0], vp_ref[0], vc_ref[0]], axis=0)
    for j in range(n_sub):
        q0 = (i * n_sub + j) * BLOCK
        pq = q0 + row
        cq = pq >> CHUNK_SHIFT
        segs = []
        vis_m = col >= PAD_FRONT
        segs.append((vis_m, jnp.minimum(jnp.abs(pq - col), WINDOW).astype(F32)))
        for pk in (q0 - BLOCK + col, q0 + col):
            ck = jnp.where(pk >= BLOCK, pk >> CHUNK_SHIFT, BIG)
            vis = (ck <= cq) & (ck >= cq - (WINDOW >> CHUNK_SHIFT))
            segs.append((vis, jnp.abs(pq - pk).astype(F32)))
        kj = jnp.concatenate([k_all[0:BLOCK], k_all[(j + 1) * BLOCK:(j + 3) * BLOCK]], axis=0)
        vj = jnp.concatenate([v_all[0:BLOCK], v_all[(j + 1) * BLOCK:(j + 3) * BLOCK]], axis=0)
        for g in range(grp):
            qg = q_ref[0, j * BLOCK:(j + 1) * BLOCK, g * LANES:(g + 1) * LANES]
            out_g = None
            for hk in range(SWA_KV_HEADS):
                head = hk * grp + g
                slope = 2.0 ** (-8.0 * (head + 1) / SWA_HEADS)
                sink = sink_ref[head]
                s_all = _dot_nt(qg * half_masks[hk], kj)
                tiles = [jnp.where(vis, s_all[:, n * LANES:(n + 1) * LANES] - slope * dist, NEG)
                         for n, (vis, dist) in enumerate(segs)]
                top = jnp.maximum(jnp.maximum(tiles[0], tiles[1]), tiles[2])
                m = jnp.broadcast_to(jnp.maximum(jnp.max(top, axis=-1, keepdims=True), sink), (BLOCK, LANES))
                e = [jnp.exp(x - m) for x in tiles]
                den = jnp.sum(e[0] + e[1] + e[2], axis=-1, keepdims=True) + jnp.exp(sink - m[:, 0:1])
                o = _dot(jnp.concatenate(e, axis=1).astype(BF16), vj) * (1.0 / den)
                out_g = o if hk == 0 else jnp.where(lo_half, out_g, o)
            o_ref[0, j * BLOCK:(j + 1) * BLOCK, g * LANES:(g + 1) * LANES] = out_g.astype(BF16)


def _swa_attention(qa, sinks, batch, seq_len):
    x = qa.reshape(batch, seq_len, 512)
    n_sub = SWA_Q_BLOCKS
    nb = seq_len // (BLOCK * n_sub)
    blk = lambda f: pl.BlockSpec((1, BLOCK, LANES), f)
    own = lambda c: pl.BlockSpec((1, BLOCK * n_sub, LANES), lambda b, i: (b, i, c))
    return pl.pallas_call(
        _swa_kernel,
        grid=(batch, nb),
        in_specs=[
            pl.BlockSpec(memory_space=pltpu.SMEM),
            pl.BlockSpec((1, BLOCK * n_sub, 2 * LANES), lambda b, i: (b, i, 0)),
            blk(lambda b, i: (b, 0, 2)),
            blk(lambda b, i: (b, jnp.maximum(i * n_sub - 1, 0), 2)),
            own(2),
            blk(lambda b, i: (b, 0, 3)),
            blk(lambda b, i: (b, jnp.maximum(i * n_sub - 1, 0), 3)),
            own(3),
        ],
        out_specs=pl.BlockSpec((1, BLOCK * n_sub, 2 * LANES), lambda b, i: (b, i, 0)),
        out_shape=jax.ShapeDtypeStruct((batch, seq_len, 2 * LANES), BF16),
        compiler_params=pltpu.CompilerParams(dimension_semantics=("arbitrary", "arbitrary"),
                                             vmem_limit_bytes=VMEM_LIMIT),
        name="swa_attention",
    )(sinks, x, x, x, x, x, x, x)


def _causal_kernel(mode, q_ref, k_ref, v_ref, o_ref, stat_ref, acc_ref):
    seq_len = q_ref.shape[1]
    Q_TILE = stat_ref.shape[1]
    n_qt = (seq_len - BLOCK) // Q_TILE
    per_tile = Q_TILE // K_TILE
    lane = lax.broadcasted_iota(jnp.int32, (1, LANES), 1)
    lo_half = lane < HEAD_DIM
    if mode == "sb":
        r = lax.broadcasted_iota(jnp.int32, (2 * K_TILE, K_TILE), 0) & (K_TILE - 1)
        c = lax.broadcasted_iota(jnp.int32, (2 * K_TILE, K_TILE), 1)
        later2 = jnp.where(r > c, 1.0, 0.0).astype(BF16)
        r1 = lax.broadcasted_iota(jnp.int32, (2 * BLOCK, BLOCK), 0) & (BLOCK - 1)
        c1 = lax.broadcasted_iota(jnp.int32, (2 * BLOCK, BLOCK), 1)
        later1 = jnp.where(r1 > c1, 1.0, 0.0).astype(BF16)

    def causal(pq, k0, tk):
        pk = k0 + lax.broadcasted_iota(jnp.int32, (1, tk), 1)
        if mode == "fox":
            return pk <= pq
        if mode == "mla":
            return (pk >> CHUNK_SHIFT) <= (pq >> CHUNK_SHIFT)
        return pk < pq

    def head_v(k0, tk, hh):
        if mode == "sb":
            return v_ref[0, pl.ds(k0, tk), :]
        return v_ref[0, pl.ds(k0, tk), hh * LANES:(hh + 1) * LANES]

    def lane_tiles(x):
        return [x[:, j * LANES:(j + 1) * LANES] for j in range(x.shape[1] // LANES)]

    def row_parts(lo, hi):
        step = min(hi - lo, ROW_PART)
        return [(r0, step) for r0 in range(lo, hi, step)]

    def softmax_chunk(q0, tq, k0, tk, masked, first, row_lo=0):
        for hh in range(2):
            qh = q_ref[0, pl.ds(q0 + row_lo, tq - row_lo), hh * LANES:(hh + 1) * LANES]
            kh = k_ref[0, pl.ds(k0, tk), hh * LANES:(hh + 1) * LANES]
            s_all = _dot_nt(qh, kh)
            vh = head_v(k0, tk, hh)
            for r0, tr in row_parts(row_lo, tq):
                s = s_all[r0 - row_lo:r0 - row_lo + tr]
                if masked and r0 < row_lo + tk:
                    pq = q0 + r0 + lax.broadcasted_iota(jnp.int32, (tr, 1), 0)
                    s = jnp.where(causal(pq, k0, tk), s, NEG)
                tiles = lane_tiles(s)
                top = tiles[0]
                for x in tiles[1:]:
                    top = jnp.maximum(top, x)
                m_new = jnp.broadcast_to(jnp.max(top, axis=-1, keepdims=True), (tr, LANES))
                if not first:
                    m_old = stat_ref[hh, r0:r0 + tr, :]
                    m_new = jnp.maximum(m_old, m_new)
                p = jnp.concatenate([jnp.exp2(x - m_new) for x in tiles], axis=1).astype(BF16)
                pv = _dot(p, vh)
                if not first:
                    pv = jnp.exp2(m_old - m_new) * acc_ref[hh, r0:r0 + tr, :] + pv
                stat_ref[hh, r0:r0 + tr, :] = m_new
                acc_ref[hh, r0:r0 + tr, :] = pv

    def stick_chunk(q0, rows, k0, tk, masked, first):
        lo_r, hi_r = rows
        tq = hi_r - lo_r
        later = later2 if tk == K_TILE else later1
        for hh in range(2):
            qh = q_ref[0, pl.ds(q0 + lo_r, tq), hh * LANES:(hh + 1) * LANES]
            kh = k_ref[0, pl.ds(k0, tk), hh * LANES:(hh + 1) * LANES]
            z = _dot_nt(qh, kh)
            ls_pos = jnp.minimum(z, 0.0) - jnp.log(1.0 + jnp.exp2(-jnp.abs(z))) * LOG2E
            log_keep = ls_pos - z
            if masked:
                pq = q0 + lo_r + lax.broadcasted_iota(jnp.int32, (tq, 1), 0)
                vis = causal(pq, k0, tk)
                log_keep = jnp.where(vis, log_keep, 0.0)
            hi = log_keep.astype(BF16)
            lo = (log_keep - hi.astype(F32)).astype(BF16)
            after = _dot(jnp.concatenate([hi, lo], axis=1), later)
            tot = ls_pos + after
            chunk_total = jnp.broadcast_to(after[:, 0:1] + log_keep[:, 0:1], (tq, LANES))
            if not first:
                carry = stat_ref[hh, lo_r:hi_r, :]
                tot = jnp.concatenate([x + carry for x in lane_tiles(tot)], axis=1)
                chunk_total = carry + chunk_total
            a = jnp.exp2(tot)
            if masked:
                a = jnp.where(vis, a, 0.0)
            pv = _dot(a.astype(BF16), head_v(k0, tk, hh))
            if not first:
                pv = acc_ref[hh, lo_r:hi_r, :] + pv
            stat_ref[hh, lo_r:hi_r, :] = chunk_total
            acc_ref[hh, lo_r:hi_r, :] = pv

    def finish(q0, tq):
        a0, a1 = acc_ref[0, 0:tq, :], acc_ref[1, 0:tq, :]
        if mode != "sb":
            a0 = a0 / a0[:, HEAD_DIM:HEAD_DIM + 1]
            a1 = a1 / a1[:, 0:1]
        o_ref[0, pl.ds(q0, tq), :] = jnp.where(lo_half, a0, a1).astype(BF16)

    def chunk_start(j):
        return pl.multiple_of(BLOCK + j * K_TILE, BLOCK)

    if mode == "sb":
        stick_chunk(0, (0, BLOCK), 0, BLOCK, True, True)
        finish(0, BLOCK)
        whole = (0, Q_TILE)

        def q_body(i, _):
            q0 = pl.multiple_of(BLOCK + i * Q_TILE, BLOCK)
            n_int = i * per_tile
            for d in reversed(range(per_tile)):
                stick_chunk(q0, whole, chunk_start(n_int + d), K_TILE, True, d == per_tile - 1)

            def alive():
                top = jnp.maximum(jnp.max(stat_ref[0]), jnp.max(stat_ref[1]))
                return (top > UNDERFLOW_LOG2).astype(jnp.int32)

            def body(st):
                jj, _ = st
                stick_chunk(q0, whole, chunk_start(n_int - 1 - jj), K_TILE, False, False)
                return jj + 1, alive()

            _, go = lax.while_loop(lambda st: (st[0] < n_int) & (st[1] > 0), body, (0, alive()))

            @pl.when(go > 0)
            def _():
                stick_chunk(q0, whole, 0, BLOCK, False, False)

            finish(q0, Q_TILE)
            return 0
    else:
        softmax_chunk(0, BLOCK, 0, BLOCK, True, True)
        finish(0, BLOCK)

        def q_body(i, _):
            q0 = pl.multiple_of(BLOCK + i * Q_TILE, BLOCK)
            n_int = i * per_tile
            softmax_chunk(q0, Q_TILE, 0, BLOCK, False, True)

            def body(j, _):
                for d in range(per_tile):
                    softmax_chunk(q0, Q_TILE, chunk_start(j * per_tile + d), K_TILE, False, False)
                return 0

            lax.fori_loop(0, i, body, 0)
            for d in range(per_tile):
                softmax_chunk(q0, Q_TILE, chunk_start(n_int + d), K_TILE, True, False, row_lo=d * K_TILE)
            finish(q0, Q_TILE)
            return 0

    lax.fori_loop(0, n_qt, q_body, 0)


def _causal_attention(mode, q, k, v, batch, seq_len):
    wide = pl.BlockSpec((1, seq_len, 2 * LANES), lambda b, p: (b, 0, p))
    narrow = pl.BlockSpec((1, seq_len, LANES), lambda b, p: (b, 0, p))
    args = [q.reshape(batch, seq_len, 512), k.reshape(batch, seq_len, 512),
            v.reshape(batch, seq_len, v.shape[1])]
    return pl.pallas_call(
        functools.partial(_causal_kernel, mode),
        grid=(batch, 2),
        in_specs=[wide, wide, narrow if mode == "sb" else wide],
        out_specs=narrow,
        out_shape=jax.ShapeDtypeStruct((batch, seq_len, 2 * LANES), BF16),
        scratch_shapes=[pltpu.VMEM((2, Q_TILES[mode], LANES), F32), pltpu.VMEM((2, Q_TILES[mode], LANES), F32)],
        compiler_params=pltpu.CompilerParams(dimension_semantics=("arbitrary", "arbitrary"),
                                             vmem_limit_bytes=VMEM_LIMIT),
        name=mode + "_attention",
    )(*args)


def _outproj_kernel(ya_ref, yb_ref, yc_ref, yd_ref, h_ref, wo_ref, g_ref, wrh_ref, wrl_ref, br_ref, tri_ref,
                    h2_ref, xn_ref, route_ref, cnt_ref):
    o = (_dot(ya_ref[...], wo_ref[0]) + _dot(yb_ref[...], wo_ref[1])
         + _dot(yc_ref[...], wo_ref[2]) + _dot(yd_ref[...], wo_ref[3]))
    h2 = h_ref[...] + o
    h2_ref[...] = h2
    xn = _rms(h2, g_ref[...])
    _rows_to_tiles(xn_ref, (), xn)
    xh = xn.astype(BF16)
    xl = (xn - xh.astype(F32)).astype(BF16)
    wrh, wrl = wrh_ref[...], wrl_ref[...]
    lg = _dot(xh, wrh) + _dot(xl, wrh) + _dot(xh, wrl) + br_ref[...]

    tm = lg.shape[0]
    lane = lax.broadcasted_iota(jnp.int32, (tm, LANES), 1)
    ninf = -jnp.inf

    def first_max(x):
        top = jnp.max(x, axis=-1, keepdims=True)
        return top, jnp.min(jnp.where(x == top, lane, LANES), axis=-1, keepdims=True)

    gl = jnp.where(lane < N_GROUPS, lg, ninf)
    g_max, g_top = first_max(gl)
    g_w = 1.0 / jnp.sum(jnp.exp(gl - g_max), axis=-1, keepdims=True)
    e_lo = N_GROUPS + g_top * EXPERTS_PER_GROUP
    el = jnp.where((lane >= e_lo) & (lane < e_lo + EXPERTS_PER_GROUP), lg, ninf)
    v1, i1 = first_max(el)
    v2, i2 = first_max(jnp.where(lane == i1, ninf, el))
    r21 = jnp.exp(v2 - v1)
    w1 = g_w / (1.0 + r21)
    w2 = w1 * r21

    @pl.when(pl.program_id(0) == 0)
    def _():
        cnt_ref[...] = jnp.zeros_like(cnt_ref)

    m1 = jnp.where(lane == i1, 1.0, 0.0)
    m2 = jnp.where(lane == i2, 1.0, 0.0)
    both = m1 + m2
    before = _dot(tri_ref[...], both.astype(BF16)) + cnt_ref[0:1, :]
    rank1 = jnp.sum(m1 * before, axis=-1, keepdims=True)
    rank2 = jnp.sum(m2 * before, axis=-1, keepdims=True)
    cnt_ref[...] = cnt_ref[...] + jnp.sum(both, axis=0, keepdims=True)
    cols = [w1, w2, (i1 - N_GROUPS).astype(F32), (i2 - N_GROUPS).astype(F32), rank1, rank2]
    route = jnp.zeros((tm, LANES), F32)
    for j, c in enumerate(cols):
        route = jnp.where(lane == j, c, route)
    route_ref[...] = route


def _outproj(ys, h, p):
    t, d = h.shape
    tm = ROW_TILE
    row = lambda i: (i, 0)
    fixed2 = lambda i: (0, 0)
    in_specs = [pl.BlockSpec((tm, 256), row)] * 4 + [
        pl.BlockSpec((tm, d), row),
        pl.BlockSpec((4, 256, d), lambda i: (0, 0, 0)),
        pl.BlockSpec((1, d), fixed2),
        pl.BlockSpec((d, LANES), fixed2),
        pl.BlockSpec((d, LANES), fixed2),
        pl.BlockSpec((1, LANES), fixed2),
        pl.BlockSpec((tm, tm), fixed2),
    ]
    earlier = jnp.asarray(np.tril(np.ones((tm, tm), np.float32), -1), BF16)
    return pl.pallas_call(
        _outproj_kernel,
        grid=(t // tm,),
        in_specs=in_specs,
        out_specs=[pl.BlockSpec((tm, d), row), pl.BlockSpec((tm * ROW_SUB, LANES), row),
                   pl.BlockSpec((tm, LANES), row), pl.BlockSpec((8, LANES), fixed2)],
        out_shape=[jax.ShapeDtypeStruct((t, d), F32), jax.ShapeDtypeStruct((t * ROW_SUB, LANES), F32),
                   jax.ShapeDtypeStruct((t, LANES), F32), jax.ShapeDtypeStruct((8, LANES), F32)],
        compiler_params=pltpu.CompilerParams(dimension_semantics=("arbitrary",),
                                             vmem_limit_bytes=VMEM_LIMIT),
        name="outproj_router",
    )(*ys, h, p["w_out"], p["ffn_norm"], p["w_r_hi"], p["w_r_lo"], p["b_r"], earlier)


def _dispatch_kernel(nv_ref, ids_ref, x_ref, xs_hbm, zero_buf, sem):
    blk = zero_buf.shape[0]

    @pl.when(pl.program_id(0) == 0)
    def _():
        zero_buf[...] = jnp.zeros_like(zero_buf)
        for phase in ("start", "wait"):
            for b in range(nv_ref.shape[0]):
                @pl.when(nv_ref[b] < MOE_BLOCK)
                def _():
                    copy = pltpu.make_async_copy(zero_buf, xs_hbm.at[pl.ds(b * blk, blk)], sem)
                    copy.start() if phase == "start" else copy.wait()

    for j in range(ids_ref.shape[2]):
        tok, _ = divmod(j, TOP_K)
        dst = pl.multiple_of(ids_ref[0, 0, j], ROW_SUB)
        pltpu.make_async_copy(x_ref.at[pl.ds(tok * ROW_SUB, ROW_SUB)], xs_hbm.at[pl.ds(dst, ROW_SUB)],
                              sem).start(priority=j % 2)
    for _ in range(TOP_K):
        pltpu.make_async_copy(x_ref, xs_hbm.at[pl.ds(0, x_ref.shape[0])], sem).wait()


def _dispatch(xn, y_ids, nvalid):
    tm = ROW_TILE
    t = xn.shape[0] // ROW_SUB
    n_blk = nvalid.shape[0]
    return pl.pallas_call(
        _dispatch_kernel,
        grid_spec=pltpu.PrefetchScalarGridSpec(
            num_scalar_prefetch=1,
            grid=(t // tm,),
            in_specs=[pl.BlockSpec((1, 1, TOP_K * tm), lambda i, nv: (i, 0, 0), memory_space=pltpu.SMEM),
                      pl.BlockSpec((tm * ROW_SUB, LANES), lambda i, nv: (i, 0))],
            out_specs=pl.BlockSpec(memory_space=pl.ANY),
            scratch_shapes=[pltpu.VMEM((MOE_BLOCK * ROW_SUB, LANES), F32), pltpu.SemaphoreType.DMA],
        ),
        out_shape=jax.ShapeDtypeStruct((n_blk * MOE_BLOCK * ROW_SUB, LANES), F32),
        compiler_params=pltpu.CompilerParams(dimension_semantics=("arbitrary",),
                                             vmem_limit_bytes=VMEM_LIMIT),
        name="moe_dispatch",
    )(nvalid, y_ids.reshape(t // tm, 1, TOP_K * tm), xn)


def _moe_kernel(be_ref, nv_ref, x_ref, wg_ref, wu_ref, wd_ref, y_ref, wg_bf, wu_bf, wd_bf):
    s = pl.program_id(0)
    nv = nv_ref[s]

    @pl.when((s == 0) | (be_ref[s] != be_ref[jnp.maximum(s - 1, 0)]))
    def _():
        wg_bf[...] = wg_ref[0].astype(BF16)
        wu_bf[...] = wu_ref[0].astype(BF16)
        wd_bf[...] = wd_ref[0].astype(BF16)

    @pl.when(nv > 0)
    def _():
        x = _rows_from_tiles(x_ref, (), MOE_BLOCK).astype(BF16)
        gate = _dot(x, wg_bf[...])
        up = _dot(x, wu_bf[...])
        hid = (gate * (1.0 / (1.0 + jnp.exp(-gate))) * up).astype(BF16)
        _rows_to_tiles(y_ref, (), _dot(hid, wd_bf[...]))

    @pl.when(nv == 0)
    def _():
        y_ref[...] = jnp.zeros_like(y_ref)


def _moe(xs, block_e, nvalid, p):
    d = xs.shape[1] * ROW_SUB
    n_blk = block_e.shape[0]
    layer = p["layer"]
    hdim = p["w_gate"].shape[3]
    wspec = lambda shape: pl.BlockSpec((None, 1) + shape, lambda s, be, nv: (layer, be[s], 0, 0))
    rows = pl.BlockSpec((MOE_BLOCK * ROW_SUB, LANES), lambda s, be, nv: (s, 0))
    return pl.pallas_call(
        _moe_kernel,
        grid_spec=pltpu.PrefetchScalarGridSpec(
            num_scalar_prefetch=2,
            grid=(n_blk,),
            in_specs=[rows, wspec((d, hdim)), wspec((d, hdim)), wspec((hdim, d))],
            out_specs=rows,
            scratch_shapes=[pltpu.VMEM((d, hdim), BF16), pltpu.VMEM((d, hdim), BF16), pltpu.VMEM((hdim, d), BF16)],
        ),
        out_shape=jax.ShapeDtypeStruct(xs.shape, F32),
        compiler_params=pltpu.CompilerParams(dimension_semantics=("arbitrary",),
                                             vmem_limit_bytes=VMEM_LIMIT),
        name="moe_experts",
    )(block_e, nvalid, xs, p["w_gate"], p["w_up"], p["w_down"])


def _route(route, counts, t):
    a = t * TOP_K
    expert = route[:, 2:4].astype(jnp.int32).reshape(a)
    pos = route[:, 4:6].astype(jnp.int32).reshape(a)
    counts = counts.astype(jnp.int32)
    padded = (counts + MOE_BLOCK - 1) // MOE_BLOCK * MOE_BLOCK
    pad_end = jnp.cumsum(padded)
    pad_start = pad_end - padded
    dest = pad_start[expert] + pos
    n_blk = (a + N_EXPERTS * MOE_BLOCK) // MOE_BLOCK
    starts = jnp.arange(n_blk, dtype=jnp.int32) * MOE_BLOCK
    block_e = jnp.minimum(jnp.sum((pad_end[None, :] <= starts[:, None]).astype(jnp.int32), axis=1), N_EXPERTS - 1)
    nvalid = jnp.clip((pad_start + counts)[block_e] - starts, 0, MOE_BLOCK)
    return block_e, nvalid, dest * ROW_SUB


def _final_kernel(h_ref, ids0_ref, idsn_ref, y_hbm, ew_ref, g_ref, o_ref, ybuf, ysem):
    h = _combine_experts(h_ref[...], ew_ref[...], y_hbm, ids0_ref, idsn_ref, ybuf, ysem,
                         pl.program_id(0), pl.num_programs(0))
    o_ref[0] = _rms(h, g_ref[...])


def _final(h, y2, y_ids, ew, g, batch, seq_len):
    t, d = h.shape
    per_seq = seq_len // BLOCK
    out_blocks = per_seq - 1
    row = lambda n: ((n // out_blocks) * per_seq + n % out_blocks + 1, 0)
    ids = y_ids.reshape(batch, per_seq, TOP_K * BLOCK)[:, 1:].reshape(batch * out_blocks, 1, TOP_K * BLOCK)
    ids = jnp.concatenate([ids, jnp.zeros_like(ids[:1])], axis=0)
    return pl.pallas_call(
        _final_kernel,
        grid=(batch * out_blocks,),
        in_specs=[pl.BlockSpec((BLOCK, d), row),
                  pl.BlockSpec((1, 1, TOP_K * BLOCK), lambda n: (0, 0, 0), memory_space=pltpu.SMEM),
                  pl.BlockSpec((1, 1, TOP_K * BLOCK), lambda n: (n + 1, 0, 0), memory_space=pltpu.SMEM),
                  pl.BlockSpec(memory_space=pl.ANY),
                  pl.BlockSpec((BLOCK, LANES), row),
                  pl.BlockSpec((1, d), lambda n: (0, 0))],
        out_specs=pl.BlockSpec((1, BLOCK, d), lambda n: (n // out_blocks, n % out_blocks, 0)),
        out_shape=jax.ShapeDtypeStruct((batch, seq_len - BLOCK, d), F32),
        scratch_shapes=[pltpu.VMEM((2, TOP_K, BLOCK * ROW_SUB, LANES), F32), pltpu.SemaphoreType.DMA((2,))],
        compiler_params=pltpu.CompilerParams(dimension_semantics=("arbitrary",),
                                             vmem_limit_bytes=VMEM_LIMIT),
        name="final_norm",
    )(h, ids, ids, y2, ew, g)


def _rope_table(seq_len):
    half = MLA_ROPE // 2
    pos = (jnp.arange(seq_len, dtype=jnp.int32) - PAD_FRONT).astype(F32)
    inv_freq = ROPE_THETA ** (-jnp.arange(half, dtype=F32) / half)
    ang = pos[:, None] * inv_freq[None, :]
    cos, sin = jnp.cos(ang), jnp.sin(ang)
    cos2 = jnp.concatenate([cos, cos], axis=1)
    sin2 = jnp.concatenate([-sin, sin], axis=1)
    z = lambda w: jnp.zeros((seq_len, w), F32)
    scale = (MLA_NOPE + MLA_ROPE) ** -0.5 * LOG2E
    cos_q = jnp.concatenate([jnp.ones((seq_len, MLA_NOPE), F32), cos2, z(32)], axis=1) * scale
    sin_q = jnp.concatenate([z(MLA_NOPE), sin2, z(32)], axis=1) * scale
    cos_k = jnp.concatenate([z(MLA_NOPE), cos2, z(32)], axis=1)
    sin_k = jnp.concatenate([z(MLA_NOPE), sin2, z(32)], axis=1)
    return jnp.concatenate([cos_q, sin_q, cos_k, sin_k], axis=1)


def _swap_halves(w):
    half = w.shape[-1] // 2
    return jnp.concatenate([w[..., half:], w[..., :half]], axis=-1)


def _layer_params(i, seq_len, attn_norm, w_in, b_forget, sinks, mla_q_norm, mla_kv_norm, mla_w_uq,
                  mla_w_ukv, w_out, ffn_norm, w_group, b_group, w_router, b_router, w_gate, w_up, w_down):
    d = w_in.shape[1]
    w = w_in[i]
    sizes = (256, 128, 128, 256, 256, 256, 4, 256, 128, 32, 256, 256, 256)
    offs = np.concatenate([[0], np.cumsum(sizes)])
    (a_q, a_k, a_v, f_q, f_k, f_v, f_g, c_q, c_kv, c_kr, s_q, s_k, s_v) = [
        w[:, offs[j]:offs[j + 1]] for j in range(len(sizes))]
    qscale = HEAD_DIM ** -0.5
    grp = SWA_HEADS // SWA_KV_HEADS
    a_q = a_q.reshape(d, SWA_KV_HEADS, grp, HEAD_DIM).transpose(0, 2, 1, 3).reshape(d, 256)
    z = lambda n: jnp.zeros((d, n), F32)
    g_grp = jnp.concatenate([f_g[:, 0:2], z(62), c_kr, z(32)], axis=1)
    gs_grp = jnp.concatenate([f_g[:, 2:4], z(62), _swap_halves(c_kr), z(32)], axis=1)
    w_perm = jnp.concatenate([a_q * qscale, a_k, a_v, f_q * qscale, f_k, f_v, s_q * qscale, s_k, s_v,
                              c_q, c_kv, g_grp, gs_grp], axis=1).astype(BF16)
    wuq = mla_w_uq[i].reshape(MLA_Q_LORA, 4, MLA_NOPE + MLA_ROPE)
    zq = lambda n: jnp.zeros((MLA_Q_LORA, 4, n), F32)
    w_uq_a = jnp.concatenate([wuq, zq(32)], axis=2).reshape(MLA_Q_LORA, 512).astype(BF16)
    w_uq_b = jnp.concatenate([zq(MLA_NOPE), _swap_halves(wuq[:, :, MLA_NOPE:]), zq(32)],
                             axis=2).reshape(MLA_Q_LORA, 512).astype(BF16)
    wukv = mla_w_ukv[i].reshape(MLA_KV_LORA, 4, MLA_NOPE + MLA_V)
    w_kv_k = jnp.concatenate([wukv[:, :, :MLA_NOPE], jnp.zeros((MLA_KV_LORA, 4, 64), F32)],
                             axis=2).reshape(MLA_KV_LORA, 512).astype(BF16)
    w_kv_v = wukv[:, :, MLA_NOPE:].reshape(MLA_KV_LORA, 256).astype(BF16)
    bf = b_forget[i].astype(F32)
    b_f = jnp.zeros((1, 256), F32).at[0, 0:2].set(bf[0:2]).at[0, 128:130].set(bf[2:4])
    wo = w_out[i]
    wo_a = wo[:256].reshape(SWA_KV_HEADS, grp, HEAD_DIM, d).transpose(1, 0, 2, 3).reshape(256, d)
    wo4 = jnp.concatenate([wo_a, wo[256:]], axis=0).reshape(4, 256, d).astype(BF16)
    w_r = jnp.concatenate([w_group[i], w_router[i], jnp.zeros((d, LANES - N_GROUPS - N_EXPERTS), F32)], axis=1)
    w_r_hi = w_r.astype(BF16)
    w_r_lo = (w_r - w_r_hi.astype(F32)).astype(BF16)
    b_r = jnp.concatenate([b_group[i], b_router[i], jnp.zeros((LANES - N_GROUPS - N_EXPERTS,), F32)])[None, :]
    return dict(
        attn_norm=attn_norm[i][None, :], w_in=w_perm, q_norm=mla_q_norm[i][None, :],
        kv_norm=mla_kv_norm[i][None, :], w_uq_a=w_uq_a, w_uq_b=w_uq_b, w_kv_k=w_kv_k, w_kv_v=w_kv_v,
        rope_tab=_rope_table(seq_len), b_forget=b_f, sinks=sinks[i].astype(F32), w_out=wo4,
        ffn_norm=ffn_norm[i][None, :], w_r_hi=w_r_hi, w_r_lo=w_r_lo, b_r=b_r.astype(F32),
        w_gate=w_gate, w_up=w_up, w_down=w_down, layer=i)


def kernel(x, meta_tokens, attn_norm, w_in, b_forget, sinks, mla_q_norm, mla_kv_norm, mla_w_uq, mla_w_ukv,
           w_out, ffn_norm, w_group, b_group, w_router, b_router, w_gate, w_up, w_down, final_norm):
    batch, seq, d = x.shape
    seq_len = seq + BLOCK
    assert seq_len % ROW_TILE == 0 and seq_len % (BLOCK * SWA_Q_BLOCKS) == 0
    assert all(seq % tile == 0 for tile in Q_TILES.values())
    t = batch * seq_len
    depth = w_in.shape[0]
    lead = jnp.concatenate([jnp.zeros((PAD_FRONT, d), x.dtype), meta_tokens.astype(x.dtype)], axis=0)
    h = (lead, x)
    y2 = y_ids = ew = None
    for i in range(depth):
        p = _layer_params(i, seq_len, attn_norm, w_in, b_forget, sinks, mla_q_norm, mla_kv_norm, mla_w_uq,
                          mla_w_ukv, w_out, ffn_norm, w_group, b_group, w_router, b_router, w_gate, w_up, w_down)
        h, (qa, fq, fk, fv, cq, ck, cv, sq, sk, sv) = _inproj(h, y2, y_ids, ew, p, seq_len)
        y_a = _swa_attention(qa, p["sinks"], batch, seq_len)
        y_b = _causal_attention("fox", fq, fk, fv, batch, seq_len)
        y_c = _causal_attention("mla", cq, ck, cv, batch, seq_len)
        y_d = _causal_attention("sb", sq, sk, sv, batch, seq_len)
        ys = [y.reshape(t, 256) for y in (y_a, y_b, y_c, y_d)]
        h, xn, ew, counts = _outproj(ys, h, p)
        block_e, nvalid, y_ids = _route(ew, counts[0, N_GROUPS:N_GROUPS + N_EXPERTS], t)
        y2 = _moe(_dispatch(xn, y_ids, nvalid), block_e, nvalid, p)
    return _final(h, y2, y_ids, ew, final_norm[None, :], batch, seq_len)
```

```python
import functools

import jax
import jax.numpy as jnp
import numpy as np
from jax import lax
from jax.experimental import pallas as pl
from jax.experimental.pallas import tpu as pltpu

F32 = jnp.float32
BF16 = jnp.bfloat16

BLOCK = 128
N_META = 16
PAD_FRONT = BLOCK - N_META
CHUNK_SHIFT = 6
HEAD_DIM = 64
NORM_EPS = 1e-6
NEG = -1e30
PAD_KEY_LOGIT = -(2.0 ** 100)
UNDERFLOW_LOG2 = -150.0
LOG2E = 1.4426950408889634
BIG = 1 << 30
SWA_HEADS, SWA_KV_HEADS, WINDOW = 4, 2, 128
MLA_Q_LORA, MLA_KV_LORA, MLA_NOPE, MLA_ROPE, MLA_V = 256, 128, 64, 32, 64
MLA_BIAS_LANE = MLA_NOPE + MLA_ROPE
ROPE_THETA = 10000.0
N_GROUPS, EXPERTS_PER_GROUP, TOP_K = 4, 8, 2
N_EXPERTS = N_GROUPS * EXPERTS_PER_GROUP
MOE_BLOCK = 512
LANES = 128
ROW_SUB = 8
ROW_TILE = 384
Q_TILES = {"fox": 1024, "mla": 1024, "sb": 512}
K_TILE = 256
ROW_PART = 128
SWA_Q_BLOCKS = 3
VMEM_LIMIT = 56 * 1024 * 1024

C_A, C_B, C_D, C_CQ, C_CKV, C_G, C_GS, C_END = 0, 512, 1280, 2048, 2304, 2432, 2560, 2688
B_F0, B_F1, B_PAD = 0, 3, 6


def _rms(x, g):
    return x * lax.rsqrt(jnp.mean(x * x, axis=-1, keepdims=True) + NORM_EPS) * g


def _log_sigmoid(x):
    return jnp.minimum(x, 0.0) - jnp.log(1.0 + jnp.exp(-jnp.abs(x)))


def _dot(a, b):
    return jnp.dot(a, b, preferred_element_type=F32)


def _dot_nt(a, b):
    return lax.dot_general(a, b, (((1,), (1,)), ((), ())), preferred_element_type=F32)


def _rows_from_tiles(ref, lead, n):
    return jnp.concatenate([ref[(*lead, pl.ds(j, n, stride=ROW_SUB), slice(None))] for j in range(ROW_SUB)], axis=1)


def _rows_to_tiles(ref, lead, x):
    n = x.shape[0]
    for j in range(ROW_SUB):
        ref[(*lead, pl.ds(j, n, stride=ROW_SUB), slice(None))] = x[:, j * LANES:(j + 1) * LANES]


def _tile4(x):
    return jnp.concatenate([x, x, x, x], axis=1)


def _split3(x):
    hi = x.astype(BF16)
    r1 = x - hi.astype(F32)
    mid = r1.astype(BF16)
    lo = (r1 - mid.astype(F32)).astype(BF16)
    return hi, mid, lo


def _free_base(head):
    return head * LANES + (HEAD_DIM if head % 2 == 0 else 0)


def _expert_rows_start(y_hbm, ids_ref, buf, sem, slot):
    for j in range(ids_ref.shape[2]):
        tok, k = divmod(j, TOP_K)
        src = pl.multiple_of(ids_ref[0, 0, j], ROW_SUB)
        pltpu.make_async_copy(y_hbm.at[pl.ds(src, ROW_SUB)], buf.at[slot, k, pl.ds(tok * ROW_SUB, ROW_SUB)],
                              sem.at[slot]).start(priority=j % 2)


def _expert_rows_wait(y_hbm, buf, sem, slot):
    for k in range(TOP_K):
        pltpu.make_async_copy(y_hbm.at[pl.ds(0, buf.shape[2])], buf.at[slot, k], sem.at[slot]).wait()


def _combine_experts(h, ew, y_hbm, ids0_ref, idsn_ref, buf, sem, step, n_steps):
    slot = step % 2

    @pl.when(step == 0)
    def _():
        _expert_rows_start(y_hbm, ids0_ref, buf, sem, 0)

    _expert_rows_start(y_hbm, idsn_ref, buf, sem, 1 - slot)
    _expert_rows_wait(y_hbm, buf, sem, slot)
    n = h.shape[0]
    out = h + ew[:, 0:1] * _rows_from_tiles(buf, (slot, 0), n) + ew[:, 1:2] * _rows_from_tiles(buf, (slot, 1), n)

    @pl.when(step == n_steps - 1)
    def _():
        _expert_rows_wait(y_hbm, buf, sem, 1 - slot)

    return out


def _inproj_kernel(has_y2, n_seq_tiles, *refs):
    if has_y2:
        (h_ref, ids0_ref, idsn_ref, y_hbm, ew_ref, *rest) = refs
    else:
        (lead_ref, *x_refs) = refs[:1 + ROW_TILE // BLOCK]
        rest = refs[1 + ROW_TILE // BLOCK:]
    (g_ref, w_ref, qn_ref, kvn_ref, wuqa_ref, wuqb_ref, wkvk_ref, wkvv_ref, tab_ref, bf_ref,
     pq_ref, pk_ref, rows_ref, hout_ref, *outs) = rest
    if has_y2:
        *outs, ybuf, ysem = outs
    (qa_ref, fq_ref, fk_ref, fv_ref, cq_ref, ck_ref, cv_ref, sq_ref, sk_ref, sv_ref, carry_ref) = outs
    tile = pl.program_id(0) % n_seq_tiles
    if has_y2:
        h = _combine_experts(h_ref[...], ew_ref[...], y_hbm, ids0_ref, idsn_ref, ybuf, ysem,
                             pl.program_id(0), pl.num_programs(0))
    else:
        first = jnp.where(tile == 0, lead_ref[...], x_refs[0][0])
        h = jnp.concatenate([first] + [r[0] for r in x_refs[1:]], axis=0)
    hout_ref[...] = h
    tm, d = h.shape
    xn = _rms(h, g_ref[...]).astype(BF16)
    acc = _dot(xn, w_ref[...])
    lane = lax.broadcasted_iota(jnp.int32, (1, LANES), 1)
    lo_half = lane < HEAD_DIM
    pad_col = jnp.where(tile * tm + lax.broadcasted_iota(jnp.int32, (tm, 1), 0) < PAD_FRONT,
                        PAD_KEY_LOGIT, 0.0)
    rows = rows_ref[...]
    fq_one, fk_one, pad_lane, sq_one, mla_one, mla_pad = (rows[j:j + 1] for j in range(6))

    def per_head(x_pair, bias, pair, scale=None):
        x = x_pair if scale is None else x_pair * scale
        even = jnp.where(lo_half, x, bias[:, (2 * pair) * LANES:(2 * pair + 1) * LANES])
        odd = jnp.where(lo_half, bias[:, (2 * pair + 1) * LANES:(2 * pair + 2) * LANES], x)
        return even, odd

    def store_heads(ref, x_off, bias, scale=None):
        for pair in range(2):
            x_pair = acc[:, x_off + pair * LANES:x_off + (pair + 1) * LANES]
            even, odd = per_head(x_pair, bias, pair, scale)
            ref[:, (2 * pair) * LANES:(2 * pair + 1) * LANES] = even.astype(BF16)
            ref[:, (2 * pair + 1) * LANES:(2 * pair + 2) * LANES] = odd.astype(BF16)

    qa_ref[...] = acc[:, C_A:C_B].astype(BF16)

    @pl.when(tile == 0)
    def _():
        carry_ref[...] = jnp.zeros_like(carry_ref)

    lf = _log_sigmoid(acc[:, C_G:C_END] + bf_ref[...]) * LOG2E
    r = lax.broadcasted_iota(jnp.int32, (BLOCK, BLOCK), 0)
    c = lax.broadcasted_iota(jnp.int32, (BLOCK, BLOCK), 1)
    tri = jnp.where(c <= r, 1.0, 0.0).astype(BF16)
    carry = carry_ref[...]
    blocks = []
    for b in range(tm // BLOCK):
        hi, mid, lo = _split3(lf[b * BLOCK:(b + 1) * BLOCK])
        y = _dot(tri, hi) + _dot(tri, mid) + _dot(tri, lo) + carry
        carry = y[BLOCK - 1:BLOCK, :]
        blocks.append(y)
    carry_ref[...] = carry
    f_hi, f_mid, f_lo = _split3(jnp.concatenate(blocks, axis=0))
    q_bias = _dot(f_hi, pq_ref[0]) + _dot(f_mid, pq_ref[1]) + _dot(f_lo, pq_ref[2]) + fq_one
    k_bias = (_dot(f_hi, pk_ref[0]) + _dot(f_mid, pk_ref[1]) + _dot(f_lo, pk_ref[2]) + fk_one
              + pad_col * pad_lane)
    ones = jnp.ones((tm, 4 * LANES), F32)
    store_heads(fq_ref, C_B, q_bias, LOG2E)
    store_heads(fk_ref, C_B + 256, k_bias)
    store_heads(fv_ref, C_B + 512, ones)

    store_heads(sq_ref, C_D, jnp.broadcast_to(sq_one, (tm, 4 * LANES)), LOG2E)
    store_heads(sk_ref, C_D + 256, pad_col * pad_lane)
    sv_ref[...] = acc[:, C_D + 512:C_CQ].astype(BF16)

    tab = tab_ref[...]
    cos_q, sin_q = tab[:, 0:128], tab[:, 128:256]
    cos_k, sin_k = tab[:, 256:384], tab[:, 384:512]
    cqn = _rms(acc[:, C_CQ:C_CKV], qn_ref[...]).astype(BF16)
    q_lin = _dot(cqn, wuqa_ref[...])
    q_swp = _dot(cqn, wuqb_ref[...])
    cq_ref[...] = (q_lin * _tile4(cos_q) + q_swp * _tile4(sin_q) + mla_one).astype(BF16)
    ckvn = _rms(acc[:, C_CKV:C_G], kvn_ref[...]).astype(BF16)
    k_nope = _dot(ckvn, wkvk_ref[...])
    grp, grp_s = acc[:, C_G:C_GS], acc[:, C_GS:C_END]
    k_rope = grp * cos_k + grp_s * sin_k
    ck_ref[...] = (k_nope + _tile4(k_rope) + pad_col * mla_pad).astype(BF16)
    vv = _dot(ckvn, wkvv_ref[...])
    for pair in range(2):
        even, odd = per_head(vv[:, pair * LANES:(pair + 1) * LANES], ones, pair)
        cv_ref[:, (2 * pair) * LANES:(2 * pair + 1) * LANES] = even.astype(BF16)
        cv_ref[:, (2 * pair + 1) * LANES:(2 * pair + 2) * LANES] = odd.astype(BF16)


def _bias_constants():
    src = (0, 1, LANES, LANES + 1)
    pq = np.zeros((3, 2 * LANES, 4 * LANES), np.float32)
    pk = np.zeros((3, 2 * LANES, 4 * LANES), np.float32)
    rows = np.zeros((8, 4 * LANES), np.float32)
    for head in range(4):
        base = _free_base(head)
        for part in range(3):
            pq[part, src[head], base + B_F0 + part] = 1.0
            pk[part, src[head], base + B_F1 + part] = -1.0
            rows[0, base + B_F1 + part] = 1.0
            rows[1, base + B_F0 + part] = 1.0
        rows[0, base + B_PAD] = 1.0
        rows[2, base + B_PAD] = 1.0
        rows[3, base + B_PAD] = 1.0
        rows[4, head * LANES + MLA_BIAS_LANE] = 1.0
        rows[5, head * LANES + MLA_BIAS_LANE] = 1.0
    return jnp.asarray(pq, BF16), jnp.asarray(pk, BF16), jnp.asarray(rows, F32)


def _inproj(h, y2, y_ids, ew, p, seq_len):
    tm = ROW_TILE
    n_seq_tiles = seq_len // tm
    has_y2 = y2 is not None
    row = lambda i: (i, 0)
    fixed = lambda i: (0, 0)
    if has_y2:
        t, d = h.shape
        in_specs = [pl.BlockSpec((tm, d), row)]
        args = [h]
    else:
        lead, x = h
        batch, seq, d = x.shape
        t = batch * seq_len
        per_tile = tm // BLOCK
        frame = lambda k: pl.BlockSpec((1, BLOCK, d), lambda i: (
            i // n_seq_tiles, jnp.maximum((i % n_seq_tiles) * per_tile - 1 + k, 0), 0))
        in_specs = [pl.BlockSpec((BLOCK, d), fixed)] + [frame(k) for k in range(per_tile)]
        args = [lead] + [x] * per_tile
    if has_y2:
        ids = y_ids.reshape(t // tm, 1, TOP_K * tm)
        ids = jnp.concatenate([ids, jnp.zeros_like(ids[:1])], axis=0)
        in_specs += [pl.BlockSpec((1, 1, TOP_K * tm), lambda i: (0, 0, 0), memory_space=pltpu.SMEM),
                     pl.BlockSpec((1, 1, TOP_K * tm), lambda i: (i + 1, 0, 0), memory_space=pltpu.SMEM),
                     pl.BlockSpec(memory_space=pl.ANY),
                     pl.BlockSpec((tm, LANES), row)]
        args += [ids, ids, y2, ew]
    pq, pk, rows = _bias_constants()
    consts = [p["attn_norm"], p["w_in"], p["q_norm"], p["kv_norm"], p["w_uq_a"], p["w_uq_b"],
              p["w_kv_k"], p["w_kv_v"]]
    in_specs += [pl.BlockSpec(c.shape, fixed) for c in consts]
    args += consts
    in_specs.append(pl.BlockSpec((tm, 512), lambda i: (i % n_seq_tiles, 0)))
    args.append(p["rope_tab"])
    in_specs += [pl.BlockSpec((1, 256), fixed),
                 pl.BlockSpec(pq.shape, lambda i: (0, 0, 0)),
                 pl.BlockSpec(pk.shape, lambda i: (0, 0, 0)),
                 pl.BlockSpec(rows.shape, fixed)]
    args += [p["b_forget"], pq, pk, rows]
    widths = [512] * 9 + [256]
    out_shape = [jax.ShapeDtypeStruct((t, d), F32)] + [jax.ShapeDtypeStruct((t, w), BF16) for w in widths]
    out_specs = [pl.BlockSpec((tm, d), row)] + [pl.BlockSpec((tm, w), row) for w in widths]
    outs = pl.pallas_call(
        functools.partial(_inproj_kernel, has_y2, n_seq_tiles),
        grid=(t // tm,),
        in_specs=in_specs,
        out_specs=out_specs,
        out_shape=out_shape,
        scratch_shapes=[pltpu.VMEM((1, 2 * LANES), F32)] + (
            [pltpu.VMEM((2, TOP_K, tm * ROW_SUB, LANES), F32), pltpu.SemaphoreType.DMA((2,))] if has_y2 else []),
        compiler_params=pltpu.CompilerParams(dimension_semantics=("arbitrary",),
                                             vmem_limit_bytes=VMEM_LIMIT),
        name="inproj_y2" if has_y2 else "inproj",
    )(*args)
    return outs[0], outs[1:]


def _swa_kernel(sink_ref, q_ref, km_ref, kp_ref, kc_ref, vm_ref, vp_ref, vc_ref, o_ref):
    i = pl.program_id(1)
    n_sub = q_ref.shape[1] // BLOCK
    lane = lax.broadcasted_iota(jnp.int32, (1, LANES), 1)
    lo_half = lane < HEAD_DIM
    half_masks = [jnp.where(lo_half, 1.0, 0.0).astype(BF16), jnp.where(lo_half, 0.0, 1.0).astype(BF16)]
    row = lax.broadcasted_iota(jnp.int32, (BLOCK, 1), 0)
    col = lax.broadcasted_iota(jnp.int32, (1, BLOCK), 1)
    grp = SWA_HEADS // SWA_KV_HEADS
    k_all = jnp.concatenate([km_ref[0], kp_ref[0], kc_ref[0]], axis=0)
    v_all = jnp.concatenate([vm_ref[0], vp_ref[0], vc_ref[0]], axis=0)
    for j in range(n_sub):
        q0 = (i * n_sub + j) * BLOCK
        pq = q0 + row
        cq = pq >> CHUNK_SHIFT
        segs = []
        vis_m = col >= PAD_FRONT
        segs.append((vis_m, jnp.minimum(jnp.abs(pq - col), WINDOW).astype(F32)))
        for pk in (q0 - BLOCK + col, q0 + col):
            ck = jnp.where(pk >= BLOCK, pk >> CHUNK_SHIFT, BIG)
            vis = (ck <= cq) & (ck >= cq - (WINDOW >> CHUNK_SHIFT))
            segs.append((vis, jnp.abs(pq - pk).astype(F32)))
        kj = jnp.concatenate([k_all[0:BLOCK], k_all[(j + 1) * BLOCK:(j + 3) * BLOCK]], axis=0)
        vj = jnp.concatenate([v_all[0:BLOCK], v_all[(j + 1) * BLOCK:(j + 3) * BLOCK]], axis=0)
        for g in range(grp):
            qg = q_ref[0, j * BLOCK:(j + 1) * BLOCK, g * LANES:(g + 1) * LANES]
            out_g = None
            for hk in range(SWA_KV_HEADS):
                head = hk * grp + g
                slope = 2.0 ** (-8.0 * (head + 1) / SWA_HEADS)
                sink = sink_ref[head]
                s_all = _dot_nt(qg * half_masks[hk], kj)
                tiles = [jnp.where(vis, s_all[:, n * LANES:(n + 1) * LANES] - slope * dist, NEG)
                         for n, (vis, dist) in enumerate(segs)]
                top = jnp.maximum(jnp.maximum(tiles[0], tiles[1]), tiles[2])
                m = jnp.broadcast_to(jnp.maximum(jnp.max(top, axis=-1, keepdims=True), sink), (BLOCK, LANES))
                e = [jnp.exp(x - m) for x in tiles]
                den = jnp.sum(e[0] + e[1] + e[2], axis=-1, keepdims=True) + jnp.exp(sink - m[:, 0:1])
                o = _dot(jnp.concatenate(e, axis=1).astype(BF16), vj) * (1.0 / den)
                out_g = o if hk == 0 else jnp.where(lo_half, out_g, o)
            o_ref[0, j * BLOCK:(j + 1) * BLOCK, g * LANES:(g + 1) * LANES] = out_g.astype(BF16)


def _swa_attention(qa, sinks, batch, seq_len):
    x = qa.reshape(batch, seq_len, 512)
    n_sub = SWA_Q_BLOCKS
    nb = seq_len // (BLOCK * n_sub)
    blk = lambda f: pl.BlockSpec((1, BLOCK, LANES), f)
    own = lambda c: pl.BlockSpec((1, BLOCK * n_sub, LANES), lambda b, i: (b, i, c))
    return pl.pallas_call(
        _swa_kernel,
        grid=(batch, nb),
        in_specs=[
            pl.BlockSpec(memory_space=pltpu.SMEM),
            pl.BlockSpec((1, BLOCK * n_sub, 2 * LANES), lambda b, i: (b, i, 0)),
            blk(lambda b, i: (b, 0, 2)),
            blk(lambda b, i: (b, jnp.maximum(i * n_sub - 1, 0), 2)),
            own(2),
            blk(lambda b, i: (b, 0, 3)),
            blk(lambda b, i: (b, jnp.maximum(i * n_sub - 1, 0), 3)),
            own(3),
        ],
        out_specs=pl.BlockSpec((1, BLOCK * n_sub, 2 * LANES), lambda b, i: (b, i, 0)),
        out_shape=jax.ShapeDtypeStruct((batch, seq_len, 2 * LANES), BF16),
        compiler_params=pltpu.CompilerParams(dimension_semantics=("arbitrary", "arbitrary"),
                                             vmem_limit_bytes=VMEM_LIMIT),
        name="swa_attention",
    )(sinks, x, x, x, x, x, x, x)


def _causal_kernel(mode, q_ref, k_ref, v_ref, o_ref, stat_ref, acc_ref):
    seq_len = q_ref.shape[1]
    Q_TILE = stat_ref.shape[1]
    n_qt = (seq_len - BLOCK) // Q_TILE
    per_tile = Q_TILE // K_TILE
    lane = lax.broadcasted_iota(jnp.int32, (1, LANES), 1)
    lo_half = lane < HEAD_DIM
    if mode == "sb":
        r = lax.broadcasted_iota(jnp.int32, (2 * K_TILE, K_TILE), 0) & (K_TILE - 1)
        c = lax.broadcasted_iota(jnp.int32, (2 * K_TILE, K_TILE), 1)
        later2 = jnp.where(r > c, 1.0, 0.0).astype(BF16)
        r1 = lax.broadcasted_iota(jnp.int32, (2 * BLOCK, BLOCK), 0) & (BLOCK - 1)
        c1 = lax.broadcasted_iota(jnp.int32, (2 * BLOCK, BLOCK), 1)
        later1 = jnp.where(r1 > c1, 1.0, 0.0).astype(BF16)

    def causal(pq, k0, tk):
        pk = k0 + lax.broadcasted_iota(jnp.int32, (1, tk), 1)
        if mode == "fox":
            return pk <= pq
        if mode == "mla":
            return (pk >> CHUNK_SHIFT) <= (pq >> CHUNK_SHIFT)
        return pk < pq

    def head_v(k0, tk, hh):
        if mode == "sb":
            return v_ref[0, pl.ds(k0, tk), :]
        return v_ref[0, pl.ds(k0, tk), hh * LANES:(hh + 1) * LANES]

    def lane_tiles(x):
        return [x[:, j * LANES:(j + 1) * LANES] for j in range(x.shape[1] // LANES)]

    def row_parts(lo, hi):
        step = min(hi - lo, ROW_PART)
        return [(r0, step) for r0 in range(lo, hi, step)]

    def softmax_chunk(q0, tq, k0, tk, masked, first, row_lo=0):
        for hh in range(2):
            qh = q_ref[0, pl.ds(q0 + row_lo, tq - row_lo), hh * LANES:(hh + 1) * LANES]
            kh = k_ref[0, pl.ds(k0, tk), hh * LANES:(hh + 1) * LANES]
            s_all = _dot_nt(qh, kh)
            vh = head_v(k0, tk, hh)
            for r0, tr in row_parts(row_lo, tq):
                s = s_all[r0 - row_lo:r0 - row_lo + tr]
                if masked and r0 < row_lo + tk:
                    pq = q0 + r0 + lax.broadcasted_iota(jnp.int32, (tr, 1), 0)
                    s = jnp.where(causal(pq, k0, tk), s, NEG)
                tiles = lane_tiles(s)
                top = tiles[0]
                for x in tiles[1:]:
                    top = jnp.maximum(top, x)
                m_new = jnp.broadcast_to(jnp.max(top, axis=-1, keepdims=True), (tr, LANES))
                if not first:
                    m_old = stat_ref[hh, r0:r0 + tr, :]
                    m_new = jnp.maximum(m_old, m_new)
                p = jnp.concatenate([jnp.exp2(x - m_new) for x in tiles], axis=1).astype(BF16)
                pv = _dot(p, vh)
                if not first:
                    pv = jnp.exp2(m_old - m_new) * acc_ref[hh, r0:r0 + tr, :] + pv
                stat_ref[hh, r0:r0 + tr, :] = m_new
                acc_ref[hh, r0:r0 + tr, :] = pv

    def stick_chunk(q0, rows, k0, tk, masked, first):
        lo_r, hi_r = rows
        tq = hi_r - lo_r
        later = later2 if tk == K_TILE else later1
        for hh in range(2):
            qh = q_ref[0, pl.ds(q0 + lo_r, tq), hh * LANES:(hh + 1) * LANES]
            kh = k_ref[0, pl.ds(k0, tk), hh * LANES:(hh + 1) * LANES]
            z = _dot_nt(qh, kh)
            ls_pos = jnp.minimum(z, 0.0) - jnp.log(1.0 + jnp.exp2(-jnp.abs(z))) * LOG2E
            log_keep = ls_pos - z
            if masked:
                pq = q0 + lo_r + lax.broadcasted_iota(jnp.int32, (tq, 1), 0)
                vis = causal(pq, k0, tk)
                log_keep = jnp.where(vis, log_keep, 0.0)
            hi = log_keep.astype(BF16)
            lo = (log_keep - hi.astype(F32)).astype(BF16)
            after = _dot(jnp.concatenate([hi, lo], axis=1), later)
            tot = ls_pos + after
            chunk_total = jnp.broadcast_to(after[:, 0:1] + log_keep[:, 0:1], (tq, LANES))
            if not first:
                carry = stat_ref[hh, lo_r:hi_r, :]
                tot = jnp.concatenate([x + carry for x in lane_tiles(tot)], axis=1)
                chunk_total = carry + chunk_total
            a = jnp.exp2(tot)
            if masked:
                a = jnp.where(vis, a, 0.0)
            pv = _dot(a.astype(BF16), head_v(k0, tk, hh))
            if not first:
                pv = acc_ref[hh, lo_r:hi_r, :] + pv
            stat_ref[hh, lo_r:hi_r, :] = chunk_total
            acc_ref[hh, lo_r:hi_r, :] = pv

    def finish(q0, tq):
        a0, a1 = acc_ref[0, 0:tq, :], acc_ref[1, 0:tq, :]
        if mode != "sb":
            a0 = a0 / a0[:, HEAD_DIM:HEAD_DIM + 1]
            a1 = a1 / a1[:, 0:1]
        o_ref[0, pl.ds(q0, tq), :] = jnp.where(lo_half, a0, a1).astype(BF16)

    def chunk_start(j):
        return pl.multiple_of(BLOCK + j * K_TILE, BLOCK)

    if mode == "sb":
        stick_chunk(0, (0, BLOCK), 0, BLOCK, True, True)
        finish(0, BLOCK)
        whole = (0, Q_TILE)

        def q_body(i, _):
            q0 = pl.multiple_of(BLOCK + i * Q_TILE, BLOCK)
            n_int = i * per_tile
            for d in reversed(range(per_tile)):
                stick_chunk(q0, whole, chunk_start(n_int + d), K_TILE, True, d == per_tile - 1)

            def alive():
                top = jnp.maximum(jnp.max(stat_ref[0]), jnp.max(stat_ref[1]))
                return (top > UNDERFLOW_LOG2).astype(jnp.int32)

            def body(st):
                jj, _ = st
                stick_chunk(q0, whole, chunk_start(n_int - 1 - jj), K_TILE, False, False)
                return jj + 1, alive()

            _, go = lax.while_loop(lambda st: (st[0] < n_int) & (st[1] > 0), body, (0, alive()))

            @pl.when(go > 0)
            def _():
                stick_chunk(q0, whole, 0, BLOCK, False, False)

            finish(q0, Q_TILE)
            return 0
    else:
        softmax_chunk(0, BLOCK, 0, BLOCK, True, True)
        finish(0, BLOCK)

        def q_body(i, _):
            q0 = pl.multiple_of(BLOCK + i * Q_TILE, BLOCK)
            n_int = i * per_tile
            softmax_chunk(q0, Q_TILE, 0, BLOCK, False, True)

            def body(j, _):
                for d in range(per_tile):
                    softmax_chunk(q0, Q_TILE, chunk_start(j * per_tile + d), K_TILE, False, False)
                return 0

            lax.fori_loop(0, i, body, 0)
            for d in range(per_tile):
                softmax_chunk(q0, Q_TILE, chunk_start(n_int + d), K_TILE, True, False, row_lo=d * K_TILE)
            finish(q0, Q_TILE)
            return 0

    lax.fori_loop(0, n_qt, q_body, 0)


def _causal_attention(mode, q, k, v, batch, seq_len):
    wide = pl.BlockSpec((1, seq_len, 2 * LANES), lambda b, p: (b, 0, p))
    narrow = pl.BlockSpec((1, seq_len, LANES), lambda b, p: (b, 0, p))
    args = [q.reshape(batch, seq_len, 512), k.reshape(batch, seq_len, 512),
            v.reshape(batch, seq_len, v.shape[1])]
    return pl.pallas_call(
        functools.partial(_causal_kernel, mode),
        grid=(batch, 2),
        in_specs=[wide, wide, narrow if mode == "sb" else wide],
        out_specs=narrow,
        out_shape=jax.ShapeDtypeStruct((batch, seq_len, 2 * LANES), BF16),
        scratch_shapes=[pltpu.VMEM((2, Q_TILES[mode], LANES), F32), pltpu.VMEM((2, Q_TILES[mode], LANES), F32)],
        compiler_params=pltpu.CompilerParams(dimension_semantics=("arbitrary", "arbitrary"),
                                             vmem_limit_bytes=VMEM_LIMIT),
        name=mode + "_attention",
    )(*args)


def _outproj_kernel(ya_ref, yb_ref, yc_ref, yd_ref, h_ref, wo_ref, g_ref, wrh_ref, wrl_ref, br_ref, tri_ref,
                    h2_ref, xn_ref, route_ref, cnt_ref):
    o = (_dot(ya_ref[...], wo_ref[0]) + _dot(yb_ref[...], wo_ref[1])
         + _dot(yc_ref[...], wo_ref[2]) + _dot(yd_ref[...], wo_ref[3]))
    h2 = h_ref[...] + o
    h2_ref[...] = h2
    xn = _rms(h2, g_ref[...])
    _rows_to_tiles(xn_ref, (), xn)
    xh = xn.astype(BF16)
    xl = (xn - xh.astype(F32)).astype(BF16)
    wrh, wrl = wrh_ref[...], wrl_ref[...]
    lg = _dot(xh, wrh) + _dot(xl, wrh) + _dot(xh, wrl) + br_ref[...]

    tm = lg.shape[0]
    lane = lax.broadcasted_iota(jnp.int32, (tm, LANES), 1)
    ninf = -jnp.inf

    def first_max(x):
        top = jnp.max(x, axis=-1, keepdims=True)
        return top, jnp.min(jnp.where(x == top, lane, LANES), axis=-1, keepdims=True)

    gl = jnp.where(lane < N_GROUPS, lg, ninf)
    g_max, g_top = first_max(gl)
    g_w = 1.0 / jnp.sum(jnp.exp(gl - g_max), axis=-1, keepdims=True)
    e_lo = N_GROUPS + g_top * EXPERTS_PER_GROUP
    el = jnp.where((lane >= e_lo) & (lane < e_lo + EXPERTS_PER_GROUP), lg, ninf)
    v1, i1 = first_max(el)
    v2, i2 = first_max(jnp.where(lane == i1, ninf, el))
    r21 = jnp.exp(v2 - v1)
    w1 = g_w / (1.0 + r21)
    w2 = w1 * r21

    @pl.when(pl.program_id(0) == 0)
    def _():
        cnt_ref[...] = jnp.zeros_like(cnt_ref)

    m1 = jnp.where(lane == i1, 1.0, 0.0)
    m2 = jnp.where(lane == i2, 1.0, 0.0)
    both = m1 + m2
    before = _dot(tri_ref[...], both.astype(BF16)) + cnt_ref[0:1, :]
    rank1 = jnp.sum(m1 * before, axis=-1, keepdims=True)
    rank2 = jnp.sum(m2 * before, axis=-1, keepdims=True)
    cnt_ref[...] = cnt_ref[...] + jnp.sum(both, axis=0, keepdims=True)
    cols = [w1, w2, (i1 - N_GROUPS).astype(F32), (i2 - N_GROUPS).astype(F32), rank1, rank2]
    route = jnp.zeros((tm, LANES), F32)
    for j, c in enumerate(cols):
        route = jnp.where(lane == j, c, route)
    route_ref[...] = route


def _outproj(ys, h, p):
    t, d = h.shape
    tm = ROW_TILE
    row = lambda i: (i, 0)
    fixed2 = lambda i: (0, 0)
    in_specs = [pl.BlockSpec((tm, 256), row)] * 4 + [
        pl.BlockSpec((tm, d), row),
        pl.BlockSpec((4, 256, d), lambda i: (0, 0, 0)),
        pl.BlockSpec((1, d), fixed2),
        pl.BlockSpec((d, LANES), fixed2),
        pl.BlockSpec((d, LANES), fixed2),
        pl.BlockSpec((1, LANES), fixed2),
        pl.BlockSpec((tm, tm), fixed2),
    ]
    earlier = jnp.asarray(np.tril(np.ones((tm, tm), np.float32), -1), BF16)
    return pl.pallas_call(
        _outproj_kernel,
        grid=(t // tm,),
        in_specs=in_specs,
        out_specs=[pl.BlockSpec((tm, d), row), pl.BlockSpec((tm * ROW_SUB, LANES), row),
                   pl.BlockSpec((tm, LANES), row), pl.BlockSpec((8, LANES), fixed2)],
        out_shape=[jax.ShapeDtypeStruct((t, d), F32), jax.ShapeDtypeStruct((t * ROW_SUB, LANES), F32),
                   jax.ShapeDtypeStruct((t, LANES), F32), jax.ShapeDtypeStruct((8, LANES), F32)],
        compiler_params=pltpu.CompilerParams(dimension_semantics=("arbitrary",),
                                             vmem_limit_bytes=VMEM_LIMIT),
        name="outproj_router",
    )(*ys, h, p["w_out"], p["ffn_norm"], p["w_r_hi"], p["w_r_lo"], p["b_r"], earlier)


def _dispatch_kernel(nv_ref, ids_ref, x_ref, xs_hbm, zero_buf, sem):
    blk = zero_buf.shape[0]

    @pl.when(pl.program_id(0) == 0)
    def _():
        zero_buf[...] = jnp.zeros_like(zero_buf)
        for phase in ("start", "wait"):
            for b in range(nv_ref.shape[0]):
                @pl.when(nv_ref[b] < MOE_BLOCK)
                def _():
                    copy = pltpu.make_async_copy(zero_buf, xs_hbm.at[pl.ds(b * blk, blk)], sem)
                    copy.start() if phase == "start" else copy.wait()

    for j in range(ids_ref.shape[2]):
        tok, _ = divmod(j, TOP_K)
        dst = pl.multiple_of(ids_ref[0, 0, j], ROW_SUB)
        pltpu.make_async_copy(x_ref.at[pl.ds(tok * ROW_SUB, ROW_SUB)], xs_hbm.at[pl.ds(dst, ROW_SUB)],
                              sem).start(priority=j % 2)
    for _ in range(TOP_K):
        pltpu.make_async_copy(x_ref, xs_hbm.at[pl.ds(0, x_ref.shape[0])], sem).wait()


def _dispatch(xn, y_ids, nvalid):
    tm = ROW_TILE
    t = xn.shape[0] // ROW_SUB
    n_blk = nvalid.shape[0]
    return pl.pallas_call(
        _dispatch_kernel,
        grid_spec=pltpu.PrefetchScalarGridSpec(
            num_scalar_prefetch=1,
            grid=(t // tm,),
            in_specs=[pl.BlockSpec((1, 1, TOP_K * tm), lambda i, nv: (i, 0, 0), memory_space=pltpu.SMEM),
                      pl.BlockSpec((tm * ROW_SUB, LANES), lambda i, nv: (i, 0))],
            out_specs=pl.BlockSpec(memory_space=pl.ANY),
            scratch_shapes=[pltpu.VMEM((MOE_BLOCK * ROW_SUB, LANES), F32), pltpu.SemaphoreType.DMA],
        ),
        out_shape=jax.ShapeDtypeStruct((n_blk * MOE_BLOCK * ROW_SUB, LANES), F32),
        compiler_params=pltpu.CompilerParams(dimension_semantics=("arbitrary",),
                                             vmem_limit_bytes=VMEM_LIMIT),
        name="moe_dispatch",
    )(nvalid, y_ids.reshape(t // tm, 1, TOP_K * tm), xn)


def _moe_kernel(be_ref, nv_ref, x_ref, wg_ref, wu_ref, wd_ref, y_ref, wg_bf, wu_bf, wd_bf):
    s = pl.program_id(0)
    nv = nv_ref[s]

    @pl.when((s == 0) | (be_ref[s] != be_ref[jnp.maximum(s - 1, 0)]))
    def _():
        wg_bf[...] = wg_ref[0].astype(BF16)
        wu_bf[...] = wu_ref[0].astype(BF16)
        wd_bf[...] = wd_ref[0].astype(BF16)

    @pl.when(nv > 0)
    def _():
        x = _rows_from_tiles(x_ref, (), MOE_BLOCK).astype(BF16)
        gate = _dot(x, wg_bf[...])
        up = _dot(x, wu_bf[...])
        hid = (gate * (1.0 / (1.0 + jnp.exp(-gate))) * up).astype(BF16)
        _rows_to_tiles(y_ref, (), _dot(hid, wd_bf[...]))

    @pl.when(nv == 0)
    def _():
        y_ref[...] = jnp.zeros_like(y_ref)


def _moe(xs, block_e, nvalid, p):
    d = xs.shape[1] * ROW_SUB
    n_blk = block_e.shape[0]
    layer = p["layer"]
    hdim = p["w_gate"].shape[3]
    wspec = lambda shape: pl.BlockSpec((None, 1) + shape, lambda s, be, nv: (layer, be[s], 0, 0))
    rows = pl.BlockSpec((MOE_BLOCK * ROW_SUB, LANES), lambda s, be, nv: (s, 0))
    return pl.pallas_call(
        _moe_kernel,
        grid_spec=pltpu.PrefetchScalarGridSpec(
            num_scalar_prefetch=2,
            grid=(n_blk,),
            in_specs=[rows, wspec((d, hdim)), wspec((d, hdim)), wspec((hdim, d))],
            out_specs=rows,
            scratch_shapes=[pltpu.VMEM((d, hdim), BF16), pltpu.VMEM((d, hdim), BF16), pltpu.VMEM((hdim, d), BF16)],
        ),
        out_shape=jax.ShapeDtypeStruct(xs.shape, F32),
        compiler_params=pltpu.CompilerParams(dimension_semantics=("arbitrary",),
                                             vmem_limit_bytes=VMEM_LIMIT),
        name="moe_experts",
    )(block_e, nvalid, xs, p["w_gate"], p["w_up"], p["w_down"])


def _route(route, counts, t):
    a = t * TOP_K
    expert = route[:, 2:4].astype(jnp.int32).reshape(a)
    pos = route[:, 4:6].astype(jnp.int32).reshape(a)
    counts = counts.astype(jnp.int32)
    padded = (counts + MOE_BLOCK - 1) // MOE_BLOCK * MOE_BLOCK
    pad_end = jnp.cumsum(padded)
    pad_start = pad_end - padded
    dest = pad_start[expert] + pos
    n_blk = (a + N_EXPERTS * MOE_BLOCK) // MOE_BLOCK
    starts = jnp.arange(n_blk, dtype=jnp.int32) * MOE_BLOCK
    block_e = jnp.minimum(jnp.sum((pad_end[None, :] <= starts[:, None]).astype(jnp.int32), axis=1), N_EXPERTS - 1)
    nvalid = jnp.clip((pad_start + counts)[block_e] - starts, 0, MOE_BLOCK)
    return block_e, nvalid, dest * ROW_SUB


def _final_kernel(h_ref, ids0_ref, idsn_ref, y_hbm, ew_ref, g_ref, o_ref, ybuf, ysem):
    h = _combine_experts(h_ref[...], ew_ref[...], y_hbm, ids0_ref, idsn_ref, ybuf, ysem,
                         pl.program_id(0), pl.num_programs(0))
    o_ref[0] = _rms(h, g_ref[...])


def _final(h, y2, y_ids, ew, g, batch, seq_len):
    t, d = h.shape
    per_seq = seq_len // BLOCK
    out_blocks = per_seq - 1
    row = lambda n: ((n // out_blocks) * per_seq + n % out_blocks + 1, 0)
    ids = y_ids.reshape(batch, per_seq, TOP_K * BLOCK)[:, 1:].reshape(batch * out_blocks, 1, TOP_K * BLOCK)
    ids = jnp.concatenate([ids, jnp.zeros_like(ids[:1])], axis=0)
    return pl.pallas_call(
        _final_kernel,
        grid=(batch * out_blocks,),
        in_specs=[pl.BlockSpec((BLOCK, d), row),
                  pl.BlockSpec((1, 1, TOP_K * BLOCK), lambda n: (0, 0, 0), memory_space=pltpu.SMEM),
                  pl.BlockSpec((1, 1, TOP_K * BLOCK), lambda n: (n + 1, 0, 0), memory_space=pltpu.SMEM),
                  pl.BlockSpec(memory_space=pl.ANY),
                  pl.BlockSpec((BLOCK, LANES), row),
                  pl.BlockSpec((1, d), lambda n: (0, 0))],
        out_specs=pl.BlockSpec((1, BLOCK, d), lambda n: (n // out_blocks, n % out_blocks, 0)),
        out_shape=jax.ShapeDtypeStruct((batch, seq_len - BLOCK, d), F32),
        scratch_shapes=[pltpu.VMEM((2, TOP_K, BLOCK * ROW_SUB, LANES), F32), pltpu.SemaphoreType.DMA((2,))],
        compiler_params=pltpu.CompilerParams(dimension_semantics=("arbitrary",),
                                             vmem_limit_bytes=VMEM_LIMIT),
        name="final_norm",
    )(h, ids, ids, y2, ew, g)


def _rope_table(seq_len):
    half = MLA_ROPE // 2
    pos = (jnp.arange(seq_len, dtype=jnp.int32) - PAD_FRONT).astype(F32)
    inv_freq = ROPE_THETA ** (-jnp.arange(half, dtype=F32) / half)
    ang = pos[:, None] * inv_freq[None, :]
    cos, sin = jnp.cos(ang), jnp.sin(ang)
    cos2 = jnp.concatenate([cos, cos], axis=1)
    sin2 = jnp.concatenate([-sin, sin], axis=1)
    z = lambda w: jnp.zeros((seq_len, w), F32)
    scale = (MLA_NOPE + MLA_ROPE) ** -0.5 * LOG2E
    cos_q = jnp.concatenate([jnp.ones((seq_len, MLA_NOPE), F32), cos2, z(32)], axis=1) * scale
    sin_q = jnp.concatenate([z(MLA_NOPE), sin2, z(32)], axis=1) * scale
    cos_k = jnp.concatenate([z(MLA_NOPE), cos2, z(32)], axis=1)
    sin_k = jnp.concatenate([z(MLA_NOPE), sin2, z(32)], axis=1)
    return jnp.concatenate([cos_q, sin_q, cos_k, sin_k], axis=1)


def _swap_halves(w):
    half = w.shape[-1] // 2
    return jnp.concatenate([w[..., half:], w[..., :half]], axis=-1)


def _layer_params(i, seq_len, attn_norm, w_in, b_forget, sinks, mla_q_norm, mla_kv_norm, mla_w_uq,
                  mla_w_ukv, w_out, ffn_norm, w_group, b_group, w_router, b_router, w_gate, w_up, w_down):
    d = w_in.shape[1]
    w = w_in[i]
    sizes = (256, 128, 128, 256, 256, 256, 4, 256, 128, 32, 256, 256, 256)
    offs = np.concatenate([[0], np.cumsum(sizes)])
    (a_q, a_k, a_v, f_q, f_k, f_v, f_g, c_q, c_kv, c_kr, s_q, s_k, s_v) = [
        w[:, offs[j]:offs[j + 1]] for j in range(len(sizes))]
    qscale = HEAD_DIM ** -0.5
    grp = SWA_HEADS // SWA_KV_HEADS
    a_q = a_q.reshape(d, SWA_KV_HEADS, grp, HEAD_DIM).transpose(0, 2, 1, 3).reshape(d, 256)
    z = lambda n: jnp.zeros((d, n), F32)
    g_grp = jnp.concatenate([f_g[:, 0:2], z(62), c_kr, z(32)], axis=1)
    gs_grp = jnp.concatenate([f_g[:, 2:4], z(62), _swap_halves(c_kr), z(32)], axis=1)
    w_perm = jnp.concatenate([a_q * qscale, a_k, a_v, f_q * qscale, f_k, f_v, s_q * qscale, s_k, s_v,
                              c_q, c_kv, g_grp, gs_grp], axis=1).astype(BF16)
    wuq = mla_w_uq[i].reshape(MLA_Q_LORA, 4, MLA_NOPE + MLA_ROPE)
    zq = lambda n: jnp.zeros((MLA_Q_LORA, 4, n), F32)
    w_uq_a = jnp.concatenate([wuq, zq(32)], axis=2).reshape(MLA_Q_LORA, 512).astype(BF16)
    w_uq_b = jnp.concatenate([zq(MLA_NOPE), _swap_halves(wuq[:, :, MLA_NOPE:]), zq(32)],
                             axis=2).reshape(MLA_Q_LORA, 512).astype(BF16)
    wukv = mla_w_ukv[i].reshape(MLA_KV_LORA, 4, MLA_NOPE + MLA_V)
    w_kv_k = jnp.concatenate([wukv[:, :, :MLA_NOPE], jnp.zeros((MLA_KV_LORA, 4, 64), F32)],
                             axis=2).reshape(MLA_KV_LORA, 512).astype(BF16)
    w_kv_v = wukv[:, :, MLA_NOPE:].reshape(MLA_KV_LORA, 256).astype(BF16)
    bf = b_forget[i].astype(F32)
    b_f = jnp.zeros((1, 256), F32).at[0, 0:2].set(bf[0:2]).at[0, 128:130].set(bf[2:4])
    wo = w_out[i]
    wo_a = wo[:256].reshape(SWA_KV_HEADS, grp, HEAD_DIM, d).transpose(1, 0, 2, 3).reshape(256, d)
    wo4 = jnp.concatenate([wo_a, wo[256:]], axis=0).reshape(4, 256, d).astype(BF16)
    w_r = jnp.concatenate([w_group[i], w_router[i], jnp.zeros((d, LANES - N_GROUPS - N_EXPERTS), F32)], axis=1)
    w_r_hi = w_r.astype(BF16)
    w_r_lo = (w_r - w_r_hi.astype(F32)).astype(BF16)
    b_r = jnp.concatenate([b_group[i], b_router[i], jnp.zeros((LANES - N_GROUPS - N_EXPERTS,), F32)])[None, :]
    return dict(
        attn_norm=attn_norm[i][None, :], w_in=w_perm, q_norm=mla_q_norm[i][None, :],
        kv_norm=mla_kv_norm[i][None, :], w_uq_a=w_uq_a, w_uq_b=w_uq_b, w_kv_k=w_kv_k, w_kv_v=w_kv_v,
        rope_tab=_rope_table(seq_len), b_forget=b_f, sinks=sinks[i].astype(F32), w_out=wo4,
        ffn_norm=ffn_norm[i][None, :], w_r_hi=w_r_hi, w_r_lo=w_r_lo, b_r=b_r.astype(F32),
        w_gate=w_gate, w_up=w_up, w_down=w_down, layer=i)


def kernel(x, meta_tokens, attn_norm, w_in, b_forget, sinks, mla_q_norm, mla_kv_norm, mla_w_uq, mla_w_ukv,
           w_out, ffn_norm, w_group, b_group, w_router, b_router, w_gate, w_up, w_down, final_norm):
    batch, seq, d = x.shape
    seq_len = seq + BLOCK
    assert seq_len % ROW_TILE == 0 and seq_len % (BLOCK * SWA_Q_BLOCKS) == 0
    assert all(seq % tile == 0 for tile in Q_TILES.values())
    t = batch * seq_len
    depth = w_in.shape[0]
    lead = jnp.concatenate([jnp.zeros((PAD_FRONT, d), x.dtype), meta_tokens.astype(x.dtype)], axis=0)
    h = (lead, x)
    y2 = y_ids = ew = None
    for i in range(depth):
        p = _layer_params(i, seq_len, attn_norm, w_in, b_forget, sinks, mla_q_norm, mla_kv_norm, mla_w_uq,
                          mla_w_ukv, w_out, ffn_norm, w_group, b_group, w_router, b_router, w_gate, w_up, w_down)
        h, (qa, fq, fk, fv, cq, ck, cv, sq, sk, sv) = _inproj(h, y2, y_ids, ew, p, seq_len)
        y_a = _swa_attention(qa, p["sinks"], batch, seq_len)
        y_b = _causal_attention("fox", fq, fk, fv, batch, seq_len)
        y_c = _causal_attention("mla", cq, ck, cv, batch, seq_len)
        y_d = _causal_attention("sb", sq, sk, sv, batch, seq_len)
        ys = [y.reshape(t, 256) for y in (y_a, y_b, y_c, y_d)]
        h, xn, ew, counts = _outproj(ys, h, p)
        block_e, nvalid, y_ids = _route(ew, counts[0, N_GROUPS:N_GROUPS + N_EXPERTS], t)
        y2 = _moe(_dispatch(xn, y_ids, nvalid), block_e, nvalid, p)
    return _final(h, y2, y_ids, ew, final_norm[None, :], batch, seq_len)
```

```python
import functools

import jax
import jax.numpy as jnp
import numpy as np
from jax import lax
from jax.experimental import pallas as pl
from jax.experimental.pallas import tpu as pltpu

F32 = jnp.float32
BF16 = jnp.bfloat16

BLOCK = 128
N_META = 16
PAD_FRONT = BLOCK - N_META
CHUNK_SHIFT = 6
HEAD_DIM = 64
NORM_EPS = 1e-6
NEG = -1e30
PAD_KEY_LOGIT = -(2.0 ** 100)
UNDERFLOW_LOG2 = -150.0
LOG2E = 1.4426950408889634
BIG = 1 << 30
SWA_HEADS, SWA_KV_HEADS, WINDOW = 4, 2, 128
MLA_Q_LORA, MLA_KV_LORA, MLA_NOPE, MLA_ROPE, MLA_V = 256, 128, 64, 32, 64
MLA_BIAS_LANE = MLA_NOPE + MLA_ROPE
ROPE_THETA = 10000.0
N_GROUPS, EXPERTS_PER_GROUP, TOP_K = 4, 8, 2
N_EXPERTS = N_GROUPS * EXPERTS_PER_GROUP
MOE_BLOCK = 512
LANES = 128
ROW_SUB = 8
ROW_TILE = 384
Q_TILES = {"fox": 1024, "mla": 1024, "sb": 512}
K_TILE = 256
ROW_PART = 128
SWA_Q_BLOCKS = 3
VMEM_LIMIT = 56 * 1024 * 1024

C_A, C_B, C_D, C_CQ, C_CKV, C_G, C_GS, C_END = 0, 512, 1280, 2048, 2304, 2432, 2560, 2688
B_F0, B_F1, B_PAD = 0, 3, 6


def _rms(x, g):
    return x * lax.rsqrt(jnp.mean(x * x, axis=-1, keepdims=True) + NORM_EPS) * g


def _log_sigmoid(x):
    return jnp.minimum(x, 0.0) - jnp.log(1.0 + jnp.exp(-jnp.abs(x)))


def _dot(a, b):
    return jnp.dot(a, b, preferred_element_type=F32)


def _dot_nt(a, b):
    return lax.dot_general(a, b, (((1,), (1,)), ((), ())), preferred_element_type=F32)


def _rows_from_tiles(ref, lead, n):
    return jnp.concatenate([ref[(*lead, pl.ds(j, n, stride=ROW_SUB), slice(None))] for j in range(ROW_SUB)], axis=1)


def _rows_to_tiles(ref, lead, x):
    n = x.shape[0]
    for j in range(ROW_SUB):
        ref[(*lead, pl.ds(j, n, stride=ROW_SUB), slice(None))] = x[:, j * LANES:(j + 1) * LANES]


def _tile4(x):
    return jnp.concatenate([x, x, x, x], axis=1)


def _split3(x):
    hi = x.astype(BF16)
    r1 = x - hi.astype(F32)
    mid = r1.astype(BF16)
    lo = (r1 - mid.astype(F32)).astype(BF16)
    return hi, mid, lo


def _free_base(head):
    return head * LANES + (HEAD_DIM if head % 2 == 0 else 0)


def _expert_rows_start(y_hbm, ids_ref, buf, sem, slot):
    for j in range(ids_ref.shape[2]):
        tok, k = divmod(j, TOP_K)
        src = pl.multiple_of(ids_ref[0, 0, j], ROW_SUB)
        pltpu.make_async_copy(y_hbm.at[pl.ds(src, ROW_SUB)], buf.at[slot, k, pl.ds(tok * ROW_SUB, ROW_SUB)],
                              sem.at[slot]).start(priority=j % 2)


def _expert_rows_wait(y_hbm, buf, sem, slot):
    for k in range(TOP_K):
        pltpu.make_async_copy(y_hbm.at[pl.ds(0, buf.shape[2])], buf.at[slot, k], sem.at[slot]).wait()


def _combine_experts(h, ew, y_hbm, ids0_ref, idsn_ref, buf, sem, step, n_steps):
    slot = step % 2

    @pl.when(step == 0)
    def _():
        _expert_rows_start(y_hbm, ids0_ref, buf, sem, 0)

    _expert_rows_start(y_hbm, idsn_ref, buf, sem, 1 - slot)
    _expert_rows_wait(y_hbm, buf, sem, slot)
    n = h.shape[0]
    out = h + ew[:, 0:1] * _rows_from_tiles(buf, (slot, 0), n) + ew[:, 1:2] * _rows_from_tiles(buf, (slot, 1), n)

    @pl.when(step == n_steps - 1)
    def _():
        _expert_rows_wait(y_hbm, buf, sem, 1 - slot)

    return out


def _inproj_kernel(has_y2, n_seq_tiles, *refs):
    if has_y2:
        (h_ref, ids0_ref, idsn_ref, y_hbm, ew_ref, *rest) = refs
    else:
        (lead_ref, *x_refs) = refs[:1 + ROW_TILE // BLOCK]
        rest = refs[1 + ROW_TILE // BLOCK:]
    (g_ref, w_ref, qn_ref, kvn_ref, wuqa_ref, wuqb_ref, wkvk_ref, wkvv_ref, tab_ref, bf_ref,
     pq_ref, pk_ref, rows_ref, hout_ref, *outs) = rest
    if has_y2:
        *outs, ybuf, ysem = outs
    (qa_ref, fq_ref, fk_ref, fv_ref, cq_ref, ck_ref, cv_ref, sq_ref, sk_ref, sv_ref, carry_ref) = outs
    tile = pl.program_id(0) % n_seq_tiles
    if has_y2:
        h = _combine_experts(h_ref[...], ew_ref[...], y_hbm, ids0_ref, idsn_ref, ybuf, ysem,
                             pl.program_id(0), pl.num_programs(0))
    else:
        first = jnp.where(tile == 0, lead_ref[...], x_refs[0][0])
        h = jnp.concatenate([first] + [r[0] for r in x_refs[1:]], axis=0)
    hout_ref[...] = h
    tm, d = h.shape
    xn = _rms(h, g_ref[...]).astype(BF16)
    acc = _dot(xn, w_ref[...])
    lane = lax.broadcasted_iota(jnp.int32, (1, LANES), 1)
    lo_half = lane < HEAD_DIM
    pad_col = jnp.where(tile * tm + lax.broadcasted_iota(jnp.int32, (tm, 1), 0) < PAD_FRONT,
                        PAD_KEY_LOGIT, 0.0)
    rows = rows_ref[...]
    fq_one, fk_one, pad_lane, sq_one, mla_one, mla_pad = (rows[j:j + 1] for j in range(6))

    def per_head(x_pair, bias, pair, scale=None):
        x = x_pair if scale is None else x_pair * scale
        even = jnp.where(lo_half, x, bias[:, (2 * pair) * LANES:(2 * pair + 1) * LANES])
        odd = jnp.where(lo_half, bias[:, (2 * pair + 1) * LANES:(2 * pair + 2) * LANES], x)
        return even, odd

    def store_heads(ref, x_off, bias, scale=None):
        for pair in range(2):
            x_pair = acc[:, x_off + pair * LANES:x_off + (pair + 1) * LANES]
            even, odd = per_head(x_pair, bias, pair, scale)
            ref[:, (2 * pair) * LANES:(2 * pair + 1) * LANES] = even.astype(BF16)
            ref[:, (2 * pair + 1) * LANES:(2 * pair + 2) * LANES] = odd.astype(BF16)

    qa_ref[...] = acc[:, C_A:C_B].astype(BF16)

    @pl.when(tile == 0)
    def _():
        carry_ref[...] = jnp.zeros_like(carry_ref)

    lf = _log_sigmoid(acc[:, C_G:C_END] + bf_ref[...]) * LOG2E
    r = lax.broadcasted_iota(jnp.int32, (BLOCK, BLOCK), 0)
    c = lax.broadcasted_iota(jnp.int32, (BLOCK, BLOCK), 1)
    tri = jnp.where(c <= r, 1.0, 0.0).astype(BF16)
    carry = carry_ref[...]
    blocks = []
    for b in range(tm // BLOCK):
        hi, mid, lo = _split3(lf[b * BLOCK:(b + 1) * BLOCK])
        y = _dot(tri, hi) + _dot(tri, mid) + _dot(tri, lo) + carry
        carry = y[BLOCK - 1:BLOCK, :]
        blocks.append(y)
    carry_ref[...] = carry
    f_hi, f_mid, f_lo = _split3(jnp.concatenate(blocks, axis=0))
    q_bias = _dot(f_hi, pq_ref[0]) + _dot(f_mid, pq_ref[1]) + _dot(f_lo, pq_ref[2]) + fq_one
    k_bias = (_dot(f_hi, pk_ref[0]) + _dot(f_mid, pk_ref[1]) + _dot(f_lo, pk_ref[2]) + fk_one
              + pad_col * pad_lane)
    ones = jnp.ones((tm, 4 * LANES), F32)
    store_heads(fq_ref, C_B, q_bias, LOG2E)
    store_heads(fk_ref, C_B + 256, k_bias)
    store_heads(fv_ref, C_B + 512, ones)

    store_heads(sq_ref, C_D, jnp.broadcast_to(sq_one, (tm, 4 * LANES)), LOG2E)
    store_heads(sk_ref, C_D + 256, pad_col * pad_lane)
    sv_ref[...] = acc[:, C_D + 512:C_CQ].astype(BF16)

    tab = tab_ref[...]
    cos_q, sin_q = tab[:, 0:128], tab[:, 128:256]
    cos_k, sin_k = tab[:, 256:384], tab[:, 384:512]
    cqn = _rms(acc[:, C_CQ:C_CKV], qn_ref[...]).astype(BF16)
    q_lin = _dot(cqn, wuqa_ref[...])
    q_swp = _dot(cqn, wuqb_ref[...])
    cq_ref[...] = (q_lin * _tile4(cos_q) + q_swp * _tile4(sin_q) + mla_one).astype(BF16)
    ckvn = _rms(acc[:, C_CKV:C_G], kvn_ref[...]).astype(BF16)
    k_nope = _dot(ckvn, wkvk_ref[...])
    grp, grp_s = acc[:, C_G:C_GS], acc[:, C_GS:C_END]
    k_rope = grp * cos_k + grp_s * sin_k
    ck_ref[...] = (k_nope + _tile4(k_rope) + pad_col * mla_pad).astype(BF16)
    vv = _dot(ckvn, wkvv_ref[...])
    for pair in range(2):
        even, odd = per_head(vv[:, pair * LANES:(pair + 1) * LANES], ones, pair)
        cv_ref[:, (2 * pair) * LANES:(2 * pair + 1) * LANES] = even.astype(BF16)
        cv_ref[:, (2 * pair + 1) * LANES:(2 * pair + 2) * LANES] = odd.astype(BF16)


def _bias_constants():
    src = (0, 1, LANES, LANES + 1)
    pq = np.zeros((3, 2 * LANES, 4 * LANES), np.float32)
    pk = np.zeros((3, 2 * LANES, 4 * LANES), np.float32)
    rows = np.zeros((8, 4 * LANES), np.float32)
    for head in range(4):
        base = _free_base(head)
        for part in range(3):
            pq[part, src[head], base + B_F0 + part] = 1.0
            pk[part, src[head], base + B_F1 + part] = -1.0
            rows[0, base + B_F1 + part] = 1.0
            rows[1, base + B_F0 + part] = 1.0
        rows[0, base + B_PAD] = 1.0
        rows[2, base + B_PAD] = 1.0
        rows[3, base + B_PAD] = 1.0
        rows[4, head * LANES + MLA_BIAS_LANE] = 1.0
        rows[5, head * LANES + MLA_BIAS_LANE] = 1.0
    return jnp.asarray(pq, BF16), jnp.asarray(pk, BF16), jnp.asarray(rows, F32)


def _inproj(h, y2, y_ids, ew, p, seq_len):
    tm = ROW_TILE
    n_seq_tiles = seq_len // tm
    has_y2 = y2 is not None
    row = lambda i: (i, 0)
    fixed = lambda i: (0, 0)
    if has_y2:
        t, d = h.shape
        in_specs = [pl.BlockSpec((tm, d), row)]
        args = [h]
    else:
        lead, x = h
        batch, seq, d = x.shape
        t = batch * seq_len
        per_tile = tm // BLOCK
        frame = lambda k: pl.BlockSpec((1, BLOCK, d), lambda i: (
            i // n_seq_tiles, jnp.maximum((i % n_seq_tiles) * per_tile - 1 + k, 0), 0))
        in_specs = [pl.BlockSpec((BLOCK, d), fixed)] + [frame(k) for k in range(per_tile)]
        args = [lead] + [x] * per_tile
    if has_y2:
        ids = y_ids.reshape(t // tm, 1, TOP_K * tm)
        ids = jnp.concatenate([ids, jnp.zeros_like(ids[:1])], axis=0)
        in_specs += [pl.BlockSpec((1, 1, TOP_K * tm), lambda i: (0, 0, 0), memory_space=pltpu.SMEM),
                     pl.BlockSpec((1, 1, TOP_K * tm), lambda i: (i + 1, 0, 0), memory_space=pltpu.SMEM),
                     pl.BlockSpec(memory_space=pl.ANY),
                     pl.BlockSpec((tm, LANES), row)]
        args += [ids, ids, y2, ew]
    pq, pk, rows = _bias_constants()
    consts = [p["attn_norm"], p["w_in"], p["q_norm"], p["kv_norm"], p["w_uq_a"], p["w_uq_b"],
              p["w_kv_k"], p["w_kv_v"]]
    in_specs += [pl.BlockSpec(c.shape, fixed) for c in consts]
    args += consts
    in_specs.append(pl.BlockSpec((tm, 512), lambda i: (i % n_seq_tiles, 0)))
    args.append(p["rope_tab"])
    in_specs += [pl.BlockSpec((1, 256), fixed),
                 pl.BlockSpec(pq.shape, lambda i: (0, 0, 0)),
                 pl.BlockSpec(pk.shape, lambda i: (0, 0, 0)),
                 pl.BlockSpec(rows.shape, fixed)]
    args += [p["b_forget"], pq, pk, rows]
    widths = [512] * 9 + [256]
    out_shape = [jax.ShapeDtypeStruct((t, d), F32)] + [jax.ShapeDtypeStruct((t, w), BF16) for w in widths]
    out_specs = [pl.BlockSpec((tm, d), row)] + [pl.BlockSpec((tm, w), row) for w in widths]
    outs = pl.pallas_call(
        functools.partial(_inproj_kernel, has_y2, n_seq_tiles),
        grid=(t // tm,),
        in_specs=in_specs,
        out_specs=out_specs,
        out_shape=out_shape,
        scratch_shapes=[pltpu.VMEM((1, 2 * LANES), F32)] + (
            [pltpu.VMEM((2, TOP_K, tm * ROW_SUB, LANES), F32), pltpu.SemaphoreType.DMA((2,))] if has_y2 else []),
        compiler_params=pltpu.CompilerParams(dimension_semantics=("arbitrary",),
                                             vmem_limit_bytes=VMEM_LIMIT),
        name="inproj_y2" if has_y2 else "inproj",
    )(*args)
    return outs[0], outs[1:]


def _swa_kernel(sink_ref, q_ref, km_ref, kp_ref, kc_ref, vm_ref, vp_ref, vc_ref, o_ref):
    i = pl.program_id(1)
    n_sub = q_ref.shape[1] // BLOCK
    lane = lax.broadcasted_iota(jnp.int32, (1, LANES), 1)
    lo_half = lane < HEAD_DIM
    half_masks = [jnp.where(lo_half, 1.0, 0.0).astype(BF16), jnp.where(lo_half, 0.0, 1.0).astype(BF16)]
    row = lax.broadcasted_iota(jnp.int32, (BLOCK, 1), 0)
    col = lax.broadcasted_iota(jnp.int32, (1, BLOCK), 1)
    grp = SWA_HEADS // SWA_KV_HEADS
    k_all = jnp.concatenate([km_ref[0], kp_ref[0], kc_ref[0]], axis=0)
    v_all = jnp.concatenate([vm_ref[0], vp_ref[0], vc_ref[0]], axis=0)
    for j in range(n_sub):
        q0 = (i * n_sub + j) * BLOCK
        pq = q0 + row
        cq = pq >> CHUNK_SHIFT
        segs = []
        vis_m = col >= PAD_FRONT
        segs.append((vis_m, jnp.minimum(jnp.abs(pq - col), WINDOW).astype(F32)))
        for pk in (q0 - BLOCK + col, q0 + col):
            ck = jnp.where(pk >= BLOCK, pk >> CHUNK_SHIFT, BIG)
            vis = (ck <= cq) & (ck >= cq - (WINDOW >> CHUNK_SHIFT))
            segs.append((vis, jnp.abs(pq - pk).astype(F32)))
        kj = jnp.concatenate([k_all[0:BLOCK], k_all[(j + 1) * BLOCK:(j + 3) * BLOCK]], axis=0)
        vj = jnp.concatenate([v_all[0:BLOCK], v_all[(j + 1) * BLOCK:(j + 3) * BLOCK]], axis=0)
        for g in range(grp):
            qg = q_ref[0, j * BLOCK:(j + 1) * BLOCK, g * LANES:(g + 1) * LANES]
            out_g = None
            for hk in range(SWA_KV_HEADS):
                head = hk * grp + g
                slope = 2.0 ** (-8.0 * (head + 1) / SWA_HEADS)
                sink = sink_ref[head]
                s_all = _dot_nt(qg * half_masks[hk], kj)
                tiles = [jnp.where(vis, s_all[:, n * LANES:(n + 1) * LANES] - slope * dist, NEG)
                         for n, (vis, dist) in enumerate(segs)]
                top = jnp.maximum(jnp.maximum(tiles[0], tiles[1]), tiles[2])
                m = jnp.broadcast_to(jnp.maximum(jnp.max(top, axis=-1, keepdims=True), sink), (BLOCK, LANES))
                e = [jnp.exp(x - m) for x in tiles]
                den = jnp.sum(e[0] + e[1] + e[2], axis=-1, keepdims=True) + jnp.exp(sink - m[:, 0:1])
                o = _dot(jnp.concatenate(e, axis=1).astype(BF16), vj) * (1.0 / den)
                out_g = o if hk == 0 else jnp.where(lo_half, out_g, o)
            o_ref[0, j * BLOCK:(j + 1) * BLOCK, g * LANES:(g + 1) * LANES] = out_g.astype(BF16)


def _swa_attention(qa, sinks, batch, seq_len):
    x = qa.reshape(batch, seq_len, 512)
    n_sub = SWA_Q_BLOCKS
    nb = seq_len // (BLOCK * n_sub)
    blk = lambda f: pl.BlockSpec((1, BLOCK, LANES), f)
    own = lambda c: pl.BlockSpec((1, BLOCK * n_sub, LANES), lambda b, i: (b, i, c))
    return pl.pallas_call(
        _swa_kernel,
        grid=(batch, nb),
        in_specs=[
            pl.BlockSpec(memory_space=pltpu.SMEM),
            pl.BlockSpec((1, BLOCK * n_sub, 2 * LANES), lambda b, i: (b, i, 0)),
            blk(lambda b, i: (b, 0, 2)),
            blk(lambda b, i: (b, jnp.maximum(i * n_sub - 1, 0), 2)),
            own(2),
            blk(lambda b, i: (b, 0, 3)),
            blk(lambda b, i: (b, jnp.maximum(i * n_sub - 1, 0), 3)),
            own(3),
        ],
        out_specs=pl.BlockSpec((1, BLOCK * n_sub, 2 * LANES), lambda b, i: (b, i, 0)),
        out_shape=jax.ShapeDtypeStruct((batch, seq_len, 2 * LANES), BF16),
        compiler_params=pltpu.CompilerParams(dimension_semantics=("arbitrary", "arbitrary"),
                                             vmem_limit_bytes=VMEM_LIMIT),
        name="swa_attention",
    )(sinks, x, x, x, x, x, x, x)


def _causal_kernel(mode, q_ref, k_ref, v_ref, o_ref, stat_ref, acc_ref):
    seq_len = q_ref.shape[1]
    Q_TILE = stat_ref.shape[1]
    n_qt = (seq_len - BLOCK) // Q_TILE
    per_tile = Q_TILE // K_TILE
    lane = lax.broadcasted_iota(jnp.int32, (1, LANES), 1)
    lo_half = lane < HEAD_DIM
    if mode == "sb":
        r = lax.broadcasted_iota(jnp.int32, (K_TILE, K_TILE), 0)
        c = lax.broadcasted_iota(jnp.int32, (K_TILE, K_TILE), 1)
        later2 = jnp.where(r > c, 1.0, 0.0).astype(BF16)
        later1 = later2[:BLOCK, :BLOCK]

    def causal(pq, k0, tk):
        pk = k0 + lax.broadcasted_iota(jnp.int32, (1, tk), 1)
        if mode == "fox":
            return pk <= pq
        if mode == "mla":
            return (pk >> CHUNK_SHIFT) <= (pq >> CHUNK_SHIFT)
        return pk < pq

    def head_v(k0, tk, hh):
        if mode == "sb":
            return v_ref[0, pl.ds(k0, tk), :]
        return v_ref[0, pl.ds(k0, tk), hh * LANES:(hh + 1) * LANES]

    def lane_tiles(x):
        return [x[:, j * LANES:(j + 1) * LANES] for j in range(x.shape[1] // LANES)]

    def row_parts(lo, hi):
        step = min(hi - lo, ROW_PART)
        return [(r0, step) for r0 in range(lo, hi, step)]

    def softmax_chunk(q0, tq, k0, tk, masked, first, row_lo=0):
        for hh in range(2):
            qh = q_ref[0, pl.ds(q0 + row_lo, tq - row_lo), hh * LANES:(hh + 1) * LANES]
            kh = k_ref[0, pl.ds(k0, tk), hh * LANES:(hh + 1) * LANES]
            s_all = _dot_nt(qh, kh)
            vh = head_v(k0, tk, hh)
            for r0, tr in row_parts(row_lo, tq):
                s = s_all[r0 - row_lo:r0 - row_lo + tr]
                if masked and r0 < row_lo + tk:
                    pq = q0 + r0 + lax.broadcasted_iota(jnp.int32, (tr, 1), 0)
                    s = jnp.where(causal(pq, k0, tk), s, NEG)
                tiles = lane_tiles(s)
                top = tiles[0]
                for x in tiles[1:]:
                    top = jnp.maximum(top, x)
                m_new = jnp.broadcast_to(jnp.max(top, axis=-1, keepdims=True), (tr, LANES))
                if not first:
                    m_old = stat_ref[hh, r0:r0 + tr, :]
                    m_new = jnp.maximum(m_old, m_new)
                p = jnp.concatenate([jnp.exp2(x - m_new) for x in tiles], axis=1).astype(BF16)
                pv = _dot(p, vh)
                if not first:
                    pv = jnp.exp2(m_old - m_new) * acc_ref[hh, r0:r0 + tr, :] + pv
                stat_ref[hh, r0:r0 + tr, :] = m_new
                acc_ref[hh, r0:r0 + tr, :] = pv

    def stick_chunk(q0, rows, k0, tk, masked, first):
        lo_r, hi_r = rows
        tq = hi_r - lo_r
        later = later2 if tk == K_TILE else later1
        for hh in range(2):
            qh = q_ref[0, pl.ds(q0 + lo_r, tq), hh * LANES:(hh + 1) * LANES]
            kh = k_ref[0, pl.ds(k0, tk), hh * LANES:(hh + 1) * LANES]
            z = _dot_nt(qh, kh)
            ls_pos = jnp.minimum(z, 0.0) - jnp.log(1.0 + jnp.exp2(-jnp.abs(z))) * LOG2E
            log_keep = ls_pos - z
            if masked:
                pq = q0 + lo_r + lax.broadcasted_iota(jnp.int32, (tq, 1), 0)
                vis = causal(pq, k0, tk)
                log_keep = jnp.where(vis, log_keep, 0.0)
            after = _dot(log_keep.astype(BF16), later)
            tot = ls_pos + after
            chunk_total = jnp.broadcast_to(after[:, 0:1] + log_keep[:, 0:1], (tq, LANES))
            if not first:
                carry = stat_ref[hh, lo_r:hi_r, :]
                tot = jnp.concatenate([x + carry for x in lane_tiles(tot)], axis=1)
                chunk_total = carry + chunk_total
            a = jnp.exp2(tot)
            if masked:
                a = jnp.where(vis, a, 0.0)
            pv = _dot(a.astype(BF16), head_v(k0, tk, hh))
            if not first:
                pv = acc_ref[hh, lo_r:hi_r, :] + pv
            stat_ref[hh, lo_r:hi_r, :] = chunk_total
            acc_ref[hh, lo_r:hi_r, :] = pv

    def finish(q0, tq):
        a0, a1 = acc_ref[0, 0:tq, :], acc_ref[1, 0:tq, :]
        if mode != "sb":
            a0 = a0 / a0[:, HEAD_DIM:HEAD_DIM + 1]
            a1 = a1 / a1[:, 0:1]
        o_ref[0, pl.ds(q0, tq), :] = jnp.where(lo_half, a0, a1).astype(BF16)

    def chunk_start(j):
        return pl.multiple_of(BLOCK + j * K_TILE, BLOCK)

    if mode == "sb":
        stick_chunk(0, (0, BLOCK), 0, BLOCK, True, True)
        finish(0, BLOCK)
        whole = (0, Q_TILE)

        def q_body(i, _):
            q0 = pl.multiple_of(BLOCK + i * Q_TILE, BLOCK)
            n_int = i * per_tile
            for d in reversed(range(per_tile)):
                stick_chunk(q0, whole, chunk_start(n_int + d), K_TILE, True, d == per_tile - 1)

            def alive():
                top = jnp.maximum(jnp.max(stat_ref[0]), jnp.max(stat_ref[1]))
                return (top > UNDERFLOW_LOG2).astype(jnp.int32)

            def body(st):
                jj, _ = st
                stick_chunk(q0, whole, chunk_start(n_int - 1 - jj), K_TILE, False, False)
                return jj + 1, alive()

            _, go = lax.while_loop(lambda st: (st[0] < n_int) & (st[1] > 0), body, (0, alive()))

            @pl.when(go > 0)
            def _():
                stick_chunk(q0, whole, 0, BLOCK, False, False)

            finish(q0, Q_TILE)
            return 0
    else:
        softmax_chunk(0, BLOCK, 0, BLOCK, True, True)
        finish(0, BLOCK)

        def q_body(i, _):
            q0 = pl.multiple_of(BLOCK + i * Q_TILE, BLOCK)
            n_int = i * per_tile
            softmax_chunk(q0, Q_TILE, 0, BLOCK, False, True)

            def body(j, _):
                for d in range(per_tile):
                    softmax_chunk(q0, Q_TILE, chunk_start(j * per_tile + d), K_TILE, False, False)
                return 0

            lax.fori_loop(0, i, body, 0)
            for d in range(per_tile):
                softmax_chunk(q0, Q_TILE, chunk_start(n_int + d), K_TILE, True, False, row_lo=d * K_TILE)
            finish(q0, Q_TILE)
            return 0

    lax.fori_loop(0, n_qt, q_body, 0)


def _causal_attention(mode, q, k, v, batch, seq_len):
    wide = pl.BlockSpec((1, seq_len, 2 * LANES), lambda b, p: (b, 0, p))
    narrow = pl.BlockSpec((1, seq_len, LANES), lambda b, p: (b, 0, p))
    args = [q.reshape(batch, seq_len, 512), k.reshape(batch, seq_len, 512),
            v.reshape(batch, seq_len, v.shape[1])]
    return pl.pallas_call(
        functools.partial(_causal_kernel, mode),
        grid=(batch, 2),
        in_specs=[wide, wide, narrow if mode == "sb" else wide],
        out_specs=narrow,
        out_shape=jax.ShapeDtypeStruct((batch, seq_len, 2 * LANES), BF16),
        scratch_shapes=[pltpu.VMEM((2, Q_TILES[mode], LANES), F32), pltpu.VMEM((2, Q_TILES[mode], LANES), F32)],
        compiler_params=pltpu.CompilerParams(dimension_semantics=("arbitrary", "arbitrary"),
                                             vmem_limit_bytes=VMEM_LIMIT),
        name=mode + "_attention",
    )(*args)


def _outproj_kernel(ya_ref, yb_ref, yc_ref, yd_ref, h_ref, wo_ref, g_ref, wrh_ref, wrl_ref, br_ref, tri_ref,
                    h2_ref, xn_ref, route_ref, cnt_ref):
    o = (_dot(ya_ref[...], wo_ref[0]) + _dot(yb_ref[...], wo_ref[1])
         + _dot(yc_ref[...], wo_ref[2]) + _dot(yd_ref[...], wo_ref[3]))
    h2 = h_ref[...] + o
    h2_ref[...] = h2
    xn = _rms(h2, g_ref[...])
    _rows_to_tiles(xn_ref, (), xn)
    xh = xn.astype(BF16)
    xl = (xn - xh.astype(F32)).astype(BF16)
    wrh, wrl = wrh_ref[...], wrl_ref[...]
    lg = _dot(xh, wrh) + _dot(xl, wrh) + _dot(xh, wrl) + br_ref[...]

    tm = lg.shape[0]
    lane = lax.broadcasted_iota(jnp.int32, (tm, LANES), 1)
    ninf = -jnp.inf

    def first_max(x):
        top = jnp.max(x, axis=-1, keepdims=True)
        return top, jnp.min(jnp.where(x == top, lane, LANES), axis=-1, keepdims=True)

    gl = jnp.where(lane < N_GROUPS, lg, ninf)
    g_max, g_top = first_max(gl)
    g_w = 1.0 / jnp.sum(jnp.exp(gl - g_max), axis=-1, keepdims=True)
    e_lo = N_GROUPS + g_top * EXPERTS_PER_GROUP
    el = jnp.where((lane >= e_lo) & (lane < e_lo + EXPERTS_PER_GROUP), lg, ninf)
    v1, i1 = first_max(el)
    v2, i2 = first_max(jnp.where(lane == i1, ninf, el))
    r21 = jnp.exp(v2 - v1)
    w1 = g_w / (1.0 + r21)
    w2 = w1 * r21

    @pl.when(pl.program_id(0) == 0)
    def _():
        cnt_ref[...] = jnp.zeros_like(cnt_ref)

    m1 = jnp.where(lane == i1, 1.0, 0.0)
    m2 = jnp.where(lane == i2, 1.0, 0.0)
    both = m1 + m2
    before = _dot(tri_ref[...], both.astype(BF16)) + cnt_ref[0:1, :]
    rank1 = jnp.sum(m1 * before, axis=-1, keepdims=True)
    rank2 = jnp.sum(m2 * before, axis=-1, keepdims=True)
    cnt_ref[...] = cnt_ref[...] + jnp.sum(both, axis=0, keepdims=True)
    cols = [w1, w2, (i1 - N_GROUPS).astype(F32), (i2 - N_GROUPS).astype(F32), rank1, rank2]
    route = jnp.zeros((tm, LANES), F32)
    for j, c in enumerate(cols):
        route = jnp.where(lane == j, c, route)
    route_ref[...] = route


def _outproj(ys, h, p):
    t, d = h.shape
    tm = ROW_TILE
    row = lambda i: (i, 0)
    fixed2 = lambda i: (0, 0)
    in_specs = [pl.BlockSpec((tm, 256), row)] * 4 + [
        pl.BlockSpec((tm, d), row),
        pl.BlockSpec((4, 256, d), lambda i: (0, 0, 0)),
        pl.BlockSpec((1, d), fixed2),
        pl.BlockSpec((d, LANES), fixed2),
        pl.BlockSpec((d, LANES), fixed2),
        pl.BlockSpec((1, LANES), fixed2),
        pl.BlockSpec((tm, tm), fixed2),
    ]
    earlier = jnp.asarray(np.tril(np.ones((tm, tm), np.float32), -1), BF16)
    return pl.pallas_call(
        _outproj_kernel,
        grid=(t // tm,),
        in_specs=in_specs,
        out_specs=[pl.BlockSpec((tm, d), row), pl.BlockSpec((tm * ROW_SUB, LANES), row),
                   pl.BlockSpec((tm, LANES), row), pl.BlockSpec((8, LANES), fixed2)],
        out_shape=[jax.ShapeDtypeStruct((t, d), F32), jax.ShapeDtypeStruct((t * ROW_SUB, LANES), F32),
                   jax.ShapeDtypeStruct((t, LANES), F32), jax.ShapeDtypeStruct((8, LANES), F32)],
        compiler_params=pltpu.CompilerParams(dimension_semantics=("arbitrary",),
                                             vmem_limit_bytes=VMEM_LIMIT),
        name="outproj_router",
    )(*ys, h, p["w_out"], p["ffn_norm"], p["w_r_hi"], p["w_r_lo"], p["b_r"], earlier)


def _dispatch_kernel(nv_ref, ids_ref, x_ref, xs_hbm, zero_buf, sem):
    blk = zero_buf.shape[0]

    @pl.when(pl.program_id(0) == 0)
    def _():
        zero_buf[...] = jnp.zeros_like(zero_buf)
        for phase in ("start", "wait"):
            for b in range(nv_ref.shape[0]):
                @pl.when(nv_ref[b] < MOE_BLOCK)
                def _():
                    copy = pltpu.make_async_copy(zero_buf, xs_hbm.at[pl.ds(b * blk, blk)], sem)
                    copy.start() if phase == "start" else copy.wait()

    for j in range(ids_ref.shape[2]):
        tok, _ = divmod(j, TOP_K)
        dst = pl.multiple_of(ids_ref[0, 0, j], ROW_SUB)
        pltpu.make_async_copy(x_ref.at[pl.ds(tok * ROW_SUB, ROW_SUB)], xs_hbm.at[pl.ds(dst, ROW_SUB)],
                              sem).start(priority=j % 2)
    for _ in range(TOP_K):
        pltpu.make_async_copy(x_ref, xs_hbm.at[pl.ds(0, x_ref.shape[0])], sem).wait()


def _dispatch(xn, y_ids, nvalid):
    tm = ROW_TILE
    t = xn.shape[0] // ROW_SUB
    n_blk = nvalid.shape[0]
    return pl.pallas_call(
        _dispatch_kernel,
        grid_spec=pltpu.PrefetchScalarGridSpec(
            num_scalar_prefetch=1,
            grid=(t // tm,),
            in_specs=[pl.BlockSpec((1, 1, TOP_K * tm), lambda i, nv: (i, 0, 0), memory_space=pltpu.SMEM),
                      pl.BlockSpec((tm * ROW_SUB, LANES), lambda i, nv: (i, 0))],
            out_specs=pl.BlockSpec(memory_space=pl.ANY),
            scratch_shapes=[pltpu.VMEM((MOE_BLOCK * ROW_SUB, LANES), F32), pltpu.SemaphoreType.DMA],
        ),
        out_shape=jax.ShapeDtypeStruct((n_blk * MOE_BLOCK * ROW_SUB, LANES), F32),
        compiler_params=pltpu.CompilerParams(dimension_semantics=("arbitrary",),
                                             vmem_limit_bytes=VMEM_LIMIT),
        name="moe_dispatch",
    )(nvalid, y_ids.reshape(t // tm, 1, TOP_K * tm), xn)


def _moe_kernel(be_ref, nv_ref, x_ref, wg_ref, wu_ref, wd_ref, y_ref, wg_bf, wu_bf, wd_bf):
    s = pl.program_id(0)
    nv = nv_ref[s]

    @pl.when((s == 0) | (be_ref[s] != be_ref[jnp.maximum(s - 1, 0)]))
    def _():
        wg_bf[...] = wg_ref[0].astype(BF16)
        wu_bf[...] = wu_ref[0].astype(BF16)
        wd_bf[...] = wd_ref[0].astype(BF16)

    @pl.when(nv > 0)
    def _():
        x = _rows_from_tiles(x_ref, (), MOE_BLOCK).astype(BF16)
        gate = _dot(x, wg_bf[...])
        up = _dot(x, wu_bf[...])
        hid = (gate * (1.0 / (1.0 + jnp.exp(-gate))) * up).astype(BF16)
        _rows_to_tiles(y_ref, (), _dot(hid, wd_bf[...]))

    @pl.when(nv == 0)
    def _():
        y_ref[...] = jnp.zeros_like(y_ref)


def _moe(xs, block_e, nvalid, p):
    d = xs.shape[1] * ROW_SUB
    n_blk = block_e.shape[0]
    layer = p["layer"]
    hdim = p["w_gate"].shape[3]
    wspec = lambda shape: pl.BlockSpec((None, 1) + shape, lambda s, be, nv: (layer, be[s], 0, 0))
    rows = pl.BlockSpec((MOE_BLOCK * ROW_SUB, LANES), lambda s, be, nv: (s, 0))
    return pl.pallas_call(
        _moe_kernel,
        grid_spec=pltpu.PrefetchScalarGridSpec(
            num_scalar_prefetch=2,
            grid=(n_blk,),
            in_specs=[rows, wspec((d, hdim)), wspec((d, hdim)), wspec((hdim, d))],
            out_specs=rows,
            scratch_shapes=[pltpu.VMEM((d, hdim), BF16), pltpu.VMEM((d, hdim), BF16), pltpu.VMEM((hdim, d), BF16)],
        ),
        out_shape=jax.ShapeDtypeStruct(xs.shape, F32),
        compiler_params=pltpu.CompilerParams(dimension_semantics=("arbitrary",),
                                             vmem_limit_bytes=VMEM_LIMIT),
        name="moe_experts",
    )(block_e, nvalid, xs, p["w_gate"], p["w_up"], p["w_down"])


def _route(route, counts, t):
    a = t * TOP_K
    expert = route[:, 2:4].astype(jnp.int32).reshape(a)
    pos = route[:, 4:6].astype(jnp.int32).reshape(a)
    counts = counts.astype(jnp.int32)
    padded = (counts + MOE_BLOCK - 1) // MOE_BLOCK * MOE_BLOCK
    pad_end = jnp.cumsum(padded)
    pad_start = pad_end - padded
    dest = pad_start[expert] + pos
    n_blk = (a + N_EXPERTS * MOE_BLOCK) // MOE_BLOCK
    starts = jnp.arange(n_blk, dtype=jnp.int32) * MOE_BLOCK
    block_e = jnp.minimum(jnp.sum((pad_end[None, :] <= starts[:, None]).astype(jnp.int32), axis=1), N_EXPERTS - 1)
    nvalid = jnp.clip((pad_start + counts)[block_e] - starts, 0, MOE_BLOCK)
    return block_e, nvalid, dest * ROW_SUB


def _final_kernel(h_ref, ids0_ref, idsn_ref, y_hbm, ew_ref, g_ref, o_ref, ybuf, ysem):
    h = _combine_experts(h_ref[...], ew_ref[...], y_hbm, ids0_ref, idsn_ref, ybuf, ysem,
                         pl.program_id(0), pl.num_programs(0))
    o_ref[0] = _rms(h, g_ref[...])


def _final(h, y2, y_ids, ew, g, batch, seq_len):
    t, d = h.shape
    per_seq = seq_len // BLOCK
    out_blocks = per_seq - 1
    row = lambda n: ((n // out_blocks) * per_seq + n % out_blocks + 1, 0)
    ids = y_ids.reshape(batch, per_seq, TOP_K * BLOCK)[:, 1:].reshape(batch * out_blocks, 1, TOP_K * BLOCK)
    ids = jnp.concatenate([ids, jnp.zeros_like(ids[:1])], axis=0)
    return pl.pallas_call(
        _final_kernel,
        grid=(batch * out_blocks,),
        in_specs=[pl.BlockSpec((BLOCK, d), row),
                  pl.BlockSpec((1, 1, TOP_K * BLOCK), lambda n: (0, 0, 0), memory_space=pltpu.SMEM),
                  pl.BlockSpec((1, 1, TOP_K * BLOCK), lambda n: (n + 1, 0, 0), memory_space=pltpu.SMEM),
                  pl.BlockSpec(memory_space=pl.ANY),
                  pl.BlockSpec((BLOCK, LANES), row),
                  pl.BlockSpec((1, d), lambda n: (0, 0))],
        out_specs=pl.BlockSpec((1, BLOCK, d), lambda n: (n // out_blocks, n % out_blocks, 0)),
        out_shape=jax.ShapeDtypeStruct((batch, seq_len - BLOCK, d), F32),
        scratch_shapes=[pltpu.VMEM((2, TOP_K, BLOCK * ROW_SUB, LANES), F32), pltpu.SemaphoreType.DMA((2,))],
        compiler_params=pltpu.CompilerParams(dimension_semantics=("arbitrary",),
                                             vmem_limit_bytes=VMEM_LIMIT),
        name="final_norm",
    )(h, ids, ids, y2, ew, g)


def _rope_table(seq_len):
    half = MLA_ROPE // 2
    pos = (jnp.arange(seq_len, dtype=jnp.int32) - PAD_FRONT).astype(F32)
    inv_freq = ROPE_THETA ** (-jnp.arange(half, dtype=F32) / half)
    ang = pos[:, None] * inv_freq[None, :]
    cos, sin = jnp.cos(ang), jnp.sin(ang)
    cos2 = jnp.concatenate([cos, cos], axis=1)
    sin2 = jnp.concatenate([-sin, sin], axis=1)
    z = lambda w: jnp.zeros((seq_len, w), F32)
    scale = (MLA_NOPE + MLA_ROPE) ** -0.5 * LOG2E
    cos_q = jnp.concatenate([jnp.ones((seq_len, MLA_NOPE), F32), cos2, z(32)], axis=1) * scale
    sin_q = jnp.concatenate([z(MLA_NOPE), sin2, z(32)], axis=1) * scale
    cos_k = jnp.concatenate([z(MLA_NOPE), cos2, z(32)], axis=1)
    sin_k = jnp.concatenate([z(MLA_NOPE), sin2, z(32)], axis=1)
    return jnp.concatenate([cos_q, sin_q, cos_k, sin_k], axis=1)


def _swap_halves(w):
    half = w.shape[-1] // 2
    return jnp.concatenate([w[..., half:], w[..., :half]], axis=-1)


def _layer_params(i, seq_len, attn_norm, w_in, b_forget, sinks, mla_q_norm, mla_kv_norm, mla_w_uq,
                  mla_w_ukv, w_out, ffn_norm, w_group, b_group, w_router, b_router, w_gate, w_up, w_down):
    d = w_in.shape[1]
    w = w_in[i]
    sizes = (256, 128, 128, 256, 256, 256, 4, 256, 128, 32, 256, 256, 256)
    offs = np.concatenate([[0], np.cumsum(sizes)])
    (a_q, a_k, a_v, f_q, f_k, f_v, f_g, c_q, c_kv, c_kr, s_q, s_k, s_v) = [
        w[:, offs[j]:offs[j + 1]] for j in range(len(sizes))]
    qscale = HEAD_DIM ** -0.5
    grp = SWA_HEADS // SWA_KV_HEADS
    a_q = a_q.reshape(d, SWA_KV_HEADS, grp, HEAD_DIM).transpose(0, 2, 1, 3).reshape(d, 256)
    z = lambda n: jnp.zeros((d, n), F32)
    g_grp = jnp.concatenate([f_g[:, 0:2], z(62), c_kr, z(32)], axis=1)
    gs_grp = jnp.concatenate([f_g[:, 2:4], z(62), _swap_halves(c_kr), z(32)], axis=1)
    w_perm = jnp.concatenate([a_q * qscale, a_k, a_v, f_q * qscale, f_k, f_v, s_q * qscale, s_k, s_v,
                              c_q, c_kv, g_grp, gs_grp], axis=1).astype(BF16)
    wuq = mla_w_uq[i].reshape(MLA_Q_LORA, 4, MLA_NOPE + MLA_ROPE)
    zq = lambda n: jnp.zeros((MLA_Q_LORA, 4, n), F32)
    w_uq_a = jnp.concatenate([wuq, zq(32)], axis=2).reshape(MLA_Q_LORA, 512).astype(BF16)
    w_uq_b = jnp.concatenate([zq(MLA_NOPE), _swap_halves(wuq[:, :, MLA_NOPE:]), zq(32)],
                             axis=2).reshape(MLA_Q_LORA, 512).astype(BF16)
    wukv = mla_w_ukv[i].reshape(MLA_KV_LORA, 4, MLA_NOPE + MLA_V)
    w_kv_k = jnp.concatenate([wukv[:, :, :MLA_NOPE], jnp.zeros((MLA_KV_LORA, 4, 64), F32)],
                             axis=2).reshape(MLA_KV_LORA, 512).astype(BF16)
    w_kv_v = wukv[:, :, MLA_NOPE:].reshape(MLA_KV_LORA, 256).astype(BF16)
    bf = b_forget[i].astype(F32)
    b_f = jnp.zeros((1, 256), F32).at[0, 0:2].set(bf[0:2]).at[0, 128:130].set(bf[2:4])
    wo = w_out[i]
    wo_a = wo[:256].reshape(SWA_KV_HEADS, grp, HEAD_DIM, d).transpose(1, 0, 2, 3).reshape(256, d)
    wo4 = jnp.concatenate([wo_a, wo[256:]], axis=0).reshape(4, 256, d).astype(BF16)
    w_r = jnp.concatenate([w_group[i], w_router[i], jnp.zeros((d, LANES - N_GROUPS - N_EXPERTS), F32)], axis=1)
    w_r_hi = w_r.astype(BF16)
    w_r_lo = (w_r - w_r_hi.astype(F32)).astype(BF16)
    b_r = jnp.concatenate([b_group[i], b_router[i], jnp.zeros((LANES - N_GROUPS - N_EXPERTS,), F32)])[None, :]
    return dict(
        attn_norm=attn_norm[i][None, :], w_in=w_perm, q_norm=mla_q_norm[i][None, :],
        kv_norm=mla_kv_norm[i][None, :], w_uq_a=w_uq_a, w_uq_b=w_uq_b, w_kv_k=w_kv_k, w_kv_v=w_kv_v,
        rope_tab=_rope_table(seq_len), b_forget=b_f, sinks=sinks[i].astype(F32), w_out=wo4,
        ffn_norm=ffn_norm[i][None, :], w_r_hi=w_r_hi, w_r_lo=w_r_lo, b_r=b_r.astype(F32),
        w_gate=w_gate, w_up=w_up, w_down=w_down, layer=i)


def kernel(x, meta_tokens, attn_norm, w_in, b_forget, sinks, mla_q_norm, mla_kv_norm, mla_w_uq, mla_w_ukv,
           w_out, ffn_norm, w_group, b_group, w_router, b_router, w_gate, w_up, w_down, final_norm):
    batch, seq, d = x.shape
    seq_len = seq + BLOCK
    assert seq_len % ROW_TILE == 0 and seq_len % (BLOCK * SWA_Q_BLOCKS) == 0
    assert all(seq % tile == 0 for tile in Q_TILES.values())
    t = batch * seq_len
    depth = w_in.shape[0]
    lead = jnp.concatenate([jnp.zeros((PAD_FRONT, d), x.dtype), meta_tokens.astype(x.dtype)], axis=0)
    h = (lead, x)
    y2 = y_ids = ew = None
    for i in range(depth):
        p = _layer_params(i, seq_len, attn_norm, w_in, b_forget, sinks, mla_q_norm, mla_kv_norm, mla_w_uq,
                          mla_w_ukv, w_out, ffn_norm, w_group, b_group, w_router, b_router, w_gate, w_up, w_down)
        h, (qa, fq, fk, fv, cq, ck, cv, sq, sk, sv) = _inproj(h, y2, y_ids, ew, p, seq_len)
        y_a = _swa_attention(qa, p["sinks"], batch, seq_len)
        y_b = _causal_attention("fox", fq, fk, fv, batch, seq_len)
        y_c = _causal_attention("mla", cq, ck, cv, batch, seq_len)
        y_d = _causal_attention("sb", sq, sk, sv, batch, seq_len)
        ys = [y.reshape(t, 256) for y in (y_a, y_b, y_c, y_d)]
        h, xn, ew, counts = _outproj(ys, h, p)
        block_e, nvalid, y_ids = _route(ew, counts[0, N_GROUPS:N_GROUPS + N_EXPERTS], t)
        y2 = _moe(_dispatch(xn, y_ids, nvalid), block_e, nvalid, p)
    return _final(h, y2, y_ids, ew, final_norm[None, :], batch, seq_len)
```

```python
import functools

import jax
import jax.numpy as jnp
import numpy as np
from jax import lax
from jax.experimental import pallas as pl
from jax.experimental.pallas import tpu as pltpu

F32 = jnp.float32
BF16 = jnp.bfloat16

BLOCK = 128
N_META = 16
PAD_FRONT = BLOCK - N_META
CHUNK_SHIFT = 6
HEAD_DIM = 64
NORM_EPS = 1e-6
NEG = -1e30
PAD_KEY_LOGIT = -(2.0 ** 100)
UNDERFLOW_LOG2 = -150.0
LOG2E = 1.4426950408889634
BIG = 1 << 30
SWA_HEADS, SWA_KV_HEADS, WINDOW = 4, 2, 128
MLA_Q_LORA, MLA_KV_LORA, MLA_NOPE, MLA_ROPE, MLA_V = 256, 128, 64, 32, 64
MLA_BIAS_LANE = MLA_NOPE + MLA_ROPE
ROPE_THETA = 10000.0
N_GROUPS, EXPERTS_PER_GROUP, TOP_K = 4, 8, 2
N_EXPERTS = N_GROUPS * EXPERTS_PER_GROUP
MOE_BLOCK = 512
LANES = 128
ROW_SUB = 8
ROW_TILE = 384
Q_TILES = {"fox": 1024, "mla": 1024, "sb": 512}
K_TILE = 256
ROW_PART = 128
SWA_Q_BLOCKS = 3
VMEM_LIMIT = 56 * 1024 * 1024

C_A, C_B, C_D, C_CQ, C_CKV, C_G, C_GS, C_END = 0, 512, 1280, 2048, 2304, 2432, 2560, 2688
B_F0, B_F1, B_PAD = 0, 3, 6


def _rms(x, g):
    return x * lax.rsqrt(jnp.mean(x * x, axis=-1, keepdims=True) + NORM_EPS) * g


def _log_sigmoid(x):
    return jnp.minimum(x, 0.0) - jnp.log(1.0 + jnp.exp(-jnp.abs(x)))


def _dot(a, b):
    return jnp.dot(a, b, preferred_element_type=F32)


def _dot_nt(a, b):
    return lax.dot_general(a, b, (((1,), (1,)), ((), ())), preferred_element_type=F32)


def _rows_from_tiles(ref, lead, n):
    return jnp.concatenate([ref[(*lead, pl.ds(j, n, stride=ROW_SUB), slice(None))] for j in range(ROW_SUB)], axis=1)


def _rows_to_tiles(ref, lead, x):
    n = x.shape[0]
    for j in range(ROW_SUB):
        ref[(*lead, pl.ds(j, n, stride=ROW_SUB), slice(None))] = x[:, j * LANES:(j + 1) * LANES]


def _tile4(x):
    return jnp.concatenate([x, x, x, x], axis=1)


def _split3(x):
    hi = x.astype(BF16)
    r1 = x - hi.astype(F32)
    mid = r1.astype(BF16)
    lo = (r1 - mid.astype(F32)).astype(BF16)
    return hi, mid, lo


def _free_base(head):
    return head * LANES + (HEAD_DIM if head % 2 == 0 else 0)


def _expert_rows_start(y_hbm, ids_ref, buf, sem, slot):
    for j in range(ids_ref.shape[2]):
        tok, k = divmod(j, TOP_K)
        src = pl.multiple_of(ids_ref[0, 0, j], ROW_SUB)
        pltpu.make_async_copy(y_hbm.at[pl.ds(src, ROW_SUB)], buf.at[slot, k, pl.ds(tok * ROW_SUB, ROW_SUB)],
                              sem.at[slot]).start(priority=j % 2)


def _expert_rows_wait(y_hbm, buf, sem, slot):
    for k in range(TOP_K):
        pltpu.make_async_copy(y_hbm.at[pl.ds(0, buf.shape[2])], buf.at[slot, k], sem.at[slot]).wait()


def _combine_experts(h, ew, y_hbm, ids0_ref, idsn_ref, buf, sem, step, n_steps):
    slot = step % 2

    @pl.when(step == 0)
    def _():
        _expert_rows_start(y_hbm, ids0_ref, buf, sem, 0)

    _expert_rows_start(y_hbm, idsn_ref, buf, sem, 1 - slot)
    _expert_rows_wait(y_hbm, buf, sem, slot)
    n = h.shape[0]
    out = h + ew[:, 0:1] * _rows_from_tiles(buf, (slot, 0), n) + ew[:, 1:2] * _rows_from_tiles(buf, (slot, 1), n)

    @pl.when(step == n_steps - 1)
    def _():
        _expert_rows_wait(y_hbm, buf, sem, 1 - slot)

    return out


def _inproj_kernel(has_y2, n_seq_tiles, *refs):
    if has_y2:
        (h_ref, ids0_ref, idsn_ref, y_hbm, ew_ref, *rest) = refs
    else:
        (lead_ref, *x_refs) = refs[:1 + ROW_TILE // BLOCK]
        rest = refs[1 + ROW_TILE // BLOCK:]
    (g_ref, w_ref, qn_ref, kvn_ref, wuqa_ref, wuqb_ref, wkvk_ref, wkvv_ref, tab_ref, bf_ref,
     pq_ref, pk_ref, rows_ref, hout_ref, *outs) = rest
    if has_y2:
        *outs, ybuf, ysem = outs
    (qa_ref, fq_ref, fk_ref, fv_ref, cq_ref, ck_ref, cv_ref, sq_ref, sk_ref, sv_ref, carry_ref) = outs
    tile = pl.program_id(0) % n_seq_tiles
    if has_y2:
        h = _combine_experts(h_ref[...], ew_ref[...], y_hbm, ids0_ref, idsn_ref, ybuf, ysem,
                             pl.program_id(0), pl.num_programs(0))
    else:
        first = jnp.where(tile == 0, lead_ref[...], x_refs[0][0])
        h = jnp.concatenate([first] + [r[0] for r in x_refs[1:]], axis=0)
    hout_ref[...] = h
    tm, d = h.shape
    xn = _rms(h, g_ref[...]).astype(BF16)
    acc = _dot(xn, w_ref[...])
    lane = lax.broadcasted_iota(jnp.int32, (1, LANES), 1)
    lo_half = lane < HEAD_DIM
    pad_col = jnp.where(tile * tm + lax.broadcasted_iota(jnp.int32, (tm, 1), 0) < PAD_FRONT,
                        PAD_KEY_LOGIT, 0.0)
    rows = rows_ref[...]
    fq_one, fk_one, pad_lane, sq_one, mla_one, mla_pad = (rows[j:j + 1] for j in range(6))

    def per_head(x_pair, bias, pair, scale=None):
        x = x_pair if scale is None else x_pair * scale
        even = jnp.where(lo_half, x, bias[:, (2 * pair) * LANES:(2 * pair + 1) * LANES])
        odd = jnp.where(lo_half, bias[:, (2 * pair + 1) * LANES:(2 * pair + 2) * LANES], x)
        return even, odd

    def store_heads(ref, x_off, bias, scale=None):
        for pair in range(2):
            x_pair = acc[:, x_off + pair * LANES:x_off + (pair + 1) * LANES]
            even, odd = per_head(x_pair, bias, pair, scale)
            ref[:, (2 * pair) * LANES:(2 * pair + 1) * LANES] = even.astype(BF16)
            ref[:, (2 * pair + 1) * LANES:(2 * pair + 2) * LANES] = odd.astype(BF16)

    qa_ref[...] = acc[:, C_A:C_B].astype(BF16)

    @pl.when(tile == 0)
    def _():
        carry_ref[...] = jnp.zeros_like(carry_ref)

    lf = _log_sigmoid(acc[:, C_G:C_END] + bf_ref[...]) * LOG2E
    r = lax.broadcasted_iota(jnp.int32, (BLOCK, BLOCK), 0)
    c = lax.broadcasted_iota(jnp.int32, (BLOCK, BLOCK), 1)
    tri = jnp.where(c <= r, 1.0, 0.0).astype(BF16)
    carry = carry_ref[...]
    blocks = []
    for b in range(tm // BLOCK):
        hi, mid, lo = _split3(lf[b * BLOCK:(b + 1) * BLOCK])
        y = _dot(tri, hi) + _dot(tri, mid) + _dot(tri, lo) + carry
        carry = y[BLOCK - 1:BLOCK, :]
        blocks.append(y)
    carry_ref[...] = carry
    f_hi, f_mid, f_lo = _split3(jnp.concatenate(blocks, axis=0))
    q_bias = _dot(f_hi, pq_ref[0]) + _dot(f_mid, pq_ref[1]) + _dot(f_lo, pq_ref[2]) + fq_one
    k_bias = (_dot(f_hi, pk_ref[0]) + _dot(f_mid, pk_ref[1]) + _dot(f_lo, pk_ref[2]) + fk_one
              + pad_col * pad_lane)
    ones = jnp.ones((tm, 4 * LANES), F32)
    store_heads(fq_ref, C_B, q_bias, LOG2E)
    store_heads(fk_ref, C_B + 256, k_bias)
    store_heads(fv_ref, C_B + 512, ones)

    store_heads(sq_ref, C_D, jnp.broadcast_to(sq_one, (tm, 4 * LANES)), LOG2E)
    store_heads(sk_ref, C_D + 256, pad_col * pad_lane)
    sv_ref[...] = acc[:, C_D + 512:C_CQ].astype(BF16)

    tab = tab_ref[...]
    cos_q, sin_q = tab[:, 0:128], tab[:, 128:256]
    cos_k, sin_k = tab[:, 256:384], tab[:, 384:512]
    cqn = _rms(acc[:, C_CQ:C_CKV], qn_ref[...]).astype(BF16)
    q_lin = _dot(cqn, wuqa_ref[...])
    q_swp = _dot(cqn, wuqb_ref[...])
    cq_ref[...] = (q_lin * _tile4(cos_q) + q_swp * _tile4(sin_q) + mla_one).astype(BF16)
    ckvn = _rms(acc[:, C_CKV:C_G], kvn_ref[...]).astype(BF16)
    k_nope = _dot(ckvn, wkvk_ref[...])
    grp, grp_s = acc[:, C_G:C_GS], acc[:, C_GS:C_END]
    k_rope = grp * cos_k + grp_s * sin_k
    ck_ref[...] = (k_nope + _tile4(k_rope) + pad_col * mla_pad).astype(BF16)
    vv = _dot(ckvn, wkvv_ref[...])
    for pair in range(2):
        even, odd = per_head(vv[:, pair * LANES:(pair + 1) * LANES], ones, pair)
        cv_ref[:, (2 * pair) * LANES:(2 * pair + 1) * LANES] = even.astype(BF16)
        cv_ref[:, (2 * pair + 1) * LANES:(2 * pair + 2) * LANES] = odd.astype(BF16)


def _bias_constants():
    src = (0, 1, LANES, LANES + 1)
    pq = np.zeros((3, 2 * LANES, 4 * LANES), np.float32)
    pk = np.zeros((3, 2 * LANES, 4 * LANES), np.float32)
    rows = np.zeros((8, 4 * LANES), np.float32)
    for head in range(4):
        base = _free_base(head)
        for part in range(3):
            pq[part, src[head], base + B_F0 + part] = 1.0
            pk[part, src[head], base + B_F1 + part] = -1.0
            rows[0, base + B_F1 + part] = 1.0
            rows[1, base + B_F0 + part] = 1.0
        rows[0, base + B_PAD] = 1.0
        rows[2, base + B_PAD] = 1.0
        rows[3, base + B_PAD] = 1.0
        rows[4, head * LANES + MLA_BIAS_LANE] = 1.0
        rows[5, head * LANES + MLA_BIAS_LANE] = 1.0
    return jnp.asarray(pq, BF16), jnp.asarray(pk, BF16), jnp.asarray(rows, F32)


def _inproj(h, y2, y_ids, ew, p, seq_len):
    tm = ROW_TILE
    n_seq_tiles = seq_len // tm
    has_y2 = y2 is not None
    row = lambda i: (i, 0)
    fixed = lambda i: (0, 0)
    if has_y2:
        t, d = h.shape
        in_specs = [pl.BlockSpec((tm, d), row)]
        args = [h]
    else:
        lead, x = h
        batch, seq, d = x.shape
        t = batch * seq_len
        per_tile = tm // BLOCK
        frame = lambda k: pl.BlockSpec((1, BLOCK, d), lambda i: (
            i // n_seq_tiles, jnp.maximum((i % n_seq_tiles) * per_tile - 1 + k, 0), 0))
        in_specs = [pl.BlockSpec((BLOCK, d), fixed)] + [frame(k) for k in range(per_tile)]
        args = [lead] + [x] * per_tile
    if has_y2:
        ids = y_ids.reshape(t // tm, 1, TOP_K * tm)
        ids = jnp.concatenate([ids, jnp.zeros_like(ids[:1])], axis=0)
        in_specs += [pl.BlockSpec((1, 1, TOP_K * tm), lambda i: (0, 0, 0), memory_space=pltpu.SMEM),
                     pl.BlockSpec((1, 1, TOP_K * tm), lambda i: (i + 1, 0, 0), memory_space=pltpu.SMEM),
                     pl.BlockSpec(memory_space=pl.ANY),
                     pl.BlockSpec((tm, LANES), row)]
        args += [ids, ids, y2, ew]
    pq, pk, rows = _bias_constants()
    consts = [p["attn_norm"], p["w_in"], p["q_norm"], p["kv_norm"], p["w_uq_a"], p["w_uq_b"],
              p["w_kv_k"], p["w_kv_v"]]
    in_specs += [pl.BlockSpec(c.shape, fixed) for c in consts]
    args += consts
    in_specs.append(pl.BlockSpec((tm, 512), lambda i: (i % n_seq_tiles, 0)))
    args.append(p["rope_tab"])
    in_specs += [pl.BlockSpec((1, 256), fixed),
                 pl.BlockSpec(pq.shape, lambda i: (0, 0, 0)),
                 pl.BlockSpec(pk.shape, lambda i: (0, 0, 0)),
                 pl.BlockSpec(rows.shape, fixed)]
    args += [p["b_forget"], pq, pk, rows]
    widths = [512] * 9 + [256]
    out_shape = [jax.ShapeDtypeStruct((t, d), F32)] + [jax.ShapeDtypeStruct((t, w), BF16) for w in widths]
    out_specs = [pl.BlockSpec((tm, d), row)] + [pl.BlockSpec((tm, w), row) for w in widths]
    outs = pl.pallas_call(
        functools.partial(_inproj_kernel, has_y2, n_seq_tiles),
        grid=(t // tm,),
        in_specs=in_specs,
        out_specs=out_specs,
        out_shape=out_shape,
        scratch_shapes=[pltpu.VMEM((1, 2 * LANES), F32)] + (
            [pltpu.VMEM((2, TOP_K, tm * ROW_SUB, LANES), F32), pltpu.SemaphoreType.DMA((2,))] if has_y2 else []),
        compiler_params=pltpu.CompilerParams(dimension_semantics=("arbitrary",),
                                             vmem_limit_bytes=VMEM_LIMIT),
        name="inproj_y2" if has_y2 else "inproj",
    )(*args)
    return outs[0], outs[1:]


def _swa_kernel(sink_ref, q_ref, km_ref, kp_ref, kc_ref, vm_ref, vp_ref, vc_ref, o_ref):
    i = pl.program_id(1)
    n_sub = q_ref.shape[1] // BLOCK
    lane = lax.broadcasted_iota(jnp.int32, (1, LANES), 1)
    lo_half = lane < HEAD_DIM
    half_masks = [jnp.where(lo_half, 1.0, 0.0).astype(BF16), jnp.where(lo_half, 0.0, 1.0).astype(BF16)]
    row = lax.broadcasted_iota(jnp.int32, (BLOCK, 1), 0)
    col = lax.broadcasted_iota(jnp.int32, (1, BLOCK), 1)
    grp = SWA_HEADS // SWA_KV_HEADS
    k_all = jnp.concatenate([km_ref[0], kp_ref[0], kc_ref[0]], axis=0)
    v_all = jnp.concatenate([vm_ref[0], vp_ref[0], vc_ref[0]], axis=0)
    for j in range(n_sub):
        q0 = (i * n_sub + j) * BLOCK
        pq = q0 + row
        cq = pq >> CHUNK_SHIFT
        segs = []
        vis_m = col >= PAD_FRONT
        segs.append((vis_m, jnp.minimum(jnp.abs(pq - col), WINDOW).astype(F32)))
        for pk in (q0 - BLOCK + col, q0 + col):
            ck = jnp.where(pk >= BLOCK, pk >> CHUNK_SHIFT, BIG)
            vis = (ck <= cq) & (ck >= cq - (WINDOW >> CHUNK_SHIFT))
            segs.append((vis, jnp.abs(pq - pk).astype(F32)))
        kj = jnp.concatenate([k_all[0:BLOCK], k_all[(j + 1) * BLOCK:(j + 3) * BLOCK]], axis=0)
        vj = jnp.concatenate([v_all[0:BLOCK], v_all[(j + 1) * BLOCK:(j + 3) * BLOCK]], axis=0)
        for g in range(grp):
            qg = q_ref[0, j * BLOCK:(j + 1) * BLOCK, g * LANES:(g + 1) * LANES]
            out_g = None
            for hk in range(SWA_KV_HEADS):
                head = hk * grp + g
                slope = 2.0 ** (-8.0 * (head + 1) / SWA_HEADS)
                sink = sink_ref[head]
                s_all = _dot_nt(qg * half_masks[hk], kj)
                tiles = [jnp.where(vis, s_all[:, n * LANES:(n + 1) * LANES] - slope * dist, NEG)
                         for n, (vis, dist) in enumerate(segs)]
                top = jnp.maximum(jnp.maximum(tiles[0], tiles[1]), tiles[2])
                m = jnp.broadcast_to(jnp.maximum(jnp.max(top, axis=-1, keepdims=True), sink), (BLOCK, LANES))
                e = [jnp.exp(x - m) for x in tiles]
                den = jnp.sum(e[0] + e[1] + e[2], axis=-1, keepdims=True) + jnp.exp(sink - m[:, 0:1])
                o = _dot(jnp.concatenate(e, axis=1).astype(BF16), vj) * (1.0 / den)
                out_g = o if hk == 0 else jnp.where(lo_half, out_g, o)
            o_ref[0, j * BLOCK:(j + 1) * BLOCK, g * LANES:(g + 1) * LANES] = out_g.astype(BF16)


def _swa_attention(qa, sinks, batch, seq_len):
    x = qa.reshape(batch, seq_len, 512)
    n_sub = SWA_Q_BLOCKS
    nb = seq_len // (BLOCK * n_sub)
    blk = lambda f: pl.BlockSpec((1, BLOCK, LANES), f)
    own = lambda c: pl.BlockSpec((1, BLOCK * n_sub, LANES), lambda b, i: (b, i, c))
    return pl.pallas_call(
        _swa_kernel,
        grid=(batch, nb),
        in_specs=[
            pl.BlockSpec(memory_space=pltpu.SMEM),
            pl.BlockSpec((1, BLOCK * n_sub, 2 * LANES), lambda b, i: (b, i, 0)),
            blk(lambda b, i: (b, 0, 2)),
            blk(lambda b, i: (b, jnp.maximum(i * n_sub - 1, 0), 2)),
            own(2),
            blk(lambda b, i: (b, 0, 3)),
            blk(lambda b, i: (b, jnp.maximum(i * n_sub - 1, 0), 3)),
            own(3),
        ],
        out_specs=pl.BlockSpec((1, BLOCK * n_sub, 2 * LANES), lambda b, i: (b, i, 0)),
        out_shape=jax.ShapeDtypeStruct((batch, seq_len, 2 * LANES), BF16),
        compiler_params=pltpu.CompilerParams(dimension_semantics=("arbitrary", "arbitrary"),
                                             vmem_limit_bytes=VMEM_LIMIT),
        name="swa_attention",
    )(sinks, x, x, x, x, x, x, x)


def _causal_kernel(mode, q_ref, k_ref, v_ref, o_ref, stat_ref, acc_ref):
    seq_len = q_ref.shape[1]
    Q_TILE = stat_ref.shape[1]
    n_qt = (seq_len - BLOCK) // Q_TILE
    per_tile = Q_TILE // K_TILE
    lane = lax.broadcasted_iota(jnp.int32, (1, LANES), 1)
    lo_half = lane < HEAD_DIM
    if mode == "sb":
        r = lax.broadcasted_iota(jnp.int32, (K_TILE, K_TILE), 0)
        c = lax.broadcasted_iota(jnp.int32, (K_TILE, K_TILE), 1)
        later2 = jnp.where(r > c, 1.0, 0.0).astype(BF16)
        later1 = later2[:BLOCK, :BLOCK]

    def causal(pq, k0, tk):
        pk = k0 + lax.broadcasted_iota(jnp.int32, (1, tk), 1)
        if mode == "fox":
            return pk <= pq
        if mode == "mla":
            return (pk >> CHUNK_SHIFT) <= (pq >> CHUNK_SHIFT)
        return pk < pq

    def head_v(k0, tk, hh):
        if mode == "sb":
            return v_ref[0, pl.ds(k0, tk), :]
        return v_ref[0, pl.ds(k0, tk), hh * LANES:(hh + 1) * LANES]

    def lane_tiles(x):
        return [x[:, j * LANES:(j + 1) * LANES] for j in range(x.shape[1] // LANES)]

    def row_parts(lo, hi):
        step = min(hi - lo, ROW_PART)
        return [(r0, step) for r0 in range(lo, hi, step)]

    def softmax_chunk(q0, tq, k0, tk, masked, first, row_lo=0):
        for hh in range(2):
            qh = q_ref[0, pl.ds(q0 + row_lo, tq - row_lo), hh * LANES:(hh + 1) * LANES]
            kh = k_ref[0, pl.ds(k0, tk), hh * LANES:(hh + 1) * LANES]
            s_all = _dot_nt(qh, kh)
            vh = head_v(k0, tk, hh)
            for r0, tr in row_parts(row_lo, tq):
                s = s_all[r0 - row_lo:r0 - row_lo + tr]
                if masked and r0 < row_lo + tk:
                    pq = q0 + r0 + lax.broadcasted_iota(jnp.int32, (tr, 1), 0)
                    s = jnp.where(causal(pq, k0, tk), s, NEG)
                tiles = lane_tiles(s)
                top = tiles[0]
                for x in tiles[1:]:
                    top = jnp.maximum(top, x)
                m_new = jnp.broadcast_to(jnp.max(top, axis=-1, keepdims=True), (tr, LANES))
                if not first:
                    m_old = stat_ref[hh, r0:r0 + tr, :]
                    m_new = jnp.maximum(m_old, m_new)
                p = jnp.concatenate([jnp.exp2(x - m_new) for x in tiles], axis=1).astype(BF16)
                pv = _dot(p, vh)
                if not first:
                    pv = jnp.exp2(m_old - m_new) * acc_ref[hh, r0:r0 + tr, :] + pv
                stat_ref[hh, r0:r0 + tr, :] = m_new
                acc_ref[hh, r0:r0 + tr, :] = pv

    def stick_chunk(q0, rows, k0, tk, masked, first):
        lo_r, hi_r = rows
        tq = hi_r - lo_r
        later = later2 if tk == K_TILE else later1
        for hh in range(2):
            qh = q_ref[0, pl.ds(q0 + lo_r, tq), hh * LANES:(hh + 1) * LANES]
            kh = k_ref[0, pl.ds(k0, tk), hh * LANES:(hh + 1) * LANES]
            z = _dot_nt(qh, kh)
            ls_pos = jnp.minimum(z, 0.0) - jnp.log(1.0 + jnp.exp2(-jnp.abs(z))) * LOG2E
            log_keep = ls_pos - z
            if masked:
                pq = q0 + lo_r + lax.broadcasted_iota(jnp.int32, (tq, 1), 0)
                vis = causal(pq, k0, tk)
                log_keep = jnp.where(vis, log_keep, 0.0)
            after = _dot(log_keep.astype(BF16), later)
            tot = ls_pos + after
            chunk_total = jnp.broadcast_to(after[:, 0:1] + log_keep[:, 0:1], (tq, LANES))
            if not first:
                carry = stat_ref[hh, lo_r:hi_r, :]
                tot = jnp.concatenate([x + carry for x in lane_tiles(tot)], axis=1)
                chunk_total = carry + chunk_total
            a = jnp.exp2(tot)
            if masked:
                a = jnp.where(vis, a, 0.0)
            pv = _dot(a.astype(BF16), head_v(k0, tk, hh))
            if not first:
                pv = acc_ref[hh, lo_r:hi_r, :] + pv
            stat_ref[hh, lo_r:hi_r, :] = chunk_total
            acc_ref[hh, lo_r:hi_r, :] = pv

    def finish(q0, tq):
        a0, a1 = acc_ref[0, 0:tq, :], acc_ref[1, 0:tq, :]
        if mode != "sb":
            a0 = a0 / a0[:, HEAD_DIM:HEAD_DIM + 1]
            a1 = a1 / a1[:, 0:1]
        o_ref[0, pl.ds(q0, tq), :] = jnp.where(lo_half, a0, a1).astype(BF16)

    def chunk_start(j):
        return pl.multiple_of(BLOCK + j * K_TILE, BLOCK)

    if mode == "sb":
        stick_chunk(0, (0, BLOCK), 0, BLOCK, True, True)
        finish(0, BLOCK)
        whole = (0, Q_TILE)

        def q_body(i, _):
            q0 = pl.multiple_of(BLOCK + i * Q_TILE, BLOCK)
            n_int = i * per_tile
            for d in reversed(range(per_tile)):
                stick_chunk(q0, whole, chunk_start(n_int + d), K_TILE, True, d == per_tile - 1)

            def alive():
                top = jnp.maximum(jnp.max(stat_ref[0]), jnp.max(stat_ref[1]))
                return (top > UNDERFLOW_LOG2).astype(jnp.int32)

            def body(st):
                jj, _ = st
                stick_chunk(q0, whole, chunk_start(n_int - 1 - jj), K_TILE, False, False)
                return jj + 1, alive()

            _, go = lax.while_loop(lambda st: (st[0] < n_int) & (st[1] > 0), body, (0, alive()))

            @pl.when(go > 0)
            def _():
                stick_chunk(q0, whole, 0, BLOCK, False, False)

            finish(q0, Q_TILE)
            return 0
    else:
        softmax_chunk(0, BLOCK, 0, BLOCK, True, True)
        finish(0, BLOCK)

        def q_body(i, _):
            q0 = pl.multiple_of(BLOCK + i * Q_TILE, BLOCK)
            n_int = i * per_tile
            stat_ref[...] = jnp.full(stat_ref.shape, NEG, F32)
            acc_ref[...] = jnp.zeros(acc_ref.shape, F32)

            def body(j, _):
                for d in range(per_tile):
                    softmax_chunk(q0, Q_TILE, chunk_start(j * per_tile + d), K_TILE, False, False)
                return 0

            lax.fori_loop(0, i, body, 0)
            for d in range(per_tile):
                softmax_chunk(q0, Q_TILE, chunk_start(n_int + d), K_TILE, True, False, row_lo=d * K_TILE)
            softmax_chunk(q0, Q_TILE, 0, BLOCK, False, False)
            finish(q0, Q_TILE)
            return 0

    lax.fori_loop(0, n_qt, q_body, 0)


def _causal_attention(mode, q, k, v, batch, seq_len):
    wide = pl.BlockSpec((1, seq_len, 2 * LANES), lambda b, p: (b, 0, p))
    narrow = pl.BlockSpec((1, seq_len, LANES), lambda b, p: (b, 0, p))
    args = [q.reshape(batch, seq_len, 512), k.reshape(batch, seq_len, 512),
            v.reshape(batch, seq_len, v.shape[1])]
    return pl.pallas_call(
        functools.partial(_causal_kernel, mode),
        grid=(batch, 2),
        in_specs=[wide, wide, narrow if mode == "sb" else wide],
        out_specs=narrow,
        out_shape=jax.ShapeDtypeStruct((batch, seq_len, 2 * LANES), BF16),
        scratch_shapes=[pltpu.VMEM((2, Q_TILES[mode], LANES), F32), pltpu.VMEM((2, Q_TILES[mode], LANES), F32)],
        compiler_params=pltpu.CompilerParams(dimension_semantics=("arbitrary", "arbitrary"),
                                             vmem_limit_bytes=VMEM_LIMIT),
        name=mode + "_attention",
    )(*args)


def _outproj_kernel(ya_ref, yb_ref, yc_ref, yd_ref, h_ref, wo_ref, g_ref, wrh_ref, wrl_ref, br_ref, tri_ref,
                    h2_ref, xn_ref, route_ref, cnt_ref):
    o = (_dot(ya_ref[...], wo_ref[0]) + _dot(yb_ref[...], wo_ref[1])
         + _dot(yc_ref[...], wo_ref[2]) + _dot(yd_ref[...], wo_ref[3]))
    h2 = h_ref[...] + o
    h2_ref[...] = h2
    xn = _rms(h2, g_ref[...])
    _rows_to_tiles(xn_ref, (), xn)
    xh = xn.astype(BF16)
    xl = (xn - xh.astype(F32)).astype(BF16)
    wrh, wrl = wrh_ref[...], wrl_ref[...]
    lg = _dot(xh, wrh) + _dot(xl, wrh) + _dot(xh, wrl) + br_ref[...]

    tm = lg.shape[0]
    lane = lax.broadcasted_iota(jnp.int32, (tm, LANES), 1)
    ninf = -jnp.inf

    def first_max(x):
        top = jnp.max(x, axis=-1, keepdims=True)
        return top, jnp.min(jnp.where(x == top, lane, LANES), axis=-1, keepdims=True)

    gl = jnp.where(lane < N_GROUPS, lg, ninf)
    g_max, g_top = first_max(gl)
    g_w = 1.0 / jnp.sum(jnp.exp(gl - g_max), axis=-1, keepdims=True)
    e_lo = N_GROUPS + g_top * EXPERTS_PER_GROUP
    el = jnp.where((lane >= e_lo) & (lane < e_lo + EXPERTS_PER_GROUP), lg, ninf)
    v1, i1 = first_max(el)
    v2, i2 = first_max(jnp.where(lane == i1, ninf, el))
    r21 = jnp.exp(v2 - v1)
    w1 = g_w / (1.0 + r21)
    w2 = w1 * r21

    @pl.when(pl.program_id(0) == 0)
    def _():
        cnt_ref[...] = jnp.zeros_like(cnt_ref)

    m1 = jnp.where(lane == i1, 1.0, 0.0)
    m2 = jnp.where(lane == i2, 1.0, 0.0)
    both = m1 + m2
    before = _dot(tri_ref[...], both.astype(BF16)) + cnt_ref[0:1, :]
    rank1 = jnp.sum(m1 * before, axis=-1, keepdims=True)
    rank2 = jnp.sum(m2 * before, axis=-1, keepdims=True)
    cnt_ref[...] = cnt_ref[...] + jnp.sum(both, axis=0, keepdims=True)
    cols = [w1, w2, (i1 - N_GROUPS).astype(F32), (i2 - N_GROUPS).astype(F32), rank1, rank2]
    route = jnp.zeros((tm, LANES), F32)
    for j, c in enumerate(cols):
        route = jnp.where(lane == j, c, route)
    route_ref[...] = route


def _outproj(ys, h, p):
    t, d = h.shape
    tm = ROW_TILE
    row = lambda i: (i, 0)
    fixed2 = lambda i: (0, 0)
    in_specs = [pl.BlockSpec((tm, 256), row)] * 4 + [
        pl.BlockSpec((tm, d), row),
        pl.BlockSpec((4, 256, d), lambda i: (0, 0, 0)),
        pl.BlockSpec((1, d), fixed2),
        pl.BlockSpec((d, LANES), fixed2),
        pl.BlockSpec((d, LANES), fixed2),
        pl.BlockSpec((1, LANES), fixed2),
        pl.BlockSpec((tm, tm), fixed2),
    ]
    earlier = jnp.asarray(np.tril(np.ones((tm, tm), np.float32), -1), BF16)
    return pl.pallas_call(
        _outproj_kernel,
        grid=(t // tm,),
        in_specs=in_specs,
        out_specs=[pl.BlockSpec((tm, d), row), pl.BlockSpec((tm * ROW_SUB, LANES), row),
                   pl.BlockSpec((tm, LANES), row), pl.BlockSpec((8, LANES), fixed2)],
        out_shape=[jax.ShapeDtypeStruct((t, d), F32), jax.ShapeDtypeStruct((t * ROW_SUB, LANES), F32),
                   jax.ShapeDtypeStruct((t, LANES), F32), jax.ShapeDtypeStruct((8, LANES), F32)],
        compiler_params=pltpu.CompilerParams(dimension_semantics=("arbitrary",),
                                             vmem_limit_bytes=VMEM_LIMIT),
        name="outproj_router",
    )(*ys, h, p["w_out"], p["ffn_norm"], p["w_r_hi"], p["w_r_lo"], p["b_r"], earlier)


def _dispatch_kernel(nv_ref, ids_ref, x_ref, xs_hbm, zero_buf, sem):
    blk = zero_buf.shape[0]

    @pl.when(pl.program_id(0) == 0)
    def _():
        zero_buf[...] = jnp.zeros_like(zero_buf)
        for phase in ("start", "wait"):
            for b in range(nv_ref.shape[0]):
                @pl.when(nv_ref[b] < MOE_BLOCK)
                def _():
                    copy = pltpu.make_async_copy(zero_buf, xs_hbm.at[pl.ds(b * blk, blk)], sem)
                    copy.start() if phase == "start" else copy.wait()

    for j in range(ids_ref.shape[2]):
        tok, _ = divmod(j, TOP_K)
        dst = pl.multiple_of(ids_ref[0, 0, j], ROW_SUB)
        pltpu.make_async_copy(x_ref.at[pl.ds(tok * ROW_SUB, ROW_SUB)], xs_hbm.at[pl.ds(dst, ROW_SUB)],
                              sem).start(priority=j % 2)
    for _ in range(TOP_K):
        pltpu.make_async_copy(x_ref, xs_hbm.at[pl.ds(0, x_ref.shape[0])], sem).wait()


def _dispatch(xn, y_ids, nvalid):
    tm = ROW_TILE
    t = xn.shape[0] // ROW_SUB
    n_blk = nvalid.shape[0]
    return pl.pallas_call(
        _dispatch_kernel,
        grid_spec=pltpu.PrefetchScalarGridSpec(
            num_scalar_prefetch=1,
            grid=(t // tm,),
            in_specs=[pl.BlockSpec((1, 1, TOP_K * tm), lambda i, nv: (i, 0, 0), memory_space=pltpu.SMEM),
                      pl.BlockSpec((tm * ROW_SUB, LANES), lambda i, nv: (i, 0))],
            out_specs=pl.BlockSpec(memory_space=pl.ANY),
            scratch_shapes=[pltpu.VMEM((MOE_BLOCK * ROW_SUB, LANES), F32), pltpu.SemaphoreType.DMA],
        ),
        out_shape=jax.ShapeDtypeStruct((n_blk * MOE_BLOCK * ROW_SUB, LANES), F32),
        compiler_params=pltpu.CompilerParams(dimension_semantics=("arbitrary",),
                                             vmem_limit_bytes=VMEM_LIMIT),
        name="moe_dispatch",
    )(nvalid, y_ids.reshape(t // tm, 1, TOP_K * tm), xn)


def _moe_kernel(be_ref, nv_ref, x_ref, wg_ref, wu_ref, wd_ref, y_ref, wg_bf, wu_bf, wd_bf):
    s = pl.program_id(0)
    nv = nv_ref[s]

    @pl.when((s == 0) | (be_ref[s] != be_ref[jnp.maximum(s - 1, 0)]))
    def _():
        wg_bf[...] = wg_ref[0].astype(BF16)
        wu_bf[...] = wu_ref[0].astype(BF16)
        wd_bf[...] = wd_ref[0].astype(BF16)

    @pl.when(nv > 0)
    def _():
        x = _rows_from_tiles(x_ref, (), MOE_BLOCK).astype(BF16)
        gate = _dot(x, wg_bf[...])
        up = _dot(x, wu_bf[...])
        hid = (gate * (1.0 / (1.0 + jnp.exp(-gate))) * up).astype(BF16)
        _rows_to_tiles(y_ref, (), _dot(hid, wd_bf[...]))

    @pl.when(nv == 0)
    def _():
        y_ref[...] = jnp.zeros_like(y_ref)


def _moe(xs, block_e, nvalid, p):
    d = xs.shape[1] * ROW_SUB
    n_blk = block_e.shape[0]
    layer = p["layer"]
    hdim = p["w_gate"].shape[3]
    wspec = lambda shape: pl.BlockSpec((None, 1) + shape, lambda s, be, nv: (layer, be[s], 0, 0))
    rows = pl.BlockSpec((MOE_BLOCK * ROW_SUB, LANES), lambda s, be, nv: (s, 0))
    return pl.pallas_call(
        _moe_kernel,
        grid_spec=pltpu.PrefetchScalarGridSpec(
            num_scalar_prefetch=2,
            grid=(n_blk,),
            in_specs=[rows, wspec((d, hdim)), wspec((d, hdim)), wspec((hdim, d))],
            out_specs=rows,
            scratch_shapes=[pltpu.VMEM((d, hdim), BF16), pltpu.VMEM((d, hdim), BF16), pltpu.VMEM((hdim, d), BF16)],
        ),
        out_shape=jax.ShapeDtypeStruct(xs.shape, F32),
        compiler_params=pltpu.CompilerParams(dimension_semantics=("arbitrary",),
                                             vmem_limit_bytes=VMEM_LIMIT),
        name="moe_experts",
    )(block_e, nvalid, xs, p["w_gate"], p["w_up"], p["w_down"])


def _route(route, counts, t):
    a = t * TOP_K
    expert = route[:, 2:4].astype(jnp.int32).reshape(a)
    pos = route[:, 4:6].astype(jnp.int32).reshape(a)
    counts = counts.astype(jnp.int32)
    padded = (counts + MOE_BLOCK - 1) // MOE_BLOCK * MOE_BLOCK
    pad_end = jnp.cumsum(padded)
    pad_start = pad_end - padded
    dest = pad_start[expert] + pos
    n_blk = (a + N_EXPERTS * MOE_BLOCK) // MOE_BLOCK
    starts = jnp.arange(n_blk, dtype=jnp.int32) * MOE_BLOCK
    block_e = jnp.minimum(jnp.sum((pad_end[None, :] <= starts[:, None]).astype(jnp.int32), axis=1), N_EXPERTS - 1)
    nvalid = jnp.clip((pad_start + counts)[block_e] - starts, 0, MOE_BLOCK)
    return block_e, nvalid, dest * ROW_SUB


def _final_kernel(h_ref, ids0_ref, idsn_ref, y_hbm, ew_ref, g_ref, o_ref, ybuf, ysem):
    h = _combine_experts(h_ref[...], ew_ref[...], y_hbm, ids0_ref, idsn_ref, ybuf, ysem,
                         pl.program_id(0), pl.num_programs(0))
    o_ref[0] = _rms(h, g_ref[...])


def _final(h, y2, y_ids, ew, g, batch, seq_len):
    t, d = h.shape
    per_seq = seq_len // BLOCK
    out_blocks = per_seq - 1
    row = lambda n: ((n // out_blocks) * per_seq + n % out_blocks + 1, 0)
    ids = y_ids.reshape(batch, per_seq, TOP_K * BLOCK)[:, 1:].reshape(batch * out_blocks, 1, TOP_K * BLOCK)
    ids = jnp.concatenate([ids, jnp.zeros_like(ids[:1])], axis=0)
    return pl.pallas_call(
        _final_kernel,
        grid=(batch * out_blocks,),
        in_specs=[pl.BlockSpec((BLOCK, d), row),
                  pl.BlockSpec((1, 1, TOP_K * BLOCK), lambda n: (0, 0, 0), memory_space=pltpu.SMEM),
                  pl.BlockSpec((1, 1, TOP_K * BLOCK), lambda n: (n + 1, 0, 0), memory_space=pltpu.SMEM),
                  pl.BlockSpec(memory_space=pl.ANY),
                  pl.BlockSpec((BLOCK, LANES), row),
                  pl.BlockSpec((1, d), lambda n: (0, 0))],
        out_specs=pl.BlockSpec((1, BLOCK, d), lambda n: (n // out_blocks, n % out_blocks, 0)),
        out_shape=jax.ShapeDtypeStruct((batch, seq_len - BLOCK, d), F32),
        scratch_shapes=[pltpu.VMEM((2, TOP_K, BLOCK * ROW_SUB, LANES), F32), pltpu.SemaphoreType.DMA((2,))],
        compiler_params=pltpu.CompilerParams(dimension_semantics=("arbitrary",),
                                             vmem_limit_bytes=VMEM_LIMIT),
        name="final_norm",
    )(h, ids, ids, y2, ew, g)


def _rope_table(seq_len):
    half = MLA_ROPE // 2
    pos = (jnp.arange(seq_len, dtype=jnp.int32) - PAD_FRONT).astype(F32)
    inv_freq = ROPE_THETA ** (-jnp.arange(half, dtype=F32) / half)
    ang = pos[:, None] * inv_freq[None, :]
    cos, sin = jnp.cos(ang), jnp.sin(ang)
    cos2 = jnp.concatenate([cos, cos], axis=1)
    sin2 = jnp.concatenate([-sin, sin], axis=1)
    z = lambda w: jnp.zeros((seq_len, w), F32)
    scale = (MLA_NOPE + MLA_ROPE) ** -0.5 * LOG2E
    cos_q = jnp.concatenate([jnp.ones((seq_len, MLA_NOPE), F32), cos2, z(32)], axis=1) * scale
    sin_q = jnp.concatenate([z(MLA_NOPE), sin2, z(32)], axis=1) * scale
    cos_k = jnp.concatenate([z(MLA_NOPE), cos2, z(32)], axis=1)
    sin_k = jnp.concatenate([z(MLA_NOPE), sin2, z(32)], axis=1)
    return jnp.concatenate([cos_q, sin_q, cos_k, sin_k], axis=1)


def _swap_halves(w):
    half = w.shape[-1] // 2
    return jnp.concatenate([w[..., half:], w[..., :half]], axis=-1)


def _layer_params(i, seq_len, attn_norm, w_in, b_forget, sinks, mla_q_norm, mla_kv_norm, mla_w_uq,
                  mla_w_ukv, w_out, ffn_norm, w_group, b_group, w_router, b_router, w_gate, w_up, w_down):
    d = w_in.shape[1]
    w = w_in[i]
    sizes = (256, 128, 128, 256, 256, 256, 4, 256, 128, 32, 256, 256, 256)
    offs = np.concatenate([[0], np.cumsum(sizes)])
    (a_q, a_k, a_v, f_q, f_k, f_v, f_g, c_q, c_kv, c_kr, s_q, s_k, s_v) = [
        w[:, offs[j]:offs[j + 1]] for j in range(len(sizes))]
    qscale = HEAD_DIM ** -0.5
    grp = SWA_HEADS // SWA_KV_HEADS
    a_q = a_q.reshape(d, SWA_KV_HEADS, grp, HEAD_DIM).transpose(0, 2, 1, 3).reshape(d, 256)
    z = lambda n: jnp.zeros((d, n), F32)
    g_grp = jnp.concatenate([f_g[:, 0:2], z(62), c_kr, z(32)], axis=1)
    gs_grp = jnp.concatenate([f_g[:, 2:4], z(62), _swap_halves(c_kr), z(32)], axis=1)
    w_perm = jnp.concatenate([a_q * qscale, a_k, a_v, f_q * qscale, f_k, f_v, s_q * qscale, s_k, s_v,
                              c_q, c_kv, g_grp, gs_grp], axis=1).astype(BF16)
    wuq = mla_w_uq[i].reshape(MLA_Q_LORA, 4, MLA_NOPE + MLA_ROPE)
    zq = lambda n: jnp.zeros((MLA_Q_LORA, 4, n), F32)
    w_uq_a = jnp.concatenate([wuq, zq(32)], axis=2).reshape(MLA_Q_LORA, 512).astype(BF16)
    w_uq_b = jnp.concatenate([zq(MLA_NOPE), _swap_halves(wuq[:, :, MLA_NOPE:]), zq(32)],
                             axis=2).reshape(MLA_Q_LORA, 512).astype(BF16)
    wukv = mla_w_ukv[i].reshape(MLA_KV_LORA, 4, MLA_NOPE + MLA_V)
    w_kv_k = jnp.concatenate([wukv[:, :, :MLA_NOPE], jnp.zeros((MLA_KV_LORA, 4, 64), F32)],
                             axis=2).reshape(MLA_KV_LORA, 512).astype(BF16)
    w_kv_v = wukv[:, :, MLA_NOPE:].reshape(MLA_KV_LORA, 256).astype(BF16)
    bf = b_forget[i].astype(F32)
    b_f = jnp.zeros((1, 256), F32).at[0, 0:2].set(bf[0:2]).at[0, 128:130].set(bf[2:4])
    wo = w_out[i]
    wo_a = wo[:256].reshape(SWA_KV_HEADS, grp, HEAD_DIM, d).transpose(1, 0, 2, 3).reshape(256, d)
    wo4 = jnp.concatenate([wo_a, wo[256:]], axis=0).reshape(4, 256, d).astype(BF16)
    w_r = jnp.concatenate([w_group[i], w_router[i], jnp.zeros((d, LANES - N_GROUPS - N_EXPERTS), F32)], axis=1)
    w_r_hi = w_r.astype(BF16)
    w_r_lo = (w_r - w_r_hi.astype(F32)).astype(BF16)
    b_r = jnp.concatenate([b_group[i], b_router[i], jnp.zeros((LANES - N_GROUPS - N_EXPERTS,), F32)])[None, :]
    return dict(
        attn_norm=attn_norm[i][None, :], w_in=w_perm, q_norm=mla_q_norm[i][None, :],
        kv_norm=mla_kv_norm[i][None, :], w_uq_a=w_uq_a, w_uq_b=w_uq_b, w_kv_k=w_kv_k, w_kv_v=w_kv_v,
        rope_tab=_rope_table(seq_len), b_forget=b_f, sinks=sinks[i].astype(F32), w_out=wo4,
        ffn_norm=ffn_norm[i][None, :], w_r_hi=w_r_hi, w_r_lo=w_r_lo, b_r=b_r.astype(F32),
        w_gate=w_gate, w_up=w_up, w_down=w_down, layer=i)


def kernel(x, meta_tokens, attn_norm, w_in, b_forget, sinks, mla_q_norm, mla_kv_norm, mla_w_uq, mla_w_ukv,
           w_out, ffn_norm, w_group, b_group, w_router, b_router, w_gate, w_up, w_down, final_norm):
    batch, seq, d = x.shape
    seq_len = seq + BLOCK
    assert seq_len % ROW_TILE == 0 and seq_len % (BLOCK * SWA_Q_BLOCKS) == 0
    assert all(seq % tile == 0 for tile in Q_TILES.values())
    t = batch * seq_len
    depth = w_in.shape[0]
    lead = jnp.concatenate([jnp.zeros((PAD_FRONT, d), x.dtype), meta_tokens.astype(x.dtype)], axis=0)
    h = (lead, x)
    y2 = y_ids = ew = None
    for i in range(depth):
        p = _layer_params(i, seq_len, attn_norm, w_in, b_forget, sinks, mla_q_norm, mla_kv_norm, mla_w_uq,
                          mla_w_ukv, w_out, ffn_norm, w_group, b_group, w_router, b_router, w_gate, w_up, w_down)
        h, (qa, fq, fk, fv, cq, ck, cv, sq, sk, sv) = _inproj(h, y2, y_ids, ew, p, seq_len)
        y_a = _swa_attention(qa, p["sinks"], batch, seq_len)
        y_b = _causal_attention("fox", fq, fk, fv, batch, seq_len)
        y_c = _causal_attention("mla", cq, ck, cv, batch, seq_len)
        y_d = _causal_attention("sb", sq, sk, sv, batch, seq_len)
        ys = [y.reshape(t, 256) for y in (y_a, y_b, y_c, y_d)]
        h, xn, ew, counts = _outproj(ys, h, p)
        block_e, nvalid, y_ids = _route(ew, counts[0, N_GROUPS:N_GROUPS + N_EXPERTS], t)
        y2 = _moe(_dispatch(xn, y_ids, nvalid), block_e, nvalid, p)
    return _final(h, y2, y_ids, ew, final_norm[None, :], batch, seq_len)
```

```python
import functools

import jax
import jax.numpy as jnp
import numpy as np
from jax import lax
from jax.experimental import pallas as pl
from jax.experimental.pallas import tpu as pltpu

F32 = jnp.float32
BF16 = jnp.bfloat16

BLOCK = 128
N_META = 16
PAD_FRONT = BLOCK - N_META
CHUNK_SHIFT = 6
HEAD_DIM = 64
NORM_EPS = 1e-6
NEG = -1e30
PAD_KEY_LOGIT = -(2.0 ** 100)
UNDERFLOW_LOG2 = -150.0
LOG2E = 1.4426950408889634
BIG = 1 << 30
SWA_HEADS, SWA_KV_HEADS, WINDOW = 4, 2, 128
MLA_Q_LORA, MLA_KV_LORA, MLA_NOPE, MLA_ROPE, MLA_V = 256, 128, 64, 32, 64
MLA_BIAS_LANE = MLA_NOPE + MLA_ROPE
ROPE_THETA = 10000.0
N_GROUPS, EXPERTS_PER_GROUP, TOP_K = 4, 8, 2
N_EXPERTS = N_GROUPS * EXPERTS_PER_GROUP
MOE_BLOCK = 512
LANES = 128
ROW_SUB = 8
ROW_TILE = 384
Q_TILES = {"fox": 1024, "mla": 1024, "sb": 512}
K_TILE = 256
ROW_PART = 128
SWA_Q_BLOCKS = 3
VMEM_LIMIT = 56 * 1024 * 1024

C_A, C_B, C_D, C_CQ, C_CKV, C_G, C_GS, C_END = 0, 512, 1280, 2048, 2304, 2432, 2560, 2688
B_F0, B_F1, B_PAD = 0, 3, 6


def _rms(x, g):
    return x * lax.rsqrt(jnp.mean(x * x, axis=-1, keepdims=True) + NORM_EPS) * g


def _log_sigmoid(x):
    return jnp.minimum(x, 0.0) - jnp.log(1.0 + jnp.exp(-jnp.abs(x)))


def _dot(a, b):
    return jnp.dot(a, b, preferred_element_type=F32)


def _dot_nt(a, b):
    return lax.dot_general(a, b, (((1,), (1,)), ((), ())), preferred_element_type=F32)


def _rows_from_tiles(ref, lead, n):
    return jnp.concatenate([ref[(*lead, pl.ds(j, n, stride=ROW_SUB), slice(None))] for j in range(ROW_SUB)], axis=1)


def _rows_to_tiles(ref, lead, x):
    n = x.shape[0]
    for j in range(ROW_SUB):
        ref[(*lead, pl.ds(j, n, stride=ROW_SUB), slice(None))] = x[:, j * LANES:(j + 1) * LANES]


def _tile4(x):
    return jnp.concatenate([x, x, x, x], axis=1)


def _split3(x):
    hi = x.astype(BF16)
    r1 = x - hi.astype(F32)
    mid = r1.astype(BF16)
    lo = (r1 - mid.astype(F32)).astype(BF16)
    return hi, mid, lo


def _free_base(head):
    return head * LANES + (HEAD_DIM if head % 2 == 0 else 0)


def _expert_rows_start(y_hbm, ids_ref, buf, sem, slot):
    for j in range(ids_ref.shape[2]):
        tok, k = divmod(j, TOP_K)
        src = pl.multiple_of(ids_ref[0, 0, j], ROW_SUB)
        pltpu.make_async_copy(y_hbm.at[pl.ds(src, ROW_SUB)], buf.at[slot, k, pl.ds(tok * ROW_SUB, ROW_SUB)],
                              sem.at[slot]).start(priority=1)


def _expert_rows_wait(y_hbm, buf, sem, slot):
    for k in range(TOP_K):
        pltpu.make_async_copy(y_hbm.at[pl.ds(0, buf.shape[2])], buf.at[slot, k], sem.at[slot]).wait()


def _combine_experts(h, ew, y_hbm, ids0_ref, idsn_ref, buf, sem, step, n_steps):
    slot = step % 2

    @pl.when(step == 0)
    def _():
        _expert_rows_start(y_hbm, ids0_ref, buf, sem, 0)

    _expert_rows_start(y_hbm, idsn_ref, buf, sem, 1 - slot)
    _expert_rows_wait(y_hbm, buf, sem, slot)
    n = h.shape[0]
    out = h + ew[:, 0:1] * _rows_from_tiles(buf, (slot, 0), n) + ew[:, 1:2] * _rows_from_tiles(buf, (slot, 1), n)

    @pl.when(step == n_steps - 1)
    def _():
        _expert_rows_wait(y_hbm, buf, sem, 1 - slot)

    return out


def _inproj_kernel(has_y2, n_seq_tiles, *refs):
    if has_y2:
        (h_ref, ids0_ref, idsn_ref, y_hbm, ew_ref, *rest) = refs
    else:
        (lead_ref, *x_refs) = refs[:1 + ROW_TILE // BLOCK]
        rest = refs[1 + ROW_TILE // BLOCK:]
    (g_ref, w_ref, qn_ref, kvn_ref, wuqa_ref, wuqb_ref, wkvk_ref, wkvv_ref, tab_ref, bf_ref,
     pq_ref, pk_ref, rows_ref, hout_ref, *outs) = rest
    if has_y2:
        *outs, ybuf, ysem = outs
    (qa_ref, fq_ref, fk_ref, fv_ref, cq_ref, ck_ref, cv_ref, sq_ref, sk_ref, sv_ref, carry_ref) = outs
    tile = pl.program_id(0) % n_seq_tiles
    if has_y2:
        h = _combine_experts(h_ref[...], ew_ref[...], y_hbm, ids0_ref, idsn_ref, ybuf, ysem,
                             pl.program_id(0), pl.num_programs(0))
    else:
        first = jnp.where(tile == 0, lead_ref[...], x_refs[0][0])
        h = jnp.concatenate([first] + [r[0] for r in x_refs[1:]], axis=0)
    hout_ref[...] = h
    tm, d = h.shape
    xn = _rms(h, g_ref[...]).astype(BF16)
    acc = _dot(xn, w_ref[...])
    lane = lax.broadcasted_iota(jnp.int32, (1, LANES), 1)
    lo_half = lane < HEAD_DIM
    pad_col = jnp.where(tile * tm + lax.broadcasted_iota(jnp.int32, (tm, 1), 0) < PAD_FRONT,
                        PAD_KEY_LOGIT, 0.0)
    rows = rows_ref[...]
    fq_one, fk_one, pad_lane, sq_one, mla_one, mla_pad = (rows[j:j + 1] for j in range(6))

    def per_head(x_pair, bias, pair, scale=None):
        x = x_pair if scale is None else x_pair * scale
        even = jnp.where(lo_half, x, bias[:, (2 * pair) * LANES:(2 * pair + 1) * LANES])
        odd = jnp.where(lo_half, bias[:, (2 * pair + 1) * LANES:(2 * pair + 2) * LANES], x)
        return even, odd

    def store_heads(ref, x_off, bias, scale=None):
        for pair in range(2):
            x_pair = acc[:, x_off + pair * LANES:x_off + (pair + 1) * LANES]
            even, odd = per_head(x_pair, bias, pair, scale)
            ref[:, (2 * pair) * LANES:(2 * pair + 1) * LANES] = even.astype(BF16)
            ref[:, (2 * pair + 1) * LANES:(2 * pair + 2) * LANES] = odd.astype(BF16)

    qa_ref[...] = acc[:, C_A:C_B].astype(BF16)

    @pl.when(tile == 0)
    def _():
        carry_ref[...] = jnp.zeros_like(carry_ref)

    lf = _log_sigmoid(acc[:, C_G:C_END] + bf_ref[...]) * LOG2E
    r = lax.broadcasted_iota(jnp.int32, (BLOCK, BLOCK), 0)
    c = lax.broadcasted_iota(jnp.int32, (BLOCK, BLOCK), 1)
    tri = jnp.where(c <= r, 1.0, 0.0).astype(BF16)
    carry = carry_ref[...]
    blocks = []
    for b in range(tm // BLOCK):
        hi, mid, lo = _split3(lf[b * BLOCK:(b + 1) * BLOCK])
        y = _dot(tri, hi) + _dot(tri, mid) + _dot(tri, lo) + carry
        carry = y[BLOCK - 1:BLOCK, :]
        blocks.append(y)
    carry_ref[...] = carry
    f_hi, f_mid, f_lo = _split3(jnp.concatenate(blocks, axis=0))
    q_bias = _dot(f_hi, pq_ref[0]) + _dot(f_mid, pq_ref[1]) + _dot(f_lo, pq_ref[2]) + fq_one
    k_bias = (_dot(f_hi, pk_ref[0]) + _dot(f_mid, pk_ref[1]) + _dot(f_lo, pk_ref[2]) + fk_one
              + pad_col * pad_lane)
    ones = jnp.ones((tm, 4 * LANES), F32)
    store_heads(fq_ref, C_B, q_bias, LOG2E)
    store_heads(fk_ref, C_B + 256, k_bias)
    store_heads(fv_ref, C_B + 512, ones)

    store_heads(sq_ref, C_D, jnp.broadcast_to(sq_one, (tm, 4 * LANES)), LOG2E)
    store_heads(sk_ref, C_D + 256, pad_col * pad_lane)
    sv_ref[...] = acc[:, C_D + 512:C_CQ].astype(BF16)

    tab = tab_ref[...]
    cos_q, sin_q = tab[:, 0:128], tab[:, 128:256]
    cos_k, sin_k = tab[:, 256:384], tab[:, 384:512]
    cqn = _rms(acc[:, C_CQ:C_CKV], qn_ref[...]).astype(BF16)
    q_lin = _dot(cqn, wuqa_ref[...])
    q_swp = _dot(cqn, wuqb_ref[...])
    cq_ref[...] = (q_lin * _tile4(cos_q) + q_swp * _tile4(sin_q) + mla_one).astype(BF16)
    ckvn = _rms(acc[:, C_CKV:C_G], kvn_ref[...]).astype(BF16)
    k_nope = _dot(ckvn, wkvk_ref[...])
    grp, grp_s = acc[:, C_G:C_GS], acc[:, C_GS:C_END]
    k_rope = grp * cos_k + grp_s * sin_k
    ck_ref[...] = (k_nope + _tile4(k_rope) + pad_col * mla_pad).astype(BF16)
    vv = _dot(ckvn, wkvv_ref[...])
    for pair in range(2):
        even, odd = per_head(vv[:, pair * LANES:(pair + 1) * LANES], ones, pair)
        cv_ref[:, (2 * pair) * LANES:(2 * pair + 1) * LANES] = even.astype(BF16)
        cv_ref[:, (2 * pair + 1) * LANES:(2 * pair + 2) * LANES] = odd.astype(BF16)


def _bias_constants():
    src = (0, 1, LANES, LANES + 1)
    pq = np.zeros((3, 2 * LANES, 4 * LANES), np.float32)
    pk = np.zeros((3, 2 * LANES, 4 * LANES), np.float32)
    rows = np.zeros((8, 4 * LANES), np.float32)
    for head in range(4):
        base = _free_base(head)
        for part in range(3):
            pq[part, src[head], base + B_F0 + part] = 1.0
            pk[part, src[head], base + B_F1 + part] = -1.0
            rows[0, base + B_F1 + part] = 1.0
            rows[1, base + B_F0 + part] = 1.0
        rows[0, base + B_PAD] = 1.0
        rows[2, base + B_PAD] = 1.0
        rows[3, base + B_PAD] = 1.0
        rows[4, head * LANES + MLA_BIAS_LANE] = 1.0
        rows[5, head * LANES + MLA_BIAS_LANE] = 1.0
    return jnp.asarray(pq, BF16), jnp.asarray(pk, BF16), jnp.asarray(rows, F32)


def _inproj(h, y2, y_ids, ew, p, seq_len):
    tm = ROW_TILE
    n_seq_tiles = seq_len // tm
    has_y2 = y2 is not None
    row = lambda i: (i, 0)
    fixed = lambda i: (0, 0)
    if has_y2:
        t, d = h.shape
        in_specs = [pl.BlockSpec((tm, d), row)]
        args = [h]
    else:
        lead, x = h
        batch, seq, d = x.shape
        t = batch * seq_len
        per_tile = tm // BLOCK
        frame = lambda k: pl.BlockSpec((1, BLOCK, d), lambda i: (
            i // n_seq_tiles, jnp.maximum((i % n_seq_tiles) * per_tile - 1 + k, 0), 0))
        in_specs = [pl.BlockSpec((BLOCK, d), fixed)] + [frame(k) for k in range(per_tile)]
        args = [lead] + [x] * per_tile
    if has_y2:
        ids = y_ids.reshape(t // tm, 1, TOP_K * tm)
        ids = jnp.concatenate([ids, jnp.zeros_like(ids[:1])], axis=0)
        in_specs += [pl.BlockSpec((1, 1, TOP_K * tm), lambda i: (0, 0, 0), memory_space=pltpu.SMEM),
                     pl.BlockSpec((1, 1, TOP_K * tm), lambda i: (i + 1, 0, 0), memory_space=pltpu.SMEM),
                     pl.BlockSpec(memory_space=pl.ANY),
                     pl.BlockSpec((tm, LANES), row)]
        args += [ids, ids, y2, ew]
    pq, pk, rows = _bias_constants()
    consts = [p["attn_norm"], p["w_in"], p["q_norm"], p["kv_norm"], p["w_uq_a"], p["w_uq_b"],
              p["w_kv_k"], p["w_kv_v"]]
    in_specs += [pl.BlockSpec(c.shape, fixed) for c in consts]
    args += consts
    in_specs.append(pl.BlockSpec((tm, 512), lambda i: (i % n_seq_tiles, 0)))
    args.append(p["rope_tab"])
    in_specs += [pl.BlockSpec((1, 256), fixed),
                 pl.BlockSpec(pq.shape, lambda i: (0, 0, 0)),
                 pl.BlockSpec(pk.shape, lambda i: (0, 0, 0)),
                 pl.BlockSpec(rows.shape, fixed)]
    args += [p["b_forget"], pq, pk, rows]
    widths = [512] * 9 + [256]
    out_shape = [jax.ShapeDtypeStruct((t, d), F32)] + [jax.ShapeDtypeStruct((t, w), BF16) for w in widths]
    out_specs = [pl.BlockSpec((tm, d), row)] + [pl.BlockSpec((tm, w), row) for w in widths]
    outs = pl.pallas_call(
        functools.partial(_inproj_kernel, has_y2, n_seq_tiles),
        grid=(t // tm,),
        in_specs=in_specs,
        out_specs=out_specs,
        out_shape=out_shape,
        scratch_shapes=[pltpu.VMEM((1, 2 * LANES), F32)] + (
            [pltpu.VMEM((2, TOP_K, tm * ROW_SUB, LANES), F32), pltpu.SemaphoreType.DMA((2,))] if has_y2 else []),
        compiler_params=pltpu.CompilerParams(dimension_semantics=("arbitrary",),
                                             vmem_limit_bytes=VMEM_LIMIT),
        name="inproj_y2" if has_y2 else "inproj",
    )(*args)
    return outs[0], outs[1:]


def _swa_kernel(sink_ref, q_ref, km_ref, kp_ref, kc_ref, vm_ref, vp_ref, vc_ref, o_ref):
    i = pl.program_id(1)
    n_sub = q_ref.shape[1] // BLOCK
    lane = lax.broadcasted_iota(jnp.int32, (1, LANES), 1)
    lo_half = lane < HEAD_DIM
    half_masks = [jnp.where(lo_half, 1.0, 0.0).astype(BF16), jnp.where(lo_half, 0.0, 1.0).astype(BF16)]
    row = lax.broadcasted_iota(jnp.int32, (BLOCK, 1), 0)
    col = lax.broadcasted_iota(jnp.int32, (1, BLOCK), 1)
    grp = SWA_HEADS // SWA_KV_HEADS
    k_all = jnp.concatenate([km_ref[0], kp_ref[0], kc_ref[0]], axis=0)
    v_all = jnp.concatenate([vm_ref[0], vp_ref[0], vc_ref[0]], axis=0)
    for j in range(n_sub):
        q0 = (i * n_sub + j) * BLOCK
        pq = q0 + row
        cq = pq >> CHUNK_SHIFT
        segs = []
        vis_m = col >= PAD_FRONT
        segs.append((vis_m, jnp.minimum(jnp.abs(pq - col), WINDOW).astype(F32)))
        for pk in (q0 - BLOCK + col, q0 + col):
            ck = jnp.where(pk >= BLOCK, pk >> CHUNK_SHIFT, BIG)
            vis = (ck <= cq) & (ck >= cq - (WINDOW >> CHUNK_SHIFT))
            segs.append((vis, jnp.abs(pq - pk).astype(F32)))
        kj = jnp.concatenate([k_all[0:BLOCK], k_all[(j + 1) * BLOCK:(j + 3) * BLOCK]], axis=0)
        vj = jnp.concatenate([v_all[0:BLOCK], v_all[(j + 1) * BLOCK:(j + 3) * BLOCK]], axis=0)
        for g in range(grp):
            qg = q_ref[0, j * BLOCK:(j + 1) * BLOCK, g * LANES:(g + 1) * LANES]
            out_g = None
            for hk in range(SWA_KV_HEADS):
                head = hk * grp + g
                slope = 2.0 ** (-8.0 * (head + 1) / SWA_HEADS)
                sink = sink_ref[head]
                s_all = _dot_nt(qg * half_masks[hk], kj)
                tiles = [jnp.where(vis, s_all[:, n * LANES:(n + 1) * LANES] - slope * dist, NEG)
                         for n, (vis, dist) in enumerate(segs)]
                top = jnp.maximum(jnp.maximum(tiles[0], tiles[1]), tiles[2])
                m = jnp.broadcast_to(jnp.maximum(jnp.max(top, axis=-1, keepdims=True), sink), (BLOCK, LANES))
                e = [jnp.exp(x - m) for x in tiles]
                den = jnp.sum(e[0] + e[1] + e[2], axis=-1, keepdims=True) + jnp.exp(sink - m[:, 0:1])
                o = _dot(jnp.concatenate(e, axis=1).astype(BF16), vj) * (1.0 / den)
                out_g = o if hk == 0 else jnp.where(lo_half, out_g, o)
            o_ref[0, j * BLOCK:(j + 1) * BLOCK, g * LANES:(g + 1) * LANES] = out_g.astype(BF16)


def _swa_attention(qa, sinks, batch, seq_len):
    x = qa.reshape(batch, seq_len, 512)
    n_sub = SWA_Q_BLOCKS
    nb = seq_len // (BLOCK * n_sub)
    blk = lambda f: pl.BlockSpec((1, BLOCK, LANES), f)
    own = lambda c: pl.BlockSpec((1, BLOCK * n_sub, LANES), lambda b, i: (b, i, c))
    return pl.pallas_call(
        _swa_kernel,
        grid=(batch, nb),
        in_specs=[
            pl.BlockSpec(memory_space=pltpu.SMEM),
            pl.BlockSpec((1, BLOCK * n_sub, 2 * LANES), lambda b, i: (b, i, 0)),
            blk(lambda b, i: (b, 0, 2)),
            blk(lambda b, i: (b, jnp.maximum(i * n_sub - 1, 0), 2)),
            own(2),
            blk(lambda b, i: (b, 0, 3)),
            blk(lambda b, i: (b, jnp.maximum(i * n_sub - 1, 0), 3)),
            own(3),
        ],
        out_specs=pl.BlockSpec((1, BLOCK * n_sub, 2 * LANES), lambda b, i: (b, i, 0)),
        out_shape=jax.ShapeDtypeStruct((batch, seq_len, 2 * LANES), BF16),
        compiler_params=pltpu.CompilerParams(dimension_semantics=("arbitrary", "arbitrary"),
                                             vmem_limit_bytes=VMEM_LIMIT),
        name="swa_attention",
    )(sinks, x, x, x, x, x, x, x)


def _causal_kernel(mode, q_ref, k_ref, v_ref, o_ref, stat_ref, acc_ref):
    seq_len = q_ref.shape[1]
    Q_TILE = stat_ref.shape[1]
    n_qt = (seq_len - BLOCK) // Q_TILE
    per_tile = Q_TILE // K_TILE
    lane = lax.broadcasted_iota(jnp.int32, (1, LANES), 1)
    lo_half = lane < HEAD_DIM
    if mode == "sb":
        r = lax.broadcasted_iota(jnp.int32, (K_TILE, K_TILE), 0)
        c = lax.broadcasted_iota(jnp.int32, (K_TILE, K_TILE), 1)
        later2 = jnp.where(r > c, 1.0, 0.0).astype(BF16)
        later1 = later2[:BLOCK, :BLOCK]

    def causal(pq, k0, tk):
        pk = k0 + lax.broadcasted_iota(jnp.int32, (1, tk), 1)
        if mode == "fox":
            return pk <= pq
        if mode == "mla":
            return (pk >> CHUNK_SHIFT) <= (pq >> CHUNK_SHIFT)
        return pk < pq

    def head_v(k0, tk, hh):
        if mode == "sb":
            return v_ref[0, pl.ds(k0, tk), :]
        return v_ref[0, pl.ds(k0, tk), hh * LANES:(hh + 1) * LANES]

    def lane_tiles(x):
        return [x[:, j * LANES:(j + 1) * LANES] for j in range(x.shape[1] // LANES)]

    def row_parts(lo, hi):
        step = min(hi - lo, ROW_PART)
        return [(r0, step) for r0 in range(lo, hi, step)]

    def softmax_chunk(q0, tq, k0, tk, masked, first, row_lo=0):
        for hh in range(2):
            qh = q_ref[0, pl.ds(q0 + row_lo, tq - row_lo), hh * LANES:(hh + 1) * LANES]
            kh = k_ref[0, pl.ds(k0, tk), hh * LANES:(hh + 1) * LANES]
            s_all = _dot_nt(qh, kh)
            vh = head_v(k0, tk, hh)
            for r0, tr in row_parts(row_lo, tq):
                s = s_all[r0 - row_lo:r0 - row_lo + tr]
                if masked and r0 < row_lo + tk:
                    pq = q0 + r0 + lax.broadcasted_iota(jnp.int32, (tr, 1), 0)
                    s = jnp.where(causal(pq, k0, tk), s, NEG)
                tiles = lane_tiles(s)
                top = tiles[0]
                for x in tiles[1:]:
                    top = jnp.maximum(top, x)
                m_new = jnp.broadcast_to(jnp.max(top, axis=-1, keepdims=True), (tr, LANES))
                if not first:
                    m_old = stat_ref[hh, r0:r0 + tr, :]
                    m_new = jnp.maximum(m_old, m_new)
                p = jnp.concatenate([jnp.exp2(x - m_new) for x in tiles], axis=1).astype(BF16)
                pv = _dot(p, vh)
                if not first:
                    pv = jnp.exp2(m_old - m_new) * acc_ref[hh, r0:r0 + tr, :] + pv
                stat_ref[hh, r0:r0 + tr, :] = m_new
                acc_ref[hh, r0:r0 + tr, :] = pv

    def stick_chunk(q0, rows, k0, tk, masked, first):
        lo_r, hi_r = rows
        tq = hi_r - lo_r
        later = later2 if tk == K_TILE else later1
        for hh in range(2):
            qh = q_ref[0, pl.ds(q0 + lo_r, tq), hh * LANES:(hh + 1) * LANES]
            kh = k_ref[0, pl.ds(k0, tk), hh * LANES:(hh + 1) * LANES]
            z = _dot_nt(qh, kh)
            ls_pos = jnp.minimum(z, 0.0) - jnp.log(1.0 + jnp.exp2(-jnp.abs(z))) * LOG2E
            log_keep = ls_pos - z
            if masked:
                pq = q0 + lo_r + lax.broadcasted_iota(jnp.int32, (tq, 1), 0)
                vis = causal(pq, k0, tk)
                log_keep = jnp.where(vis, log_keep, 0.0)
            after = _dot(log_keep.astype(BF16), later)
            tot = ls_pos + after
            chunk_total = jnp.broadcast_to(after[:, 0:1] + log_keep[:, 0:1], (tq, LANES))
            if not first:
                carry = stat_ref[hh, lo_r:hi_r, :]
                tot = jnp.concatenate([x + carry for x in lane_tiles(tot)], axis=1)
                chunk_total = carry + chunk_total
            a = jnp.exp2(tot)
            if masked:
                a = jnp.where(vis, a, 0.0)
            pv = _dot(a.astype(BF16), head_v(k0, tk, hh))
            if not first:
                pv = acc_ref[hh, lo_r:hi_r, :] + pv
            stat_ref[hh, lo_r:hi_r, :] = chunk_total
            acc_ref[hh, lo_r:hi_r, :] = pv

    def finish(q0, tq):
        a0, a1 = acc_ref[0, 0:tq, :], acc_ref[1, 0:tq, :]
        if mode != "sb":
            a0 = a0 / a0[:, HEAD_DIM:HEAD_DIM + 1]
            a1 = a1 / a1[:, 0:1]
        o_ref[0, pl.ds(q0, tq), :] = jnp.where(lo_half, a0, a1).astype(BF16)

    def chunk_start(j):
        return pl.multiple_of(BLOCK + j * K_TILE, BLOCK)

    if mode == "sb":
        stick_chunk(0, (0, BLOCK), 0, BLOCK, True, True)
        finish(0, BLOCK)
        whole = (0, Q_TILE)

        def q_body(i, _):
            q0 = pl.multiple_of(BLOCK + i * Q_TILE, BLOCK)
            n_int = i * per_tile
            for d in reversed(range(per_tile)):
                stick_chunk(q0, whole, chunk_start(n_int + d), K_TILE, True, d == per_tile - 1)

            def alive():
                top = jnp.maximum(jnp.max(stat_ref[0]), jnp.max(stat_ref[1]))
                return (top > UNDERFLOW_LOG2).astype(jnp.int32)

            def body(st):
                jj, _ = st
                stick_chunk(q0, whole, chunk_start(n_int - 1 - jj), K_TILE, False, False)
                return jj + 1, alive()

            _, go = lax.while_loop(lambda st: (st[0] < n_int) & (st[1] > 0), body, (0, alive()))

            @pl.when(go > 0)
            def _():
                stick_chunk(q0, whole, 0, BLOCK, False, False)

            finish(q0, Q_TILE)
            return 0
    else:
        softmax_chunk(0, BLOCK, 0, BLOCK, True, True)
        finish(0, BLOCK)

        def q_body(i, _):
            q0 = pl.multiple_of(BLOCK + i * Q_TILE, BLOCK)
            n_int = i * per_tile
            stat_ref[...] = jnp.full(stat_ref.shape, NEG, F32)
            acc_ref[...] = jnp.zeros(acc_ref.shape, F32)

            def body(j, _):
                for d in range(per_tile):
                    softmax_chunk(q0, Q_TILE, chunk_start(j * per_tile + d), K_TILE, False, False)
                return 0

            lax.fori_loop(0, i, body, 0)
            for d in range(per_tile):
                softmax_chunk(q0, Q_TILE, chunk_start(n_int + d), K_TILE, True, False, row_lo=d * K_TILE)
            softmax_chunk(q0, Q_TILE, 0, BLOCK, False, False)
            finish(q0, Q_TILE)
            return 0

    lax.fori_loop(0, n_qt, q_body, 0)


def _causal_attention(mode, q, k, v, batch, seq_len):
    wide = pl.BlockSpec((1, seq_len, 2 * LANES), lambda b, p: (b, 0, p))
    narrow = pl.BlockSpec((1, seq_len, LANES), lambda b, p: (b, 0, p))
    args = [q.reshape(batch, seq_len, 512), k.reshape(batch, seq_len, 512),
            v.reshape(batch, seq_len, v.shape[1])]
    return pl.pallas_call(
        functools.partial(_causal_kernel, mode),
        grid=(batch, 2),
        in_specs=[wide, wide, narrow if mode == "sb" else wide],
        out_specs=narrow,
        out_shape=jax.ShapeDtypeStruct((batch, seq_len, 2 * LANES), BF16),
        scratch_shapes=[pltpu.VMEM((2, Q_TILES[mode], LANES), F32), pltpu.VMEM((2, Q_TILES[mode], LANES), F32)],
        compiler_params=pltpu.CompilerParams(dimension_semantics=("arbitrary", "arbitrary"),
                                             vmem_limit_bytes=VMEM_LIMIT),
        name=mode + "_attention",
    )(*args)


def _outproj_kernel(ya_ref, yb_ref, yc_ref, yd_ref, h_ref, wo_ref, g_ref, wrh_ref, wrl_ref, br_ref, tri_ref,
                    h2_ref, xn_ref, route_ref, cnt_ref):
    o = (_dot(ya_ref[...], wo_ref[0]) + _dot(yb_ref[...], wo_ref[1])
         + _dot(yc_ref[...], wo_ref[2]) + _dot(yd_ref[...], wo_ref[3]))
    h2 = h_ref[...] + o
    h2_ref[...] = h2
    xn = _rms(h2, g_ref[...])
    _rows_to_tiles(xn_ref, (), xn)
    xh = xn.astype(BF16)
    xl = (xn - xh.astype(F32)).astype(BF16)
    wrh, wrl = wrh_ref[...], wrl_ref[...]
    lg = _dot(xh, wrh) + _dot(xl, wrh) + _dot(xh, wrl) + br_ref[...]

    tm = lg.shape[0]
    lane = lax.broadcasted_iota(jnp.int32, (tm, LANES), 1)
    ninf = -jnp.inf

    def first_max(x):
        top = jnp.max(x, axis=-1, keepdims=True)
        return top, jnp.min(jnp.where(x == top, lane, LANES), axis=-1, keepdims=True)

    gl = jnp.where(lane < N_GROUPS, lg, ninf)
    g_max, g_top = first_max(gl)
    g_w = 1.0 / jnp.sum(jnp.exp(gl - g_max), axis=-1, keepdims=True)
    e_lo = N_GROUPS + g_top * EXPERTS_PER_GROUP
    el = jnp.where((lane >= e_lo) & (lane < e_lo + EXPERTS_PER_GROUP), lg, ninf)
    v1, i1 = first_max(el)
    v2, i2 = first_max(jnp.where(lane == i1, ninf, el))
    r21 = jnp.exp(v2 - v1)
    w1 = g_w / (1.0 + r21)
    w2 = w1 * r21

    @pl.when(pl.program_id(0) == 0)
    def _():
        cnt_ref[...] = jnp.zeros_like(cnt_ref)

    m1 = jnp.where(lane == i1, 1.0, 0.0)
    m2 = jnp.where(lane == i2, 1.0, 0.0)
    both = m1 + m2
    before = _dot(tri_ref[...], both.astype(BF16)) + cnt_ref[0:1, :]
    rank1 = jnp.sum(m1 * before, axis=-1, keepdims=True)
    rank2 = jnp.sum(m2 * before, axis=-1, keepdims=True)
    cnt_ref[...] = cnt_ref[...] + jnp.sum(both, axis=0, keepdims=True)
    cols = [w1, w2, (i1 - N_GROUPS).astype(F32), (i2 - N_GROUPS).astype(F32), rank1, rank2]
    route = jnp.zeros((tm, LANES), F32)
    for j, c in enumerate(cols):
        route = jnp.where(lane == j, c, route)
    route_ref[...] = route


def _outproj(ys, h, p):
    t, d = h.shape
    tm = ROW_TILE
    row = lambda i: (i, 0)
    fixed2 = lambda i: (0, 0)
    in_specs = [pl.BlockSpec((tm, 256), row)] * 4 + [
        pl.BlockSpec((tm, d), row),
        pl.BlockSpec((4, 256, d), lambda i: (0, 0, 0)),
        pl.BlockSpec((1, d), fixed2),
        pl.BlockSpec((d, LANES), fixed2),
        pl.BlockSpec((d, LANES), fixed2),
        pl.BlockSpec((1, LANES), fixed2),
        pl.BlockSpec((tm, tm), fixed2),
    ]
    earlier = jnp.asarray(np.tril(np.ones((tm, tm), np.float32), -1), BF16)
    return pl.pallas_call(
        _outproj_kernel,
        grid=(t // tm,),
        in_specs=in_specs,
        out_specs=[pl.BlockSpec((tm, d), row), pl.BlockSpec((tm * ROW_SUB, LANES), row),
                   pl.BlockSpec((tm, LANES), row), pl.BlockSpec((8, LANES), fixed2)],
        out_shape=[jax.ShapeDtypeStruct((t, d), F32), jax.ShapeDtypeStruct((t * ROW_SUB, LANES), F32),
                   jax.ShapeDtypeStruct((t, LANES), F32), jax.ShapeDtypeStruct((8, LANES), F32)],
        compiler_params=pltpu.CompilerParams(dimension_semantics=("arbitrary",),
                                             vmem_limit_bytes=VMEM_LIMIT),
        name="outproj_router",
    )(*ys, h, p["w_out"], p["ffn_norm"], p["w_r_hi"], p["w_r_lo"], p["b_r"], earlier)


def _dispatch_kernel(nv_ref, ids_ref, x_ref, xs_hbm, zero_buf, sem):
    blk = zero_buf.shape[0]

    @pl.when(pl.program_id(0) == 0)
    def _():
        zero_buf[...] = jnp.zeros_like(zero_buf)
        for phase in ("start", "wait"):
            for b in range(nv_ref.shape[0]):
                @pl.when(nv_ref[b] < MOE_BLOCK)
                def _():
                    copy = pltpu.make_async_copy(zero_buf, xs_hbm.at[pl.ds(b * blk, blk)], sem)
                    copy.start() if phase == "start" else copy.wait()

    for j in range(ids_ref.shape[2]):
        tok, _ = divmod(j, TOP_K)
        dst = pl.multiple_of(ids_ref[0, 0, j], ROW_SUB)
        pltpu.make_async_copy(x_ref.at[pl.ds(tok * ROW_SUB, ROW_SUB)], xs_hbm.at[pl.ds(dst, ROW_SUB)],
                              sem).start(priority=j % 2)
    for _ in range(TOP_K):
        pltpu.make_async_copy(x_ref, xs_hbm.at[pl.ds(0, x_ref.shape[0])], sem).wait()


def _dispatch(xn, y_ids, nvalid):
    tm = ROW_TILE
    t = xn.shape[0] // ROW_SUB
    n_blk = nvalid.shape[0]
    return pl.pallas_call(
        _dispatch_kernel,
        grid_spec=pltpu.PrefetchScalarGridSpec(
            num_scalar_prefetch=1,
            grid=(t // tm,),
            in_specs=[pl.BlockSpec((1, 1, TOP_K * tm), lambda i, nv: (i, 0, 0), memory_space=pltpu.SMEM),
                      pl.BlockSpec((tm * ROW_SUB, LANES), lambda i, nv: (i, 0))],
            out_specs=pl.BlockSpec(memory_space=pl.ANY),
            scratch_shapes=[pltpu.VMEM((MOE_BLOCK * ROW_SUB, LANES), F32), pltpu.SemaphoreType.DMA],
        ),
        out_shape=jax.ShapeDtypeStruct((n_blk * MOE_BLOCK * ROW_SUB, LANES), F32),
        compiler_params=pltpu.CompilerParams(dimension_semantics=("arbitrary",),
                                             vmem_limit_bytes=VMEM_LIMIT),
        name="moe_dispatch",
    )(nvalid, y_ids.reshape(t // tm, 1, TOP_K * tm), xn)


def _moe_kernel(be_ref, nv_ref, x_ref, wg_ref, wu_ref, wd_ref, y_ref, wg_bf, wu_bf, wd_bf):
    s = pl.program_id(0)
    nv = nv_ref[s]

    @pl.when((s == 0) | (be_ref[s] != be_ref[jnp.maximum(s - 1, 0)]))
    def _():
        wg_bf[...] = wg_ref[0].astype(BF16)
        wu_bf[...] = wu_ref[0].astype(BF16)
        wd_bf[...] = wd_ref[0].astype(BF16)

    @pl.when(nv > 0)
    def _():
        x = _rows_from_tiles(x_ref, (), MOE_BLOCK).astype(BF16)
        gate = _dot(x, wg_bf[...])
        up = _dot(x, wu_bf[...])
        hid = (gate * (1.0 / (1.0 + jnp.exp(-gate))) * up).astype(BF16)
        _rows_to_tiles(y_ref, (), _dot(hid, wd_bf[...]))

    @pl.when(nv == 0)
    def _():
        y_ref[...] = jnp.zeros_like(y_ref)


def _moe(xs, block_e, nvalid, p):
    d = xs.shape[1] * ROW_SUB
    n_blk = block_e.shape[0]
    layer = p["layer"]
    hdim = p["w_gate"].shape[3]
    wspec = lambda shape: pl.BlockSpec((None, 1) + shape, lambda s, be, nv: (layer, be[s], 0, 0))
    rows = pl.BlockSpec((MOE_BLOCK * ROW_SUB, LANES), lambda s, be, nv: (s, 0))
    return pl.pallas_call(
        _moe_kernel,
        grid_spec=pltpu.PrefetchScalarGridSpec(
            num_scalar_prefetch=2,
            grid=(n_blk,),
            in_specs=[rows, wspec((d, hdim)), wspec((d, hdim)), wspec((hdim, d))],
            out_specs=rows,
            scratch_shapes=[pltpu.VMEM((d, hdim), BF16), pltpu.VMEM((d, hdim), BF16), pltpu.VMEM((hdim, d), BF16)],
        ),
        out_shape=jax.ShapeDtypeStruct(xs.shape, F32),
        compiler_params=pltpu.CompilerParams(dimension_semantics=("arbitrary",),
                                             vmem_limit_bytes=VMEM_LIMIT),
        name="moe_experts",
    )(block_e, nvalid, xs, p["w_gate"], p["w_up"], p["w_down"])


def _route(route, counts, t):
    a = t * TOP_K
    expert = route[:, 2:4].astype(jnp.int32).reshape(a)
    pos = route[:, 4:6].astype(jnp.int32).reshape(a)
    counts = counts.astype(jnp.int32)
    padded = (counts + MOE_BLOCK - 1) // MOE_BLOCK * MOE_BLOCK
    pad_end = jnp.cumsum(padded)
    pad_start = pad_end - padded
    dest = pad_start[expert] + pos
    n_blk = (a + N_EXPERTS * MOE_BLOCK) // MOE_BLOCK
    starts = jnp.arange(n_blk, dtype=jnp.int32) * MOE_BLOCK
    block_e = jnp.minimum(jnp.sum((pad_end[None, :] <= starts[:, None]).astype(jnp.int32), axis=1), N_EXPERTS - 1)
    nvalid = jnp.clip((pad_start + counts)[block_e] - starts, 0, MOE_BLOCK)
    return block_e, nvalid, dest * ROW_SUB


def _final_kernel(h_ref, ids0_ref, idsn_ref, y_hbm, ew_ref, g_ref, o_ref, ybuf, ysem):
    h = _combine_experts(h_ref[...], ew_ref[...], y_hbm, ids0_ref, idsn_ref, ybuf, ysem,
                         pl.program_id(0), pl.num_programs(0))
    o_ref[0] = _rms(h, g_ref[...])


def _final(h, y2, y_ids, ew, g, batch, seq_len):
    t, d = h.shape
    per_seq = seq_len // BLOCK
    out_blocks = per_seq - 1
    row = lambda n: ((n // out_blocks) * per_seq + n % out_blocks + 1, 0)
    ids = y_ids.reshape(batch, per_seq, TOP_K * BLOCK)[:, 1:].reshape(batch * out_blocks, 1, TOP_K * BLOCK)
    ids = jnp.concatenate([ids, jnp.zeros_like(ids[:1])], axis=0)
    return pl.pallas_call(
        _final_kernel,
        grid=(batch * out_blocks,),
        in_specs=[pl.BlockSpec((BLOCK, d), row),
                  pl.BlockSpec((1, 1, TOP_K * BLOCK), lambda n: (0, 0, 0), memory_space=pltpu.SMEM),
                  pl.BlockSpec((1, 1, TOP_K * BLOCK), lambda n: (n + 1, 0, 0), memory_space=pltpu.SMEM),
                  pl.BlockSpec(memory_space=pl.ANY),
                  pl.BlockSpec((BLOCK, LANES), row),
                  pl.BlockSpec((1, d), lambda n: (0, 0))],
        out_specs=pl.BlockSpec((1, BLOCK, d), lambda n: (n // out_blocks, n % out_blocks, 0)),
        out_shape=jax.ShapeDtypeStruct((batch, seq_len - BLOCK, d), F32),
        scratch_shapes=[pltpu.VMEM((2, TOP_K, BLOCK * ROW_SUB, LANES), F32), pltpu.SemaphoreType.DMA((2,))],
        compiler_params=pltpu.CompilerParams(dimension_semantics=("arbitrary",),
                                             vmem_limit_bytes=VMEM_LIMIT),
        name="final_norm",
    )(h, ids, ids, y2, ew, g)


def _rope_table(seq_len):
    half = MLA_ROPE // 2
    pos = (jnp.arange(seq_len, dtype=jnp.int32) - PAD_FRONT).astype(F32)
    inv_freq = ROPE_THETA ** (-jnp.arange(half, dtype=F32) / half)
    ang = pos[:, None] * inv_freq[None, :]
    cos, sin = jnp.cos(ang), jnp.sin(ang)
    cos2 = jnp.concatenate([cos, cos], axis=1)
    sin2 = jnp.concatenate([-sin, sin], axis=1)
    z = lambda w: jnp.zeros((seq_len, w), F32)
    scale = (MLA_NOPE + MLA_ROPE) ** -0.5 * LOG2E
    cos_q = jnp.concatenate([jnp.ones((seq_len, MLA_NOPE), F32), cos2, z(32)], axis=1) * scale
    sin_q = jnp.concatenate([z(MLA_NOPE), sin2, z(32)], axis=1) * scale
    cos_k = jnp.concatenate([z(MLA_NOPE), cos2, z(32)], axis=1)
    sin_k = jnp.concatenate([z(MLA_NOPE), sin2, z(32)], axis=1)
    return jnp.concatenate([cos_q, sin_q, cos_k, sin_k], axis=1)


def _swap_halves(w):
    half = w.shape[-1] // 2
    return jnp.concatenate([w[..., half:], w[..., :half]], axis=-1)


def _layer_params(i, seq_len, attn_norm, w_in, b_forget, sinks, mla_q_norm, mla_kv_norm, mla_w_uq,
                  mla_w_ukv, w_out, ffn_norm, w_group, b_group, w_router, b_router, w_gate, w_up, w_down):
    d = w_in.shape[1]
    w = w_in[i]
    sizes = (256, 128, 128, 256, 256, 256, 4, 256, 128, 32, 256, 256, 256)
    offs = np.concatenate([[0], np.cumsum(sizes)])
    (a_q, a_k, a_v, f_q, f_k, f_v, f_g, c_q, c_kv, c_kr, s_q, s_k, s_v) = [
        w[:, offs[j]:offs[j + 1]] for j in range(len(sizes))]
    qscale = HEAD_DIM ** -0.5
    grp = SWA_HEADS // SWA_KV_HEADS
    a_q = a_q.reshape(d, SWA_KV_HEADS, grp, HEAD_DIM).transpose(0, 2, 1, 3).reshape(d, 256)
    z = lambda n: jnp.zeros((d, n), F32)
    g_grp = jnp.concatenate([f_g[:, 0:2], z(62), c_kr, z(32)], axis=1)
    gs_grp = jnp.concatenate([f_g[:, 2:4], z(62), _swap_halves(c_kr), z(32)], axis=1)
    w_perm = jnp.concatenate([a_q * qscale, a_k, a_v, f_q * qscale, f_k, f_v, s_q * qscale, s_k, s_v,
                              c_q, c_kv, g_grp, gs_grp], axis=1).astype(BF16)
    wuq = mla_w_uq[i].reshape(MLA_Q_LORA, 4, MLA_NOPE + MLA_ROPE)
    zq = lambda n: jnp.zeros((MLA_Q_LORA, 4, n), F32)
    w_uq_a = jnp.concatenate([wuq, zq(32)], axis=2).reshape(MLA_Q_LORA, 512).astype(BF16)
    w_uq_b = jnp.concatenate([zq(MLA_NOPE), _swap_halves(wuq[:, :, MLA_NOPE:]), zq(32)],
                             axis=2).reshape(MLA_Q_LORA, 512).astype(BF16)
    wukv = mla_w_ukv[i].reshape(MLA_KV_LORA, 4, MLA_NOPE + MLA_V)
    w_kv_k = jnp.concatenate([wukv[:, :, :MLA_NOPE], jnp.zeros((MLA_KV_LORA, 4, 64), F32)],
                             axis=2).reshape(MLA_KV_LORA, 512).astype(BF16)
    w_kv_v = wukv[:, :, MLA_NOPE:].reshape(MLA_KV_LORA, 256).astype(BF16)
    bf = b_forget[i].astype(F32)
    b_f = jnp.zeros((1, 256), F32).at[0, 0:2].set(bf[0:2]).at[0, 128:130].set(bf[2:4])
    wo = w_out[i]
    wo_a = wo[:256].reshape(SWA_KV_HEADS, grp, HEAD_DIM, d).transpose(1, 0, 2, 3).reshape(256, d)
    wo4 = jnp.concatenate([wo_a, wo[256:]], axis=0).reshape(4, 256, d).astype(BF16)
    w_r = jnp.concatenate([w_group[i], w_router[i], jnp.zeros((d, LANES - N_GROUPS - N_EXPERTS), F32)], axis=1)
    w_r_hi = w_r.astype(BF16)
    w_r_lo = (w_r - w_r_hi.astype(F32)).astype(BF16)
    b_r = jnp.concatenate([b_group[i], b_router[i], jnp.zeros((LANES - N_GROUPS - N_EXPERTS,), F32)])[None, :]
    return dict(
        attn_norm=attn_norm[i][None, :], w_in=w_perm, q_norm=mla_q_norm[i][None, :],
        kv_norm=mla_kv_norm[i][None, :], w_uq_a=w_uq_a, w_uq_b=w_uq_b, w_kv_k=w_kv_k, w_kv_v=w_kv_v,
        rope_tab=_rope_table(seq_len), b_forget=b_f, sinks=sinks[i].astype(F32), w_out=wo4,
        ffn_norm=ffn_norm[i][None, :], w_r_hi=w_r_hi, w_r_lo=w_r_lo, b_r=b_r.astype(F32),
        w_gate=w_gate, w_up=w_up, w_down=w_down, layer=i)


def kernel(x, meta_tokens, attn_norm, w_in, b_forget, sinks, mla_q_norm, mla_kv_norm, mla_w_uq, mla_w_ukv,
           w_out, ffn_norm, w_group, b_group, w_router, b_router, w_gate, w_up, w_down, final_norm):
    batch, seq, d = x.shape
    seq_len = seq + BLOCK
    assert seq_len % ROW_TILE == 0 and seq_len % (BLOCK * SWA_Q_BLOCKS) == 0
    assert all(seq % tile == 0 for tile in Q_TILES.values())
    t = batch * seq_len
    depth = w_in.shape[0]
    lead = jnp.concatenate([jnp.zeros((PAD_FRONT, d), x.dtype), meta_tokens.astype(x.dtype)], axis=0)
    h = (lead, x)
    y2 = y_ids = ew = None
    for i in range(depth):
        p = _layer_params(i, seq_len, attn_norm, w_in, b_forget, sinks, mla_q_norm, mla_kv_norm, mla_w_uq,
                          mla_w_ukv, w_out, ffn_norm, w_group, b_group, w_router, b_router, w_gate, w_up, w_down)
        h, (qa, fq, fk, fv, cq, ck, cv, sq, sk, sv) = _inproj(h, y2, y_ids, ew, p, seq_len)
        y_a = _swa_attention(qa, p["sinks"], batch, seq_len)
        y_b = _causal_attention("fox", fq, fk, fv, batch, seq_len)
        y_c = _causal_attention("mla", cq, ck, cv, batch, seq_len)
        y_d = _causal_attention("sb", sq, sk, sv, batch, seq_len)
        ys = [y.reshape(t, 256) for y in (y_a, y_b, y_c, y_d)]
        h, xn, ew, counts = _outproj(ys, h, p)
        block_e, nvalid, y_ids = _route(ew, counts[0, N_GROUPS:N_GROUPS + N_EXPERTS], t)
        y2 = _moe(_dispatch(xn, y_ids, nvalid), block_e, nvalid, p)
    return _final(h, y2, y_ids, ew, final_norm[None, :], batch, seq_len)
```

```python
import functools

import jax
import jax.numpy as jnp
import numpy as np
from jax import lax
from jax.experimental import pallas as pl
from jax.experimental.pallas import tpu as pltpu

F32 = jnp.float32
BF16 = jnp.bfloat16

BLOCK = 128
N_META = 16
PAD_FRONT = BLOCK - N_META
CHUNK_SHIFT = 6
HEAD_DIM = 64
NORM_EPS = 1e-6
NEG = -1e30
PAD_KEY_LOGIT = -(2.0 ** 100)
UNDERFLOW_LOG2 = -150.0
LOG2E = 1.4426950408889634
BIG = 1 << 30
SWA_HEADS, SWA_KV_HEADS, WINDOW = 4, 2, 128
MLA_Q_LORA, MLA_KV_LORA, MLA_NOPE, MLA_ROPE, MLA_V = 256, 128, 64, 32, 64
MLA_BIAS_LANE = MLA_NOPE + MLA_ROPE
ROPE_THETA = 10000.0
N_GROUPS, EXPERTS_PER_GROUP, TOP_K = 4, 8, 2
N_EXPERTS = N_GROUPS * EXPERTS_PER_GROUP
MOE_BLOCK = 512
LANES = 128
ROW_SUB = 8
ROW_TILE = 384
Q_TILES = {"fox": 1024, "mla": 1024, "sb": 512}
K_TILE = 256
ROW_PART = 128
SWA_Q_BLOCKS = 3
VMEM_LIMIT = 56 * 1024 * 1024

C_A, C_B, C_D, C_CQ, C_CKV, C_G, C_GS, C_END = 0, 512, 1280, 2048, 2304, 2432, 2560, 2688
B_F0, B_F1, B_PAD = 0, 3, 6


def _rms(x, g):
    return x * lax.rsqrt(jnp.mean(x * x, axis=-1, keepdims=True) + NORM_EPS) * g


def _log_sigmoid(x):
    return jnp.minimum(x, 0.0) - jnp.log(1.0 + jnp.exp(-jnp.abs(x)))


def _dot(a, b):
    return jnp.dot(a, b, preferred_element_type=F32)


def _dot_nt(a, b):
    return lax.dot_general(a, b, (((1,), (1,)), ((), ())), preferred_element_type=F32)


def _rows_from_tiles(ref, lead, n):
    return jnp.concatenate([ref[(*lead, pl.ds(j, n, stride=ROW_SUB), slice(None))] for j in range(ROW_SUB)], axis=1)


def _rows_to_tiles(ref, lead, x):
    n = x.shape[0]
    for j in range(ROW_SUB):
        ref[(*lead, pl.ds(j, n, stride=ROW_SUB), slice(None))] = x[:, j * LANES:(j + 1) * LANES]


def _tile4(x):
    return jnp.concatenate([x, x, x, x], axis=1)


def _split3(x):
    hi = x.astype(BF16)
    r1 = x - hi.astype(F32)
    mid = r1.astype(BF16)
    lo = (r1 - mid.astype(F32)).astype(BF16)
    return hi, mid, lo


def _free_base(head):
    return head * LANES + (HEAD_DIM if head % 2 == 0 else 0)


def _expert_rows_start(y_hbm, ids_ref, buf, sem, slot):
    for j in range(ids_ref.shape[2]):
        tok, k = divmod(j, TOP_K)
        src = pl.multiple_of(ids_ref[0, 0, j], ROW_SUB)
        pltpu.make_async_copy(y_hbm.at[pl.ds(src, ROW_SUB)], buf.at[slot, k, pl.ds(tok * ROW_SUB, ROW_SUB)],
                              sem.at[slot]).start(priority=j % 2)


def _expert_rows_wait(y_hbm, buf, sem, slot):
    for k in range(TOP_K):
        pltpu.make_async_copy(y_hbm.at[pl.ds(0, buf.shape[2])], buf.at[slot, k], sem.at[slot]).wait()


def _combine_experts(h, ew, y_hbm, ids0_ref, idsn_ref, buf, sem, step, n_steps):
    slot = step % 2

    @pl.when(step == 0)
    def _():
        _expert_rows_start(y_hbm, ids0_ref, buf, sem, 0)

    _expert_rows_start(y_hbm, idsn_ref, buf, sem, 1 - slot)
    _expert_rows_wait(y_hbm, buf, sem, slot)
    n = h.shape[0]
    out = h + ew[:, 0:1] * _rows_from_tiles(buf, (slot, 0), n) + ew[:, 1:2] * _rows_from_tiles(buf, (slot, 1), n)

    @pl.when(step == n_steps - 1)
    def _():
        _expert_rows_wait(y_hbm, buf, sem, 1 - slot)

    return out


def _inproj_kernel(has_y2, n_seq_tiles, *refs):
    if has_y2:
        (h_ref, ids0_ref, idsn_ref, y_hbm, ew_ref, *rest) = refs
    else:
        (lead_ref, *x_refs) = refs[:1 + ROW_TILE // BLOCK]
        rest = refs[1 + ROW_TILE // BLOCK:]
    (g_ref, w_ref, qn_ref, kvn_ref, wuqa_ref, wuqb_ref, wkvk_ref, wkvv_ref, tab_ref, bf_ref,
     pq_ref, pk_ref, rows_ref, hout_ref, *outs) = rest
    if has_y2:
        *outs, ybuf, ysem = outs
    (qa_ref, fq_ref, fk_ref, fv_ref, cq_ref, ck_ref, cv_ref, sq_ref, sk_ref, sv_ref, carry_ref) = outs
    tile = pl.program_id(0) % n_seq_tiles
    if has_y2:
        h = _combine_experts(h_ref[...], ew_ref[...], y_hbm, ids0_ref, idsn_ref, ybuf, ysem,
                             pl.program_id(0), pl.num_programs(0))
    else:
        first = jnp.where(tile == 0, lead_ref[...], x_refs[0][0])
        h = jnp.concatenate([first] + [r[0] for r in x_refs[1:]], axis=0)
    hout_ref[...] = h
    tm, d = h.shape
    xn = _rms(h, g_ref[...]).astype(BF16)
    acc = _dot(xn, w_ref[...])
    lane = lax.broadcasted_iota(jnp.int32, (1, LANES), 1)
    lo_half = lane < HEAD_DIM
    pad_col = jnp.where(tile * tm + lax.broadcasted_iota(jnp.int32, (tm, 1), 0) < PAD_FRONT,
                        PAD_KEY_LOGIT, 0.0)
    rows = rows_ref[...]
    fq_one, fk_one, pad_lane, sq_one, mla_one, mla_pad = (rows[j:j + 1] for j in range(6))

    def per_head(x_pair, bias, pair, scale=None):
        x = x_pair if scale is None else x_pair * scale
        even = jnp.where(lo_half, x, bias[:, (2 * pair) * LANES:(2 * pair + 1) * LANES])
        odd = jnp.where(lo_half, bias[:, (2 * pair + 1) * LANES:(2 * pair + 2) * LANES], x)
        return even, odd

    def store_heads(ref, x_off, bias, scale=None):
        for pair in range(2):
            x_pair = acc[:, x_off + pair * LANES:x_off + (pair + 1) * LANES]
            even, odd = per_head(x_pair, bias, pair, scale)
            ref[:, (2 * pair) * LANES:(2 * pair + 1) * LANES] = even.astype(BF16)
            ref[:, (2 * pair + 1) * LANES:(2 * pair + 2) * LANES] = odd.astype(BF16)

    qa_ref[...] = acc[:, C_A:C_B].astype(BF16)

    @pl.when(tile == 0)
    def _():
        carry_ref[...] = jnp.zeros_like(carry_ref)

    lf = _log_sigmoid(acc[:, C_G:C_END] + bf_ref[...]) * LOG2E
    r = lax.broadcasted_iota(jnp.int32, (BLOCK, BLOCK), 0)
    c = lax.broadcasted_iota(jnp.int32, (BLOCK, BLOCK), 1)
    tri = jnp.where(c <= r, 1.0, 0.0).astype(BF16)
    carry = carry_ref[...]
    blocks = []
    for b in range(tm // BLOCK):
        hi, mid, lo = _split3(lf[b * BLOCK:(b + 1) * BLOCK])
        y = _dot(tri, hi) + _dot(tri, mid) + _dot(tri, lo) + carry
        carry = y[BLOCK - 1:BLOCK, :]
        blocks.append(y)
    carry_ref[...] = carry
    f_hi, f_mid, f_lo = _split3(jnp.concatenate(blocks, axis=0))
    q_bias = _dot(f_hi, pq_ref[0]) + _dot(f_mid, pq_ref[1]) + _dot(f_lo, pq_ref[2]) + fq_one
    k_bias = (_dot(f_hi, pk_ref[0]) + _dot(f_mid, pk_ref[1]) + _dot(f_lo, pk_ref[2]) + fk_one
              + pad_col * pad_lane)
    ones = jnp.ones((tm, 4 * LANES), F32)
    store_heads(fq_ref, C_B, q_bias, LOG2E)
    store_heads(fk_ref, C_B + 256, k_bias)
    store_heads(fv_ref, C_B + 512, ones)

    store_heads(sq_ref, C_D, jnp.broadcast_to(sq_one, (tm, 4 * LANES)), LOG2E)
    store_heads(sk_ref, C_D + 256, pad_col * pad_lane)
    sv_ref[...] = acc[:, C_D + 512:C_CQ].astype(BF16)

    tab = tab_ref[...]
    cos_q, sin_q = tab[:, 0:128], tab[:, 128:256]
    cos_k, sin_k = tab[:, 256:384], tab[:, 384:512]
    cqn = _rms(acc[:, C_CQ:C_CKV], qn_ref[...]).astype(BF16)
    q_lin = _dot(cqn, wuqa_ref[...])
    q_swp = _dot(cqn, wuqb_ref[...])
    cq_ref[...] = (q_lin * _tile4(cos_q) + q_swp * _tile4(sin_q) + mla_one).astype(BF16)
    ckvn = _rms(acc[:, C_CKV:C_G], kvn_ref[...]).astype(BF16)
    k_nope = _dot(ckvn, wkvk_ref[...])
    grp, grp_s = acc[:, C_G:C_GS], acc[:, C_GS:C_END]
    k_rope = grp * cos_k + grp_s * sin_k
    ck_ref[...] = (k_nope + _tile4(k_rope) + pad_col * mla_pad).astype(BF16)
    vv = _dot(ckvn, wkvv_ref[...])
    for pair in range(2):
        even, odd = per_head(vv[:, pair * LANES:(pair + 1) * LANES], ones, pair)
        cv_ref[:, (2 * pair) * LANES:(2 * pair + 1) * LANES] = even.astype(BF16)
        cv_ref[:, (2 * pair + 1) * LANES:(2 * pair + 2) * LANES] = odd.astype(BF16)


def _bias_constants():
    src = (0, 1, LANES, LANES + 1)
    pq = np.zeros((3, 2 * LANES, 4 * LANES), np.float32)
    pk = np.zeros((3, 2 * LANES, 4 * LANES), np.float32)
    rows = np.zeros((8, 4 * LANES), np.float32)
    for head in range(4):
        base = _free_base(head)
        for part in range(3):
            pq[part, src[head], base + B_F0 + part] = 1.0
            pk[part, src[head], base + B_F1 + part] = -1.0
            rows[0, base + B_F1 + part] = 1.0
            rows[1, base + B_F0 + part] = 1.0
        rows[0, base + B_PAD] = 1.0
        rows[2, base + B_PAD] = 1.0
        rows[3, base + B_PAD] = 1.0
        rows[4, head * LANES + MLA_BIAS_LANE] = 1.0
        rows[5, head * LANES + MLA_BIAS_LANE] = 1.0
    return jnp.asarray(pq, BF16), jnp.asarray(pk, BF16), jnp.asarray(rows, F32)


def _inproj(h, y2, y_ids, ew, p, seq_len):
    tm = ROW_TILE
    n_seq_tiles = seq_len // tm
    has_y2 = y2 is not None
    row = lambda i: (i, 0)
    fixed = lambda i: (0, 0)
    if has_y2:
        t, d = h.shape
        in_specs = [pl.BlockSpec((tm, d), row)]
        args = [h]
    else:
        lead, x = h
        batch, seq, d = x.shape
        t = batch * seq_len
        per_tile = tm // BLOCK
        frame = lambda k: pl.BlockSpec((1, BLOCK, d), lambda i: (
            i // n_seq_tiles, jnp.maximum((i % n_seq_tiles) * per_tile - 1 + k, 0), 0))
        in_specs = [pl.BlockSpec((BLOCK, d), fixed)] + [frame(k) for k in range(per_tile)]
        args = [lead] + [x] * per_tile
    if has_y2:
        ids = y_ids.reshape(t // tm, 1, TOP_K * tm)
        ids = jnp.concatenate([ids, jnp.zeros_like(ids[:1])], axis=0)
        in_specs += [pl.BlockSpec((1, 1, TOP_K * tm), lambda i: (0, 0, 0), memory_space=pltpu.SMEM),
                     pl.BlockSpec((1, 1, TOP_K * tm), lambda i: (i + 1, 0, 0), memory_space=pltpu.SMEM),
                     pl.BlockSpec(memory_space=pl.ANY),
                     pl.BlockSpec((tm, LANES), row)]
        args += [ids, ids, y2, ew]
    pq, pk, rows = _bias_constants()
    consts = [p["attn_norm"], p["w_in"], p["q_norm"], p["kv_norm"], p["w_uq_a"], p["w_uq_b"],
              p["w_kv_k"], p["w_kv_v"]]
    in_specs += [pl.BlockSpec(c.shape, fixed) for c in consts]
    args += consts
    in_specs.append(pl.BlockSpec((tm, 512), lambda i: (i % n_seq_tiles, 0)))
    args.append(p["rope_tab"])
    in_specs += [pl.BlockSpec((1, 256), fixed),
                 pl.BlockSpec(pq.shape, lambda i: (0, 0, 0)),
                 pl.BlockSpec(pk.shape, lambda i: (0, 0, 0)),
                 pl.BlockSpec(rows.shape, fixed)]
    args += [p["b_forget"], pq, pk, rows]
    widths = [512] * 9 + [256]
    out_shape = [jax.ShapeDtypeStruct((t, d), F32)] + [jax.ShapeDtypeStruct((t, w), BF16) for w in widths]
    out_specs = [pl.BlockSpec((tm, d), row)] + [pl.BlockSpec((tm, w), row) for w in widths]
    outs = pl.pallas_call(
        functools.partial(_inproj_kernel, has_y2, n_seq_tiles),
        grid=(t // tm,),
        in_specs=in_specs,
        out_specs=out_specs,
        out_shape=out_shape,
        scratch_shapes=[pltpu.VMEM((1, 2 * LANES), F32)] + (
            [pltpu.VMEM((2, TOP_K, tm * ROW_SUB, LANES), F32), pltpu.SemaphoreType.DMA((2,))] if has_y2 else []),
        compiler_params=pltpu.CompilerParams(dimension_semantics=("arbitrary",),
                                             vmem_limit_bytes=VMEM_LIMIT),
        name="inproj_y2" if has_y2 else "inproj",
    )(*args)
    return outs[0], outs[1:]


def _swa_kernel(sink_ref, q_ref, km_ref, kp_ref, kc_ref, vm_ref, vp_ref, vc_ref, o_ref):
    i = pl.program_id(1)
    n_sub = q_ref.shape[1] // BLOCK
    lane = lax.broadcasted_iota(jnp.int32, (1, LANES), 1)
    lo_half = lane < HEAD_DIM
    half_masks = [jnp.where(lo_half, 1.0, 0.0).astype(BF16), jnp.where(lo_half, 0.0, 1.0).astype(BF16)]
    row = lax.broadcasted_iota(jnp.int32, (BLOCK, 1), 0)
    col = lax.broadcasted_iota(jnp.int32, (1, BLOCK), 1)
    grp = SWA_HEADS // SWA_KV_HEADS
    k_all = jnp.concatenate([km_ref[0], kp_ref[0], kc_ref[0]], axis=0)
    v_all = jnp.concatenate([vm_ref[0], vp_ref[0], vc_ref[0]], axis=0)
    for j in range(n_sub):
        q0 = (i * n_sub + j) * BLOCK
        pq = q0 + row
        cq = pq >> CHUNK_SHIFT
        segs = []
        vis_m = col >= PAD_FRONT
        segs.append((vis_m, jnp.minimum(jnp.abs(pq - col), WINDOW).astype(F32)))
        for pk in (q0 - BLOCK + col, q0 + col):
            ck = jnp.where(pk >= BLOCK, pk >> CHUNK_SHIFT, BIG)
            vis = (ck <= cq) & (ck >= cq - (WINDOW >> CHUNK_SHIFT))
            segs.append((vis, jnp.abs(pq - pk).astype(F32)))
        kj = jnp.concatenate([k_all[0:BLOCK], k_all[(j + 1) * BLOCK:(j + 3) * BLOCK]], axis=0)
        vj = jnp.concatenate([v_all[0:BLOCK], v_all[(j + 1) * BLOCK:(j + 3) * BLOCK]], axis=0)
        for g in range(grp):
            qg = q_ref[0, j * BLOCK:(j + 1) * BLOCK, g * LANES:(g + 1) * LANES]
            out_g = None
            for hk in range(SWA_KV_HEADS):
                head = hk * grp + g
                slope = 2.0 ** (-8.0 * (head + 1) / SWA_HEADS)
                sink = sink_ref[head]
                s_all = _dot_nt(qg * half_masks[hk], kj)
                tiles = [jnp.where(vis, s_all[:, n * LANES:(n + 1) * LANES] - slope * dist, NEG)
                         for n, (vis, dist) in enumerate(segs)]
                top = jnp.maximum(jnp.maximum(tiles[0], tiles[1]), tiles[2])
                m = jnp.broadcast_to(jnp.maximum(jnp.max(top, axis=-1, keepdims=True), sink), (BLOCK, LANES))
                e = [jnp.exp(x - m) for x in tiles]
                den = jnp.sum(e[0] + e[1] + e[2], axis=-1, keepdims=True) + jnp.exp(sink - m[:, 0:1])
                o = _dot(jnp.concatenate(e, axis=1).astype(BF16), vj) * (1.0 / den)
                out_g = o if hk == 0 else jnp.where(lo_half, out_g, o)
            o_ref[0, j * BLOCK:(j + 1) * BLOCK, g * LANES:(g + 1) * LANES] = out_g.astype(BF16)


def _swa_attention(qa, sinks, batch, seq_len):
    x = qa.reshape(batch, seq_len, 512)
    n_sub = SWA_Q_BLOCKS
    nb = seq_len // (BLOCK * n_sub)
    blk = lambda f: pl.BlockSpec((1, BLOCK, LANES), f)
    own = lambda c: pl.BlockSpec((1, BLOCK * n_sub, LANES), lambda b, i: (b, i, c))
    return pl.pallas_call(
        _swa_kernel,
        grid=(batch, nb),
        in_specs=[
            pl.BlockSpec(memory_space=pltpu.SMEM),
            pl.BlockSpec((1, BLOCK * n_sub, 2 * LANES), lambda b, i: (b, i, 0)),
            blk(lambda b, i: (b, 0, 2)),
            blk(lambda b, i: (b, jnp.maximum(i * n_sub - 1, 0), 2)),
            own(2),
            blk(lambda b, i: (b, 0, 3)),
            blk(lambda b, i: (b, jnp.maximum(i * n_sub - 1, 0), 3)),
            own(3),
        ],
        out_specs=pl.BlockSpec((1, BLOCK * n_sub, 2 * LANES), lambda b, i: (b, i, 0)),
        out_shape=jax.ShapeDtypeStruct((batch, seq_len, 2 * LANES), BF16),
        compiler_params=pltpu.CompilerParams(dimension_semantics=("arbitrary", "arbitrary"),
                                             vmem_limit_bytes=VMEM_LIMIT),
        name="swa_attention",
    )(sinks, x, x, x, x, x, x, x)


def _causal_kernel(mode, q_ref, k_ref, v_ref, o_ref, stat_ref, acc_ref):
    seq_len = q_ref.shape[1]
    Q_TILE = stat_ref.shape[1]
    n_qt = (seq_len - BLOCK) // Q_TILE
    per_tile = Q_TILE // K_TILE
    lane = lax.broadcasted_iota(jnp.int32, (1, LANES), 1)
    lo_half = lane < HEAD_DIM
    if mode == "sb":
        r = lax.broadcasted_iota(jnp.int32, (K_TILE, K_TILE), 0)
        c = lax.broadcasted_iota(jnp.int32, (K_TILE, K_TILE), 1)
        later2 = jnp.where(r > c, 1.0, 0.0).astype(BF16)
        later1 = later2[:BLOCK, :BLOCK]

    def causal(pq, k0, tk):
        pk = k0 + lax.broadcasted_iota(jnp.int32, (1, tk), 1)
        if mode == "fox":
            return pk <= pq
        if mode == "mla":
            return (pk >> CHUNK_SHIFT) <= (pq >> CHUNK_SHIFT)
        return pk < pq

    def head_v(k0, tk, hh):
        if mode == "sb":
            return v_ref[0, pl.ds(k0, tk), :]
        return v_ref[0, pl.ds(k0, tk), hh * LANES:(hh + 1) * LANES]

    def lane_tiles(x):
        return [x[:, j * LANES:(j + 1) * LANES] for j in range(x.shape[1] // LANES)]

    def row_parts(lo, hi):
        step = min(hi - lo, ROW_PART)
        return [(r0, step) for r0 in range(lo, hi, step)]

    def softmax_chunk(q0, tq, k0, tk, masked, first, row_lo=0):
        for hh in range(2):
            qh = q_ref[0, pl.ds(q0 + row_lo, tq - row_lo), hh * LANES:(hh + 1) * LANES]
            kh = k_ref[0, pl.ds(k0, tk), hh * LANES:(hh + 1) * LANES]
            s_all = _dot_nt(qh, kh)
            vh = head_v(k0, tk, hh)
            for r0, tr in row_parts(row_lo, tq):
                s = s_all[r0 - row_lo:r0 - row_lo + tr]
                if masked and r0 < row_lo + tk:
                    pq = q0 + r0 + lax.broadcasted_iota(jnp.int32, (tr, 1), 0)
                    s = jnp.where(causal(pq, k0, tk), s, NEG)
                tiles = lane_tiles(s)
                top = tiles[0]
                for x in tiles[1:]:
                    top = jnp.maximum(top, x)
                m_new = jnp.broadcast_to(jnp.max(top, axis=-1, keepdims=True), (tr, LANES))
                if not first:
                    m_old = stat_ref[hh, r0:r0 + tr, :]
                    m_new = jnp.maximum(m_old, m_new)
                p = jnp.concatenate([jnp.exp2(x - m_new) for x in tiles], axis=1).astype(BF16)
                pv = _dot(p, vh)
                if not first:
                    pv = jnp.exp2(m_old - m_new) * acc_ref[hh, r0:r0 + tr, :] + pv
                stat_ref[hh, r0:r0 + tr, :] = m_new
                acc_ref[hh, r0:r0 + tr, :] = pv

    def stick_chunk(q0, rows, k0, tk, masked, first):
        lo_r, hi_r = rows
        tq = hi_r - lo_r
        later = later2 if tk == K_TILE else later1
        for hh in range(2):
            qh = q_ref[0, pl.ds(q0 + lo_r, tq), hh * LANES:(hh + 1) * LANES]
            kh = k_ref[0, pl.ds(k0, tk), hh * LANES:(hh + 1) * LANES]
            z = _dot_nt(qh, kh)
            ls_pos = jnp.minimum(z, 0.0) - jnp.log(1.0 + jnp.exp2(-jnp.abs(z))) * LOG2E
            log_keep = ls_pos - z
            if masked:
                pq = q0 + lo_r + lax.broadcasted_iota(jnp.int32, (tq, 1), 0)
                vis = causal(pq, k0, tk)
                log_keep = jnp.where(vis, log_keep, 0.0)
            after = _dot(log_keep.astype(BF16), later)
            tot = ls_pos + after
            chunk_total = jnp.broadcast_to(after[:, 0:1] + log_keep[:, 0:1], (tq, LANES))
            if not first:
                carry = stat_ref[hh, lo_r:hi_r, :]
                tot = jnp.concatenate([x + carry for x in lane_tiles(tot)], axis=1)
                chunk_total = carry + chunk_total
            a = jnp.exp2(tot)
            if masked:
                a = jnp.where(vis, a, 0.0)
            pv = _dot(a.astype(BF16), head_v(k0, tk, hh))
            if not first:
                pv = acc_ref[hh, lo_r:hi_r, :] + pv
            stat_ref[hh, lo_r:hi_r, :] = chunk_total
            acc_ref[hh, lo_r:hi_r, :] = pv

    def finish(q0, tq):
        a0, a1 = acc_ref[0, 0:tq, :], acc_ref[1, 0:tq, :]
        if mode != "sb":
            a0 = a0 / a0[:, HEAD_DIM:HEAD_DIM + 1]
            a1 = a1 / a1[:, 0:1]
        o_ref[0, pl.ds(q0, tq), :] = jnp.where(lo_half, a0, a1).astype(BF16)

    def chunk_start(j):
        return pl.multiple_of(BLOCK + j * K_TILE, BLOCK)

    if mode == "sb":
        stick_chunk(0, (0, BLOCK), 0, BLOCK, True, True)
        finish(0, BLOCK)
        whole = (0, Q_TILE)

        def q_body(i, _):
            q0 = pl.multiple_of(BLOCK + i * Q_TILE, BLOCK)
            n_int = i * per_tile
            stat_ref[...] = jnp.zeros(stat_ref.shape, F32)
            acc_ref[...] = jnp.zeros(acc_ref.shape, F32)
            for d in reversed(range(per_tile)):
                stick_chunk(q0, (d * K_TILE, Q_TILE), chunk_start(n_int + d), K_TILE, True, False)

            def alive():
                top = jnp.maximum(jnp.max(stat_ref[0]), jnp.max(stat_ref[1]))
                return (top > UNDERFLOW_LOG2).astype(jnp.int32)

            def body(st):
                jj, _ = st
                stick_chunk(q0, whole, chunk_start(n_int - 1 - jj), K_TILE, False, False)
                return jj + 1, alive()

            _, go = lax.while_loop(lambda st: (st[0] < n_int) & (st[1] > 0), body, (0, alive()))

            @pl.when(go > 0)
            def _():
                stick_chunk(q0, whole, 0, BLOCK, False, False)

            finish(q0, Q_TILE)
            return 0
    else:
        softmax_chunk(0, BLOCK, 0, BLOCK, True, True)
        finish(0, BLOCK)

        def q_body(i, _):
            q0 = pl.multiple_of(BLOCK + i * Q_TILE, BLOCK)
            n_int = i * per_tile
            stat_ref[...] = jnp.full(stat_ref.shape, NEG, F32)
            acc_ref[...] = jnp.zeros(acc_ref.shape, F32)

            def body(j, _):
                for d in range(per_tile):
                    softmax_chunk(q0, Q_TILE, chunk_start(j * per_tile + d), K_TILE, False, False)
                return 0

            lax.fori_loop(0, i, body, 0)
            for d in range(per_tile):
                softmax_chunk(q0, Q_TILE, chunk_start(n_int + d), K_TILE, True, False, row_lo=d * K_TILE)
            softmax_chunk(q0, Q_TILE, 0, BLOCK, False, False)
            finish(q0, Q_TILE)
            return 0

    lax.fori_loop(0, n_qt, q_body, 0)


def _causal_attention(mode, q, k, v, batch, seq_len):
    wide = pl.BlockSpec((1, seq_len, 2 * LANES), lambda b, p: (b, 0, p))
    narrow = pl.BlockSpec((1, seq_len, LANES), lambda b, p: (b, 0, p))
    args = [q.reshape(batch, seq_len, 512), k.reshape(batch, seq_len, 512),
            v.reshape(batch, seq_len, v.shape[1])]
    return pl.pallas_call(
        functools.partial(_causal_kernel, mode),
        grid=(batch, 2),
        in_specs=[wide, wide, narrow if mode == "sb" else wide],
        out_specs=narrow,
        out_shape=jax.ShapeDtypeStruct((batch, seq_len, 2 * LANES), BF16),
        scratch_shapes=[pltpu.VMEM((2, Q_TILES[mode], LANES), F32), pltpu.VMEM((2, Q_TILES[mode], LANES), F32)],
        compiler_params=pltpu.CompilerParams(dimension_semantics=("arbitrary", "arbitrary"),
                                             vmem_limit_bytes=VMEM_LIMIT),
        name=mode + "_attention",
    )(*args)


def _outproj_kernel(ya_ref, yb_ref, yc_ref, yd_ref, h_ref, wo_ref, g_ref, wrh_ref, wrl_ref, br_ref, tri_ref,
                    h2_ref, xn_ref, route_ref, cnt_ref):
    o = (_dot(ya_ref[...], wo_ref[0]) + _dot(yb_ref[...], wo_ref[1])
         + _dot(yc_ref[...], wo_ref[2]) + _dot(yd_ref[...], wo_ref[3]))
    h2 = h_ref[...] + o
    h2_ref[...] = h2
    xn = _rms(h2, g_ref[...])
    _rows_to_tiles(xn_ref, (), xn)
    xh = xn.astype(BF16)
    xl = (xn - xh.astype(F32)).astype(BF16)
    wrh, wrl = wrh_ref[...], wrl_ref[...]
    lg = _dot(xh, wrh) + _dot(xl, wrh) + _dot(xh, wrl) + br_ref[...]

    tm = lg.shape[0]
    lane = lax.broadcasted_iota(jnp.int32, (tm, LANES), 1)
    ninf = -jnp.inf

    def first_max(x):
        top = jnp.max(x, axis=-1, keepdims=True)
        return top, jnp.min(jnp.where(x == top, lane, LANES), axis=-1, keepdims=True)

    gl = jnp.where(lane < N_GROUPS, lg, ninf)
    g_max, g_top = first_max(gl)
    g_w = 1.0 / jnp.sum(jnp.exp(gl - g_max), axis=-1, keepdims=True)
    e_lo = N_GROUPS + g_top * EXPERTS_PER_GROUP
    el = jnp.where((lane >= e_lo) & (lane < e_lo + EXPERTS_PER_GROUP), lg, ninf)
    v1, i1 = first_max(el)
    v2, i2 = first_max(jnp.where(lane == i1, ninf, el))
    r21 = jnp.exp(v2 - v1)
    w1 = g_w / (1.0 + r21)
    w2 = w1 * r21

    @pl.when(pl.program_id(0) == 0)
    def _():
        cnt_ref[...] = jnp.zeros_like(cnt_ref)

    m1 = jnp.where(lane == i1, 1.0, 0.0)
    m2 = jnp.where(lane == i2, 1.0, 0.0)
    both = m1 + m2
    before = _dot(tri_ref[...], both.astype(BF16)) + cnt_ref[0:1, :]
    rank1 = jnp.sum(m1 * before, axis=-1, keepdims=True)
    rank2 = jnp.sum(m2 * before, axis=-1, keepdims=True)
    cnt_ref[...] = cnt_ref[...] + jnp.sum(both, axis=0, keepdims=True)
    cols = [w1, w2, (i1 - N_GROUPS).astype(F32), (i2 - N_GROUPS).astype(F32), rank1, rank2]
    route = jnp.zeros((tm, LANES), F32)
    for j, c in enumerate(cols):
        route = jnp.where(lane == j, c, route)
    route_ref[...] = route


def _outproj(ys, h, p):
    t, d = h.shape
    tm = ROW_TILE
    row = lambda i: (i, 0)
    fixed2 = lambda i: (0, 0)
    in_specs = [pl.BlockSpec((tm, 256), row)] * 4 + [
        pl.BlockSpec((tm, d), row),
        pl.BlockSpec((4, 256, d), lambda i: (0, 0, 0)),
        pl.BlockSpec((1, d), fixed2),
        pl.BlockSpec((d, LANES), fixed2),
        pl.BlockSpec((d, LANES), fixed2),
        pl.BlockSpec((1, LANES), fixed2),
        pl.BlockSpec((tm, tm), fixed2),
    ]
    earlier = jnp.asarray(np.tril(np.ones((tm, tm), np.float32), -1), BF16)
    return pl.pallas_call(
        _outproj_kernel,
        grid=(t // tm,),
        in_specs=in_specs,
        out_specs=[pl.BlockSpec((tm, d), row), pl.BlockSpec((tm * ROW_SUB, LANES), row),
                   pl.BlockSpec((tm, LANES), row), pl.BlockSpec((8, LANES), fixed2)],
        out_shape=[jax.ShapeDtypeStruct((t, d), F32), jax.ShapeDtypeStruct((t * ROW_SUB, LANES), F32),
                   jax.ShapeDtypeStruct((t, LANES), F32), jax.ShapeDtypeStruct((8, LANES), F32)],
        compiler_params=pltpu.CompilerParams(dimension_semantics=("arbitrary",),
                                             vmem_limit_bytes=VMEM_LIMIT),
        name="outproj_router",
    )(*ys, h, p["w_out"], p["ffn_norm"], p["w_r_hi"], p["w_r_lo"], p["b_r"], earlier)


def _dispatch_kernel(nv_ref, ids_ref, x_ref, xs_hbm, zero_buf, sem):
    blk = zero_buf.shape[0]

    @pl.when(pl.program_id(0) == 0)
    def _():
        zero_buf[...] = jnp.zeros_like(zero_buf)
        for phase in ("start", "wait"):
            for b in range(nv_ref.shape[0]):
                @pl.when(nv_ref[b] < MOE_BLOCK)
                def _():
                    copy = pltpu.make_async_copy(zero_buf, xs_hbm.at[pl.ds(b * blk, blk)], sem)
                    copy.start() if phase == "start" else copy.wait()

    for j in range(ids_ref.shape[2]):
        tok, _ = divmod(j, TOP_K)
        dst = pl.multiple_of(ids_ref[0, 0, j], ROW_SUB)
        pltpu.make_async_copy(x_ref.at[pl.ds(tok * ROW_SUB, ROW_SUB)], xs_hbm.at[pl.ds(dst, ROW_SUB)],
                              sem).start(priority=j % 2)
    for _ in range(TOP_K):
        pltpu.make_async_copy(x_ref, xs_hbm.at[pl.ds(0, x_ref.shape[0])], sem).wait()


def _dispatch(xn, y_ids, nvalid):
    tm = ROW_TILE
    t = xn.shape[0] // ROW_SUB
    n_blk = nvalid.shape[0]
    return pl.pallas_call(
        _dispatch_kernel,
        grid_spec=pltpu.PrefetchScalarGridSpec(
            num_scalar_prefetch=1,
            grid=(t // tm,),
            in_specs=[pl.BlockSpec((1, 1, TOP_K * tm), lambda i, nv: (i, 0, 0), memory_space=pltpu.SMEM),
                      pl.BlockSpec((tm * ROW_SUB, LANES), lambda i, nv: (i, 0))],
            out_specs=pl.BlockSpec(memory_space=pl.ANY),
            scratch_shapes=[pltpu.VMEM((MOE_BLOCK * ROW_SUB, LANES), F32), pltpu.SemaphoreType.DMA],
        ),
        out_shape=jax.ShapeDtypeStruct((n_blk * MOE_BLOCK * ROW_SUB, LANES), F32),
        compiler_params=pltpu.CompilerParams(dimension_semantics=("arbitrary",),
                                             vmem_limit_bytes=VMEM_LIMIT),
        name="moe_dispatch",
    )(nvalid, y_ids.reshape(t // tm, 1, TOP_K * tm), xn)


def _moe_kernel(be_ref, nv_ref, x_ref, wg_ref, wu_ref, wd_ref, y_ref, wg_bf, wu_bf, wd_bf):
    s = pl.program_id(0)
    nv = nv_ref[s]

    @pl.when((s == 0) | (be_ref[s] != be_ref[jnp.maximum(s - 1, 0)]))
    def _():
        wg_bf[...] = wg_ref[0].astype(BF16)
        wu_bf[...] = wu_ref[0].astype(BF16)
        wd_bf[...] = wd_ref[0].astype(BF16)

    @pl.when(nv > 0)
    def _():
        x = _rows_from_tiles(x_ref, (), MOE_BLOCK).astype(BF16)
        gate = _dot(x, wg_bf[...])
        up = _dot(x, wu_bf[...])
        hid = (gate * (1.0 / (1.0 + jnp.exp(-gate))) * up).astype(BF16)
        _rows_to_tiles(y_ref, (), _dot(hid, wd_bf[...]))

    @pl.when(nv == 0)
    def _():
        y_ref[...] = jnp.zeros_like(y_ref)


def _moe(xs, block_e, nvalid, p):
    d = xs.shape[1] * ROW_SUB
    n_blk = block_e.shape[0]
    layer = p["layer"]
    hdim = p["w_gate"].shape[3]
    wspec = lambda shape: pl.BlockSpec((None, 1) + shape, lambda s, be, nv: (layer, be[s], 0, 0))
    rows = pl.BlockSpec((MOE_BLOCK * ROW_SUB, LANES), lambda s, be, nv: (s, 0))
    return pl.pallas_call(
        _moe_kernel,
        grid_spec=pltpu.PrefetchScalarGridSpec(
            num_scalar_prefetch=2,
            grid=(n_blk,),
            in_specs=[rows, wspec((d, hdim)), wspec((d, hdim)), wspec((hdim, d))],
            out_specs=rows,
            scratch_shapes=[pltpu.VMEM((d, hdim), BF16), pltpu.VMEM((d, hdim), BF16), pltpu.VMEM((hdim, d), BF16)],
        ),
        out_shape=jax.ShapeDtypeStruct(xs.shape, F32),
        compiler_params=pltpu.CompilerParams(dimension_semantics=("arbitrary",),
                                             vmem_limit_bytes=VMEM_LIMIT),
        name="moe_experts",
    )(block_e, nvalid, xs, p["w_gate"], p["w_up"], p["w_down"])


def _route(route, counts, t):
    a = t * TOP_K
    expert = route[:, 2:4].astype(jnp.int32).reshape(a)
    pos = route[:, 4:6].astype(jnp.int32).reshape(a)
    counts = counts.astype(jnp.int32)
    padded = (counts + MOE_BLOCK - 1) // MOE_BLOCK * MOE_BLOCK
    pad_end = jnp.cumsum(padded)
    pad_start = pad_end - padded
    dest = pad_start[expert] + pos
    n_blk = (a + N_EXPERTS * MOE_BLOCK) // MOE_BLOCK
    starts = jnp.arange(n_blk, dtype=jnp.int32) * MOE_BLOCK
    block_e = jnp.minimum(jnp.sum((pad_end[None, :] <= starts[:, None]).astype(jnp.int32), axis=1), N_EXPERTS - 1)
    nvalid = jnp.clip((pad_start + counts)[block_e] - starts, 0, MOE_BLOCK)
    return block_e, nvalid, dest * ROW_SUB


def _final_kernel(h_ref, ids0_ref, idsn_ref, y_hbm, ew_ref, g_ref, o_ref, ybuf, ysem):
    h = _combine_experts(h_ref[...], ew_ref[...], y_hbm, ids0_ref, idsn_ref, ybuf, ysem,
                         pl.program_id(0), pl.num_programs(0))
    o_ref[0] = _rms(h, g_ref[...])


def _final(h, y2, y_ids, ew, g, batch, seq_len):
    t, d = h.shape
    per_seq = seq_len // BLOCK
    out_blocks = per_seq - 1
    row = lambda n: ((n // out_blocks) * per_seq + n % out_blocks + 1, 0)
    ids = y_ids.reshape(batch, per_seq, TOP_K * BLOCK)[:, 1:].reshape(batch * out_blocks, 1, TOP_K * BLOCK)
    ids = jnp.concatenate([ids, jnp.zeros_like(ids[:1])], axis=0)
    return pl.pallas_call(
        _final_kernel,
        grid=(batch * out_blocks,),
        in_specs=[pl.BlockSpec((BLOCK, d), row),
                  pl.BlockSpec((1, 1, TOP_K * BLOCK), lambda n: (0, 0, 0), memory_space=pltpu.SMEM),
                  pl.BlockSpec((1, 1, TOP_K * BLOCK), lambda n: (n + 1, 0, 0), memory_space=pltpu.SMEM),
                  pl.BlockSpec(memory_space=pl.ANY),
                  pl.BlockSpec((BLOCK, LANES), row),
                  pl.BlockSpec((1, d), lambda n: (0, 0))],
        out_specs=pl.BlockSpec((1, BLOCK, d), lambda n: (n // out_blocks, n % out_blocks, 0)),
        out_shape=jax.ShapeDtypeStruct((batch, seq_len - BLOCK, d), F32),
        scratch_shapes=[pltpu.VMEM((2, TOP_K, BLOCK * ROW_SUB, LANES), F32), pltpu.SemaphoreType.DMA((2,))],
        compiler_params=pltpu.CompilerParams(dimension_semantics=("arbitrary",),
                                             vmem_limit_bytes=VMEM_LIMIT),
        name="final_norm",
    )(h, ids, ids, y2, ew, g)


def _rope_table(seq_len):
    half = MLA_ROPE // 2
    pos = (jnp.arange(seq_len, dtype=jnp.int32) - PAD_FRONT).astype(F32)
    inv_freq = ROPE_THETA ** (-jnp.arange(half, dtype=F32) / half)
    ang = pos[:, None] * inv_freq[None, :]
    cos, sin = jnp.cos(ang), jnp.sin(ang)
    cos2 = jnp.concatenate([cos, cos], axis=1)
    sin2 = jnp.concatenate([-sin, sin], axis=1)
    z = lambda w: jnp.zeros((seq_len, w), F32)
    scale = (MLA_NOPE + MLA_ROPE) ** -0.5 * LOG2E
    cos_q = jnp.concatenate([jnp.ones((seq_len, MLA_NOPE), F32), cos2, z(32)], axis=1) * scale
    sin_q = jnp.concatenate([z(MLA_NOPE), sin2, z(32)], axis=1) * scale
    cos_k = jnp.concatenate([z(MLA_NOPE), cos2, z(32)], axis=1)
    sin_k = jnp.concatenate([z(MLA_NOPE), sin2, z(32)], axis=1)
    return jnp.concatenate([cos_q, sin_q, cos_k, sin_k], axis=1)


def _swap_halves(w):
    half = w.shape[-1] // 2
    return jnp.concatenate([w[..., half:], w[..., :half]], axis=-1)


def _layer_params(i, seq_len, attn_norm, w_in, b_forget, sinks, mla_q_norm, mla_kv_norm, mla_w_uq,
                  mla_w_ukv, w_out, ffn_norm, w_group, b_group, w_router, b_router, w_gate, w_up, w_down):
    d = w_in.shape[1]
    w = w_in[i]
    sizes = (256, 128, 128, 256, 256, 256, 4, 256, 128, 32, 256, 256, 256)
    offs = np.concatenate([[0], np.cumsum(sizes)])
    (a_q, a_k, a_v, f_q, f_k, f_v, f_g, c_q, c_kv, c_kr, s_q, s_k, s_v) = [
        w[:, offs[j]:offs[j + 1]] for j in range(len(sizes))]
    qscale = HEAD_DIM ** -0.5
    grp = SWA_HEADS // SWA_KV_HEADS
    a_q = a_q.reshape(d, SWA_KV_HEADS, grp, HEAD_DIM).transpose(0, 2, 1, 3).reshape(d, 256)
    z = lambda n: jnp.zeros((d, n), F32)
    g_grp = jnp.concatenate([f_g[:, 0:2], z(62), c_kr, z(32)], axis=1)
    gs_grp = jnp.concatenate([f_g[:, 2:4], z(62), _swap_halves(c_kr), z(32)], axis=1)
    w_perm = jnp.concatenate([a_q * qscale, a_k, a_v, f_q * qscale, f_k, f_v, s_q * qscale, s_k, s_v,
                              c_q, c_kv, g_grp, gs_grp], axis=1).astype(BF16)
    wuq = mla_w_uq[i].reshape(MLA_Q_LORA, 4, MLA_NOPE + MLA_ROPE)
    zq = lambda n: jnp.zeros((MLA_Q_LORA, 4, n), F32)
    w_uq_a = jnp.concatenate([wuq, zq(32)], axis=2).reshape(MLA_Q_LORA, 512).astype(BF16)
    w_uq_b = jnp.concatenate([zq(MLA_NOPE), _swap_halves(wuq[:, :, MLA_NOPE:]), zq(32)],
                             axis=2).reshape(MLA_Q_LORA, 512).astype(BF16)
    wukv = mla_w_ukv[i].reshape(MLA_KV_LORA, 4, MLA_NOPE + MLA_V)
    w_kv_k = jnp.concatenate([wukv[:, :, :MLA_NOPE], jnp.zeros((MLA_KV_LORA, 4, 64), F32)],
                             axis=2).reshape(MLA_KV_LORA, 512).astype(BF16)
    w_kv_v = wukv[:, :, MLA_NOPE:].reshape(MLA_KV_LORA, 256).astype(BF16)
    bf = b_forget[i].astype(F32)
    b_f = jnp.zeros((1, 256), F32).at[0, 0:2].set(bf[0:2]).at[0, 128:130].set(bf[2:4])
    wo = w_out[i]
    wo_a = wo[:256].reshape(SWA_KV_HEADS, grp, HEAD_DIM, d).transpose(1, 0, 2, 3).reshape(256, d)
    wo4 = jnp.concatenate([wo_a, wo[256:]], axis=0).reshape(4, 256, d).astype(BF16)
    w_r = jnp.concatenate([w_group[i], w_router[i], jnp.zeros((d, LANES - N_GROUPS - N_EXPERTS), F32)], axis=1)
    w_r_hi = w_r.astype(BF16)
    w_r_lo = (w_r - w_r_hi.astype(F32)).astype(BF16)
    b_r = jnp.concatenate([b_group[i], b_router[i], jnp.zeros((LANES - N_GROUPS - N_EXPERTS,), F32)])[None, :]
    return dict(
        attn_norm=attn_norm[i][None, :], w_in=w_perm, q_norm=mla_q_norm[i][None, :],
        kv_norm=mla_kv_norm[i][None, :], w_uq_a=w_uq_a, w_uq_b=w_uq_b, w_kv_k=w_kv_k, w_kv_v=w_kv_v,
        rope_tab=_rope_table(seq_len), b_forget=b_f, sinks=sinks[i].astype(F32), w_out=wo4,
        ffn_norm=ffn_norm[i][None, :], w_r_hi=w_r_hi, w_r_lo=w_r_lo, b_r=b_r.astype(F32),
        w_gate=w_gate, w_up=w_up, w_down=w_down, layer=i)


def kernel(x, meta_tokens, attn_norm, w_in, b_forget, sinks, mla_q_norm, mla_kv_norm, mla_w_uq, mla_w_ukv,
           w_out, ffn_norm, w_group, b_group, w_router, b_router, w_gate, w_up, w_down, final_norm):
    batch, seq, d = x.shape
    seq_len = seq + BLOCK
    assert seq_len % ROW_TILE == 0 and seq_len % (BLOCK * SWA_Q_BLOCKS) == 0
    assert all(seq % tile == 0 for tile in Q_TILES.values())
    t = batch * seq_len
    depth = w_in.shape[0]
    lead = jnp.concatenate([jnp.zeros((PAD_FRONT, d), x.dtype), meta_tokens.astype(x.dtype)], axis=0)
    h = (lead, x)
    y2 = y_ids = ew = None
    for i in range(depth):
        p = _layer_params(i, seq_len, attn_norm, w_in, b_forget, sinks, mla_q_norm, mla_kv_norm, mla_w_uq,
                          mla_w_ukv, w_out, ffn_norm, w_group, b_group, w_router, b_router, w_gate, w_up, w_down)
        h, (qa, fq, fk, fv, cq, ck, cv, sq, sk, sv) = _inproj(h, y2, y_ids, ew, p, seq_len)
        y_a = _swa_attention(qa, p["sinks"], batch, seq_len)
        y_b = _causal_attention("fox", fq, fk, fv, batch, seq_len)
        y_c = _causal_attention("mla", cq, ck, cv, batch, seq_len)
        y_d = _causal_attention("sb", sq, sk, sv, batch, seq_len)
        ys = [y.reshape(t, 256) for y in (y_a, y_b, y_c, y_d)]
        h, xn, ew, counts = _outproj(ys, h, p)
        block_e, nvalid, y_ids = _route(ew, counts[0, N_GROUPS:N_GROUPS + N_EXPERTS], t)
        y2 = _moe(_dispatch(xn, y_ids, nvalid), block_e, nvalid, p)
    return _final(h, y2, y_ids, ew, final_norm[None, :], batch, seq_len)
```
